```python
import math
import jax, jax.numpy as jnp
from jax import lax
import numpy as np

D_MODEL = 1024
BATCH = 2
SEQ = 8192
DEPTH = 4

PLE_DIM = 256
SB_HEADS = 8
SB_HEAD_DIM = 64
SB_WIDTH = SB_HEADS * SB_HEAD_DIM
RW_HEADS = 8
RW_HEAD_DIM = 64
RW_WIDTH = RW_HEADS * RW_HEAD_DIM
RW_DECAY_LORA = 64
RW_AAA_LORA = 64
RW_GATE_LORA = 160
RW_GN_EPS = 64e-5
RW_IN = 3 * RW_WIDTH + RW_DECAY_LORA + RW_AAA_LORA + RW_GATE_LORA
HYB_IN = 3 * SB_WIDTH + RW_IN
HYB_OUT = SB_WIDTH + RW_WIDTH
MLA_HEADS = 16
MLA_NOPE = 64
MLA_ROPE = 32
MLA_V = 64
MLA_Q_RANK = 384
MLA_KV_RANK = 256
MLA_DOWN = MLA_Q_RANK + MLA_KV_RANK + MLA_ROPE
ROPE_THETA = 10000.0
FFN_DIM = 2816
CONV_WIDTH = 3
Q_BLOCK = 128
NORM_EPS = 1e-6
N_EVEN = (DEPTH + 1) // 2
N_ODD = DEPTH // 2

kernel_name = "hybrid_sb_rwkv7_mla_convglu"

F32 = jnp.float32


def rmsnorm(x, g):
    xf = x.astype(F32)
    y = xf * lax.rsqrt(jnp.mean(xf * xf, axis=-1, keepdims=True) + NORM_EPS)
    return (y * g.astype(F32)).astype(x.dtype)


def shift_right(x, n):
    return jnp.pad(x, ((0, 0), (n, 0), (0, 0)))[:, : x.shape[1]]


def sweep_query_blocks(fn, q):
    b, s, h, e = q.shape
    nb = s // Q_BLOCK
    qb = q.reshape(b, nb, Q_BLOCK, h, e).transpose(1, 0, 2, 3, 4)
    starts = jnp.arange(nb, dtype=jnp.int32) * Q_BLOCK
    out = lax.map(lambda args: fn(args[0], args[1]), (qb, starts))
    return out.transpose(1, 0, 2, 3, 4).reshape(b, s, h, out.shape[-1])


def stick_breaking_attention(q, k, v):
    s_len = k.shape[1]
    scale = 1.0 / math.sqrt(q.shape[-1])
    key_pos = jnp.arange(s_len, dtype=jnp.int32)
    kf = k.astype(F32)
    vf = v.astype(F32)

    def block(qb, t0):
        z = jnp.einsum('bqhe,bshe->bhqs', qb.astype(F32), kf) * scale
        qpos = t0 + jnp.arange(Q_BLOCK, dtype=jnp.int32)
        mask = key_pos[None, :] < qpos[:, None]
        log_one_minus = jnp.where(mask, -jax.nn.softplus(z), 0.0)
        rev = lax.cumsum(log_one_minus, axis=3, reverse=True)
        after = jnp.concatenate([rev[..., 1:], jnp.zeros_like(rev[..., :1])], axis=3)
        w = jnp.where(mask, jnp.exp(jax.nn.log_sigmoid(z) + after), 0.0)
        return jnp.einsum('bhqs,bshe->bqhe', w, vf)

    return sweep_query_blocks(block, q).astype(v.dtype)


def causal_softmax_attention(q, k, v, scale):
    s_len = k.shape[1]
    key_pos = jnp.arange(s_len, dtype=jnp.int32)
    kf = k.astype(F32)
    vf = v.astype(F32)

    def block(qb, t0):
        sc = jnp.einsum('bqhe,bshe->bhqs', qb.astype(F32), kf) * scale
        qpos = t0 + jnp.arange(Q_BLOCK, dtype=jnp.int32)
        mask = key_pos[None, :] <= qpos[:, None]
        w = jax.nn.softmax(jnp.where(mask, sc, -jnp.inf), axis=-1)
        return jnp.einsum('bhqs,bshe->bqhe', w, vf)

    return sweep_query_blocks(block, q).astype(v.dtype)


def apply_rope(x, positions):
    half = x.shape[-1] // 2
    inv_freq = ROPE_THETA ** (-jnp.arange(half, dtype=F32) / half)
    ang = positions.astype(F32)[:, :, None, None] * inv_freq
    cos, sin = jnp.cos(ang), jnp.sin(ang)
    xf = x.astype(F32)
    x1, x2 = xf[..., :half], xf[..., half:]
    return jnp.concatenate([x1 * cos - x2 * sin, x1 * sin + x2 * cos], axis=-1).astype(x.dtype)


def rwkv7_time_mix(proj, mu, w0, w2, a0, a2, g2, k_k, k_a, r_k, ln_w, ln_b):
    b, s, _ = proj.shape
    pf = proj.astype(F32)
    xm = pf + (shift_right(pf, 1) - pf) * mu
    r, k, v, wl, al, gl = jnp.split(
        xm, [RW_WIDTH, 2 * RW_WIDTH, 3 * RW_WIDTH, 3 * RW_WIDTH + RW_DECAY_LORA,
             3 * RW_WIDTH + RW_DECAY_LORA + RW_AAA_LORA], axis=-1)
    w = -jax.nn.softplus(-(w0 + jnp.tanh(wl) @ w2)) - 0.5
    decay = jnp.exp(-jnp.exp(w))
    a = jax.nn.sigmoid(a0 + al @ a2)
    g = jax.nn.sigmoid(gl) @ g2

    hd = lambda t: t.reshape(b, s, RW_HEADS, RW_HEAD_DIM)
    ph = lambda t: t.astype(F32).reshape(RW_HEADS, RW_HEAD_DIM)
    r, k, v, decay, a = hd(r), hd(k), hd(v), hd(decay), hd(a)
    kk = k * ph(k_k)
    kk = kk * lax.rsqrt(jnp.maximum(jnp.sum(kk * kk, axis=-1, keepdims=True), 1e-24))
    k = k * (1.0 + (a - 1.0) * ph(k_a))

    def step(state, inp):
        r_t, w_t, k_t, v_t, kk_t, a_t = inp
        sa = jnp.einsum('bhij,bhj->bhi', state, -kk_t)
        state = (state * w_t[:, :, None, :] + sa[..., None] * (kk_t * a_t)[:, :, None, :]
                 + v_t[..., None] * k_t[:, :, None, :])
        return state, jnp.einsum('bhij,bhj->bhi', state, r_t)

    tm = lambda t: jnp.moveaxis(t, 1, 0)
    state0 = jnp.zeros((b, RW_HEADS, RW_HEAD_DIM, RW_HEAD_DIM), F32)
    _, y = lax.scan(step, state0, (tm(r), tm(decay), tm(k), tm(v), tm(kk), tm(a)))
    y = jnp.moveaxis(y, 0, 1)

    mean = jnp.mean(y, axis=-1, keepdims=True)
    var = jnp.mean(jnp.square(y - mean), axis=-1, keepdims=True)
    y = (y - mean) * lax.rsqrt(var + RW_GN_EPS)
    y = y * ph(ln_w) + ph(ln_b)
    y = y + jnp.sum(r * k * r_k.astype(F32), axis=-1, keepdims=True) * v
    return (y.reshape(b, s, RW_WIDTH) * g).astype(proj.dtype)


def sb_rwkv_mixer(hn, w_in, w_out, mu, w0, w2, a0, a2, g2, k_k, k_a, r_k, ln_w, ln_b):
    b, s, _ = hn.shape
    proj = hn @ w_in
    qa, ka, va, rw = jnp.split(proj, [SB_WIDTH, 2 * SB_WIDTH, 3 * SB_WIDTH], axis=-1)
    hd = lambda t: t.reshape(b, s, SB_HEADS, SB_HEAD_DIM)
    o_a = stick_breaking_attention(hd(qa), hd(ka), hd(va)).reshape(b, s, SB_WIDTH)
    o_b = rwkv7_time_mix(rw, mu, w0, w2, a0, a2, g2, k_k, k_a, r_k, ln_w, ln_b)
    return jnp.concatenate([o_a, o_b.astype(o_a.dtype)], axis=-1) @ w_out


def mla_mixer(hn, positions, w_down, q_norm, kv_norm, w_uq, w_ukv, w_o):
    b, s, _ = hn.shape
    c = hn @ w_down
    cq, ckv, kr = jnp.split(c, [MLA_Q_RANK, MLA_Q_RANK + MLA_KV_RANK], axis=-1)
    q = (rmsnorm(cq, q_norm) @ w_uq).reshape(b, s, MLA_HEADS, MLA_NOPE + MLA_ROPE)
    kv = (rmsnorm(ckv, kv_norm) @ w_ukv).reshape(b, s, MLA_HEADS, MLA_NOPE + MLA_V)
    q_nope, q_rope = jnp.split(q, [MLA_NOPE], axis=-1)
    k_nope, v = jnp.split(kv, [MLA_NOPE], axis=-1)
    q = jnp.concatenate([q_nope, apply_rope(q_rope, positions)], axis=-1)
    k_rope = jnp.broadcast_to(apply_rope(kr[:, :, None, :], positions), (b, s, MLA_HEADS, MLA_ROPE))
    k = jnp.concatenate([k_nope, k_rope], axis=-1)
    o = causal_softmax_attention(q, k, v, 1.0 / math.sqrt(MLA_NOPE + MLA_ROPE))
    return o.reshape(b, s, MLA_HEADS * MLA_V) @ w_o


def conv_glu(hn, w_in, conv_w, conv_b, w_out):
    u, gate_in = jnp.split(hn @ w_in, 2, axis=-1)
    c = (conv_w[0] * shift_right(gate_in, 2) + conv_w[1] * shift_right(gate_in, 1)
         + conv_w[2] * gate_in + conv_b)
    return (jax.nn.gelu(c, approximate=False) * u) @ w_out


def setup_inputs(seed: int = 0) -> dict:
    key = jax.random.key(seed)
    keys = jax.random.split(key, 48)
    counter = [0]

    def nk():
        kk = keys[counter[0]]
        counter[0] += 1
        return kk

    def nrm(shape, scale):
        return jax.random.normal(nk(), shape, F32) * scale

    def gain(shape):
        return 1.0 + nrm(shape, 0.05)

    x = nrm((BATCH, SEQ, D_MODEL), 1.0)
    p = nrm((DEPTH, BATCH, SEQ, PLE_DIM), 1.0)
    offset = jax.random.randint(nk(), (BATCH, 1), 0, 4096, dtype=jnp.int32)
    positions = offset + jnp.arange(SEQ, dtype=jnp.int32)[None, :]
    return {
        "x": x,
        "p": p,
        "positions": positions,
        "attn_norm": gain((DEPTH, D_MODEL)),
        "ffn_norm": gain((DEPTH, D_MODEL)),
        "ffn_w_in": nrm((DEPTH, D_MODEL, 2 * FFN_DIM), D_MODEL ** -0.5),
        "ffn_conv_w": nrm((DEPTH, CONV_WIDTH, FFN_DIM), CONV_WIDTH ** -0.5),
        "ffn_conv_b": nrm((DEPTH, FFN_DIM), 0.01),
        "ffn_w_out": nrm((DEPTH, FFN_DIM, D_MODEL), FFN_DIM ** -0.5),
        "ple_w_proj": nrm((DEPTH, PLE_DIM, D_MODEL), PLE_DIM ** -0.5),
        "ple_norm": gain((DEPTH, D_MODEL)),
        "ple_gate_norm": gain((DEPTH, D_MODEL)),
        "ple_w_gate": nrm((DEPTH, D_MODEL, D_MODEL), D_MODEL ** -0.5),
        "hyb_w_in": nrm((N_EVEN, D_MODEL, HYB_IN), D_MODEL ** -0.5),
        "hyb_w_out": nrm((N_EVEN, HYB_OUT, D_MODEL), HYB_OUT ** -0.5),
        "rw_mu": jax.random.uniform(nk(), (N_EVEN, RW_IN), F32),
        "rw_w0": jax.random.uniform(nk(), (N_EVEN, RW_WIDTH), F32, -6.0, 1.0),
        "rw_w2": nrm((N_EVEN, RW_DECAY_LORA, RW_WIDTH), RW_DECAY_LORA ** -0.5),
        "rw_a0": nrm((N_EVEN, RW_WIDTH), 0.1),
        "rw_a2": nrm((N_EVEN, RW_AAA_LORA, RW_WIDTH), RW_AAA_LORA ** -0.5),
        "rw_g2": nrm((N_EVEN, RW_GATE_LORA, RW_WIDTH), RW_GATE_LORA ** -0.5),
        "rw_k_k": 0.85 + nrm((N_EVEN, RW_WIDTH), 0.05),
        "rw_k_a": gain((N_EVEN, RW_WIDTH)),
        "rw_r_k": nrm((N_EVEN, RW_HEADS, RW_HEAD_DIM), 0.1),
        "rw_ln_w": gain((N_EVEN, RW_WIDTH)),
        "rw_ln_b": nrm((N_EVEN, RW_WIDTH), 0.01),
        "mla_w_down": nrm((N_ODD, D_MODEL, MLA_DOWN), D_MODEL ** -0.5),
        "mla_q_norm": gain((N_ODD, MLA_Q_RANK)),
        "mla_kv_norm": gain((N_ODD, MLA_KV_RANK)),
        "mla_w_uq": nrm((N_ODD, MLA_Q_RANK, MLA_HEADS * (MLA_NOPE + MLA_ROPE)), MLA_Q_RANK ** -0.5),
        "mla_w_ukv": nrm((N_ODD, MLA_KV_RANK, MLA_HEADS * (MLA_NOPE + MLA_V)), MLA_KV_RANK ** -0.5),
        "mla_w_o": nrm((N_ODD, MLA_HEADS * MLA_V, D_MODEL), (MLA_HEADS * MLA_V) ** -0.5),
        "final_norm": gain((D_MODEL,)),
    }


def reference(x, p, positions, attn_norm, ffn_norm, ffn_w_in, ffn_conv_w, ffn_conv_b, ffn_w_out,
              ple_w_proj, ple_norm, ple_gate_norm, ple_w_gate,
              hyb_w_in, hyb_w_out, rw_mu, rw_w0, rw_w2, rw_a0, rw_a2, rw_g2, rw_k_k, rw_k_a,
              rw_r_k, rw_ln_w, rw_ln_b,
              mla_w_down, mla_q_norm, mla_kv_norm, mla_w_uq, mla_w_ukv, mla_w_o, final_norm):
    h = x
    for i in range(DEPTH):
        j = i // 2
        hn = rmsnorm(h, attn_norm[i])
        if i % 2 == 0:
            mix = sb_rwkv_mixer(hn, hyb_w_in[j], hyb_w_out[j], rw_mu[j], rw_w0[j], rw_w2[j],
                                rw_a0[j], rw_a2[j], rw_g2[j], rw_k_k[j], rw_k_a[j], rw_r_k[j],
                                rw_ln_w[j], rw_ln_b[j])
        else:
            mix = mla_mixer(hn, positions, mla_w_down[j], mla_q_norm[j], mla_kv_norm[j],
                            mla_w_uq[j], mla_w_ukv[j], mla_w_o[j])
        h = h + mix
        h = h + conv_glu(rmsnorm(h, ffn_norm[i]), ffn_w_in[i], ffn_conv_w[i], ffn_conv_b[i], ffn_w_out[i])
        e = rmsnorm(p[i] @ ple_w_proj[i], ple_norm[i])
        gate = jax.nn.sigmoid(rmsnorm(h, ple_gate_norm[i]) @ ple_w_gate[i])
        h = h + gate * e
    return rmsnorm(h, final_norm)
```

```python
import functools
import math

import jax
import jax.numpy as jnp
from jax import lax
from jax.experimental import pallas as pl
from jax.experimental.pallas import tpu as pltpu

F32 = jnp.float32
BF16 = jnp.bfloat16
HI = lax.Precision.HIGHEST

NORM_EPS = 1e-6
GN_EPS = 64e-5
ROPE_THETA = 10000.0
LANE = 128
VMEM_LIMIT = 48 * 1024 * 1024

SB_HEADS = 8
RW_HEADS = 8
HEAD_DIM = 64
MLA_HEADS = 16
MLA_NOPE = 64
MLA_ROPE = 32
MLA_V = 64
MLA_Q_RANK = 384
MLA_KV_RANK = 256
RW_CHUNK = 64
RW_SUB = 16


def _cp(*sem):
    return pltpu.CompilerParams(dimension_semantics=sem, vmem_limit_bytes=VMEM_LIMIT)


def _iota(shape, dim):
    return lax.broadcasted_iota(jnp.int32, shape, dim)


def _softplus(z):
    return jnp.maximum(z, 0.0) + jnp.log1p(jnp.exp(-jnp.abs(z)))


def _sigmoid(z):
    return 1.0 / (1.0 + jnp.exp(-z))


def _dot(a, b, precision=None):
    return jnp.dot(a, b, preferred_element_type=F32, precision=precision)


def _dot_nt(a, b, precision=None):
    return lax.dot_general(a, b, (((1,), (1,)), ((), ())), preferred_element_type=F32,
                           precision=precision)


def _dot_tn(a, b, precision=None):
    return lax.dot_general(a, b, (((0,), (0,)), ((), ())), preferred_element_type=F32,
                           precision=precision)


def _rms(x, g, eps=NORM_EPS):
    return x * lax.rsqrt(jnp.mean(x * x, axis=-1, keepdims=True) + eps) * g


def _rms_matmul_kernel(x_ref, g_ref, w_ref, o_ref, xn_ref):
    @pl.when(pl.program_id(1) == 0)
    def _():
        xn_ref[...] = _rms(x_ref[...].astype(F32), g_ref[...]).astype(BF16)

    o_ref[...] = _dot(xn_ref[...], w_ref[...]).astype(o_ref.dtype)


def rms_matmul(x, g, w, *, x_col=0, tm=512, tn=512, out_dtype=F32):
    t = x.shape[0]
    k, n = w.shape
    tn = min(tn, n)
    return pl.pallas_call(
        _rms_matmul_kernel,
        grid=(t // tm, n // tn),
        in_specs=[pl.BlockSpec((tm, k), lambda i, j: (i, x_col)),
                  pl.BlockSpec((1, k), lambda i, j: (0, 0)),
                  pl.BlockSpec((k, tn), lambda i, j: (0, j))],
        out_specs=pl.BlockSpec((tm, tn), lambda i, j: (i, j)),
        out_shape=jax.ShapeDtypeStruct((t, n), out_dtype),
        scratch_shapes=[pltpu.VMEM((tm, k), BF16)],
        compiler_params=_cp("parallel", "arbitrary"),
        name="rms_matmul",
    )(x, g.reshape(1, k).astype(F32), w)


def _matmul_res_kernel(a_ref, w_ref, r_ref, o_ref):
    o_ref[...] = r_ref[...] + _dot(a_ref[...].astype(BF16), w_ref[...])


def matmul_residual(res, a, w, *, tm=512, tn=512):
    t, k = a.shape
    n = w.shape[1]
    return pl.pallas_call(
        _matmul_res_kernel,
        grid=(t // tm, n // tn),
        in_specs=[pl.BlockSpec((tm, k), lambda i, j: (i, 0)),
                  pl.BlockSpec((k, tn), lambda i, j: (0, j)),
                  pl.BlockSpec((tm, tn), lambda i, j: (i, j))],
        out_specs=pl.BlockSpec((tm, tn), lambda i, j: (i, j)),
        out_shape=jax.ShapeDtypeStruct((t, n), F32),
        compiler_params=_cp("parallel", "arbitrary"),
        name="matmul_residual",
    )(a, w, res)


def _convglu_out_kernel(u_ref, g_ref, gp_ref, cw_ref, cb_ref, w_ref, r_ref, o_ref, *, tm, seq):
    i = pl.program_id(0)
    g = g_ref[...]
    rows = _iota(g.shape, 0)
    at_start = (i * tm) % seq == 0
    halo = jnp.where(at_start, 0.0, gp_ref[...])
    p1 = jnp.where(rows == 0, halo[7:8, :], pltpu.roll(g, 1, axis=0))
    p2 = jnp.where(rows == 0, halo[6:7, :], jnp.where(rows == 1, halo[7:8, :],
                                                      pltpu.roll(g, 2, axis=0)))
    c = cw_ref[0:1, :] * p2 + cw_ref[1:2, :] * p1 + cw_ref[2:3, :] * g + cb_ref[...]
    act = 0.5 * c * (1.0 + lax.erf(c * (1.0 / math.sqrt(2.0)))) * u_ref[...]
    o_ref[...] = r_ref[...] + _dot(act.astype(BF16), w_ref[...])


def convglu_out(res, ug, conv_w, conv_b, w_out, *, seq, tm=256):
    t = res.shape[0]
    f, n = w_out.shape
    hb = tm // 8
    return pl.pallas_call(
        functools.partial(_convglu_out_kernel, tm=tm, seq=seq),
        grid=(t // tm,),
        in_specs=[pl.BlockSpec((tm, f), lambda i: (i, 0)),
                  pl.BlockSpec((tm, f), lambda i: (i, 1)),
                  pl.BlockSpec((8, f), lambda i: (jnp.maximum(i * hb - 1, 0), 1)),
                  pl.BlockSpec((3, f), lambda i: (0, 0)),
                  pl.BlockSpec((1, f), lambda i: (0, 0)),
                  pl.BlockSpec((f, n), lambda i: (0, 0)),
                  pl.BlockSpec((tm, n), lambda i: (i, 0))],
        out_specs=pl.BlockSpec((tm, n), lambda i: (i, 0)),
        out_shape=jax.ShapeDtypeStruct((t, n), F32),
        compiler_params=_cp("parallel"),
        name="convglu_out",
    )(ug, ug, ug, conv_w, conv_b.reshape(1, f), w_out, res)


def _ple_kernel(h_ref, p_ref, wp_ref, pn_ref, gn_ref, wg_ref, fn_ref, o_ref, *, final):
    h = h_ref[...]
    e = _rms(_dot(p_ref[...].astype(BF16), wp_ref[...]), pn_ref[...])
    gate = _sigmoid(_dot(_rms(h, gn_ref[...]).astype(BF16), wg_ref[...]))
    out = h + gate * e
    if final:
        out = _rms(out, fn_ref[...])
    o_ref[...] = out


def ple(h, p, w_proj, p_norm, g_norm, w_gate, f_norm, *, final, tm=512):
    t, d = h.shape
    pd = p.shape[1]
    row = lambda i: (i, 0)
    fix = lambda i: (0, 0)
    return pl.pallas_call(
        functools.partial(_ple_kernel, final=final),
        grid=(t // tm,),
        in_specs=[pl.BlockSpec((tm, d), row), pl.BlockSpec((tm, pd), row),
                  pl.BlockSpec((pd, d), fix), pl.BlockSpec((1, d), fix), pl.BlockSpec((1, d), fix),
                  pl.BlockSpec((d, d), fix), pl.BlockSpec((1, d), fix)],
        out_specs=pl.BlockSpec((tm, d), row),
        out_shape=jax.ShapeDtypeStruct((t, d), F32),
        compiler_params=_cp("parallel"),
        name="ple",
    )(h, p, w_proj, p_norm.reshape(1, d), g_norm.reshape(1, d), w_gate, f_norm.reshape(1, d))


def _sb_attn_kernel(q_ref, k_ref, v_ref, u_ref, o_ref, *, tq, tk):
    qi = pl.program_id(1)
    q = q_ref[0]
    nkb = (qi * tq + tq) // tk
    rows = qi * tq + _iota((tq, tk), 0)
    cols0 = _iota((tq, tk), 1)

    def body(it, carry):
        run, acc = carry
        j = nkb - 1 - it
        k0 = pl.multiple_of(j * tk, tk)
        kb = k_ref[0, pl.ds(k0, tk), :]
        vb = v_ref[0, pl.ds(k0, tk), :]
        z = _dot_nt(q, kb)
        valid = (cols0 + k0) < rows
        sp = _softplus(z)
        lom = jnp.where(valid, -sp, 0.0)
        hi = lom.astype(BF16)
        lo = (lom - hi.astype(F32)).astype(BF16)
        sums = _dot(jnp.concatenate([hi, lo], axis=1), u_ref[...])
        after = run + sums[:, :tk]
        w = jnp.where(valid, jnp.exp(z - sp + after), 0.0)
        acc = acc + _dot(w.astype(BF16), vb)
        return run + sums[:, tk:], acc

    init = (jnp.zeros((tq, tk), F32), jnp.zeros((tq, v_ref.shape[2]), F32))
    _, acc = lax.fori_loop(0, nkb, body, init)
    o_ref[0] = acc


def sb_attention(q, k, v, *, tq=256):
    bh, s, e = q.shape
    tk = LANE
    jj = jnp.arange(2 * tk)[:, None] % tk
    ss = jnp.arange(2 * tk)[None, :]
    u = jnp.where(ss < tk, jj > ss, True).astype(BF16)
    return pl.pallas_call(
        functools.partial(_sb_attn_kernel, tq=tq, tk=tk),
        grid=(bh, s // tq),
        in_specs=[pl.BlockSpec((1, tq, e), lambda b, i: (b, i, 0)),
                  pl.BlockSpec((1, s, e), lambda b, i: (b, 0, 0)),
                  pl.BlockSpec((1, s, e), lambda b, i: (b, 0, 0)),
                  pl.BlockSpec((2 * tk, 2 * tk), lambda b, i: (0, 0))],
        out_specs=pl.BlockSpec((1, tq, e), lambda b, i: (b, i, 0)),
        out_shape=jax.ShapeDtypeStruct((bh, s, e), F32),
        compiler_params=_cp("parallel", "arbitrary"),
        name="sb_attention",
    )(q, k, v, u)


def _softmax_attn_kernel(q_ref, k_ref, v_ref, o_ref, *, tq, tk, scale):
    qi = pl.program_id(1)
    q = q_ref[0]
    nkb = (qi * tq + tq + tk - 1) // tk
    rows = qi * tq + _iota((tq, tk), 0)
    cols0 = _iota((tq, tk), 1)

    def body(j, carry):
        m, l, acc = carry
        k0 = pl.multiple_of(j * tk, tk)
        kb = k_ref[0, pl.ds(k0, tk), :]
        vb = v_ref[0, pl.ds(k0, tk), :]
        s = _dot_nt(q, kb) * scale
        s = jnp.where((cols0 + k0) <= rows, s, -1e30)
        m_new = jnp.maximum(m, jnp.max(s, axis=1, keepdims=True))
        p = jnp.exp(s - m_new)
        alpha = jnp.exp(m - m_new)
        l = alpha * l + jnp.sum(p, axis=1, keepdims=True)
        acc = alpha * acc + _dot(p.astype(BF16), vb)
        return m_new, l, acc

    init = (jnp.full((tq, 1), -1e30, F32), jnp.zeros((tq, 1), F32),
            jnp.zeros((tq, v_ref.shape[2]), F32))
    _, l, acc = lax.fori_loop(0, nkb, body, init)
    o_ref[0] = acc / l


def softmax_attention(q, k, v, scale, *, tq=512, tk=512):
    bh, s, e = q.shape
    ev = v.shape[2]
    return pl.pallas_call(
        functools.partial(_softmax_attn_kernel, tq=tq, tk=tk, scale=scale),
        grid=(bh, s // tq),
        in_specs=[pl.BlockSpec((1, tq, e), lambda b, i: (b, i, 0)),
                  pl.BlockSpec((1, s, e), lambda b, i: (b, 0, 0)),
                  pl.BlockSpec((1, s, ev), lambda b, i: (b, 0, 0))],
        out_specs=pl.BlockSpec((1, tq, ev), lambda b, i: (b, i, 0)),
        out_shape=jax.ShapeDtypeStruct((bh, s, ev), F32),
        compiler_params=_cp("parallel", "arbitrary"),
        name="softmax_attention",
    )(q, k, v)


def _rope_kernel(q1_ref, q2_ref, k1_ref, k2_ref, pos_ref, f_ref, oq1_ref, oq2_ref, ok1_ref, ok2_ref):
    ang = pos_ref[...].astype(F32) * f_ref[...]
    cos, sin = jnp.cos(ang), jnp.sin(ang)
    reps = q1_ref.shape[1] // LANE
    cq = jnp.concatenate([cos] * reps, axis=1)
    sq = jnp.concatenate([sin] * reps, axis=1)
    q1, q2 = q1_ref[...], q2_ref[...]
    oq1_ref[...] = q1 * cq - q2 * sq
    oq2_ref[...] = q1 * sq + q2 * cq
    k1, k2 = k1_ref[...], k2_ref[...]
    ok1_ref[...] = k1 * cos - k2 * sin
    ok2_ref[...] = k1 * sin + k2 * cos


def rope(q, c, positions, *, q1_col, q2_col, k1_col, k2_col, qw, tm=512):
    t = q.shape[0]
    half = MLA_ROPE // 2
    inv_freq = ROPE_THETA ** (-jnp.arange(half, dtype=F32) / half)
    freq = jnp.tile(inv_freq, LANE // half).reshape(1, LANE)
    spec = lambda w, col: pl.BlockSpec((tm, w), lambda i: (i, col))
    return pl.pallas_call(
        _rope_kernel,
        grid=(t // tm,),
        in_specs=[spec(qw, q1_col), spec(qw, q2_col), spec(LANE, k1_col), spec(LANE, k2_col),
                  pl.BlockSpec((tm, 1), lambda i: (i, 0)), pl.BlockSpec((1, LANE), lambda i: (0, 0))],
        out_specs=[spec(qw, 0), spec(qw, 0), spec(LANE, 0), spec(LANE, 0)],
        out_shape=[jax.ShapeDtypeStruct((t, qw), F32), jax.ShapeDtypeStruct((t, qw), F32),
                   jax.ShapeDtypeStruct((t, LANE), F32), jax.ShapeDtypeStruct((t, LANE), F32)],
        compiler_params=_cp("parallel"),
        name="rope",
    )(q, q, c, c, positions.reshape(t, 1), freq)


def _rwkv_prep_kernel(x_ref, xp_ref, mu_ref, w0_ref, a0_ref, kk_ref, ka_ref, rk_ref,
                      w2_ref, a2_ref, g2_ref, tri_ref, hs_ref,
                      at_ref, bt_ref, kt_ref, rt_ref, v_ref, eg_ref, g_ref, bv_ref, *, tm, seq, width):
    i = pl.program_id(0)
    x = x_ref[...]
    rows = _iota(x.shape, 0)
    at_start = (i * tm) % seq == 0
    prev_row = jnp.where(at_start, 0.0, xp_ref[7:8, :])
    prev = jnp.where(rows == 0, prev_row, pltpu.roll(x, 1, axis=0))
    xm = x + (prev - x) * mu_ref[...]
    r = xm[:, :width]
    k = xm[:, width:2 * width]
    v = xm[:, 2 * width:3 * width]
    l1 = xm[:, 3 * width:3 * width + LANE]
    l2 = xm[:, 3 * width + LANE:]
    wpre = w0_ref[...] + _dot(jnp.tanh(l1), w2_ref[...], HI)
    a = _sigmoid(a0_ref[...] + _dot(l1, a2_ref[...], HI))
    g = _dot(_sigmoid(l2), g2_ref[...], HI)
    lw = -jnp.exp(-_softplus(-wpre) - 0.5)
    hs = hs_ref[...]
    kk = k * kk_ref[...]
    kk = kk * lax.rsqrt(jnp.maximum(_dot(kk * kk, hs, HI), 1e-24))
    km = k * (1.0 + (a - 1.0) * ka_ref[...])
    bonus = _dot(r * km * rk_ref[...], hs, HI)
    cum = _dot(tri_ref[...], lw, HI)
    eg = jnp.exp(cum)
    ieg = jnp.exp(-cum)
    at_ref[...] = -kk * jnp.exp(cum - lw)
    bt_ref[...] = kk * a * ieg
    kt_ref[...] = km * ieg
    rt_ref[...] = r * eg
    v_ref[...] = v
    eg_ref[...] = eg
    g_ref[...] = g
    bv_ref[...] = bonus * v


def rwkv_prep(x, mu, w0, a0, k_k, k_a, r_k, w2p, a2p, g2p, *, seq, width, tm=256):
    t, wx = x.shape
    ch = RW_CHUNK
    ti = jnp.arange(tm)
    tri = ((ti[:, None] // ch == ti[None, :] // ch) & (ti[None, :] <= ti[:, None])).astype(F32)
    ci = jnp.arange(width) // HEAD_DIM
    hs = (ci[:, None] == ci[None, :]).astype(F32)
    hb = tm // 8
    row = lambda i: (i, 0)
    fix = lambda i: (0, 0)
    vec = pl.BlockSpec((1, width), fix)
    out = jax.ShapeDtypeStruct((t, width), F32)
    return pl.pallas_call(
        functools.partial(_rwkv_prep_kernel, tm=tm, seq=seq, width=width),
        grid=(t // tm,),
        in_specs=[pl.BlockSpec((tm, wx), row),
                  pl.BlockSpec((8, wx), lambda i: (jnp.maximum(i * hb - 1, 0), 0)),
                  pl.BlockSpec((1, wx), fix), vec, vec, vec, vec, vec,
                  pl.BlockSpec(w2p.shape, fix), pl.BlockSpec(a2p.shape, fix),
                  pl.BlockSpec(g2p.shape, fix), pl.BlockSpec((tm, tm), fix),
                  pl.BlockSpec((width, width), fix)],
        out_specs=[pl.BlockSpec((tm, width), row)] * 8,
        out_shape=[out] * 8,
        compiler_params=_cp("parallel"),
        name="rwkv_prep",
    )(x, x, mu, w0, a0, k_k, k_a, r_k, w2p, a2p, g2p, tri, hs)


def _rwkv_chunk_kernel(at_ref, bt_ref, kt_ref, rt_ref, v_ref, eg_ref,
                       m_ref, n_ref, rq_ref, yl_ref, *, nchunk):
    ch, n = RW_CHUNK, HEAD_DIM
    ti = _iota((ch, ch), 0)
    si = _iota((ch, ch), 1)
    strict = si < ti
    incl = si <= ti
    same = (ti // RW_SUB) == (si // RW_SUB)
    eye = _iota((n, n), 0) == _iota((n, n), 1)

    def body(c, _):
        r0 = pl.multiple_of(c * ch, ch)
        sl = pl.ds(r0, ch)
        a_, b_, k_, r_, v_ = at_ref[0, sl, :], bt_ref[0, sl, :], kt_ref[0, sl, :], rt_ref[0, sl, :], v_ref[0, sl, :]
        g_end = eg_ref[0, pl.ds(r0 + ch - 1, 1), :]
        ar = jnp.concatenate([a_, r_], axis=0)
        pb = _dot_nt(ar, b_, HI)
        pk = _dot_nt(ar, k_, HI)
        a_ab = jnp.where(strict, pb[:ch], 0.0)
        a_ak = jnp.where(strict, pk[:ch], 0.0)
        a_rb = jnp.where(incl, pb[ch:], 0.0)
        a_rk = jnp.where(incl, pk[ch:], 0.0)
        d1 = jnp.where(same, a_ab, 0.0)
        lb = a_ab - d1
        d2 = _dot(d1, d1, HI)
        d4 = _dot(d2, d2, HI)
        d8 = _dot(d4, d4, HI)
        x = jnp.concatenate([lb, a_, _dot(a_ak, v_, HI)], axis=1)
        x = x + _dot(d1, x, HI)
        x = x + _dot(d2, x, HI)
        x = x + _dot(d4, x, HI)
        x = x + _dot(d8, x, HI)
        e1 = x[:, :ch]
        wu = x[:, ch:]
        e2 = _dot(e1, e1, HI)
        wu = wu + _dot(e1, wu, HI)
        wu = wu + _dot(e2, wu, HI)
        ry = _dot(a_rb, wu, HI)
        rq_ref[0, sl, :] = r_ + ry[:, :n]
        yl_ref[0, sl, :] = ry[:, n:] + _dot(a_rk, v_, HI)
        mn = _dot_tn(b_ * g_end, wu, HI)
        m_ref[0, c] = mn[:, :n] + jnp.where(eye, g_end, 0.0)
        n_ref[0, c] = mn[:, n:] + _dot_tn(k_ * g_end, v_, HI)
        return 0

    lax.fori_loop(0, nchunk, body, 0)


def rwkv_chunk(at, bt, kt, rt, v, eg, *, nchunk=8):
    bh, s, n = at.shape
    ch = RW_CHUNK
    rows = nchunk * ch
    spec = pl.BlockSpec((1, rows, n), lambda b, i: (b, i, 0))
    mspec = pl.BlockSpec((1, nchunk, n, n), lambda b, i: (b, i, 0, 0))
    return pl.pallas_call(
        functools.partial(_rwkv_chunk_kernel, nchunk=nchunk),
        grid=(bh, s // rows),
        in_specs=[spec] * 6,
        out_specs=[mspec, mspec, spec, spec],
        out_shape=[jax.ShapeDtypeStruct((bh, s // ch, n, n), F32)] * 2
        + [jax.ShapeDtypeStruct((bh, s, n), F32)] * 2,
        compiler_params=_cp("parallel", "parallel"),
        name="rwkv_chunk",
    )(at, bt, kt, rt, v, eg)


def _rwkv_scan_kernel(m_ref, n_ref, s_ref, st_ref, *, heads, nchunk):
    @pl.when(pl.program_id(1) == 0)
    def _():
        st_ref[...] = jnp.zeros_like(st_ref)

    def body(c, _):
        for hh in range(heads):
            st = st_ref[hh]
            s_ref[hh, c] = st
            st_ref[hh] = _dot(m_ref[hh, c], st, HI) + n_ref[hh, c]
        return 0

    lax.fori_loop(0, nchunk, body, 0)


def rwkv_scan(m, nn, *, heads=4, nchunk=32):
    bh, nc, n, _ = m.shape
    nchunk = min(nchunk, nc)
    spec = pl.BlockSpec((heads, nchunk, n, n), lambda b, i: (b, i, 0, 0))
    return pl.pallas_call(
        functools.partial(_rwkv_scan_kernel, heads=heads, nchunk=nchunk),
        grid=(bh // heads, nc // nchunk),
        in_specs=[spec, spec],
        out_specs=spec,
        out_shape=jax.ShapeDtypeStruct((bh, nc, n, n), F32),
        scratch_shapes=[pltpu.VMEM((heads, n, n), F32)],
        compiler_params=_cp("parallel", "arbitrary"),
        name="rwkv_scan",
    )(m, nn)


def _rwkv_out_kernel(rq_ref, yl_ref, s_ref, lw_ref, lb_ref, o_ref, *, nchunk):
    ch, n = RW_CHUNK, HEAD_DIM
    avg = jnp.full((n, n), 1.0 / n, F32)
    lw, lb = lw_ref[0], lb_ref[0]

    def body(c, _):
        sl = pl.ds(pl.multiple_of(c * ch, ch), ch)
        y = _dot(rq_ref[0, sl, :], s_ref[0, c], HI) + yl_ref[0, sl, :]
        d = y - _dot(y, avg, HI)
        var = _dot(d * d, avg, HI)
        o_ref[0, sl, :] = d * lax.rsqrt(var + GN_EPS) * lw + lb
        return 0

    lax.fori_loop(0, nchunk, body, 0)


def rwkv_out(rq, yl, s0, ln_w, ln_b, *, heads, nchunk=8):
    bh, s, n = rq.shape
    rows = nchunk * RW_CHUNK
    spec = pl.BlockSpec((1, rows, n), lambda b, i: (b, i, 0))
    hspec = pl.BlockSpec((1, 1, n), lambda b, i: (b % heads, 0, 0))
    return pl.pallas_call(
        functools.partial(_rwkv_out_kernel, nchunk=nchunk),
        grid=(bh, s // rows),
        in_specs=[spec, spec, pl.BlockSpec((1, nchunk, n, n), lambda b, i: (b, i, 0, 0)),
                  hspec, hspec],
        out_specs=spec,
        out_shape=jax.ShapeDtypeStruct((bh, s, n), F32),
        compiler_params=_cp("parallel", "parallel"),
        name="rwkv_out",
    )(rq, yl, s0, ln_w.reshape(heads, 1, n), ln_b.reshape(heads, 1, n))


def _hyb_out_kernel(oa_ref, y_ref, bv_ref, g_ref, wa_ref, wb_ref, r_ref, o_ref):
    ob = (y_ref[...] + bv_ref[...]) * g_ref[...]
    o_ref[...] = (r_ref[...] + _dot(oa_ref[...].astype(BF16), wa_ref[...])
                  + _dot(ob.astype(BF16), wb_ref[...]))


def hyb_out(res, oa, y, bv, g, wa, wb, *, tm=512):
    t, d = res.shape
    row = lambda i: (i, 0)
    fix = lambda i: (0, 0)
    act = lambda a: pl.BlockSpec((tm, a.shape[1]), row)
    return pl.pallas_call(
        _hyb_out_kernel,
        grid=(t // tm,),
        in_specs=[act(oa), act(y), act(bv), act(g), pl.BlockSpec(wa.shape, fix),
                  pl.BlockSpec(wb.shape, fix), pl.BlockSpec((tm, d), row)],
        out_specs=pl.BlockSpec((tm, d), row),
        out_shape=jax.ShapeDtypeStruct((t, d), F32),
        compiler_params=_cp("parallel"),
        name="hyb_out",
    )(oa, y, bv, g, wa, wb, res)


def _to_heads(x, b, s, heads):
    e = x.shape[1] // heads
    return x.reshape(b, s, heads, e).transpose(0, 2, 1, 3).reshape(b * heads, s, e)


def _from_heads(x, b, s, heads):
    e = x.shape[2]
    return x.reshape(b, heads, s, e).transpose(0, 2, 1, 3).reshape(b * s, heads * e)


def _pad_cols(w, n):
    return jnp.pad(w, ((0, 0), (0, n - w.shape[1])))


def sb_rwkv_layer(h, b, s, attn_norm, w_in, w_out, mu, w0, w2, a0, a2, g2, k_k, k_a, r_k, ln_w, ln_b):
    sbw = SB_HEADS * HEAD_DIM
    rww = RW_HEADS * HEAD_DIM
    rw_in = w_in.shape[1] - 3 * sbw
    rw_pad = -(-rw_in // LANE) * LANE
    dl, al = w2.shape[0], a2.shape[0]
    gl = g2.shape[0]
    assert dl + al == LANE and 3 * rww % LANE == 0

    qkv = rms_matmul(h, attn_norm, w_in[:, :3 * sbw].astype(BF16), out_dtype=BF16)
    rw = rms_matmul(h, attn_norm, _pad_cols(w_in[:, 3 * sbw:], rw_pad).astype(BF16), tn=rw_pad // 3)

    q, k, v = (_to_heads(qkv[:, j * sbw:(j + 1) * sbw], b, s, SB_HEADS) for j in range(3))
    q = q * (1.0 / math.sqrt(HEAD_DIM))
    oa = _from_heads(sb_attention(q, k, v), b, s, SB_HEADS)

    w2p = jnp.zeros((LANE, rww), F32).at[:dl].set(w2)
    a2p = jnp.zeros((LANE, rww), F32).at[dl:].set(a2)
    g2p = jnp.zeros((rw_pad - 3 * rww - LANE, rww), F32).at[:gl].set(g2)
    vec = lambda t: t.reshape(1, rww).astype(F32)
    mu_p = _pad_cols(mu.reshape(1, rw_in), rw_pad)
    at, bt, kt, rt, vv, eg, g, bv = rwkv_prep(rw, mu_p, vec(w0), vec(a0), vec(k_k), vec(k_a), vec(r_k),
                                              w2p, a2p, g2p, seq=s, width=rww)
    th = lambda t: _to_heads(t, b, s, RW_HEADS)
    m, nn, rq, yl = rwkv_chunk(th(at), th(bt), th(kt), th(rt), th(vv), th(eg))
    s0 = rwkv_scan(m, nn)
    y = _from_heads(rwkv_out(rq, yl, s0, ln_w, ln_b, heads=RW_HEADS), b, s, RW_HEADS)
    return hyb_out(h, oa, y, bv, g, w_out[:sbw].astype(BF16), w_out[sbw:].astype(BF16))


def mla_layer(h, b, s, positions, attn_norm, w_down, q_norm, kv_norm, w_uq, w_ukv, w_o):
    d = h.shape[1]
    nh, dn, dr, dv = MLA_HEADS, MLA_NOPE, MLA_ROPE, MLA_V
    half = dr // 2
    qr, kvr = MLA_Q_RANK, MLA_KV_RANK
    z = jnp.zeros((d, LANE - half), F32)
    wd = jnp.concatenate([w_down[:, :qr], w_down[:, qr + kvr:qr + kvr + half], z,
                          w_down[:, qr:qr + kvr], w_down[:, qr + kvr + half:], z], axis=1)
    c = rms_matmul(h, attn_norm, wd.astype(BF16), tn=wd.shape[1])
    wq = w_uq.reshape(qr, nh, dn + dr)
    wq = jnp.concatenate([wq[:, :, :dn].reshape(qr, nh * dn), wq[:, :, dn:dn + half].reshape(qr, nh * half),
                          wq[:, :, dn + half:].reshape(qr, nh * half)], axis=1)
    q = rms_matmul(c, q_norm, wq.astype(BF16), x_col=0)
    wkv = w_ukv.reshape(kvr, nh, dn + dv)
    wkv = jnp.concatenate([wkv[:, :, :dn].reshape(kvr, nh * dn), wkv[:, :, dn:].reshape(kvr, nh * dv)], axis=1)
    kv = rms_matmul(c, kv_norm, wkv.astype(BF16), x_col=(qr + LANE) // kvr, out_dtype=BF16)

    qw = nh * half
    q1, q2, k1, k2 = rope(q, c, positions, q1_col=nh * dn // qw, q2_col=nh * dn // qw + 1,
                          k1_col=qr // LANE, k2_col=(qr + LANE + kvr) // LANE, qw=qw)
    qh = jnp.concatenate([q[:, :nh * dn].reshape(b, s, nh, dn), q1.reshape(b, s, nh, half),
                          q2.reshape(b, s, nh, half)], axis=-1)
    qh = qh.transpose(0, 2, 1, 3).reshape(b * nh, s, dn + dr).astype(BF16)
    kr = jnp.concatenate([k1[:, :half], k2[:, :half]], axis=-1).reshape(b, s, 1, dr).astype(BF16)
    kh = jnp.concatenate([kv[:, :nh * dn].reshape(b, s, nh, dn),
                          jnp.broadcast_to(kr, (b, s, nh, dr))], axis=-1)
    kh = kh.transpose(0, 2, 1, 3).reshape(b * nh, s, dn + dr)
    vh = _to_heads(kv[:, nh * dn:], b, s, nh)
    o = softmax_attention(qh, kh, vh, 1.0 / math.sqrt(dn + dr))
    return matmul_residual(h, _from_heads(o, b, s, nh), w_o.astype(BF16))


def kernel(x, p, positions, attn_norm, ffn_norm, ffn_w_in, ffn_conv_w, ffn_conv_b, ffn_w_out, ple_w_proj, ple_norm, ple_gate_norm, ple_w_gate, hyb_w_in, hyb_w_out, rw_mu, rw_w0, rw_w2, rw_a0, rw_a2, rw_g2, rw_k_k, rw_k_a, rw_r_k, rw_ln_w, rw_ln_b, mla_w_down, mla_q_norm, mla_kv_norm, mla_w_uq, mla_w_ukv, mla_w_o, final_norm):
    b, s, d = x.shape
    depth = p.shape[0]
    h = x.reshape(b * s, d)
    pos = positions.reshape(b * s)
    for i in range(depth):
        j = i // 2
        if i % 2 == 0:
            h = sb_rwkv_layer(h, b, s, attn_norm[i], hyb_w_in[j], hyb_w_out[j], rw_mu[j], rw_w0[j],
                              rw_w2[j], rw_a0[j], rw_a2[j], rw_g2[j], rw_k_k[j], rw_k_a[j], rw_r_k[j],
                              rw_ln_w[j], rw_ln_b[j])
        else:
            h = mla_layer(h, b, s, pos, attn_norm[i], mla_w_down[j], mla_q_norm[j], mla_kv_norm[j],
                          mla_w_uq[j], mla_w_ukv[j], mla_w_o[j])
        ug = rms_matmul(h, ffn_norm[i], ffn_w_in[i].astype(BF16))
        h = convglu_out(h, ug, ffn_conv_w[i], ffn_conv_b[i], ffn_w_out[i].astype(BF16), seq=s)
        h = ple(h, p[i].reshape(b * s, -1), ple_w_proj[i].astype(BF16), ple_norm[i], ple_gate_norm[i],
                ple_w_gate[i].astype(BF16), final_norm, final=(i == depth - 1))
    return h.reshape(b, s, d)
```

```python
import functools
import math

import jax
import jax.numpy as jnp
from jax import lax
from jax.experimental import pallas as pl
from jax.experimental.pallas import tpu as pltpu

F32 = jnp.float32
BF16 = jnp.bfloat16
HI = lax.Precision.HIGHEST

NORM_EPS = 1e-6
GN_EPS = 64e-5
ROPE_THETA = 10000.0
LANE = 128
VMEM_LIMIT = 48 * 1024 * 1024

SB_HEADS = 8
RW_HEADS = 8
HEAD_DIM = 64
MLA_HEADS = 16
MLA_NOPE = 64
MLA_ROPE = 32
MLA_V = 64
MLA_Q_RANK = 384
MLA_KV_RANK = 256
SB_CUT = -104.0
RW_CHUNK = 64
RW_SUB = 16


def _cp(*sem):
    return pltpu.CompilerParams(dimension_semantics=sem, vmem_limit_bytes=VMEM_LIMIT)


def _iota(shape, dim):
    return lax.broadcasted_iota(jnp.int32, shape, dim)


def _softplus(z):
    return jnp.maximum(z, 0.0) + jnp.log1p(jnp.exp(-jnp.abs(z)))


def _sigmoid(z):
    return 1.0 / (1.0 + jnp.exp(-z))


def _dot(a, b, precision=None):
    return jnp.dot(a, b, preferred_element_type=F32, precision=precision)


def _dot_nt(a, b, precision=None):
    return lax.dot_general(a, b, (((1,), (1,)), ((), ())), preferred_element_type=F32,
                           precision=precision)


def _dot_tn(a, b, precision=None):
    return lax.dot_general(a, b, (((0,), (0,)), ((), ())), preferred_element_type=F32,
                           precision=precision)


def _rms(x, g, eps=NORM_EPS):
    return x * lax.rsqrt(jnp.mean(x * x, axis=-1, keepdims=True) + eps) * g


def _rms_matmul_kernel(x_ref, g_ref, w_ref, o_ref, xn_ref):
    @pl.when(pl.program_id(1) == 0)
    def _():
        xn_ref[...] = _rms(x_ref[...].astype(F32), g_ref[...]).astype(BF16)

    o_ref[...] = _dot(xn_ref[...], w_ref[...]).astype(o_ref.dtype)


def rms_matmul(x, g, w, *, x_col=0, tm=512, tn=512, out_dtype=F32):
    t = x.shape[0]
    k, n = w.shape
    tn = min(tn, n)
    return pl.pallas_call(
        _rms_matmul_kernel,
        grid=(t // tm, n // tn),
        in_specs=[pl.BlockSpec((tm, k), lambda i, j: (i, x_col)),
                  pl.BlockSpec((1, k), lambda i, j: (0, 0)),
                  pl.BlockSpec((k, tn), lambda i, j: (0, j))],
        out_specs=pl.BlockSpec((tm, tn), lambda i, j: (i, j)),
        out_shape=jax.ShapeDtypeStruct((t, n), out_dtype),
        scratch_shapes=[pltpu.VMEM((tm, k), BF16)],
        compiler_params=_cp("parallel", "arbitrary"),
        name="rms_matmul",
    )(x, g.reshape(1, k).astype(F32), w)


def _matmul_res_kernel(a_ref, w_ref, r_ref, o_ref):
    o_ref[...] = r_ref[...] + _dot(a_ref[...].astype(BF16), w_ref[...])


def matmul_residual(res, a, w, *, tm=512, tn=512):
    t, k = a.shape
    n = w.shape[1]
    return pl.pallas_call(
        _matmul_res_kernel,
        grid=(t // tm, n // tn),
        in_specs=[pl.BlockSpec((tm, k), lambda i, j: (i, 0)),
                  pl.BlockSpec((k, tn), lambda i, j: (0, j)),
                  pl.BlockSpec((tm, tn), lambda i, j: (i, j))],
        out_specs=pl.BlockSpec((tm, tn), lambda i, j: (i, j)),
        out_shape=jax.ShapeDtypeStruct((t, n), F32),
        compiler_params=_cp("parallel", "arbitrary"),
        name="matmul_residual",
    )(a, w, res)


def _convglu_out_kernel(u_ref, g_ref, gp_ref, cw_ref, cb_ref, w_ref, r_ref, o_ref, *, tm, seq):
    i = pl.program_id(0)
    g = g_ref[...]
    rows = _iota(g.shape, 0)
    at_start = (i * tm) % seq == 0
    halo = jnp.where(at_start, 0.0, gp_ref[...])
    p1 = jnp.where(rows == 0, halo[7:8, :], pltpu.roll(g, 1, axis=0))
    p2 = jnp.where(rows == 0, halo[6:7, :], jnp.where(rows == 1, halo[7:8, :],
                                                      pltpu.roll(g, 2, axis=0)))
    c = cw_ref[0:1, :] * p2 + cw_ref[1:2, :] * p1 + cw_ref[2:3, :] * g + cb_ref[...]
    act = 0.5 * c * (1.0 + lax.erf(c * (1.0 / math.sqrt(2.0)))) * u_ref[...]
    o_ref[...] = r_ref[...] + _dot(act.astype(BF16), w_ref[...])


def convglu_out(res, ug, conv_w, conv_b, w_out, *, seq, tm=256):
    t = res.shape[0]
    f, n = w_out.shape
    hb = tm // 8
    return pl.pallas_call(
        functools.partial(_convglu_out_kernel, tm=tm, seq=seq),
        grid=(t // tm,),
        in_specs=[pl.BlockSpec((tm, f), lambda i: (i, 0)),
                  pl.BlockSpec((tm, f), lambda i: (i, 1)),
                  pl.BlockSpec((8, f), lambda i: (jnp.maximum(i * hb - 1, 0), 1)),
                  pl.BlockSpec((3, f), lambda i: (0, 0)),
                  pl.BlockSpec((1, f), lambda i: (0, 0)),
                  pl.BlockSpec((f, n), lambda i: (0, 0)),
                  pl.BlockSpec((tm, n), lambda i: (i, 0))],
        out_specs=pl.BlockSpec((tm, n), lambda i: (i, 0)),
        out_shape=jax.ShapeDtypeStruct((t, n), F32),
        compiler_params=_cp("parallel"),
        name="convglu_out",
    )(ug, ug, ug, conv_w, conv_b.reshape(1, f), w_out, res)


def _ple_kernel(h_ref, p_ref, wp_ref, pn_ref, gn_ref, wg_ref, fn_ref, o_ref, *, final):
    h = h_ref[...]
    e = _rms(_dot(p_ref[...].astype(BF16), wp_ref[...]), pn_ref[...])
    gate = _sigmoid(_dot(_rms(h, gn_ref[...]).astype(BF16), wg_ref[...]))
    out = h + gate * e
    if final:
        out = _rms(out, fn_ref[...])
    o_ref[...] = out


def ple(h, p, w_proj, p_norm, g_norm, w_gate, f_norm, *, final, tm=512):
    t, d = h.shape
    pd = p.shape[1]
    row = lambda i: (i, 0)
    fix = lambda i: (0, 0)
    return pl.pallas_call(
        functools.partial(_ple_kernel, final=final),
        grid=(t // tm,),
        in_specs=[pl.BlockSpec((tm, d), row), pl.BlockSpec((tm, pd), row),
                  pl.BlockSpec((pd, d), fix), pl.BlockSpec((1, d), fix), pl.BlockSpec((1, d), fix),
                  pl.BlockSpec((d, d), fix), pl.BlockSpec((1, d), fix)],
        out_specs=pl.BlockSpec((tm, d), row),
        out_shape=jax.ShapeDtypeStruct((t, d), F32),
        compiler_params=_cp("parallel"),
        name="ple",
    )(h, p, w_proj, p_norm.reshape(1, d), g_norm.reshape(1, d), w_gate, f_norm.reshape(1, d))


def _sb_attn_kernel(q_ref, k_ref, v_ref, u_ref, o_ref, *, tq, tk):
    qi = pl.program_id(1)
    q = q_ref[0]
    nblk = tq // tk
    q0 = qi * tq

    def sweep(k_lo, run, acc, masked):
        parts = []
        for d in reversed(range(nblk)):
            k0 = k_lo + d * tk
            kb = k_ref[0, pl.ds(k0, tk), :]
            z = _dot_nt(q, kb)
            sp = _softplus(z)
            lom = -sp
            valid = None
            if masked:
                valid = _iota((tq, tk), 1) + d * tk < _iota((tq, tk), 0)
                lom = jnp.where(valid, lom, 0.0)
            hi = lom.astype(BF16)
            lo = (lom - hi.astype(F32)).astype(BF16)
            sums = _dot(jnp.concatenate([hi, lo], axis=1), u_ref[...])
            parts.append((k0, z - sp, sums, valid))
        for k0, logsig, sums, valid in parts:
            w = jnp.exp(logsig + run + sums[:, :tk])
            if masked:
                w = jnp.where(valid, w, 0.0)
            acc = acc + _dot(w.astype(BF16), v_ref[0, pl.ds(k0, tk), :])
            run = run + sums[:, tk:]
        return run, acc

    run = jnp.zeros((tq, tk), F32)
    acc = jnp.zeros((tq, v_ref.shape[2]), F32)
    run, acc = sweep(pl.multiple_of(q0, tq), run, acc, True)

    def cond(c):
        return c[0]

    def body(c):
        _, j, run, acc = c
        run, acc = sweep(pl.multiple_of(j * tq, tq), run, acc, False)
        return (j > 0) & (jnp.max(run) > SB_CUT), j - 1, run, acc

    go = (qi > 0) & (jnp.max(run) > SB_CUT)
    _, _, _, acc = lax.while_loop(cond, body, (go, qi - 1, run, acc))
    o_ref[0] = acc


def sb_attention(q, k, v, *, tq=256):
    bh, s, e = q.shape
    tk = LANE
    jj = jnp.arange(2 * tk)[:, None] % tk
    ss = jnp.arange(2 * tk)[None, :]
    u = jnp.where(ss < tk, jj > ss, True).astype(BF16)
    return pl.pallas_call(
        functools.partial(_sb_attn_kernel, tq=tq, tk=tk),
        grid=(bh, s // tq),
        in_specs=[pl.BlockSpec((1, tq, e), lambda b, i: (b, i, 0)),
                  pl.BlockSpec((1, s, e), lambda b, i: (b, 0, 0)),
                  pl.BlockSpec((1, s, e), lambda b, i: (b, 0, 0)),
                  pl.BlockSpec((2 * tk, 2 * tk), lambda b, i: (0, 0))],
        out_specs=pl.BlockSpec((1, tq, e), lambda b, i: (b, i, 0)),
        out_shape=jax.ShapeDtypeStruct((bh, s, e), F32),
        compiler_params=_cp("parallel", "arbitrary"),
        name="sb_attention",
    )(q, k, v, u)


def _softmax_attn_kernel(q_ref, k_ref, v_ref, o_ref, *, tq, tk, scale):
    qi = pl.program_id(1)
    q = (q_ref[0] * (scale * math.log2(math.e))).astype(BF16)

    def step(k0, carry, masked):
        m, l, acc = carry
        kb = k_ref[0, pl.ds(k0, tk), :]
        vb = v_ref[0, pl.ds(k0, tk), :]
        s = _dot_nt(q, kb)
        if masked:
            s = jnp.where(_iota((tq, tk), 1) <= _iota((tq, tk), 0), s, -1e30)
        m_new = jnp.maximum(m, jnp.max(s, axis=1, keepdims=True))
        p = jnp.exp2(s - m_new)
        alpha = jnp.exp2(m - m_new)
        l = alpha * l + jnp.sum(p, axis=1, keepdims=True)
        acc = alpha * acc + _dot(p.astype(BF16), vb)
        return m_new, l, acc

    init = (jnp.full((tq, 1), -1e30, F32), jnp.zeros((tq, 1), F32),
            jnp.zeros((tq, v_ref.shape[2]), F32))
    carry = lax.fori_loop(0, qi, lambda j, c: step(pl.multiple_of(j * tk, tk), c, False), init)
    _, l, acc = step(pl.multiple_of(qi * tq, tq), carry, True)
    o_ref[0] = acc / l


def softmax_attention(q, k, v, scale, *, tq=512):
    tk = tq
    bh, s, e = q.shape
    ev = v.shape[2]
    return pl.pallas_call(
        functools.partial(_softmax_attn_kernel, tq=tq, tk=tk, scale=scale),
        grid=(bh, s // tq),
        in_specs=[pl.BlockSpec((1, tq, e), lambda b, i: (b, i, 0)),
                  pl.BlockSpec((1, s, e), lambda b, i: (b, 0, 0)),
                  pl.BlockSpec((1, s, ev), lambda b, i: (b, 0, 0))],
        out_specs=pl.BlockSpec((1, tq, ev), lambda b, i: (b, i, 0)),
        out_shape=jax.ShapeDtypeStruct((bh, s, ev), F32),
        compiler_params=_cp("parallel", "arbitrary"),
        name="softmax_attention",
    )(q, k, v)


def _rope_kernel(q1_ref, q2_ref, k1_ref, k2_ref, pos_ref, f_ref, oq1_ref, oq2_ref, ok1_ref, ok2_ref):
    ang = pos_ref[...].astype(F32) * f_ref[...]
    cos, sin = jnp.cos(ang), jnp.sin(ang)
    reps = q1_ref.shape[1] // LANE
    cq = jnp.concatenate([cos] * reps, axis=1)
    sq = jnp.concatenate([sin] * reps, axis=1)
    q1, q2 = q1_ref[...], q2_ref[...]
    oq1_ref[...] = q1 * cq - q2 * sq
    oq2_ref[...] = q1 * sq + q2 * cq
    k1, k2 = k1_ref[...], k2_ref[...]
    ok1_ref[...] = k1 * cos - k2 * sin
    ok2_ref[...] = k1 * sin + k2 * cos


def rope(q, c, positions, *, q1_col, q2_col, k1_col, k2_col, qw, tm=512):
    t = q.shape[0]
    half = MLA_ROPE // 2
    inv_freq = ROPE_THETA ** (-jnp.arange(half, dtype=F32) / half)
    freq = jnp.tile(inv_freq, LANE // half).reshape(1, LANE)
    spec = lambda w, col: pl.BlockSpec((tm, w), lambda i: (i, col))
    return pl.pallas_call(
        _rope_kernel,
        grid=(t // tm,),
        in_specs=[spec(qw, q1_col), spec(qw, q2_col), spec(LANE, k1_col), spec(LANE, k2_col),
                  pl.BlockSpec((tm, 1), lambda i: (i, 0)), pl.BlockSpec((1, LANE), lambda i: (0, 0))],
        out_specs=[spec(qw, 0), spec(qw, 0), spec(LANE, 0), spec(LANE, 0)],
        out_shape=[jax.ShapeDtypeStruct((t, qw), F32), jax.ShapeDtypeStruct((t, qw), F32),
                   jax.ShapeDtypeStruct((t, LANE), F32), jax.ShapeDtypeStruct((t, LANE), F32)],
        compiler_params=_cp("parallel"),
        name="rope",
    )(q, q, c, c, positions.reshape(t, 1), freq)


def _rwkv_prep_kernel(x_ref, xp_ref, mu_ref, w0_ref, a0_ref, kk_ref, ka_ref, rk_ref,
                      w2_ref, a2_ref, g2_ref, tri_ref, hs_ref,
                      at_ref, bt_ref, kt_ref, rt_ref, v_ref, eg_ref, g_ref, bv_ref, *, tm, seq, width):
    i = pl.program_id(0)
    x = x_ref[...]
    rows = _iota(x.shape, 0)
    at_start = (i * tm) % seq == 0
    prev_row = jnp.where(at_start, 0.0, xp_ref[7:8, :])
    prev = jnp.where(rows == 0, prev_row, pltpu.roll(x, 1, axis=0))
    xm = x + (prev - x) * mu_ref[...]
    r = xm[:, :width]
    k = xm[:, width:2 * width]
    v = xm[:, 2 * width:3 * width]
    l1 = xm[:, 3 * width:3 * width + LANE]
    l2 = xm[:, 3 * width + LANE:]
    wpre = w0_ref[...] + _dot(jnp.tanh(l1), w2_ref[...], HI)
    a = _sigmoid(a0_ref[...] + _dot(l1, a2_ref[...], HI))
    g = _dot(_sigmoid(l2), g2_ref[...], HI)
    lw = -jnp.exp(-_softplus(-wpre) - 0.5)
    hs = hs_ref[...]
    kk = k * kk_ref[...]
    kk = kk * lax.rsqrt(jnp.maximum(_dot(kk * kk, hs, HI), 1e-24))
    km = k * (1.0 + (a - 1.0) * ka_ref[...])
    bonus = _dot(r * km * rk_ref[...], hs, HI)
    cum = _dot(tri_ref[...], lw, HI)
    eg = jnp.exp(cum)
    ieg = jnp.exp(-cum)
    at_ref[...] = -kk * jnp.exp(cum - lw)
    bt_ref[...] = kk * a * ieg
    kt_ref[...] = km * ieg
    rt_ref[...] = r * eg
    v_ref[...] = v
    eg_ref[...] = eg
    g_ref[...] = g
    bv_ref[...] = bonus * v


def rwkv_prep(x, mu, w0, a0, k_k, k_a, r_k, w2p, a2p, g2p, *, seq, width, tm=256):
    t, wx = x.shape
    ch = RW_CHUNK
    ti = jnp.arange(tm)
    tri = ((ti[:, None] // ch == ti[None, :] // ch) & (ti[None, :] <= ti[:, None])).astype(F32)
    ci = jnp.arange(width) // HEAD_DIM
    hs = (ci[:, None] == ci[None, :]).astype(F32)
    hb = tm // 8
    row = lambda i: (i, 0)
    fix = lambda i: (0, 0)
    vec = pl.BlockSpec((1, width), fix)
    out = jax.ShapeDtypeStruct((t, width), F32)
    return pl.pallas_call(
        functools.partial(_rwkv_prep_kernel, tm=tm, seq=seq, width=width),
        grid=(t // tm,),
        in_specs=[pl.BlockSpec((tm, wx), row),
                  pl.BlockSpec((8, wx), lambda i: (jnp.maximum(i * hb - 1, 0), 0)),
                  pl.BlockSpec((1, wx), fix), vec, vec, vec, vec, vec,
                  pl.BlockSpec(w2p.shape, fix), pl.BlockSpec(a2p.shape, fix),
                  pl.BlockSpec(g2p.shape, fix), pl.BlockSpec((tm, tm), fix),
                  pl.BlockSpec((width, width), fix)],
        out_specs=[pl.BlockSpec((tm, width), row)] * 8,
        out_shape=[out] * 8,
        compiler_params=_cp("parallel"),
        name="rwkv_prep",
    )(x, x, mu, w0, a0, k_k, k_a, r_k, w2p, a2p, g2p, tri, hs)


def _rwkv_chunk_kernel(at_ref, bt_ref, kt_ref, rt_ref, v_ref, eg_ref,
                       m_ref, n_ref, rq_ref, yl_ref, *, nchunk, unroll):
    ch, n = RW_CHUNK, HEAD_DIM
    ti = _iota((ch, ch), 0)
    si = _iota((ch, ch), 1)
    strict = si < ti
    incl = si <= ti
    same = (ti // RW_SUB) == (si // RW_SUB)
    eye = _iota((n, n), 0) == _iota((n, n), 1)

    def body(it, _):
        cs = [it * unroll + u for u in range(unroll)]
        sls = [pl.ds(pl.multiple_of(c * ch, ch), ch) for c in cs]
        each = lambda f, *ls: [f(*t) for t in zip(*ls)]
        mm = lambda x, y: _dot(x, y, HI)
        a_ = [at_ref[0, sl, :] for sl in sls]
        b_ = [bt_ref[0, sl, :] for sl in sls]
        k_ = [kt_ref[0, sl, :] for sl in sls]
        r_ = [rt_ref[0, sl, :] for sl in sls]
        v_ = [v_ref[0, sl, :] for sl in sls]
        g_end = [eg_ref[0, pl.ds(c * ch + ch - 1, 1), :] for c in cs]
        ar = each(lambda a, r: jnp.concatenate([a, r], axis=0), a_, r_)
        pb = each(lambda x, y: _dot_nt(x, y, HI), ar, b_)
        pk = each(lambda x, y: _dot_nt(x, y, HI), ar, k_)
        a_ab = [jnp.where(strict, t[:ch], 0.0) for t in pb]
        a_ak = [jnp.where(strict, t[:ch], 0.0) for t in pk]
        a_rb = [jnp.where(incl, t[ch:], 0.0) for t in pb]
        a_rk = [jnp.where(incl, t[ch:], 0.0) for t in pk]
        d1 = [jnp.where(same, t, 0.0) for t in a_ab]
        lb = each(lambda t, d: t - d, a_ab, d1)
        d2 = each(mm, d1, d1)
        akv = each(mm, a_ak, v_)
        x = each(lambda l, a, t: jnp.concatenate([l, a, t], axis=1), lb, a_, akv)
        d4 = each(mm, d2, d2)
        x = each(lambda d, t: t + mm(d, t), d1, x)
        d8 = each(mm, d4, d4)
        x = each(lambda d, t: t + mm(d, t), d2, x)
        x = each(lambda d, t: t + mm(d, t), d4, x)
        x = each(lambda d, t: t + mm(d, t), d8, x)
        e1 = [t[:, :ch] for t in x]
        wu = [t[:, ch:] for t in x]
        e2 = each(mm, e1, e1)
        wu = each(lambda e, t: t + mm(e, t), e1, wu)
        wu = each(lambda e, t: t + mm(e, t), e2, wu)
        ry = each(mm, a_rb, wu)
        rkv = each(mm, a_rk, v_)
        mn = each(lambda b, g, t: _dot_tn(b * g, t, HI), b_, g_end, wu)
        kv = each(lambda k, g, t: _dot_tn(k * g, t, HI), k_, g_end, v_)
        for u in range(unroll):
            rq_ref[0, sls[u], :] = r_[u] + ry[u][:, :n]
            yl_ref[0, sls[u], :] = ry[u][:, n:] + rkv[u]
            m_ref[0, cs[u]] = mn[u][:, :n] + jnp.where(eye, g_end[u], 0.0)
            n_ref[0, cs[u]] = mn[u][:, n:] + kv[u]
        return 0

    lax.fori_loop(0, nchunk // unroll, body, 0)


def rwkv_chunk(at, bt, kt, rt, v, eg, *, nchunk=8, unroll=8):
    bh, s, n = at.shape
    ch = RW_CHUNK
    rows = nchunk * ch
    spec = pl.BlockSpec((1, rows, n), lambda b, i: (b, i, 0))
    mspec = pl.BlockSpec((1, nchunk, n, n), lambda b, i: (b, i, 0, 0))
    return pl.pallas_call(
        functools.partial(_rwkv_chunk_kernel, nchunk=nchunk, unroll=unroll),
        grid=(bh, s // rows),
        in_specs=[spec] * 6,
        out_specs=[mspec, mspec, spec, spec],
        out_shape=[jax.ShapeDtypeStruct((bh, s // ch, n, n), F32)] * 2
        + [jax.ShapeDtypeStruct((bh, s, n), F32)] * 2,
        compiler_params=_cp("parallel", "parallel"),
        name="rwkv_chunk",
    )(at, bt, kt, rt, v, eg)


def _rwkv_scan_kernel(m_ref, n_ref, s_ref, st_ref, *, heads, nchunk):
    @pl.when(pl.program_id(1) == 0)
    def _():
        st_ref[...] = jnp.zeros_like(st_ref)

    def body(c, _):
        for hh in range(heads):
            st = st_ref[hh]
            s_ref[hh, c] = st
            st_ref[hh] = _dot(m_ref[hh, c], st, HI) + n_ref[hh, c]
        return 0

    lax.fori_loop(0, nchunk, body, 0)


def rwkv_scan(m, nn, *, heads=4, nchunk=32):
    bh, nc, n, _ = m.shape
    nchunk = min(nchunk, nc)
    spec = pl.BlockSpec((heads, nchunk, n, n), lambda b, i: (b, i, 0, 0))
    return pl.pallas_call(
        functools.partial(_rwkv_scan_kernel, heads=heads, nchunk=nchunk),
        grid=(bh // heads, nc // nchunk),
        in_specs=[spec, spec],
        out_specs=spec,
        out_shape=jax.ShapeDtypeStruct((bh, nc, n, n), F32),
        scratch_shapes=[pltpu.VMEM((heads, n, n), F32)],
        compiler_params=_cp("parallel", "arbitrary"),
        name="rwkv_scan",
    )(m, nn)


def _rwkv_out_kernel(rq_ref, yl_ref, s_ref, lw_ref, lb_ref, o_ref, *, nchunk):
    ch, n = RW_CHUNK, HEAD_DIM
    avg = jnp.full((n, n), 1.0 / n, F32)
    lw, lb = lw_ref[0], lb_ref[0]

    def body(c, _):
        sl = pl.ds(pl.multiple_of(c * ch, ch), ch)
        y = _dot(rq_ref[0, sl, :], s_ref[0, c], HI) + yl_ref[0, sl, :]
        d = y - _dot(y, avg, HI)
        var = _dot(d * d, avg, HI)
        o_ref[0, sl, :] = d * lax.rsqrt(var + GN_EPS) * lw + lb
        return 0

    lax.fori_loop(0, nchunk, body, 0)


def rwkv_out(rq, yl, s0, ln_w, ln_b, *, heads, nchunk=8):
    bh, s, n = rq.shape
    rows = nchunk * RW_CHUNK
    spec = pl.BlockSpec((1, rows, n), lambda b, i: (b, i, 0))
    hspec = pl.BlockSpec((1, 1, n), lambda b, i: (b % heads, 0, 0))
    return pl.pallas_call(
        functools.partial(_rwkv_out_kernel, nchunk=nchunk),
        grid=(bh, s // rows),
        in_specs=[spec, spec, pl.BlockSpec((1, nchunk, n, n), lambda b, i: (b, i, 0, 0)),
                  hspec, hspec],
        out_specs=spec,
        out_shape=jax.ShapeDtypeStruct((bh, s, n), F32),
        compiler_params=_cp("parallel", "parallel"),
        name="rwkv_out",
    )(rq, yl, s0, ln_w.reshape(heads, 1, n), ln_b.reshape(heads, 1, n))


def _hyb_out_kernel(oa_ref, y_ref, bv_ref, g_ref, wa_ref, wb_ref, r_ref, o_ref):
    ob = (y_ref[...] + bv_ref[...]) * g_ref[...]
    o_ref[...] = (r_ref[...] + _dot(oa_ref[...].astype(BF16), wa_ref[...])
                  + _dot(ob.astype(BF16), wb_ref[...]))


def hyb_out(res, oa, y, bv, g, wa, wb, *, tm=512):
    t, d = res.shape
    row = lambda i: (i, 0)
    fix = lambda i: (0, 0)
    act = lambda a: pl.BlockSpec((tm, a.shape[1]), row)
    return pl.pallas_call(
        _hyb_out_kernel,
        grid=(t // tm,),
        in_specs=[act(oa), act(y), act(bv), act(g), pl.BlockSpec(wa.shape, fix),
                  pl.BlockSpec(wb.shape, fix), pl.BlockSpec((tm, d), row)],
        out_specs=pl.BlockSpec((tm, d), row),
        out_shape=jax.ShapeDtypeStruct((t, d), F32),
        compiler_params=_cp("parallel"),
        name="hyb_out",
    )(oa, y, bv, g, wa, wb, res)


def _to_heads(x, b, s, heads):
    e = x.shape[1] // heads
    return x.reshape(b, s, heads, e).transpose(0, 2, 1, 3).reshape(b * heads, s, e)


def _from_heads(x, b, s, heads):
    e = x.shape[2]
    return x.reshape(b, heads, s, e).transpose(0, 2, 1, 3).reshape(b * s, heads * e)


def _pad_cols(w, n):
    return jnp.pad(w, ((0, 0), (0, n - w.shape[1])))


def sb_rwkv_layer(h, b, s, attn_norm, w_in, w_out, mu, w0, w2, a0, a2, g2, k_k, k_a, r_k, ln_w, ln_b):
    sbw = SB_HEADS * HEAD_DIM
    rww = RW_HEADS * HEAD_DIM
    rw_in = w_in.shape[1] - 3 * sbw
    rw_pad = -(-rw_in // LANE) * LANE
    dl, al = w2.shape[0], a2.shape[0]
    gl = g2.shape[0]
    assert dl + al == LANE and 3 * rww % LANE == 0

    qkv = rms_matmul(h, attn_norm, w_in[:, :3 * sbw].astype(BF16), out_dtype=BF16)
    rw = rms_matmul(h, attn_norm, _pad_cols(w_in[:, 3 * sbw:], rw_pad).astype(BF16), tn=rw_pad // 3)

    q, k, v = (_to_heads(qkv[:, j * sbw:(j + 1) * sbw], b, s, SB_HEADS) for j in range(3))
    q = q * (1.0 / math.sqrt(HEAD_DIM))
    oa = _from_heads(sb_attention(q, k, v), b, s, SB_HEADS)

    w2p = jnp.zeros((LANE, rww), F32).at[:dl].set(w2)
    a2p = jnp.zeros((LANE, rww), F32).at[dl:].set(a2)
    g2p = jnp.zeros((rw_pad - 3 * rww - LANE, rww), F32).at[:gl].set(g2)
    vec = lambda t: t.reshape(1, rww).astype(F32)
    mu_p = _pad_cols(mu.reshape(1, rw_in), rw_pad)
    at, bt, kt, rt, vv, eg, g, bv = rwkv_prep(rw, mu_p, vec(w0), vec(a0), vec(k_k), vec(k_a), vec(r_k),
                                              w2p, a2p, g2p, seq=s, width=rww)
    th = lambda t: _to_heads(t, b, s, RW_HEADS)
    m, nn, rq, yl = rwkv_chunk(th(at), th(bt), th(kt), th(rt), th(vv), th(eg))
    s0 = rwkv_scan(m, nn)
    y = _from_heads(rwkv_out(rq, yl, s0, ln_w, ln_b, heads=RW_HEADS), b, s, RW_HEADS)
    return hyb_out(h, oa, y, bv, g, w_out[:sbw].astype(BF16), w_out[sbw:].astype(BF16))


def mla_layer(h, b, s, positions, attn_norm, w_down, q_norm, kv_norm, w_uq, w_ukv, w_o):
    d = h.shape[1]
    nh, dn, dr, dv = MLA_HEADS, MLA_NOPE, MLA_ROPE, MLA_V
    half = dr // 2
    qr, kvr = MLA_Q_RANK, MLA_KV_RANK
    z = jnp.zeros((d, LANE - half), F32)
    wd = jnp.concatenate([w_down[:, :qr], w_down[:, qr + kvr:qr + kvr + half], z,
                          w_down[:, qr:qr + kvr], w_down[:, qr + kvr + half:], z], axis=1)
    c = rms_matmul(h, attn_norm, wd.astype(BF16), tn=wd.shape[1])
    wq = w_uq.reshape(qr, nh, dn + dr)
    wq = jnp.concatenate([wq[:, :, :dn].reshape(qr, nh * dn), wq[:, :, dn:dn + half].reshape(qr, nh * half),
                          wq[:, :, dn + half:].reshape(qr, nh * half)], axis=1)
    q = rms_matmul(c, q_norm, wq.astype(BF16), x_col=0)
    wkv = w_ukv.reshape(kvr, nh, dn + dv)
    wkv = jnp.concatenate([wkv[:, :, :dn].reshape(kvr, nh * dn), wkv[:, :, dn:].reshape(kvr, nh * dv)], axis=1)
    kv = rms_matmul(c, kv_norm, wkv.astype(BF16), x_col=(qr + LANE) // kvr, out_dtype=BF16)

    qw = nh * half
    q1, q2, k1, k2 = rope(q, c, positions, q1_col=nh * dn // qw, q2_col=nh * dn // qw + 1,
                          k1_col=qr // LANE, k2_col=(qr + LANE + kvr) // LANE, qw=qw)
    qh = jnp.concatenate([q[:, :nh * dn].reshape(b, s, nh, dn), q1.reshape(b, s, nh, half),
                          q2.reshape(b, s, nh, half)], axis=-1)
    qh = qh.transpose(0, 2, 1, 3).reshape(b * nh, s, dn + dr)
    kr = jnp.concatenate([k1[:, :half], k2[:, :half]], axis=-1).reshape(b, s, 1, dr).astype(BF16)
    kh = jnp.concatenate([kv[:, :nh * dn].reshape(b, s, nh, dn),
                          jnp.broadcast_to(kr, (b, s, nh, dr))], axis=-1)
    kh = kh.transpose(0, 2, 1, 3).reshape(b * nh, s, dn + dr)
    vh = _to_heads(kv[:, nh * dn:], b, s, nh)
    o = softmax_attention(qh, kh, vh, 1.0 / math.sqrt(dn + dr))
    return matmul_residual(h, _from_heads(o, b, s, nh), w_o.astype(BF16))


def kernel(x, p, positions, attn_norm, ffn_norm, ffn_w_in, ffn_conv_w, ffn_conv_b, ffn_w_out, ple_w_proj, ple_norm, ple_gate_norm, ple_w_gate, hyb_w_in, hyb_w_out, rw_mu, rw_w0, rw_w2, rw_a0, rw_a2, rw_g2, rw_k_k, rw_k_a, rw_r_k, rw_ln_w, rw_ln_b, mla_w_down, mla_q_norm, mla_kv_norm, mla_w_uq, mla_w_ukv, mla_w_o, final_norm):
    b, s, d = x.shape
    depth = p.shape[0]
    h = x.reshape(b * s, d)
    pos = positions.reshape(b * s)
    for i in range(depth):
        j = i // 2
        if i % 2 == 0:
            h = sb_rwkv_layer(h, b, s, attn_norm[i], hyb_w_in[j], hyb_w_out[j], rw_mu[j], rw_w0[j],
                              rw_w2[j], rw_a0[j], rw_a2[j], rw_g2[j], rw_k_k[j], rw_k_a[j], rw_r_k[j],
                              rw_ln_w[j], rw_ln_b[j])
        else:
            h = mla_layer(h, b, s, pos, attn_norm[i], mla_w_down[j], mla_q_norm[j], mla_kv_norm[j],
                          mla_w_uq[j], mla_w_ukv[j], mla_w_o[j])
        ug = rms_matmul(h, ffn_norm[i], ffn_w_in[i].astype(BF16))
        h = convglu_out(h, ug, ffn_conv_w[i], ffn_conv_b[i], ffn_w_out[i].astype(BF16), seq=s)
        h = ple(h, p[i].reshape(b * s, -1), ple_w_proj[i].astype(BF16), ple_norm[i], ple_gate_norm[i],
                ple_w_gate[i].astype(BF16), final_norm, final=(i == depth - 1))
    return h.reshape(b, s, d)
```

```python
import functools
import math

import jax
import jax.numpy as jnp
from jax import lax
from jax.experimental import pallas as pl
from jax.experimental.pallas import tpu as pltpu

F32 = jnp.float32
BF16 = jnp.bfloat16
HI = lax.Precision.HIGHEST

NORM_EPS = 1e-6
GN_EPS = 64e-5
ROPE_THETA = 10000.0
LANE = 128
VMEM_LIMIT = 48 * 1024 * 1024

SB_HEADS = 8
RW_HEADS = 8
HEAD_DIM = 64
MLA_HEADS = 16
MLA_NOPE = 64
MLA_ROPE = 32
MLA_V = 64
MLA_Q_RANK = 384
MLA_KV_RANK = 256
SB_CUT = -104.0
RW_CHUNK = 64
RW_SUB = 16


def _cp(*sem):
    return pltpu.CompilerParams(dimension_semantics=sem, vmem_limit_bytes=VMEM_LIMIT)


def _iota(shape, dim):
    return lax.broadcasted_iota(jnp.int32, shape, dim)


def _softplus(z):
    return jnp.maximum(z, 0.0) + jnp.log1p(jnp.exp(-jnp.abs(z)))


def _sigmoid(z):
    return 1.0 / (1.0 + jnp.exp(-z))


def _dot(a, b, precision=None):
    return jnp.dot(a, b, preferred_element_type=F32, precision=precision)


def _dot_nt(a, b, precision=None):
    return lax.dot_general(a, b, (((1,), (1,)), ((), ())), preferred_element_type=F32,
                           precision=precision)


def _dot_tn(a, b, precision=None):
    return lax.dot_general(a, b, (((0,), (0,)), ((), ())), preferred_element_type=F32,
                           precision=precision)


def _rms(x, g, eps=NORM_EPS):
    return x * lax.rsqrt(jnp.mean(x * x, axis=-1, keepdims=True) + eps) * g


def _rms_matmul_kernel(x_ref, g_ref, w_ref, o_ref, xn_ref):
    @pl.when(pl.program_id(1) == 0)
    def _():
        xn_ref[...] = _rms(x_ref[...].astype(F32), g_ref[...]).astype(BF16)

    o_ref[...] = _dot(xn_ref[...], w_ref[...]).astype(o_ref.dtype)


def rms_matmul(x, g, w, *, x_col=0, tm=512, tn=512, out_dtype=F32):
    t = x.shape[0]
    k, n = w.shape
    tn = min(tn, n)
    return pl.pallas_call(
        _rms_matmul_kernel,
        grid=(t // tm, n // tn),
        in_specs=[pl.BlockSpec((tm, k), lambda i, j: (i, x_col)),
                  pl.BlockSpec((1, k), lambda i, j: (0, 0)),
                  pl.BlockSpec((k, tn), lambda i, j: (0, j))],
        out_specs=pl.BlockSpec((tm, tn), lambda i, j: (i, j)),
        out_shape=jax.ShapeDtypeStruct((t, n), out_dtype),
        scratch_shapes=[pltpu.VMEM((tm, k), BF16)],
        compiler_params=_cp("parallel", "arbitrary"),
        name="rms_matmul",
    )(x, g.reshape(1, k).astype(F32), w)


def _matmul_res_kernel(a_ref, w_ref, r_ref, o_ref):
    o_ref[...] = r_ref[...] + _dot(a_ref[...].astype(BF16), w_ref[...])


def matmul_residual(res, a, w, *, tm=512, tn=512):
    t, k = a.shape
    n = w.shape[1]
    return pl.pallas_call(
        _matmul_res_kernel,
        grid=(t // tm, n // tn),
        in_specs=[pl.BlockSpec((tm, k), lambda i, j: (i, 0)),
                  pl.BlockSpec((k, tn), lambda i, j: (0, j)),
                  pl.BlockSpec((tm, tn), lambda i, j: (i, j))],
        out_specs=pl.BlockSpec((tm, tn), lambda i, j: (i, j)),
        out_shape=jax.ShapeDtypeStruct((t, n), F32),
        compiler_params=_cp("parallel", "arbitrary"),
        name="matmul_residual",
    )(a, w, res)


def _convglu_out_kernel(u_ref, g_ref, gp_ref, cw_ref, cb_ref, w_ref, r_ref, o_ref, *, tm, seq):
    i = pl.program_id(0)
    g = g_ref[...]
    rows = _iota(g.shape, 0)
    at_start = (i * tm) % seq == 0
    halo = jnp.where(at_start, 0.0, gp_ref[...])
    p1 = jnp.where(rows == 0, halo[7:8, :], pltpu.roll(g, 1, axis=0))
    p2 = jnp.where(rows == 0, halo[6:7, :], jnp.where(rows == 1, halo[7:8, :],
                                                      pltpu.roll(g, 2, axis=0)))
    c = cw_ref[0:1, :] * p2 + cw_ref[1:2, :] * p1 + cw_ref[2:3, :] * g + cb_ref[...]
    act = 0.5 * c * (1.0 + lax.erf(c * (1.0 / math.sqrt(2.0)))) * u_ref[...]
    o_ref[...] = r_ref[...] + _dot(act.astype(BF16), w_ref[...])


def convglu_out(res, ug, conv_w, conv_b, w_out, *, seq, tm=256):
    t = res.shape[0]
    f, n = w_out.shape
    hb = tm // 8
    return pl.pallas_call(
        functools.partial(_convglu_out_kernel, tm=tm, seq=seq),
        grid=(t // tm,),
        in_specs=[pl.BlockSpec((tm, f), lambda i: (i, 0)),
                  pl.BlockSpec((tm, f), lambda i: (i, 1)),
                  pl.BlockSpec((8, f), lambda i: (jnp.maximum(i * hb - 1, 0), 1)),
                  pl.BlockSpec((3, f), lambda i: (0, 0)),
                  pl.BlockSpec((1, f), lambda i: (0, 0)),
                  pl.BlockSpec((f, n), lambda i: (0, 0)),
                  pl.BlockSpec((tm, n), lambda i: (i, 0))],
        out_specs=pl.BlockSpec((tm, n), lambda i: (i, 0)),
        out_shape=jax.ShapeDtypeStruct((t, n), F32),
        compiler_params=_cp("parallel"),
        name="convglu_out",
    )(ug, ug, ug, conv_w, conv_b.reshape(1, f), w_out, res)


def _ple_kernel(h_ref, p_ref, wp_ref, pn_ref, gn_ref, wg_ref, fn_ref, o_ref, *, final):
    h = h_ref[...]
    e = _rms(_dot(p_ref[...].astype(BF16), wp_ref[...]), pn_ref[...])
    gate = _sigmoid(_dot(_rms(h, gn_ref[...]).astype(BF16), wg_ref[...]))
    out = h + gate * e
    if final:
        out = _rms(out, fn_ref[...])
    o_ref[...] = out


def ple(h, p, w_proj, p_norm, g_norm, w_gate, f_norm, *, final, tm=512):
    t, d = h.shape
    pd = p.shape[1]
    row = lambda i: (i, 0)
    fix = lambda i: (0, 0)
    return pl.pallas_call(
        functools.partial(_ple_kernel, final=final),
        grid=(t // tm,),
        in_specs=[pl.BlockSpec((tm, d), row), pl.BlockSpec((tm, pd), row),
                  pl.BlockSpec((pd, d), fix), pl.BlockSpec((1, d), fix), pl.BlockSpec((1, d), fix),
                  pl.BlockSpec((d, d), fix), pl.BlockSpec((1, d), fix)],
        out_specs=pl.BlockSpec((tm, d), row),
        out_shape=jax.ShapeDtypeStruct((t, d), F32),
        compiler_params=_cp("parallel"),
        name="ple",
    )(h, p, w_proj, p_norm.reshape(1, d), g_norm.reshape(1, d), w_gate, f_norm.reshape(1, d))


def _sb_attn_kernel(q_ref, k_ref, v_ref, u_ref, o_ref, *, tq, tk):
    qi = pl.program_id(1)
    q = q_ref[0]
    nblk = tq // tk
    q0 = qi * tq

    def sweep(k_lo, run, acc, masked):
        parts = []
        for d in reversed(range(nblk)):
            k0 = k_lo + d * tk
            kb = k_ref[0, pl.ds(k0, tk), :]
            z = _dot_nt(q, kb)
            sp = _softplus(z)
            lom = -sp
            valid = None
            if masked:
                valid = _iota((tq, tk), 1) + d * tk < _iota((tq, tk), 0)
                lom = jnp.where(valid, lom, 0.0)
            hi = lom.astype(BF16)
            lo = (lom - hi.astype(F32)).astype(BF16)
            sums = _dot(jnp.concatenate([hi, lo], axis=1), u_ref[...])
            parts.append((k0, z - sp, sums, valid))
        for k0, logsig, sums, valid in parts:
            w = jnp.exp(logsig + run + sums[:, :tk])
            if masked:
                w = jnp.where(valid, w, 0.0)
            acc = acc + _dot(w.astype(BF16), v_ref[0, pl.ds(k0, tk), :])
            run = run + sums[:, tk:]
        return run, acc

    run = jnp.zeros((tq, tk), F32)
    acc = jnp.zeros((tq, v_ref.shape[2]), F32)
    run, acc = sweep(pl.multiple_of(q0, tq), run, acc, True)

    def cond(c):
        return c[0]

    def body(c):
        _, j, run, acc = c
        run, acc = sweep(pl.multiple_of(j * tq, tq), run, acc, False)
        return (j > 0) & (jnp.max(run) > SB_CUT), j - 1, run, acc

    go = (qi > 0) & (jnp.max(run) > SB_CUT)
    _, _, _, acc = lax.while_loop(cond, body, (go, qi - 1, run, acc))
    o_ref[0] = acc


def sb_attention(q, k, v, *, tq=256):
    bh, s, e = q.shape
    tk = LANE
    jj = jnp.arange(2 * tk)[:, None] % tk
    ss = jnp.arange(2 * tk)[None, :]
    u = jnp.where(ss < tk, jj > ss, True).astype(BF16)
    return pl.pallas_call(
        functools.partial(_sb_attn_kernel, tq=tq, tk=tk),
        grid=(bh, s // tq),
        in_specs=[pl.BlockSpec((1, tq, e), lambda b, i: (b, i, 0)),
                  pl.BlockSpec((1, s, e), lambda b, i: (b, 0, 0)),
                  pl.BlockSpec((1, s, e), lambda b, i: (b, 0, 0)),
                  pl.BlockSpec((2 * tk, 2 * tk), lambda b, i: (0, 0))],
        out_specs=pl.BlockSpec((1, tq, e), lambda b, i: (b, i, 0)),
        out_shape=jax.ShapeDtypeStruct((bh, s, e), F32),
        compiler_params=_cp("parallel", "arbitrary"),
        name="sb_attention",
    )(q, k, v, u)


def _softmax_attn_kernel(q_ref, k_ref, v_ref, o_ref, sa_ref, sb_ref, *, tq, tk, scale):
    qi = pl.program_id(1)
    q = (q_ref[0] * (scale * math.log2(math.e))).astype(BF16)

    def scores(j, dst_ref):
        dst_ref[...] = _dot_nt(q, k_ref[0, pl.ds(pl.multiple_of(j * tk, tk), tk), :])

    def consume(src_ref, j, carry, masked):
        m, l, acc = carry
        s = src_ref[...]
        if masked:
            s = jnp.where(_iota((tq, tk), 1) <= _iota((tq, tk), 0), s, -1e30)
        m_new = jnp.maximum(m, jnp.max(s, axis=1, keepdims=True))
        p = jnp.exp2(s - m_new)
        alpha = jnp.exp2(m - m_new)
        l = alpha * l + jnp.sum(p, axis=1, keepdims=True)
        vb = v_ref[0, pl.ds(pl.multiple_of(j * tk, tk), tk), :]
        acc = alpha * acc + _dot(p.astype(BF16), vb)
        return m_new, l, acc

    def pair(t, carry):
        scores(2 * t + 1, sb_ref)
        carry = consume(sa_ref, 2 * t, carry, False)
        scores(2 * t + 2, sa_ref)
        return consume(sb_ref, 2 * t + 1, carry, False)

    def tail_odd(carry):
        scores(qi, sb_ref)
        carry = consume(sa_ref, qi - 1, carry, False)
        return consume(sb_ref, qi, carry, True)

    def tail_even(carry):
        return consume(sa_ref, qi, carry, True)

    init = (jnp.full((tq, 1), -1e30, F32), jnp.zeros((tq, 1), F32),
            jnp.zeros((tq, v_ref.shape[2]), F32))
    scores(0, sa_ref)
    carry = lax.fori_loop(0, qi // 2, pair, init)
    _, l, acc = lax.cond(qi % 2 == 1, tail_odd, tail_even, carry)
    o_ref[0] = acc / l


def softmax_attention(q, k, v, scale, *, tq=512):
    tk = tq
    bh, s, e = q.shape
    ev = v.shape[2]
    return pl.pallas_call(
        functools.partial(_softmax_attn_kernel, tq=tq, tk=tk, scale=scale),
        scratch_shapes=[pltpu.VMEM((tq, tk), F32), pltpu.VMEM((tq, tk), F32)],
        grid=(bh, s // tq),
        in_specs=[pl.BlockSpec((1, tq, e), lambda b, i: (b, i, 0)),
                  pl.BlockSpec((1, s, e), lambda b, i: (b, 0, 0)),
                  pl.BlockSpec((1, s, ev), lambda b, i: (b, 0, 0))],
        out_specs=pl.BlockSpec((1, tq, ev), lambda b, i: (b, i, 0)),
        out_shape=jax.ShapeDtypeStruct((bh, s, ev), F32),
        compiler_params=_cp("parallel", "arbitrary"),
        name="softmax_attention",
    )(q, k, v)


def _rope_kernel(q1_ref, q2_ref, k1_ref, k2_ref, pos_ref, f_ref, oq1_ref, oq2_ref, ok1_ref, ok2_ref):
    ang = pos_ref[...].astype(F32) * f_ref[...]
    cos, sin = jnp.cos(ang), jnp.sin(ang)
    reps = q1_ref.shape[1] // LANE
    cq = jnp.concatenate([cos] * reps, axis=1)
    sq = jnp.concatenate([sin] * reps, axis=1)
    q1, q2 = q1_ref[...], q2_ref[...]
    oq1_ref[...] = q1 * cq - q2 * sq
    oq2_ref[...] = q1 * sq + q2 * cq
    k1, k2 = k1_ref[...], k2_ref[...]
    ok1_ref[...] = k1 * cos - k2 * sin
    ok2_ref[...] = k1 * sin + k2 * cos


def rope(q, c, positions, *, q1_col, q2_col, k1_col, k2_col, qw, tm=512):
    t = q.shape[0]
    half = MLA_ROPE // 2
    inv_freq = ROPE_THETA ** (-jnp.arange(half, dtype=F32) / half)
    freq = jnp.tile(inv_freq, LANE // half).reshape(1, LANE)
    spec = lambda w, col: pl.BlockSpec((tm, w), lambda i: (i, col))
    return pl.pallas_call(
        _rope_kernel,
        grid=(t // tm,),
        in_specs=[spec(qw, q1_col), spec(qw, q2_col), spec(LANE, k1_col), spec(LANE, k2_col),
                  pl.BlockSpec((tm, 1), lambda i: (i, 0)), pl.BlockSpec((1, LANE), lambda i: (0, 0))],
        out_specs=[spec(qw, 0), spec(qw, 0), spec(LANE, 0), spec(LANE, 0)],
        out_shape=[jax.ShapeDtypeStruct((t, qw), F32), jax.ShapeDtypeStruct((t, qw), F32),
                   jax.ShapeDtypeStruct((t, LANE), F32), jax.ShapeDtypeStruct((t, LANE), F32)],
        compiler_params=_cp("parallel"),
        name="rope",
    )(q, q, c, c, positions.reshape(t, 1), freq)


def _rwkv_prep_kernel(x_ref, xp_ref, mu_ref, w0_ref, a0_ref, kk_ref, ka_ref, rk_ref,
                      w2_ref, a2_ref, g2_ref, tri_ref, hs_ref,
                      at_ref, bt_ref, kt_ref, rt_ref, v_ref, eg_ref, g_ref, bv_ref, *, tm, seq, width):
    i = pl.program_id(0)
    x = x_ref[...]
    rows = _iota(x.shape, 0)
    at_start = (i * tm) % seq == 0
    prev_row = jnp.where(at_start, 0.0, xp_ref[7:8, :])
    prev = jnp.where(rows == 0, prev_row, pltpu.roll(x, 1, axis=0))
    xm = x + (prev - x) * mu_ref[...]
    r = xm[:, :width]
    k = xm[:, width:2 * width]
    v = xm[:, 2 * width:3 * width]
    l1 = xm[:, 3 * width:3 * width + LANE]
    l2 = xm[:, 3 * width + LANE:]
    wpre = w0_ref[...] + _dot(jnp.tanh(l1), w2_ref[...], HI)
    a = _sigmoid(a0_ref[...] + _dot(l1, a2_ref[...], HI))
    g = _dot(_sigmoid(l2), g2_ref[...], HI)
    lw = -jnp.exp(-_softplus(-wpre) - 0.5)
    hs = hs_ref[...]
    kk = k * kk_ref[...]
    kk = kk * lax.rsqrt(jnp.maximum(_dot(kk * kk, hs, HI), 1e-24))
    km = k * (1.0 + (a - 1.0) * ka_ref[...])
    bonus = _dot(r * km * rk_ref[...], hs, HI)
    cum = _dot(tri_ref[...], lw, HI)
    eg = jnp.exp(cum)
    ieg = jnp.exp(-cum)
    at_ref[...] = -kk * jnp.exp(cum - lw)
    bt_ref[...] = kk * a * ieg
    kt_ref[...] = km * ieg
    rt_ref[...] = r * eg
    v_ref[...] = v
    eg_ref[...] = eg
    g_ref[...] = g
    bv_ref[...] = bonus * v


def rwkv_prep(x, mu, w0, a0, k_k, k_a, r_k, w2p, a2p, g2p, *, seq, width, tm=256):
    t, wx = x.shape
    ch = RW_CHUNK
    ti = jnp.arange(tm)
    tri = ((ti[:, None] // ch == ti[None, :] // ch) & (ti[None, :] <= ti[:, None])).astype(F32)
    ci = jnp.arange(width) // HEAD_DIM
    hs = (ci[:, None] == ci[None, :]).astype(F32)
    hb = tm // 8
    row = lambda i: (i, 0)
    fix = lambda i: (0, 0)
    vec = pl.BlockSpec((1, width), fix)
    out = jax.ShapeDtypeStruct((t, width), F32)
    return pl.pallas_call(
        functools.partial(_rwkv_prep_kernel, tm=tm, seq=seq, width=width),
        grid=(t // tm,),
        in_specs=[pl.BlockSpec((tm, wx), row),
                  pl.BlockSpec((8, wx), lambda i: (jnp.maximum(i * hb - 1, 0), 0)),
                  pl.BlockSpec((1, wx), fix), vec, vec, vec, vec, vec,
                  pl.BlockSpec(w2p.shape, fix), pl.BlockSpec(a2p.shape, fix),
                  pl.BlockSpec(g2p.shape, fix), pl.BlockSpec((tm, tm), fix),
                  pl.BlockSpec((width, width), fix)],
        out_specs=[pl.BlockSpec((tm, width), row)] * 8,
        out_shape=[out] * 8,
        compiler_params=_cp("parallel"),
        name="rwkv_prep",
    )(x, x, mu, w0, a0, k_k, k_a, r_k, w2p, a2p, g2p, tri, hs)


def _rwkv_chunk_kernel(at_ref, bt_ref, kt_ref, rt_ref, v_ref, eg_ref,
                       m_ref, n_ref, rq_ref, yl_ref, *, nchunk, unroll):
    ch, n = RW_CHUNK, HEAD_DIM
    ti = _iota((ch, ch), 0)
    si = _iota((ch, ch), 1)
    strict = si < ti
    incl = si <= ti
    same = (ti // RW_SUB) == (si // RW_SUB)
    eye = _iota((n, n), 0) == _iota((n, n), 1)

    def body(it, _):
        cs = [it * unroll + u for u in range(unroll)]
        sls = [pl.ds(pl.multiple_of(c * ch, ch), ch) for c in cs]
        each = lambda f, *ls: [f(*t) for t in zip(*ls)]
        mm = lambda x, y: _dot(x, y, HI)
        a_ = [at_ref[0, sl, :] for sl in sls]
        b_ = [bt_ref[0, sl, :] for sl in sls]
        k_ = [kt_ref[0, sl, :] for sl in sls]
        r_ = [rt_ref[0, sl, :] for sl in sls]
        v_ = [v_ref[0, sl, :] for sl in sls]
        g_end = [eg_ref[0, pl.ds(c * ch + ch - 1, 1), :] for c in cs]
        ar = each(lambda a, r: jnp.concatenate([a, r], axis=0), a_, r_)
        pb = each(lambda x, y: _dot_nt(x, y, HI), ar, b_)
        pk = each(lambda x, y: _dot_nt(x, y, HI), ar, k_)
        a_ab = [jnp.where(strict, t[:ch], 0.0) for t in pb]
        a_ak = [jnp.where(strict, t[:ch], 0.0) for t in pk]
        a_rb = [jnp.where(incl, t[ch:], 0.0) for t in pb]
        a_rk = [jnp.where(incl, t[ch:], 0.0) for t in pk]
        d1 = [jnp.where(same, t, 0.0) for t in a_ab]
        lb = each(lambda t, d: t - d, a_ab, d1)
        d2 = each(mm, d1, d1)
        akv = each(mm, a_ak, v_)
        x = each(lambda l, a, t: jnp.concatenate([l, a, t], axis=1), lb, a_, akv)
        d4 = each(mm, d2, d2)
        x = each(lambda d, t: t + mm(d, t), d1, x)
        d8 = each(mm, d4, d4)
        x = each(lambda d, t: t + mm(d, t), d2, x)
        x = each(lambda d, t: t + mm(d, t), d4, x)
        x = each(lambda d, t: t + mm(d, t), d8, x)
        e1 = [t[:, :ch] for t in x]
        wu = [t[:, ch:] for t in x]
        e2 = each(mm, e1, e1)
        wu = each(lambda e, t: t + mm(e, t), e1, wu)
        wu = each(lambda e, t: t + mm(e, t), e2, wu)
        ry = each(mm, a_rb, wu)
        rkv = each(mm, a_rk, v_)
        mn = each(lambda b, g, t: _dot_tn(b * g, t, HI), b_, g_end, wu)
        kv = each(lambda k, g, t: _dot_tn(k * g, t, HI), k_, g_end, v_)
        for u in range(unroll):
            rq_ref[0, sls[u], :] = r_[u] + ry[u][:, :n]
            yl_ref[0, sls[u], :] = ry[u][:, n:] + rkv[u]
            m_ref[0, cs[u]] = mn[u][:, :n] + jnp.where(eye, g_end[u], 0.0)
            n_ref[0, cs[u]] = mn[u][:, n:] + kv[u]
        return 0

    lax.fori_loop(0, nchunk // unroll, body, 0)


def rwkv_chunk(at, bt, kt, rt, v, eg, *, nchunk=8, unroll=8):
    bh, s, n = at.shape
    ch = RW_CHUNK
    rows = nchunk * ch
    spec = pl.BlockSpec((1, rows, n), lambda b, i: (b, i, 0))
    mspec = pl.BlockSpec((1, nchunk, n, n), lambda b, i: (b, i, 0, 0))
    return pl.pallas_call(
        functools.partial(_rwkv_chunk_kernel, nchunk=nchunk, unroll=unroll),
        grid=(bh, s // rows),
        in_specs=[spec] * 6,
        out_specs=[mspec, mspec, spec, spec],
        out_shape=[jax.ShapeDtypeStruct((bh, s // ch, n, n), F32)] * 2
        + [jax.ShapeDtypeStruct((bh, s, n), F32)] * 2,
        compiler_params=_cp("parallel", "parallel"),
        name="rwkv_chunk",
    )(at, bt, kt, rt, v, eg)


def _rwkv_scan_kernel(m_ref, n_ref, s_ref, st_ref, *, heads, nchunk):
    @pl.when(pl.program_id(1) == 0)
    def _():
        st_ref[...] = jnp.zeros_like(st_ref)

    def body(c, _):
        for hh in range(heads):
            st = st_ref[hh]
            s_ref[hh, c] = st
            st_ref[hh] = _dot(m_ref[hh, c], st, HI) + n_ref[hh, c]
        return 0

    lax.fori_loop(0, nchunk, body, 0)


def rwkv_scan(m, nn, *, heads=4, nchunk=32):
    bh, nc, n, _ = m.shape
    nchunk = min(nchunk, nc)
    spec = pl.BlockSpec((heads, nchunk, n, n), lambda b, i: (b, i, 0, 0))
    return pl.pallas_call(
        functools.partial(_rwkv_scan_kernel, heads=heads, nchunk=nchunk),
        grid=(bh // heads, nc // nchunk),
        in_specs=[spec, spec],
        out_specs=spec,
        out_shape=jax.ShapeDtypeStruct((bh, nc, n, n), F32),
        scratch_shapes=[pltpu.VMEM((heads, n, n), F32)],
        compiler_params=_cp("parallel", "arbitrary"),
        name="rwkv_scan",
    )(m, nn)


def _rwkv_out_kernel(rq_ref, yl_ref, s_ref, lw_ref, lb_ref, o_ref, *, nchunk):
    ch, n = RW_CHUNK, HEAD_DIM
    avg = jnp.full((n, n), 1.0 / n, F32)
    lw, lb = lw_ref[0], lb_ref[0]

    y = jnp.concatenate(
        [_dot(rq_ref[0, c * ch:(c + 1) * ch, :], s_ref[0, c], HI) for c in range(nchunk)], axis=0)
    y = y + yl_ref[0]
    d = y - _dot(y, avg, HI)
    var = _dot(d * d, avg, HI)
    o_ref[0] = d * lax.rsqrt(var + GN_EPS) * lw + lb


def rwkv_out(rq, yl, s0, ln_w, ln_b, *, heads, nchunk=8):
    bh, s, n = rq.shape
    rows = nchunk * RW_CHUNK
    spec = pl.BlockSpec((1, rows, n), lambda b, i: (b, i, 0))
    hspec = pl.BlockSpec((1, 1, n), lambda b, i: (b % heads, 0, 0))
    return pl.pallas_call(
        functools.partial(_rwkv_out_kernel, nchunk=nchunk),
        grid=(bh, s // rows),
        in_specs=[spec, spec, pl.BlockSpec((1, nchunk, n, n), lambda b, i: (b, i, 0, 0)),
                  hspec, hspec],
        out_specs=spec,
        out_shape=jax.ShapeDtypeStruct((bh, s, n), F32),
        compiler_params=_cp("parallel", "parallel"),
        name="rwkv_out",
    )(rq, yl, s0, ln_w.reshape(heads, 1, n), ln_b.reshape(heads, 1, n))


def _hyb_out_kernel(oa_ref, y_ref, bv_ref, g_ref, wa_ref, wb_ref, r_ref, o_ref):
    ob = (y_ref[...] + bv_ref[...]) * g_ref[...]
    o_ref[...] = (r_ref[...] + _dot(oa_ref[...].astype(BF16), wa_ref[...])
                  + _dot(ob.astype(BF16), wb_ref[...]))


def hyb_out(res, oa, y, bv, g, wa, wb, *, tm=512):
    t, d = res.shape
    row = lambda i: (i, 0)
    fix = lambda i: (0, 0)
    act = lambda a: pl.BlockSpec((tm, a.shape[1]), row)
    return pl.pallas_call(
        _hyb_out_kernel,
        grid=(t // tm,),
        in_specs=[act(oa), act(y), act(bv), act(g), pl.BlockSpec(wa.shape, fix),
                  pl.BlockSpec(wb.shape, fix), pl.BlockSpec((tm, d), row)],
        out_specs=pl.BlockSpec((tm, d), row),
        out_shape=jax.ShapeDtypeStruct((t, d), F32),
        compiler_params=_cp("parallel"),
        name="hyb_out",
    )(oa, y, bv, g, wa, wb, res)


def _to_heads(x, b, s, heads):
    e = x.shape[1] // heads
    return x.reshape(b, s, heads, e).transpose(0, 2, 1, 3).reshape(b * heads, s, e)


def _from_heads(x, b, s, heads):
    e = x.shape[2]
    return x.reshape(b, heads, s, e).transpose(0, 2, 1, 3).reshape(b * s, heads * e)


def _pad_cols(w, n):
    return jnp.pad(w, ((0, 0), (0, n - w.shape[1])))


def sb_rwkv_layer(h, b, s, attn_norm, w_in, w_out, mu, w0, w2, a0, a2, g2, k_k, k_a, r_k, ln_w, ln_b):
    sbw = SB_HEADS * HEAD_DIM
    rww = RW_HEADS * HEAD_DIM
    rw_in = w_in.shape[1] - 3 * sbw
    rw_pad = -(-rw_in // LANE) * LANE
    dl, al = w2.shape[0], a2.shape[0]
    gl = g2.shape[0]
    assert dl + al == LANE and 3 * rww % LANE == 0

    qkv = rms_matmul(h, attn_norm, w_in[:, :3 * sbw].astype(BF16), out_dtype=BF16)
    rw = rms_matmul(h, attn_norm, _pad_cols(w_in[:, 3 * sbw:], rw_pad).astype(BF16), tn=rw_pad // 3)

    q, k, v = (_to_heads(qkv[:, j * sbw:(j + 1) * sbw], b, s, SB_HEADS) for j in range(3))
    q = q * (1.0 / math.sqrt(HEAD_DIM))
    oa = _from_heads(sb_attention(q, k, v), b, s, SB_HEADS)

    w2p = jnp.zeros((LANE, rww), F32).at[:dl].set(w2)
    a2p = jnp.zeros((LANE, rww), F32).at[dl:].set(a2)
    g2p = jnp.zeros((rw_pad - 3 * rww - LANE, rww), F32).at[:gl].set(g2)
    vec = lambda t: t.reshape(1, rww).astype(F32)
    mu_p = _pad_cols(mu.reshape(1, rw_in), rw_pad)
    at, bt, kt, rt, vv, eg, g, bv = rwkv_prep(rw, mu_p, vec(w0), vec(a0), vec(k_k), vec(k_a), vec(r_k),
                                              w2p, a2p, g2p, seq=s, width=rww)
    th = lambda t: _to_heads(t, b, s, RW_HEADS)
    m, nn, rq, yl = rwkv_chunk(th(at), th(bt), th(kt), th(rt), th(vv), th(eg))
    s0 = rwkv_scan(m, nn)
    y = _from_heads(rwkv_out(rq, yl, s0, ln_w, ln_b, heads=RW_HEADS), b, s, RW_HEADS)
    return hyb_out(h, oa, y, bv, g, w_out[:sbw].astype(BF16), w_out[sbw:].astype(BF16))


def mla_layer(h, b, s, positions, attn_norm, w_down, q_norm, kv_norm, w_uq, w_ukv, w_o):
    d = h.shape[1]
    nh, dn, dr, dv = MLA_HEADS, MLA_NOPE, MLA_ROPE, MLA_V
    half = dr // 2
    qr, kvr = MLA_Q_RANK, MLA_KV_RANK
    z = jnp.zeros((d, LANE - half), F32)
    wd = jnp.concatenate([w_down[:, :qr], w_down[:, qr + kvr:qr + kvr + half], z,
                          w_down[:, qr:qr + kvr], w_down[:, qr + kvr + half:], z], axis=1)
    c = rms_matmul(h, attn_norm, wd.astype(BF16), tn=wd.shape[1])
    wq = w_uq.reshape(qr, nh, dn + dr)
    wq = jnp.concatenate([wq[:, :, :dn].reshape(qr, nh * dn), wq[:, :, dn:dn + half].reshape(qr, nh * half),
                          wq[:, :, dn + half:].reshape(qr, nh * half)], axis=1)
    q = rms_matmul(c, q_norm, wq.astype(BF16), x_col=0)
    wkv = w_ukv.reshape(kvr, nh, dn + dv)
    wkv = jnp.concatenate([wkv[:, :, :dn].reshape(kvr, nh * dn), wkv[:, :, dn:].reshape(kvr, nh * dv)], axis=1)
    kv = rms_matmul(c, kv_norm, wkv.astype(BF16), x_col=(qr + LANE) // kvr, out_dtype=BF16)

    qw = nh * half
    q1, q2, k1, k2 = rope(q, c, positions, q1_col=nh * dn // qw, q2_col=nh * dn // qw + 1,
                          k1_col=qr // LANE, k2_col=(qr + LANE + kvr) // LANE, qw=qw)
    qh = jnp.concatenate([q[:, :nh * dn].reshape(b, s, nh, dn), q1.reshape(b, s, nh, half),
                          q2.reshape(b, s, nh, half)], axis=-1)
    qh = qh.transpose(0, 2, 1, 3).reshape(b * nh, s, dn + dr)
    kr = jnp.concatenate([k1[:, :half], k2[:, :half]], axis=-1).reshape(b, s, 1, dr).astype(BF16)
    kh = jnp.concatenate([kv[:, :nh * dn].reshape(b, s, nh, dn),
                          jnp.broadcast_to(kr, (b, s, nh, dr))], axis=-1)
    kh = kh.transpose(0, 2, 1, 3).reshape(b * nh, s, dn + dr)
    vh = _to_heads(kv[:, nh * dn:], b, s, nh)
    o = softmax_attention(qh, kh, vh, 1.0 / math.sqrt(dn + dr))
    return matmul_residual(h, _from_heads(o, b, s, nh), w_o.astype(BF16))


def kernel(x, p, positions, attn_norm, ffn_norm, ffn_w_in, ffn_conv_w, ffn_conv_b, ffn_w_out, ple_w_proj, ple_norm, ple_gate_norm, ple_w_gate, hyb_w_in, hyb_w_out, rw_mu, rw_w0, rw_w2, rw_a0, rw_a2, rw_g2, rw_k_k, rw_k_a, rw_r_k, rw_ln_w, rw_ln_b, mla_w_down, mla_q_norm, mla_kv_norm, mla_w_uq, mla_w_ukv, mla_w_o, final_norm):
    b, s, d = x.shape
    depth = p.shape[0]
    h = x.reshape(b * s, d)
    pos = positions.reshape(b * s)
    for i in range(depth):
        j = i // 2
        if i % 2 == 0:
            h = sb_rwkv_layer(h, b, s, attn_norm[i], hyb_w_in[j], hyb_w_out[j], rw_mu[j], rw_w0[j],
                              rw_w2[j], rw_a0[j], rw_a2[j], rw_g2[j], rw_k_k[j], rw_k_a[j], rw_r_k[j],
                              rw_ln_w[j], rw_ln_b[j])
        else:
            h = mla_layer(h, b, s, pos, attn_norm[i], mla_w_down[j], mla_q_norm[j], mla_kv_norm[j],
                          mla_w_uq[j], mla_w_ukv[j], mla_w_o[j])
        ug = rms_matmul(h, ffn_norm[i], ffn_w_in[i].astype(BF16))
        h = convglu_out(h, ug, ffn_conv_w[i], ffn_conv_b[i], ffn_w_out[i].astype(BF16), seq=s)
        h = ple(h, p[i].reshape(b * s, -1), ple_w_proj[i].astype(BF16), ple_norm[i], ple_gate_norm[i],
                ple_w_gate[i].astype(BF16), final_norm, final=(i == depth - 1))
    return h.reshape(b, s, d)
```

```python
import functools
import math

import jax
import jax.numpy as jnp
from jax import lax
from jax.experimental import pallas as pl
from jax.experimental.pallas import tpu as pltpu

F32 = jnp.float32
BF16 = jnp.bfloat16
HI = lax.Precision.HIGHEST

NORM_EPS = 1e-6
GN_EPS = 64e-5
ROPE_THETA = 10000.0
LANE = 128
VMEM_LIMIT = 48 * 1024 * 1024
HALO = 16

SB_HEADS = 8
RW_HEADS = 8
HEAD_DIM = 64
MLA_HEADS = 16
MLA_NOPE = 64
MLA_ROPE = 32
MLA_V = 64
MLA_Q_RANK = 384
MLA_KV_RANK = 256
SB_CUT = -104.0
RW_CHUNK = 64
RW_SUB = 16


def _cp(*sem):
    return pltpu.CompilerParams(dimension_semantics=sem, vmem_limit_bytes=VMEM_LIMIT)


def _iota(shape, dim):
    return lax.broadcasted_iota(jnp.int32, shape, dim)


def _softplus(z):
    return jnp.maximum(z, 0.0) + jnp.log1p(jnp.exp(-jnp.abs(z)))


def _sigmoid(z):
    return 1.0 / (1.0 + jnp.exp(-z))


def _dot(a, b, precision=None):
    return jnp.dot(a, b, preferred_element_type=F32, precision=precision)


def _dot_nt(a, b, precision=None):
    return lax.dot_general(a, b, (((1,), (1,)), ((), ())), preferred_element_type=F32,
                           precision=precision)


def _dot_tn(a, b, precision=None):
    return lax.dot_general(a, b, (((0,), (0,)), ((), ())), preferred_element_type=F32,
                           precision=precision)


def _rms(x, g, eps=NORM_EPS):
    return x * lax.rsqrt(jnp.mean(x * x, axis=-1, keepdims=True) + eps) * g


def _rms_matmul_kernel(x_ref, g_ref, w_ref, o_ref, xn_ref):
    @pl.when(pl.program_id(1) == 0)
    def _():
        xn_ref[...] = _rms(x_ref[...].astype(F32), g_ref[...]).astype(BF16)

    o_ref[...] = _dot(xn_ref[...], w_ref[...]).astype(o_ref.dtype)


def rms_matmul(x, g, w, *, x_col=0, tm=512, tn=512, out_dtype=F32):
    t = x.shape[0]
    k, n = w.shape
    tn = min(tn, n)
    return pl.pallas_call(
        _rms_matmul_kernel,
        grid=(t // tm, n // tn),
        in_specs=[pl.BlockSpec((tm, k), lambda i, j: (i, x_col)),
                  pl.BlockSpec((1, k), lambda i, j: (0, 0)),
                  pl.BlockSpec((k, tn), lambda i, j: (0, j))],
        out_specs=pl.BlockSpec((tm, tn), lambda i, j: (i, j)),
        out_shape=jax.ShapeDtypeStruct((t, n), out_dtype),
        scratch_shapes=[pltpu.VMEM((tm, k), BF16)],
        compiler_params=_cp("parallel", "arbitrary"),
        name="rms_matmul",
    )(x, g.reshape(1, k).astype(F32), w)


def _matmul_res_kernel(a_ref, w_ref, r_ref, o_ref):
    o_ref[...] = r_ref[...] + _dot(a_ref[...].astype(BF16), w_ref[...])


def matmul_residual(res, a, w, *, tm=512, tn=512):
    t, k = a.shape
    n = w.shape[1]
    return pl.pallas_call(
        _matmul_res_kernel,
        grid=(t // tm, n // tn),
        in_specs=[pl.BlockSpec((tm, k), lambda i, j: (i, 0)),
                  pl.BlockSpec((k, tn), lambda i, j: (0, j)),
                  pl.BlockSpec((tm, tn), lambda i, j: (i, j))],
        out_specs=pl.BlockSpec((tm, tn), lambda i, j: (i, j)),
        out_shape=jax.ShapeDtypeStruct((t, n), F32),
        compiler_params=_cp("parallel", "arbitrary"),
        name="matmul_residual",
    )(a, w, res)


def _convglu_out_kernel(u_ref, g_ref, gp_ref, cw_ref, cb_ref, w_ref, r_ref, o_ref, *, tm, seq):
    i = pl.program_id(0)
    g = g_ref[...].astype(F32)
    rows = _iota(g.shape, 0)
    at_start = (i * tm) % seq == 0
    halo = jnp.where(at_start, 0.0, gp_ref[...].astype(F32))
    h1, h2 = halo[HALO - 1:HALO, :], halo[HALO - 2:HALO - 1, :]
    p1 = jnp.where(rows == 0, h1, pltpu.roll(g, 1, axis=0))
    p2 = jnp.where(rows == 0, h2, jnp.where(rows == 1, h1, pltpu.roll(g, 2, axis=0)))
    c = cw_ref[0:1, :] * p2 + cw_ref[1:2, :] * p1 + cw_ref[2:3, :] * g + cb_ref[...]
    act = 0.5 * c * (1.0 + lax.erf(c * (1.0 / math.sqrt(2.0)))) * u_ref[...].astype(F32)
    o_ref[...] = r_ref[...] + _dot(act.astype(BF16), w_ref[...])


def convglu_out(res, ug, conv_w, conv_b, w_out, *, seq, tm=256):
    t = res.shape[0]
    f, n = w_out.shape
    hb = tm // HALO
    return pl.pallas_call(
        functools.partial(_convglu_out_kernel, tm=tm, seq=seq),
        grid=(t // tm,),
        in_specs=[pl.BlockSpec((tm, f), lambda i: (i, 0)),
                  pl.BlockSpec((tm, f), lambda i: (i, 1)),
                  pl.BlockSpec((HALO, f), lambda i: (jnp.maximum(i * hb - 1, 0), 1)),
                  pl.BlockSpec((3, f), lambda i: (0, 0)),
                  pl.BlockSpec((1, f), lambda i: (0, 0)),
                  pl.BlockSpec((f, n), lambda i: (0, 0)),
                  pl.BlockSpec((tm, n), lambda i: (i, 0))],
        out_specs=pl.BlockSpec((tm, n), lambda i: (i, 0)),
        out_shape=jax.ShapeDtypeStruct((t, n), F32),
        compiler_params=_cp("parallel"),
        name="convglu_out",
    )(ug, ug, ug, conv_w, conv_b.reshape(1, f), w_out, res)


def _ple_kernel(h_ref, p_ref, wp_ref, pn_ref, gn_ref, wg_ref, fn_ref, o_ref, *, final):
    h = h_ref[...]
    e = _rms(_dot(p_ref[...].astype(BF16), wp_ref[...]), pn_ref[...])
    gate = _sigmoid(_dot(_rms(h, gn_ref[...]).astype(BF16), wg_ref[...]))
    out = h + gate * e
    if final:
        out = _rms(out, fn_ref[...])
    o_ref[...] = out


def ple(h, p, w_proj, p_norm, g_norm, w_gate, f_norm, *, final, tm=512):
    t, d = h.shape
    pd = p.shape[1]
    row = lambda i: (i, 0)
    fix = lambda i: (0, 0)
    return pl.pallas_call(
        functools.partial(_ple_kernel, final=final),
        grid=(t // tm,),
        in_specs=[pl.BlockSpec((tm, d), row), pl.BlockSpec((tm, pd), row),
                  pl.BlockSpec((pd, d), fix), pl.BlockSpec((1, d), fix), pl.BlockSpec((1, d), fix),
                  pl.BlockSpec((d, d), fix), pl.BlockSpec((1, d), fix)],
        out_specs=pl.BlockSpec((tm, d), row),
        out_shape=jax.ShapeDtypeStruct((t, d), F32),
        compiler_params=_cp("parallel"),
        name="ple",
    )(h, p, w_proj, p_norm.reshape(1, d), g_norm.reshape(1, d), w_gate, f_norm.reshape(1, d))


def _sb_attn_kernel(q_ref, k_ref, v_ref, u_ref, o_ref, *, tq, tk, scale):
    qi = pl.program_id(2)
    nblk = tq // tk
    first = _iota((tq, LANE), 1) < HEAD_DIM
    qf = q_ref[...] * scale
    zero = jnp.zeros_like(qf)
    qs = (jnp.where(first, qf, zero), jnp.where(first, zero, qf))

    def sweep(k_lo, runs, accs, masked):
        parts = []
        for d in reversed(range(nblk)):
            k0 = k_lo + d * tk
            kb = k_ref[pl.ds(k0, tk), :]
            valid = None
            if masked:
                valid = _iota((tq, tk), 1) + d * tk < _iota((tq, tk), 0)
            for h in range(2):
                z = _dot_nt(qs[h], kb)
                sp = _softplus(z)
                lom = -sp
                if masked:
                    lom = jnp.where(valid, lom, 0.0)
                hi = lom.astype(BF16)
                lo = (lom - hi.astype(F32)).astype(BF16)
                sums = _dot(jnp.concatenate([hi, lo], axis=1), u_ref[...])
                parts.append((h, k0, z - sp, sums, valid))
        runs, accs = list(runs), list(accs)
        for h, k0, logsig, sums, valid in parts:
            w = jnp.exp(logsig + runs[h] + sums[:, :tk])
            if masked:
                w = jnp.where(valid, w, 0.0)
            accs[h] = accs[h] + _dot(w.astype(BF16), v_ref[pl.ds(k0, tk), :])
            runs[h] = runs[h] + sums[:, tk:]
        return tuple(runs), tuple(accs)

    runs = (jnp.zeros((tq, tk), F32),) * 2
    accs = (jnp.zeros((tq, LANE), F32),) * 2
    runs, accs = sweep(pl.multiple_of(qi * tq, tq), runs, accs, True)

    def live(runs):
        return jnp.max(jnp.maximum(runs[0], runs[1])) > SB_CUT

    def cond(c):
        return c[0]

    def body(c):
        _, j, runs, accs = c
        runs, accs = sweep(pl.multiple_of(j * tq, tq), runs, accs, False)
        return (j > 0) & live(runs), j - 1, runs, accs

    _, _, _, accs = lax.while_loop(cond, body, ((qi > 0) & live(runs), qi - 1, runs, accs))
    o_ref[...] = jnp.where(first, accs[0], accs[1])


def sb_attention(qkv, *, batch, heads, tq=256):
    t, w3 = qkv.shape
    s = t // batch
    nq = s // tq
    pairs = heads * HEAD_DIM // LANE
    tk = LANE
    jj = jnp.arange(2 * tk)[:, None] % tk
    ss = jnp.arange(2 * tk)[None, :]
    u = jnp.where(ss < tk, jj > ss, True).astype(BF16)
    return pl.pallas_call(
        functools.partial(_sb_attn_kernel, tq=tq, tk=tk, scale=1.0 / math.sqrt(HEAD_DIM)),
        grid=(batch, pairs, nq),
        in_specs=[pl.BlockSpec((tq, LANE), lambda b, p, i: (b * nq + i, p)),
                  pl.BlockSpec((s, LANE), lambda b, p, i: (b, pairs + p)),
                  pl.BlockSpec((s, LANE), lambda b, p, i: (b, 2 * pairs + p)),
                  pl.BlockSpec((2 * tk, 2 * tk), lambda b, p, i: (0, 0))],
        out_specs=pl.BlockSpec((tq, LANE), lambda b, p, i: (b * nq + i, p)),
        out_shape=jax.ShapeDtypeStruct((t, w3 // 3), F32),
        compiler_params=_cp("parallel", "parallel", "arbitrary"),
        name="sb_attention",
    )(qkv, qkv, qkv, u)


def _softmax_attn_kernel(qn_ref, qr_ref, kn_ref, kr_ref, v_ref, o_ref, sa0, sa1, sb0, sb1,
                         *, tq, tk, scale):
    hp = pl.program_id(1)
    qi = pl.program_id(2)
    lane = _iota((tq, LANE), 1)
    qn = qn_ref[...] * (scale * math.log2(math.e))
    qr = qr_ref[...] * (scale * math.log2(math.e))
    qs = []
    for h in range(2):
        nope = jnp.where(lane // MLA_NOPE == h, qn, 0.0)
        rope_ = jnp.where(lane // MLA_ROPE == (hp % 2) * 2 + h, qr, 0.0)
        qs.append(jnp.concatenate([nope, rope_], axis=1).astype(BF16))

    def scores(j, dsts):
        k0 = pl.multiple_of(j * tk, tk)
        kb = jnp.concatenate([kn_ref[pl.ds(k0, tk), :], kr_ref[pl.ds(k0, tk), :]], axis=1)
        for h in range(2):
            dsts[h][...] = _dot_nt(qs[h], kb)

    def consume(srcs, j, carry, masked):
        vb = v_ref[pl.ds(pl.multiple_of(j * tk, tk), tk), :]
        out = []
        for h in range(2):
            m, l, acc = carry[h]
            s = srcs[h][...]
            if masked:
                s = jnp.where(_iota((tq, tk), 1) <= _iota((tq, tk), 0), s, -1e30)
            m_new = jnp.maximum(m, jnp.max(s, axis=1, keepdims=True))
            p = jnp.exp2(s - m_new)
            alpha = jnp.exp2(m - m_new)
            l = alpha * l + jnp.sum(p, axis=1, keepdims=True)
            acc = alpha * acc + _dot(p.astype(BF16), vb)
            out.append((m_new, l, acc))
        return tuple(out)

    sa, sb = (sa0, sa1), (sb0, sb1)

    def pair(t, carry):
        scores(2 * t + 1, sb)
        carry = consume(sa, 2 * t, carry, False)
        scores(2 * t + 2, sa)
        return consume(sb, 2 * t + 1, carry, False)

    def tail_odd(carry):
        scores(qi, sb)
        carry = consume(sa, qi - 1, carry, False)
        return consume(sb, qi, carry, True)

    def tail_even(carry):
        return consume(sa, qi, carry, True)

    init = ((jnp.full((tq, 1), -1e30, F32), jnp.zeros((tq, 1), F32), jnp.zeros((tq, LANE), F32)),) * 2
    scores(0, sa)
    carry = lax.fori_loop(0, qi // 2, pair, init)
    (_, l0, acc0), (_, l1, acc1) = lax.cond(qi % 2 == 1, tail_odd, tail_even, carry)
    o_ref[...] = jnp.where(lane < MLA_V, acc0 / l0, acc1 / l1)


def softmax_attention(q, qrot, kv, krot, scale, *, batch, heads, tq=512):
    tk = tq
    t = q.shape[0]
    s = t // batch
    nq = s // tq
    pairs = heads * MLA_NOPE // LANE
    buf = pltpu.VMEM((tq, tk), F32)
    return pl.pallas_call(
        functools.partial(_softmax_attn_kernel, tq=tq, tk=tk, scale=scale),
        grid=(batch, pairs, nq),
        in_specs=[pl.BlockSpec((tq, LANE), lambda b, p, i: (b * nq + i, p)),
                  pl.BlockSpec((tq, LANE), lambda b, p, i: (b * nq + i, p // 2)),
                  pl.BlockSpec((s, LANE), lambda b, p, i: (b, p)),
                  pl.BlockSpec((s, LANE), lambda b, p, i: (b, 0)),
                  pl.BlockSpec((s, LANE), lambda b, p, i: (b, pairs + p))],
        out_specs=pl.BlockSpec((tq, LANE), lambda b, p, i: (b * nq + i, p)),
        out_shape=jax.ShapeDtypeStruct((t, heads * MLA_V), F32),
        scratch_shapes=[buf, buf, buf, buf],
        compiler_params=_cp("parallel", "parallel", "arbitrary"),
        name="softmax_attention",
    )(q, qrot, kv, krot, kv)


def _rope_kernel(q_ref, k_ref, pos_ref, f_ref, oq_ref, ok_ref):
    half = MLA_ROPE // 2
    ang = pos_ref[...].astype(F32) * f_ref[...]
    cos, sin = jnp.cos(ang), jnp.sin(ang)
    low = _iota(ang.shape, 1) % MLA_ROPE < half

    def rot(x):
        partner = jnp.where(low, -pltpu.roll(x, LANE - half, axis=1), pltpu.roll(x, half, axis=1))
        return x * cos + partner * sin

    for c in range(q_ref.shape[1] // LANE):
        cols = slice(c * LANE, (c + 1) * LANE)
        oq_ref[:, cols] = rot(q_ref[:, cols])
    ok_ref[...] = rot(k_ref[...]).astype(ok_ref.dtype)


def rope(q, c, positions, *, q_col, qw, k_col, tm=512):
    t = q.shape[0]
    half = MLA_ROPE // 2
    inv_freq = ROPE_THETA ** (-jnp.arange(half, dtype=F32) / half)
    freq = jnp.tile(inv_freq, LANE // half).reshape(1, LANE)
    return pl.pallas_call(
        _rope_kernel,
        grid=(t // tm,),
        in_specs=[pl.BlockSpec((tm, qw), lambda i: (i, q_col)), pl.BlockSpec((tm, LANE), lambda i: (i, k_col)),
                  pl.BlockSpec((tm, 1), lambda i: (i, 0)), pl.BlockSpec((1, LANE), lambda i: (0, 0))],
        out_specs=[pl.BlockSpec((tm, qw), lambda i: (i, 0)), pl.BlockSpec((tm, LANE), lambda i: (i, 0))],
        out_shape=[jax.ShapeDtypeStruct((t, qw), F32), jax.ShapeDtypeStruct((t, LANE), BF16)],
        compiler_params=_cp("parallel"),
        name="rope",
    )(q, c, positions.reshape(t, 1), freq)


def _rwkv_prep_kernel(x_ref, xp_ref, mu_ref, w0_ref, a0_ref, kk_ref, ka_ref, rk_ref,
                      w2_ref, a2_ref, g2_ref, tri_ref, hs_ref,
                      at_ref, bt_ref, kt_ref, rt_ref, v_ref, eg_ref, g_ref, bv_ref, *, tm, seq, width):
    i = pl.program_id(0)
    x = x_ref[...]
    rows = _iota(x.shape, 0)
    at_start = (i * tm) % seq == 0
    prev_row = jnp.where(at_start, 0.0, xp_ref[7:8, :])
    prev = jnp.where(rows == 0, prev_row, pltpu.roll(x, 1, axis=0))
    xm = x + (prev - x) * mu_ref[...]
    r = xm[:, :width]
    k = xm[:, width:2 * width]
    v = xm[:, 2 * width:3 * width]
    l1 = xm[:, 3 * width:3 * width + LANE]
    l2 = xm[:, 3 * width + LANE:]
    wpre = w0_ref[...] + _dot(jnp.tanh(l1), w2_ref[...], HI)
    a = _sigmoid(a0_ref[...] + _dot(l1, a2_ref[...], HI))
    g = _dot(_sigmoid(l2), g2_ref[...], HI)
    lw = -jnp.exp(-_softplus(-wpre) - 0.5)
    hs = hs_ref[...]
    kk = k * kk_ref[...]
    kk = kk * lax.rsqrt(jnp.maximum(_dot(kk * kk, hs, HI), 1e-24))
    km = k * (1.0 + (a - 1.0) * ka_ref[...])
    bonus = _dot(r * km * rk_ref[...], hs, HI)
    cum = _dot(tri_ref[...], lw, HI)
    eg = jnp.exp(cum)
    ieg = jnp.exp(-cum)
    at_ref[...] = -kk * jnp.exp(cum - lw)
    bt_ref[...] = kk * a * ieg
    kt_ref[...] = km * ieg
    rt_ref[...] = r * eg
    v_ref[...] = v
    eg_ref[...] = eg
    g_ref[...] = g
    bv_ref[...] = bonus * v


def rwkv_prep(x, mu, w0, a0, k_k, k_a, r_k, w2p, a2p, g2p, *, seq, width, tm=256):
    t, wx = x.shape
    ch = RW_CHUNK
    ti = jnp.arange(tm)
    tri = ((ti[:, None] // ch == ti[None, :] // ch) & (ti[None, :] <= ti[:, None])).astype(F32)
    ci = jnp.arange(width) // HEAD_DIM
    hs = (ci[:, None] == ci[None, :]).astype(F32)
    hb = tm // 8
    row = lambda i: (i, 0)
    fix = lambda i: (0, 0)
    vec = pl.BlockSpec((1, width), fix)
    out = jax.ShapeDtypeStruct((t, width), F32)
    return pl.pallas_call(
        functools.partial(_rwkv_prep_kernel, tm=tm, seq=seq, width=width),
        grid=(t // tm,),
        in_specs=[pl.BlockSpec((tm, wx), row),
                  pl.BlockSpec((8, wx), lambda i: (jnp.maximum(i * hb - 1, 0), 0)),
                  pl.BlockSpec((1, wx), fix), vec, vec, vec, vec, vec,
                  pl.BlockSpec(w2p.shape, fix), pl.BlockSpec(a2p.shape, fix),
                  pl.BlockSpec(g2p.shape, fix), pl.BlockSpec((tm, tm), fix),
                  pl.BlockSpec((width, width), fix)],
        out_specs=[pl.BlockSpec((tm, width), row)] * 8,
        out_shape=[out] * 8,
        compiler_params=_cp("parallel"),
        name="rwkv_prep",
    )(x, x, mu, w0, a0, k_k, k_a, r_k, w2p, a2p, g2p, tri, hs)


def _rwkv_chunk_kernel(at_ref, bt_ref, kt_ref, rt_ref, v_ref, eg_ref,
                       m_ref, n_ref, rq_ref, yl_ref, *, nchunk, unroll):
    ch, n = RW_CHUNK, HEAD_DIM
    ti = _iota((ch, ch), 0)
    si = _iota((ch, ch), 1)
    strict = si < ti
    incl = si <= ti
    same = (ti // RW_SUB) == (si // RW_SUB)
    eye = _iota((n, n), 0) == _iota((n, n), 1)
    each = lambda f, *ls: [f(*t) for t in zip(*ls)]
    mm = lambda x, y: _dot(x, y, HI)

    def body(it, _):
        cs = [it * unroll + u for u in range(unroll)]
        sls = [pl.ds(pl.multiple_of(c * ch, ch), ch) for c in cs]
        ids = [(h, u) for h in range(2) for u in range(unroll)]
        ld = lambda ref: [ref[sls[u], h * n:(h + 1) * n] for h, u in ids]
        a_, b_, k_, r_, v_ = ld(at_ref), ld(bt_ref), ld(kt_ref), ld(rt_ref), ld(v_ref)
        g_end = [eg_ref[pl.ds(cs[u] * ch + ch - 1, 1), h * n:(h + 1) * n] for h, u in ids]
        ar = each(lambda a, r: jnp.concatenate([a, r], axis=0), a_, r_)
        pb = each(lambda x, y: _dot_nt(x, y, HI), ar, b_)
        pk = each(lambda x, y: _dot_nt(x, y, HI), ar, k_)
        a_ab = [jnp.where(strict, t[:ch], 0.0) for t in pb]
        a_ak = [jnp.where(strict, t[:ch], 0.0) for t in pk]
        a_rb = [jnp.where(incl, t[ch:], 0.0) for t in pb]
        a_rk = [jnp.where(incl, t[ch:], 0.0) for t in pk]
        d1 = [jnp.where(same, t, 0.0) for t in a_ab]
        lb = each(lambda t, d: t - d, a_ab, d1)
        d2 = each(mm, d1, d1)
        akv = each(mm, a_ak, v_)
        x = each(lambda l, a, t: jnp.concatenate([l, a, t], axis=1), lb, a_, akv)
        d4 = each(mm, d2, d2)
        x = each(lambda d, t: t + mm(d, t), d1, x)
        d8 = each(mm, d4, d4)
        x = each(lambda d, t: t + mm(d, t), d2, x)
        x = each(lambda d, t: t + mm(d, t), d4, x)
        x = each(lambda d, t: t + mm(d, t), d8, x)
        e1 = [t[:, :ch] for t in x]
        wu = [t[:, ch:] for t in x]
        e2 = each(mm, e1, e1)
        wu = each(lambda e, t: t + mm(e, t), e1, wu)
        wu = each(lambda e, t: t + mm(e, t), e2, wu)
        ry = each(mm, a_rb, wu)
        rkv = each(mm, a_rk, v_)
        mn = each(lambda b, g, t: _dot_tn(b * g, t, HI), b_, g_end, wu)
        kv = each(lambda k, g, t: _dot_tn(k * g, t, HI), k_, g_end, v_)
        rq = each(lambda r, t: r + t[:, :n], r_, ry)
        yl = each(lambda t, w: t[:, n:] + w, ry, rkv)
        for u in range(unroll):
            rq_ref[sls[u], :] = jnp.concatenate([rq[u], rq[unroll + u]], axis=1)
            yl_ref[sls[u], :] = jnp.concatenate([yl[u], yl[unroll + u]], axis=1)
        for i, (h, u) in enumerate(ids):
            m_ref[h, cs[u]] = mn[i][:, :n] + jnp.where(eye, g_end[i], 0.0)
            n_ref[h, cs[u]] = mn[i][:, n:] + kv[i]
        return 0

    lax.fori_loop(0, nchunk // unroll, body, 0)


def rwkv_chunk(at, bt, kt, rt, v, eg, *, batch, nchunk=8, unroll=4):
    t, w = at.shape
    s = t // batch
    ch, n = RW_CHUNK, HEAD_DIM
    pairs = w // LANE
    rows = nchunk * ch
    nr = s // rows
    spec = pl.BlockSpec((rows, LANE), lambda b, p, i: (b * nr + i, p))
    mspec = pl.BlockSpec((2, nchunk, n, n), lambda b, p, i: (b * pairs + p, i, 0, 0))
    return pl.pallas_call(
        functools.partial(_rwkv_chunk_kernel, nchunk=nchunk, unroll=unroll),
        grid=(batch, pairs, nr),
        in_specs=[spec] * 6,
        out_specs=[mspec, mspec, spec, spec],
        out_shape=[jax.ShapeDtypeStruct((batch * w // n, s // ch, n, n), F32)] * 2
        + [jax.ShapeDtypeStruct((t, w), F32)] * 2,
        compiler_params=_cp("parallel", "parallel", "parallel"),
        name="rwkv_chunk",
    )(at, bt, kt, rt, v, eg)


def _rwkv_scan_kernel(m_ref, n_ref, s_ref, st_ref, *, heads, nchunk):
    @pl.when(pl.program_id(1) == 0)
    def _():
        st_ref[...] = jnp.zeros_like(st_ref)

    def body(c, _):
        for hh in range(heads):
            st = st_ref[hh]
            s_ref[hh, c] = st
            st_ref[hh] = _dot(m_ref[hh, c], st, HI) + n_ref[hh, c]
        return 0

    lax.fori_loop(0, nchunk, body, 0)


def rwkv_scan(m, nn, *, heads=4, nchunk=32):
    bh, nc, n, _ = m.shape
    nchunk = min(nchunk, nc)
    spec = pl.BlockSpec((heads, nchunk, n, n), lambda b, i: (b, i, 0, 0))
    return pl.pallas_call(
        functools.partial(_rwkv_scan_kernel, heads=heads, nchunk=nchunk),
        grid=(bh // heads, nc // nchunk),
        in_specs=[spec, spec],
        out_specs=spec,
        out_shape=jax.ShapeDtypeStruct((bh, nc, n, n), F32),
        scratch_shapes=[pltpu.VMEM((heads, n, n), F32)],
        compiler_params=_cp("parallel", "arbitrary"),
        name="rwkv_scan",
    )(m, nn)


def _rwkv_out_kernel(rq_ref, yl_ref, s_ref, lw_ref, lb_ref, avg_ref, o_ref, *, nchunk):
    ch, n = RW_CHUNK, HEAD_DIM
    ys = []
    for h in range(2):
        cols = slice(h * n, (h + 1) * n)
        ys.append(jnp.concatenate(
            [_dot(rq_ref[c * ch:(c + 1) * ch, cols], s_ref[h, c], HI) for c in range(nchunk)], axis=0))
    y = jnp.concatenate(ys, axis=1) + yl_ref[...]
    avg = avg_ref[...]
    d = y - _dot(y, avg, HI)
    var = _dot(d * d, avg, HI)
    o_ref[...] = d * lax.rsqrt(var + GN_EPS) * lw_ref[...] + lb_ref[...]


def rwkv_out(rq, yl, s0, ln_w, ln_b, *, batch, nchunk=8):
    t, w = rq.shape
    s = t // batch
    n = HEAD_DIM
    pairs = w // LANE
    rows = nchunk * RW_CHUNK
    nr = s // rows
    li = jnp.arange(LANE) // n
    avg = (li[:, None] == li[None, :]).astype(F32) / n
    spec = pl.BlockSpec((rows, LANE), lambda b, p, i: (b * nr + i, p))
    hspec = pl.BlockSpec((1, LANE), lambda b, p, i: (0, p))
    return pl.pallas_call(
        functools.partial(_rwkv_out_kernel, nchunk=nchunk),
        grid=(batch, pairs, nr),
        in_specs=[spec, spec, pl.BlockSpec((2, nchunk, n, n), lambda b, p, i: (b * pairs + p, i, 0, 0)),
                  hspec, hspec, pl.BlockSpec((LANE, LANE), lambda b, p, i: (0, 0))],
        out_specs=spec,
        out_shape=jax.ShapeDtypeStruct((t, w), F32),
        compiler_params=_cp("parallel", "parallel", "parallel"),
        name="rwkv_out",
    )(rq, yl, s0, ln_w.reshape(1, w), ln_b.reshape(1, w), avg)


def _hyb_out_kernel(oa_ref, y_ref, bv_ref, g_ref, wa_ref, wb_ref, r_ref, o_ref):
    ob = (y_ref[...] + bv_ref[...]) * g_ref[...]
    o_ref[...] = (r_ref[...] + _dot(oa_ref[...].astype(BF16), wa_ref[...])
                  + _dot(ob.astype(BF16), wb_ref[...]))


def hyb_out(res, oa, y, bv, g, wa, wb, *, tm=512):
    t, d = res.shape
    row = lambda i: (i, 0)
    fix = lambda i: (0, 0)
    act = lambda a: pl.BlockSpec((tm, a.shape[1]), row)
    return pl.pallas_call(
        _hyb_out_kernel,
        grid=(t // tm,),
        in_specs=[act(oa), act(y), act(bv), act(g), pl.BlockSpec(wa.shape, fix),
                  pl.BlockSpec(wb.shape, fix), pl.BlockSpec((tm, d), row)],
        out_specs=pl.BlockSpec((tm, d), row),
        out_shape=jax.ShapeDtypeStruct((t, d), F32),
        compiler_params=_cp("parallel"),
        name="hyb_out",
    )(oa, y, bv, g, wa, wb, res)


def _pad_cols(w, n):
    return jnp.pad(w, ((0, 0), (0, n - w.shape[1])))


def sb_rwkv_layer(h, b, s, attn_norm, w_in, w_out, mu, w0, w2, a0, a2, g2, k_k, k_a, r_k, ln_w, ln_b):
    sbw = SB_HEADS * HEAD_DIM
    rww = RW_HEADS * HEAD_DIM
    rw_in = w_in.shape[1] - 3 * sbw
    rw_pad = -(-rw_in // LANE) * LANE
    dl, al = w2.shape[0], a2.shape[0]
    gl = g2.shape[0]
    assert dl + al == LANE and 3 * rww % LANE == 0

    qkv = rms_matmul(h, attn_norm, w_in[:, :3 * sbw].astype(BF16), out_dtype=BF16)
    rw = rms_matmul(h, attn_norm, _pad_cols(w_in[:, 3 * sbw:], rw_pad).astype(BF16), tn=rw_pad // 3)
    oa = sb_attention(qkv, batch=b, heads=SB_HEADS)

    w2p = jnp.zeros((LANE, rww), F32).at[:dl].set(w2)
    a2p = jnp.zeros((LANE, rww), F32).at[dl:].set(a2)
    g2p = jnp.zeros((rw_pad - 3 * rww - LANE, rww), F32).at[:gl].set(g2)
    vec = lambda t: t.reshape(1, rww).astype(F32)
    mu_p = _pad_cols(mu.reshape(1, rw_in), rw_pad)
    at, bt, kt, rt, vv, eg, g, bv = rwkv_prep(rw, mu_p, vec(w0), vec(a0), vec(k_k), vec(k_a), vec(r_k),
                                              w2p, a2p, g2p, seq=s, width=rww)
    m, nn, rq, yl = rwkv_chunk(at, bt, kt, rt, vv, eg, batch=b)
    s0 = rwkv_scan(m, nn)
    y = rwkv_out(rq, yl, s0, ln_w, ln_b, batch=b)
    return hyb_out(h, oa, y, bv, g, w_out[:sbw].astype(BF16), w_out[sbw:].astype(BF16))


def mla_layer(h, b, s, positions, attn_norm, w_down, q_norm, kv_norm, w_uq, w_ukv, w_o):
    nh, dn, dr, dv = MLA_HEADS, MLA_NOPE, MLA_ROPE, MLA_V
    qr, kvr = MLA_Q_RANK, MLA_KV_RANK
    assert dn == dv and qr % LANE == 0 and (qr + LANE) % kvr == 0
    wd = jnp.concatenate([w_down[:, :qr], jnp.tile(w_down[:, qr + kvr:], (1, LANE // dr)),
                          w_down[:, qr:qr + kvr]], axis=1)
    c = rms_matmul(h, attn_norm, wd.astype(BF16), tn=wd.shape[1])
    wq = w_uq.reshape(qr, nh, dn + dr)
    wq = jnp.concatenate([wq[:, :, :dn].reshape(qr, nh * dn), wq[:, :, dn:].reshape(qr, nh * dr)], axis=1)
    q = rms_matmul(c, q_norm, wq.astype(BF16), x_col=0)
    wkv = w_ukv.reshape(kvr, nh, dn + dv)
    wkv = jnp.concatenate([wkv[:, :, :dn].reshape(kvr, nh * dn), wkv[:, :, dn:].reshape(kvr, nh * dv)], axis=1)
    kv = rms_matmul(c, kv_norm, wkv.astype(BF16), x_col=(qr + LANE) // kvr, out_dtype=BF16)
    qrot, krot = rope(q, c, positions, q_col=dn // dr, qw=nh * dr, k_col=qr // LANE)
    o = softmax_attention(q, qrot, kv, krot, 1.0 / math.sqrt(dn + dr), batch=b, heads=nh)
    return matmul_residual(h, o, w_o.astype(BF16))


def kernel(x, p, positions, attn_norm, ffn_norm, ffn_w_in, ffn_conv_w, ffn_conv_b, ffn_w_out, ple_w_proj, ple_norm, ple_gate_norm, ple_w_gate, hyb_w_in, hyb_w_out, rw_mu, rw_w0, rw_w2, rw_a0, rw_a2, rw_g2, rw_k_k, rw_k_a, rw_r_k, rw_ln_w, rw_ln_b, mla_w_down, mla_q_norm, mla_kv_norm, mla_w_uq, mla_w_ukv, mla_w_o, final_norm):
    b, s, d = x.shape
    depth = p.shape[0]
    h = x.reshape(b * s, d)
    pos = positions.reshape(b * s)
    for i in range(depth):
        j = i // 2
        if i % 2 == 0:
            h = sb_rwkv_layer(h, b, s, attn_norm[i], hyb_w_in[j], hyb_w_out[j], rw_mu[j], rw_w0[j],
                              rw_w2[j], rw_a0[j], rw_a2[j], rw_g2[j], rw_k_k[j], rw_k_a[j], rw_r_k[j],
                              rw_ln_w[j], rw_ln_b[j])
        else:
            h = mla_layer(h, b, s, pos, attn_norm[i], mla_w_down[j], mla_q_norm[j], mla_kv_norm[j],
                          mla_w_uq[j], mla_w_ukv[j], mla_w_o[j])
        ug = rms_matmul(h, ffn_norm[i], ffn_w_in[i].astype(BF16), tm=1024, out_dtype=BF16)
        h = convglu_out(h, ug, ffn_conv_w[i], ffn_conv_b[i], ffn_w_out[i].astype(BF16), seq=s)
        h = ple(h, p[i].reshape(b * s, -1), ple_w_proj[i].astype(BF16), ple_norm[i], ple_gate_norm[i],
                ple_w_gate[i].astype(BF16), final_norm, final=(i == depth - 1))
    return h.reshape(b, s, d)
```

```python
import functools
import math

import jax
import jax.numpy as jnp
from jax import lax
from jax.experimental import pallas as pl
from jax.experimental.pallas import tpu as pltpu

F32 = jnp.float32
BF16 = jnp.bfloat16
HI = lax.Precision.HIGHEST

NORM_EPS = 1e-6
GN_EPS = 64e-5
ROPE_THETA = 10000.0
LANE = 128
VMEM_LIMIT = 48 * 1024 * 1024
HALO = 16

SB_HEADS = 8
RW_HEADS = 8
HEAD_DIM = 64
MLA_HEADS = 16
MLA_NOPE = 64
MLA_ROPE = 32
MLA_V = 64
MLA_Q_RANK = 384
MLA_KV_RANK = 256
SB_CUT = -104.0
RW_CHUNK = 64
RW_SUB = 16


def _cp(*sem):
    return pltpu.CompilerParams(dimension_semantics=sem, vmem_limit_bytes=VMEM_LIMIT)


def _iota(shape, dim):
    return lax.broadcasted_iota(jnp.int32, shape, dim)


def _softplus(z):
    return jnp.maximum(z, 0.0) + jnp.log1p(jnp.exp(-jnp.abs(z)))


def _sigmoid(z):
    return 1.0 / (1.0 + jnp.exp(-z))


def _dot(a, b, precision=None):
    return jnp.dot(a, b, preferred_element_type=F32, precision=precision)


def _dot_nt(a, b, precision=None):
    return lax.dot_general(a, b, (((1,), (1,)), ((), ())), preferred_element_type=F32,
                           precision=precision)


def _dot_tn(a, b, precision=None):
    return lax.dot_general(a, b, (((0,), (0,)), ((), ())), preferred_element_type=F32,
                           precision=precision)


class _Split:
    def __init__(self, x):
        self.x = x
        self.hi = x.astype(BF16)
        self.lo = (x - self.hi.astype(F32)).astype(BF16)
        self._packed = {}

    def left(self, axis):
        key = ("l", axis)
        if key not in self._packed:
            self._packed[key] = jnp.concatenate([self.hi, self.lo, self.hi], axis=axis)
        return self._packed[key]

    def right(self, axis):
        key = ("r", axis)
        if key not in self._packed:
            self._packed[key] = jnp.concatenate([self.hi, self.hi, self.lo], axis=axis)
        return self._packed[key]


def _mm3(a, b):
    return _dot(a.left(1), b.right(0))


def _mm3_nt(a, b):
    return _dot_nt(a.left(1), b.right(1))


def _mm3_tn(a, b):
    return _dot_tn(a.left(0), b.right(0))


def _rms(x, g, eps=NORM_EPS):
    return x * lax.rsqrt(jnp.mean(x * x, axis=-1, keepdims=True) + eps) * g


def _rms_matmul_kernel(x_ref, g_ref, w_ref, o_ref, xn_ref):
    @pl.when(pl.program_id(1) == 0)
    def _():
        xn_ref[...] = _rms(x_ref[...].astype(F32), g_ref[...]).astype(BF16)

    o_ref[...] = _dot(xn_ref[...], w_ref[...]).astype(o_ref.dtype)


def rms_matmul(x, g, w, *, x_col=0, tm=512, tn=512, out_dtype=F32):
    t = x.shape[0]
    k, n = w.shape
    tn = min(tn, n)
    return pl.pallas_call(
        _rms_matmul_kernel,
        grid=(t // tm, n // tn),
        in_specs=[pl.BlockSpec((tm, k), lambda i, j: (i, x_col)),
                  pl.BlockSpec((1, k), lambda i, j: (0, 0)),
                  pl.BlockSpec((k, tn), lambda i, j: (0, j))],
        out_specs=pl.BlockSpec((tm, tn), lambda i, j: (i, j)),
        out_shape=jax.ShapeDtypeStruct((t, n), out_dtype),
        scratch_shapes=[pltpu.VMEM((tm, k), BF16)],
        compiler_params=_cp("parallel", "arbitrary"),
        name="rms_matmul",
    )(x, g.reshape(1, k).astype(F32), w)


def _matmul_res_kernel(a_ref, w_ref, r_ref, o_ref):
    o_ref[...] = r_ref[...] + _dot(a_ref[...].astype(BF16), w_ref[...])


def matmul_residual(res, a, w, *, tm=512, tn=512):
    t, k = a.shape
    n = w.shape[1]
    return pl.pallas_call(
        _matmul_res_kernel,
        grid=(t // tm, n // tn),
        in_specs=[pl.BlockSpec((tm, k), lambda i, j: (i, 0)),
                  pl.BlockSpec((k, tn), lambda i, j: (0, j)),
                  pl.BlockSpec((tm, tn), lambda i, j: (i, j))],
        out_specs=pl.BlockSpec((tm, tn), lambda i, j: (i, j)),
        out_shape=jax.ShapeDtypeStruct((t, n), F32),
        compiler_params=_cp("parallel", "arbitrary"),
        name="matmul_residual",
    )(a, w, res)


def _convglu_out_kernel(u_ref, g_ref, gp_ref, cw_ref, cb_ref, w_ref, r_ref, o_ref, *, tm, seq):
    i = pl.program_id(0)
    g = g_ref[...].astype(F32)
    rows = _iota(g.shape, 0)
    at_start = (i * tm) % seq == 0
    halo = jnp.where(at_start, 0.0, gp_ref[...].astype(F32))
    h1, h2 = halo[HALO - 1:HALO, :], halo[HALO - 2:HALO - 1, :]
    p1 = jnp.where(rows == 0, h1, pltpu.roll(g, 1, axis=0))
    p2 = jnp.where(rows == 0, h2, jnp.where(rows == 1, h1, pltpu.roll(g, 2, axis=0)))
    c = cw_ref[0:1, :] * p2 + cw_ref[1:2, :] * p1 + cw_ref[2:3, :] * g + cb_ref[...]
    act = 0.5 * c * (1.0 + lax.erf(c * (1.0 / math.sqrt(2.0)))) * u_ref[...].astype(F32)
    o_ref[...] = r_ref[...] + _dot(act.astype(BF16), w_ref[...])


def convglu_out(res, ug, conv_w, conv_b, w_out, *, seq, tm=256):
    t = res.shape[0]
    f, n = w_out.shape
    hb = tm // HALO
    return pl.pallas_call(
        functools.partial(_convglu_out_kernel, tm=tm, seq=seq),
        grid=(t // tm,),
        in_specs=[pl.BlockSpec((tm, f), lambda i: (i, 0)),
                  pl.BlockSpec((tm, f), lambda i: (i, 1)),
                  pl.BlockSpec((HALO, f), lambda i: (jnp.maximum(i * hb - 1, 0), 1)),
                  pl.BlockSpec((3, f), lambda i: (0, 0)),
                  pl.BlockSpec((1, f), lambda i: (0, 0)),
                  pl.BlockSpec((f, n), lambda i: (0, 0)),
                  pl.BlockSpec((tm, n), lambda i: (i, 0))],
        out_specs=pl.BlockSpec((tm, n), lambda i: (i, 0)),
        out_shape=jax.ShapeDtypeStruct((t, n), F32),
        compiler_params=_cp("parallel"),
        name="convglu_out",
    )(ug, ug, ug, conv_w, conv_b.reshape(1, f), w_out, res)


def _ple_kernel(h_ref, p_ref, wp_ref, pn_ref, gn_ref, wg_ref, fn_ref, o_ref, *, final):
    h = h_ref[...]
    e = _rms(_dot(p_ref[...].astype(BF16), wp_ref[...]), pn_ref[...])
    gate = _sigmoid(_dot(_rms(h, gn_ref[...]).astype(BF16), wg_ref[...]))
    out = h + gate * e
    if final:
        out = _rms(out, fn_ref[...])
    o_ref[...] = out


def ple(h, p, w_proj, p_norm, g_norm, w_gate, f_norm, *, final, tm=512):
    t, d = h.shape
    pd = p.shape[1]
    row = lambda i: (i, 0)
    fix = lambda i: (0, 0)
    return pl.pallas_call(
        functools.partial(_ple_kernel, final=final),
        grid=(t // tm,),
        in_specs=[pl.BlockSpec((tm, d), row), pl.BlockSpec((tm, pd), row),
                  pl.BlockSpec((pd, d), fix), pl.BlockSpec((1, d), fix), pl.BlockSpec((1, d), fix),
                  pl.BlockSpec((d, d), fix), pl.BlockSpec((1, d), fix)],
        out_specs=pl.BlockSpec((tm, d), row),
        out_shape=jax.ShapeDtypeStruct((t, d), F32),
        compiler_params=_cp("parallel"),
        name="ple",
    )(h, p, w_proj, p_norm.reshape(1, d), g_norm.reshape(1, d), w_gate, f_norm.reshape(1, d))


def _sb_attn_kernel(q_ref, k_ref, v_ref, u_ref, o_ref, *, tq, tk, scale):
    qi = pl.program_id(2)
    nblk = tq // tk
    first = _iota((tq, LANE), 1) < HEAD_DIM
    qf = q_ref[...] * scale
    zero = jnp.zeros_like(qf)
    qs = (jnp.where(first, qf, zero), jnp.where(first, zero, qf))

    def sweep(k_lo, runs, accs, masked):
        parts = []
        for d in reversed(range(nblk)):
            k0 = k_lo + d * tk
            kb = k_ref[pl.ds(k0, tk), :]
            valid = None
            if masked:
                valid = _iota((tq, tk), 1) + d * tk < _iota((tq, tk), 0)
            for h in range(2):
                z = _dot_nt(qs[h], kb)
                sp = _softplus(z)
                lom = -sp
                if masked:
                    lom = jnp.where(valid, lom, 0.0)
                hi = lom.astype(BF16)
                lo = (lom - hi.astype(F32)).astype(BF16)
                sums = _dot(jnp.concatenate([hi, lo], axis=1), u_ref[...])
                parts.append((h, k0, z - sp, sums, valid))
        runs, accs = list(runs), list(accs)
        for h, k0, logsig, sums, valid in parts:
            w = jnp.exp(logsig + runs[h] + sums[:, :tk])
            if masked:
                w = jnp.where(valid, w, 0.0)
            accs[h] = accs[h] + _dot(w.astype(BF16), v_ref[pl.ds(k0, tk), :])
            runs[h] = runs[h] + sums[:, tk:]
        return tuple(runs), tuple(accs)

    runs = (jnp.zeros((tq, tk), F32),) * 2
    accs = (jnp.zeros((tq, LANE), F32),) * 2
    runs, accs = sweep(pl.multiple_of(qi * tq, tq), runs, accs, True)

    def live(runs):
        return jnp.max(jnp.maximum(runs[0], runs[1])) > SB_CUT

    def cond(c):
        return c[0]

    def body(c):
        _, j, runs, accs = c
        runs, accs = sweep(pl.multiple_of(j * tq, tq), runs, accs, False)
        return (j > 0) & live(runs), j - 1, runs, accs

    _, _, _, accs = lax.while_loop(cond, body, ((qi > 0) & live(runs), qi - 1, runs, accs))
    o_ref[...] = jnp.where(first, accs[0], accs[1])


def sb_attention(qkv, *, batch, heads, tq=256):
    t, w3 = qkv.shape
    s = t // batch
    nq = s // tq
    pairs = heads * HEAD_DIM // LANE
    tk = LANE
    jj = jnp.arange(2 * tk)[:, None] % tk
    ss = jnp.arange(2 * tk)[None, :]
    u = jnp.where(ss < tk, jj > ss, True).astype(BF16)
    return pl.pallas_call(
        functools.partial(_sb_attn_kernel, tq=tq, tk=tk, scale=1.0 / math.sqrt(HEAD_DIM)),
        grid=(batch, pairs, nq),
        in_specs=[pl.BlockSpec((tq, LANE), lambda b, p, i: (b * nq + i, p)),
                  pl.BlockSpec((s, LANE), lambda b, p, i: (b, pairs + p)),
                  pl.BlockSpec((s, LANE), lambda b, p, i: (b, 2 * pairs + p)),
                  pl.BlockSpec((2 * tk, 2 * tk), lambda b, p, i: (0, 0))],
        out_specs=pl.BlockSpec((tq, LANE), lambda b, p, i: (b * nq + i, p)),
        out_shape=jax.ShapeDtypeStruct((t, w3 // 3), F32),
        compiler_params=_cp("parallel", "parallel", "arbitrary"),
        name="sb_attention",
    )(qkv, qkv, qkv, u)


def _softmax_attn_kernel(qn_ref, qr_ref, kn_ref, kr_ref, v_ref, o_ref, sa0, sa1, sb0, sb1,
                         *, tq, tk, scale):
    hp = pl.program_id(1)
    qi = pl.program_id(2)
    lane = _iota((tq, LANE), 1)
    qn = qn_ref[...] * (scale * math.log2(math.e))
    qr = qr_ref[...] * (scale * math.log2(math.e))
    qs = []
    for h in range(2):
        nope = jnp.where(lane // MLA_NOPE == h, qn, 0.0)
        rope_ = jnp.where(lane // MLA_ROPE == (hp % 2) * 2 + h, qr, 0.0)
        qs.append(jnp.concatenate([nope, rope_], axis=1).astype(BF16))

    def scores(j, dsts):
        k0 = pl.multiple_of(j * tk, tk)
        kb = jnp.concatenate([kn_ref[pl.ds(k0, tk), :], kr_ref[pl.ds(k0, tk), :]], axis=1)
        for h in range(2):
            dsts[h][...] = _dot_nt(qs[h], kb)

    def consume(srcs, j, carry, masked):
        vb = v_ref[pl.ds(pl.multiple_of(j * tk, tk), tk), :]
        out = []
        for h in range(2):
            m, l, acc = carry[h]
            s = srcs[h][...]
            if masked:
                s = jnp.where(_iota((tq, tk), 1) <= _iota((tq, tk), 0), s, -1e30)
            m_new = jnp.maximum(m, jnp.max(s, axis=1, keepdims=True))
            p = jnp.exp2(s - m_new)
            alpha = jnp.exp2(m - m_new)
            l = alpha * l + jnp.sum(p, axis=1, keepdims=True)
            acc = alpha * acc + _dot(p.astype(BF16), vb)
            out.append((m_new, l, acc))
        return tuple(out)

    sa, sb = (sa0, sa1), (sb0, sb1)

    def pair(t, carry):
        scores(2 * t + 1, sb)
        carry = consume(sa, 2 * t, carry, False)
        scores(2 * t + 2, sa)
        return consume(sb, 2 * t + 1, carry, False)

    def tail_odd(carry):
        scores(qi, sb)
        carry = consume(sa, qi - 1, carry, False)
        return consume(sb, qi, carry, True)

    def tail_even(carry):
        return consume(sa, qi, carry, True)

    init = ((jnp.full((tq, 1), -1e30, F32), jnp.zeros((tq, 1), F32), jnp.zeros((tq, LANE), F32)),) * 2
    scores(0, sa)
    carry = lax.fori_loop(0, qi // 2, pair, init)
    (_, l0, acc0), (_, l1, acc1) = lax.cond(qi % 2 == 1, tail_odd, tail_even, carry)
    o_ref[...] = jnp.where(lane < MLA_V, acc0 / l0, acc1 / l1)


def softmax_attention(q, qrot, kv, krot, scale, *, batch, heads, tq=512):
    tk = tq
    t = q.shape[0]
    s = t // batch
    nq = s // tq
    pairs = heads * MLA_NOPE // LANE
    buf = pltpu.VMEM((tq, tk), F32)
    return pl.pallas_call(
        functools.partial(_softmax_attn_kernel, tq=tq, tk=tk, scale=scale),
        grid=(batch, pairs, nq),
        in_specs=[pl.BlockSpec((tq, LANE), lambda b, p, i: (b * nq + i, p)),
                  pl.BlockSpec((tq, LANE), lambda b, p, i: (b * nq + i, p // 2)),
                  pl.BlockSpec((s, LANE), lambda b, p, i: (b, p)),
                  pl.BlockSpec((s, LANE), lambda b, p, i: (b, 0)),
                  pl.BlockSpec((s, LANE), lambda b, p, i: (b, pairs + p))],
        out_specs=pl.BlockSpec((tq, LANE), lambda b, p, i: (b * nq + i, p)),
        out_shape=jax.ShapeDtypeStruct((t, heads * MLA_V), F32),
        scratch_shapes=[buf, buf, buf, buf],
        compiler_params=_cp("parallel", "parallel", "arbitrary"),
        name="softmax_attention",
    )(q, qrot, kv, krot, kv)


def _rope_kernel(q_ref, k_ref, pos_ref, f_ref, oq_ref, ok_ref):
    half = MLA_ROPE // 2
    ang = pos_ref[...].astype(F32) * f_ref[...]
    cos, sin = jnp.cos(ang), jnp.sin(ang)
    low = _iota(ang.shape, 1) % MLA_ROPE < half

    def rot(x):
        partner = jnp.where(low, -pltpu.roll(x, LANE - half, axis=1), pltpu.roll(x, half, axis=1))
        return x * cos + partner * sin

    for c in range(q_ref.shape[1] // LANE):
        cols = slice(c * LANE, (c + 1) * LANE)
        oq_ref[:, cols] = rot(q_ref[:, cols])
    ok_ref[...] = rot(k_ref[...]).astype(ok_ref.dtype)


def rope(q, c, positions, *, q_col, qw, k_col, tm=512):
    t = q.shape[0]
    half = MLA_ROPE // 2
    inv_freq = ROPE_THETA ** (-jnp.arange(half, dtype=F32) / half)
    freq = jnp.tile(inv_freq, LANE // half).reshape(1, LANE)
    return pl.pallas_call(
        _rope_kernel,
        grid=(t // tm,),
        in_specs=[pl.BlockSpec((tm, qw), lambda i: (i, q_col)), pl.BlockSpec((tm, LANE), lambda i: (i, k_col)),
                  pl.BlockSpec((tm, 1), lambda i: (i, 0)), pl.BlockSpec((1, LANE), lambda i: (0, 0))],
        out_specs=[pl.BlockSpec((tm, qw), lambda i: (i, 0)), pl.BlockSpec((tm, LANE), lambda i: (i, 0))],
        out_shape=[jax.ShapeDtypeStruct((t, qw), F32), jax.ShapeDtypeStruct((t, LANE), BF16)],
        compiler_params=_cp("parallel"),
        name="rope",
    )(q, c, positions.reshape(t, 1), freq)


def _rwkv_prep_kernel(x_ref, xp_ref, mu_ref, w0_ref, a0_ref, kk_ref, ka_ref, rk_ref,
                      w2_ref, a2_ref, g2_ref, tri_ref, hs_ref,
                      at_ref, bt_ref, kt_ref, rt_ref, v_ref, eg_ref, g_ref, bv_ref, *, tm, seq, width):
    i = pl.program_id(0)
    x = x_ref[...]
    rows = _iota(x.shape, 0)
    at_start = (i * tm) % seq == 0
    prev_row = jnp.where(at_start, 0.0, xp_ref[7:8, :])
    prev = jnp.where(rows == 0, prev_row, pltpu.roll(x, 1, axis=0))
    xm = x + (prev - x) * mu_ref[...]
    r = xm[:, :width]
    k = xm[:, width:2 * width]
    v = xm[:, 2 * width:3 * width]
    l1 = xm[:, 3 * width:3 * width + LANE]
    l2 = xm[:, 3 * width + LANE:]
    wpre = w0_ref[...] + _dot(jnp.tanh(l1), w2_ref[...], HI)
    a = _sigmoid(a0_ref[...] + _dot(l1, a2_ref[...], HI))
    g = _dot(_sigmoid(l2), g2_ref[...], HI)
    lw = -jnp.exp(-_softplus(-wpre) - 0.5)
    hs = hs_ref[...]
    kk = k * kk_ref[...]
    kk = kk * lax.rsqrt(jnp.maximum(_dot(kk * kk, hs, HI), 1e-24))
    km = k * (1.0 + (a - 1.0) * ka_ref[...])
    bonus = _dot(r * km * rk_ref[...], hs, HI)
    cum = _dot(tri_ref[...], lw, HI)
    eg = jnp.exp(cum)
    ieg = jnp.exp(-cum)
    at_ref[...] = -kk * jnp.exp(cum - lw)
    bt_ref[...] = kk * a * ieg
    kt_ref[...] = km * ieg
    rt_ref[...] = r * eg
    v_ref[...] = v
    eg_ref[...] = eg
    g_ref[...] = g
    bv_ref[...] = bonus * v


def rwkv_prep(x, mu, w0, a0, k_k, k_a, r_k, w2p, a2p, g2p, *, seq, width, tm=256):
    t, wx = x.shape
    ch = RW_CHUNK
    ti = jnp.arange(tm)
    tri = ((ti[:, None] // ch == ti[None, :] // ch) & (ti[None, :] <= ti[:, None])).astype(F32)
    ci = jnp.arange(width) // HEAD_DIM
    hs = (ci[:, None] == ci[None, :]).astype(F32)
    hb = tm // 8
    row = lambda i: (i, 0)
    fix = lambda i: (0, 0)
    vec = pl.BlockSpec((1, width), fix)
    out = jax.ShapeDtypeStruct((t, width), F32)
    return pl.pallas_call(
        functools.partial(_rwkv_prep_kernel, tm=tm, seq=seq, width=width),
        grid=(t // tm,),
        in_specs=[pl.BlockSpec((tm, wx), row),
                  pl.BlockSpec((8, wx), lambda i: (jnp.maximum(i * hb - 1, 0), 0)),
                  pl.BlockSpec((1, wx), fix), vec, vec, vec, vec, vec,
                  pl.BlockSpec(w2p.shape, fix), pl.BlockSpec(a2p.shape, fix),
                  pl.BlockSpec(g2p.shape, fix), pl.BlockSpec((tm, tm), fix),
                  pl.BlockSpec((width, width), fix)],
        out_specs=[pl.BlockSpec((tm, width), row)] * 8,
        out_shape=[out] * 8,
        compiler_params=_cp("parallel"),
        name="rwkv_prep",
    )(x, x, mu, w0, a0, k_k, k_a, r_k, w2p, a2p, g2p, tri, hs)


def _rwkv_chunk_kernel(at_ref, bt_ref, kt_ref, rt_ref, v_ref, eg_ref,
                       m_ref, n_ref, rq_ref, yl_ref, *, nchunk, unroll):
    ch, n = RW_CHUNK, HEAD_DIM
    ti = _iota((ch, ch), 0)
    si = _iota((ch, ch), 1)
    strict = si < ti
    incl = si <= ti
    same = (ti // RW_SUB) == (si // RW_SUB)
    eye = _iota((n, n), 0) == _iota((n, n), 1)
    each = lambda f, *ls: [f(*t) for t in zip(*ls)]

    def body(it, _):
        cs = [it * unroll + u for u in range(unroll)]
        sls = [pl.ds(pl.multiple_of(c * ch, ch), ch) for c in cs]
        ids = [(h, u) for h in range(2) for u in range(unroll)]
        ld = lambda ref: [ref[sls[u], h * n:(h + 1) * n] for h, u in ids]
        sp = lambda xs: [_Split(t) for t in xs]
        a_, b_, k_, r_, v_ = ld(at_ref), ld(bt_ref), ld(kt_ref), ld(rt_ref), ld(v_ref)
        g_end = [eg_ref[pl.ds(cs[u] * ch + ch - 1, 1), h * n:(h + 1) * n] for h, u in ids]
        ar = sp(each(lambda a, r: jnp.concatenate([a, r], axis=0), a_, r_))
        vs = sp(v_)
        pb = each(_mm3_nt, ar, sp(b_))
        pk = each(_mm3_nt, ar, sp(k_))
        a_ab = [jnp.where(strict, t[:ch], 0.0) for t in pb]
        a_ak = sp([jnp.where(strict, t[:ch], 0.0) for t in pk])
        a_rb = sp([jnp.where(incl, t[ch:], 0.0) for t in pb])
        a_rk = sp([jnp.where(incl, t[ch:], 0.0) for t in pk])
        d1 = [jnp.where(same, t, 0.0) for t in a_ab]
        lb = each(lambda t, d: t - d, a_ab, d1)
        d1 = sp(d1)
        d2 = sp(each(_mm3, d1, d1))
        akv = each(_mm3, a_ak, vs)
        x = each(lambda l, a, t: jnp.concatenate([l, a, t], axis=1), lb, a_, akv)
        d4 = sp(each(_mm3, d2, d2))
        x = each(lambda d, t: t + _mm3(d, _Split(t)), d1, x)
        d8 = sp(each(_mm3, d4, d4))
        x = each(lambda d, t: t + _mm3(d, _Split(t)), d2, x)
        x = each(lambda d, t: t + _mm3(d, _Split(t)), d4, x)
        x = each(lambda d, t: t + _mm3(d, _Split(t)), d8, x)
        e1 = sp([t[:, :ch] for t in x])
        wu = [t[:, ch:] for t in x]
        e2 = sp(each(_mm3, e1, e1))
        wu = each(lambda e, t: t + _mm3(e, _Split(t)), e1, wu)
        wu = sp(each(lambda e, t: t + _mm3(e, _Split(t)), e2, wu))
        ry = each(_mm3, a_rb, wu)
        rkv = each(_mm3, a_rk, vs)
        mn = each(lambda b, g, t: _mm3_tn(_Split(b * g), t), b_, g_end, wu)
        kv = each(lambda k, g, t: _mm3_tn(_Split(k * g), t), k_, g_end, vs)
        rq = each(lambda r, t: r + t[:, :n], r_, ry)
        yl = each(lambda t, w: t[:, n:] + w, ry, rkv)
        for u in range(unroll):
            rq_ref[sls[u], :] = jnp.concatenate([rq[u], rq[unroll + u]], axis=1)
            yl_ref[sls[u], :] = jnp.concatenate([yl[u], yl[unroll + u]], axis=1)
        for i, (h, u) in enumerate(ids):
            m_ref[h, cs[u]] = mn[i][:, :n] + jnp.where(eye, g_end[i], 0.0)
            n_ref[h, cs[u]] = mn[i][:, n:] + kv[i]
        return 0

    lax.fori_loop(0, nchunk // unroll, body, 0)


def rwkv_chunk(at, bt, kt, rt, v, eg, *, batch, nchunk=8, unroll=4):
    t, w = at.shape
    s = t // batch
    ch, n = RW_CHUNK, HEAD_DIM
    pairs = w // LANE
    rows = nchunk * ch
    nr = s // rows
    spec = pl.BlockSpec((rows, LANE), lambda b, p, i: (b * nr + i, p))
    mspec = pl.BlockSpec((2, nchunk, n, n), lambda b, p, i: (b * pairs + p, i, 0, 0))
    return pl.pallas_call(
        functools.partial(_rwkv_chunk_kernel, nchunk=nchunk, unroll=unroll),
        grid=(batch, pairs, nr),
        in_specs=[spec] * 6,
        out_specs=[mspec, mspec, spec, spec],
        out_shape=[jax.ShapeDtypeStruct((batch * w // n, s // ch, n, n), F32)] * 2
        + [jax.ShapeDtypeStruct((t, w), F32)] * 2,
        compiler_params=_cp("parallel", "parallel", "parallel"),
        name="rwkv_chunk",
    )(at, bt, kt, rt, v, eg)


def _rwkv_scan_kernel(m_ref, n_ref, s_ref, st_ref, *, heads, nchunk):
    @pl.when(pl.program_id(1) == 0)
    def _():
        st_ref[...] = jnp.zeros_like(st_ref)

    def body(c, _):
        for hh in range(heads):
            st = st_ref[hh]
            s_ref[hh, c] = st
            st_ref[hh] = _dot(m_ref[hh, c], st, HI) + n_ref[hh, c]
        return 0

    lax.fori_loop(0, nchunk, body, 0)


def rwkv_scan(m, nn, *, heads=4, nchunk=32):
    bh, nc, n, _ = m.shape
    nchunk = min(nchunk, nc)
    spec = pl.BlockSpec((heads, nchunk, n, n), lambda b, i: (b, i, 0, 0))
    return pl.pallas_call(
        functools.partial(_rwkv_scan_kernel, heads=heads, nchunk=nchunk),
        grid=(bh // heads, nc // nchunk),
        in_specs=[spec, spec],
        out_specs=spec,
        out_shape=jax.ShapeDtypeStruct((bh, nc, n, n), F32),
        scratch_shapes=[pltpu.VMEM((heads, n, n), F32)],
        compiler_params=_cp("parallel", "arbitrary"),
        name="rwkv_scan",
    )(m, nn)


def _rwkv_out_kernel(rq_ref, yl_ref, s_ref, lw_ref, lb_ref, avg_ref, o_ref, *, nchunk):
    ch, n = RW_CHUNK, HEAD_DIM
    ys = []
    for h in range(2):
        cols = slice(h * n, (h + 1) * n)
        ys.append(jnp.concatenate(
            [_mm3(_Split(rq_ref[c * ch:(c + 1) * ch, cols]), _Split(s_ref[h, c])) for c in range(nchunk)],
            axis=0))
    y = jnp.concatenate(ys, axis=1) + yl_ref[...]
    avg2 = jnp.concatenate([avg_ref[...]] * 2, axis=0)

    def mean(t):
        ts = _Split(t)
        return _dot(jnp.concatenate([ts.hi, ts.lo], axis=1), avg2)

    d = y - mean(y)
    var = mean(d * d)
    o_ref[...] = d * lax.rsqrt(var + GN_EPS) * lw_ref[...] + lb_ref[...]


def rwkv_out(rq, yl, s0, ln_w, ln_b, *, batch, nchunk=8):
    t, w = rq.shape
    s = t // batch
    n = HEAD_DIM
    pairs = w // LANE
    rows = nchunk * RW_CHUNK
    nr = s // rows
    li = jnp.arange(LANE) // n
    avg = ((li[:, None] == li[None, :]).astype(F32) / n).astype(BF16)
    spec = pl.BlockSpec((rows, LANE), lambda b, p, i: (b * nr + i, p))
    hspec = pl.BlockSpec((1, LANE), lambda b, p, i: (0, p))
    return pl.pallas_call(
        functools.partial(_rwkv_out_kernel, nchunk=nchunk),
        grid=(batch, pairs, nr),
        in_specs=[spec, spec, pl.BlockSpec((2, nchunk, n, n), lambda b, p, i: (b * pairs + p, i, 0, 0)),
                  hspec, hspec, pl.BlockSpec((LANE, LANE), lambda b, p, i: (0, 0))],
        out_specs=spec,
        out_shape=jax.ShapeDtypeStruct((t, w), F32),
        compiler_params=_cp("parallel", "parallel", "parallel"),
        name="rwkv_out",
    )(rq, yl, s0, ln_w.reshape(1, w), ln_b.reshape(1, w), avg)


def _hyb_out_kernel(oa_ref, y_ref, bv_ref, g_ref, wa_ref, wb_ref, r_ref, o_ref):
    ob = (y_ref[...] + bv_ref[...]) * g_ref[...]
    o_ref[...] = (r_ref[...] + _dot(oa_ref[...].astype(BF16), wa_ref[...])
                  + _dot(ob.astype(BF16), wb_ref[...]))


def hyb_out(res, oa, y, bv, g, wa, wb, *, tm=512):
    t, d = res.shape
    row = lambda i: (i, 0)
    fix = lambda i: (0, 0)
    act = lambda a: pl.BlockSpec((tm, a.shape[1]), row)
    return pl.pallas_call(
        _hyb_out_kernel,
        grid=(t // tm,),
        in_specs=[act(oa), act(y), act(bv), act(g), pl.BlockSpec(wa.shape, fix),
                  pl.BlockSpec(wb.shape, fix), pl.BlockSpec((tm, d), row)],
        out_specs=pl.BlockSpec((tm, d), row),
        out_shape=jax.ShapeDtypeStruct((t, d), F32),
        compiler_params=_cp("parallel"),
        name="hyb_out",
    )(oa, y, bv, g, wa, wb, res)


def _pad_cols(w, n):
    return jnp.pad(w, ((0, 0), (0, n - w.shape[1])))


def sb_rwkv_layer(h, b, s, attn_norm, w_in, w_out, mu, w0, w2, a0, a2, g2, k_k, k_a, r_k, ln_w, ln_b):
    sbw = SB_HEADS * HEAD_DIM
    rww = RW_HEADS * HEAD_DIM
    rw_in = w_in.shape[1] - 3 * sbw
    rw_pad = -(-rw_in // LANE) * LANE
    dl, al = w2.shape[0], a2.shape[0]
    gl = g2.shape[0]
    assert dl + al == LANE and 3 * rww % LANE == 0

    qkv = rms_matmul(h, attn_norm, w_in[:, :3 * sbw].astype(BF16), out_dtype=BF16)
    rw = rms_matmul(h, attn_norm, _pad_cols(w_in[:, 3 * sbw:], rw_pad).astype(BF16), tn=rw_pad // 3)
    oa = sb_attention(qkv, batch=b, heads=SB_HEADS)

    w2p = jnp.zeros((LANE, rww), F32).at[:dl].set(w2)
    a2p = jnp.zeros((LANE, rww), F32).at[dl:].set(a2)
    g2p = jnp.zeros((rw_pad - 3 * rww - LANE, rww), F32).at[:gl].set(g2)
    vec = lambda t: t.reshape(1, rww).astype(F32)
    mu_p = _pad_cols(mu.reshape(1, rw_in), rw_pad)
    at, bt, kt, rt, vv, eg, g, bv = rwkv_prep(rw, mu_p, vec(w0), vec(a0), vec(k_k), vec(k_a), vec(r_k),
                                              w2p, a2p, g2p, seq=s, width=rww)
    m, nn, rq, yl = rwkv_chunk(at, bt, kt, rt, vv, eg, batch=b)
    s0 = rwkv_scan(m, nn)
    y = rwkv_out(rq, yl, s0, ln_w, ln_b, batch=b)
    return hyb_out(h, oa, y, bv, g, w_out[:sbw].astype(BF16), w_out[sbw:].astype(BF16))


def mla_layer(h, b, s, positions, attn_norm, w_down, q_norm, kv_norm, w_uq, w_ukv, w_o):
    nh, dn, dr, dv = MLA_HEADS, MLA_NOPE, MLA_ROPE, MLA_V
    qr, kvr = MLA_Q_RANK, MLA_KV_RANK
    assert dn == dv and qr % LANE == 0 and (qr + LANE) % kvr == 0
    wd = jnp.concatenate([w_down[:, :qr], jnp.tile(w_down[:, qr + kvr:], (1, LANE // dr)),
                          w_down[:, qr:qr + kvr]], axis=1)
    c = rms_matmul(h, attn_norm, wd.astype(BF16), tn=wd.shape[1])
    wq = w_uq.reshape(qr, nh, dn + dr)
    wq = jnp.concatenate([wq[:, :, :dn].reshape(qr, nh * dn), wq[:, :, dn:].reshape(qr, nh * dr)], axis=1)
    q = rms_matmul(c, q_norm, wq.astype(BF16), x_col=0)
    wkv = w_ukv.reshape(kvr, nh, dn + dv)
    wkv = jnp.concatenate([wkv[:, :, :dn].reshape(kvr, nh * dn), wkv[:, :, dn:].reshape(kvr, nh * dv)], axis=1)
    kv = rms_matmul(c, kv_norm, wkv.astype(BF16), x_col=(qr + LANE) // kvr, out_dtype=BF16)
    qrot, krot = rope(q, c, positions, q_col=dn // dr, qw=nh * dr, k_col=qr // LANE)
    o = softmax_attention(q, qrot, kv, krot, 1.0 / math.sqrt(dn + dr), batch=b, heads=nh)
    return matmul_residual(h, o, w_o.astype(BF16))


def kernel(x, p, positions, attn_norm, ffn_norm, ffn_w_in, ffn_conv_w, ffn_conv_b, ffn_w_out, ple_w_proj, ple_norm, ple_gate_norm, ple_w_gate, hyb_w_in, hyb_w_out, rw_mu, rw_w0, rw_w2, rw_a0, rw_a2, rw_g2, rw_k_k, rw_k_a, rw_r_k, rw_ln_w, rw_ln_b, mla_w_down, mla_q_norm, mla_kv_norm, mla_w_uq, mla_w_ukv, mla_w_o, final_norm):
    b, s, d = x.shape
    depth = p.shape[0]
    h = x.reshape(b * s, d)
    pos = positions.reshape(b * s)
    for i in range(depth):
        j = i // 2
        if i % 2 == 0:
            h = sb_rwkv_layer(h, b, s, attn_norm[i], hyb_w_in[j], hyb_w_out[j], rw_mu[j], rw_w0[j],
                              rw_w2[j], rw_a0[j], rw_a2[j], rw_g2[j], rw_k_k[j], rw_k_a[j], rw_r_k[j],
                              rw_ln_w[j], rw_ln_b[j])
        else:
            h = mla_layer(h, b, s, pos, attn_norm[i], mla_w_down[j], mla_q_norm[j], mla_kv_norm[j],
                          mla_w_uq[j], mla_w_ukv[j], mla_w_o[j])
        ug = rms_matmul(h, ffn_norm[i], ffn_w_in[i].astype(BF16), tm=1024, out_dtype=BF16)
        h = convglu_out(h, ug, ffn_conv_w[i], ffn_conv_b[i], ffn_w_out[i].astype(BF16), seq=s)
        h = ple(h, p[i].reshape(b * s, -1), ple_w_proj[i].astype(BF16), ple_norm[i], ple_gate_norm[i],
                ple_w_gate[i].astype(BF16), final_norm, final=(i == depth - 1))
    return h.reshape(b, s, d)
```

```python
import functools
import math

import jax
import jax.numpy as jnp
from jax import lax
from jax.experimental import pallas as pl
from jax.experimental.pallas import tpu as pltpu

F32 = jnp.float32
BF16 = jnp.bfloat16
HI = lax.Precision.HIGHEST

NORM_EPS = 1e-6
GN_EPS = 64e-5
ROPE_THETA = 10000.0
LANE = 128
VMEM_LIMIT = 48 * 1024 * 1024
HALO = 16

SB_HEADS = 8
RW_HEADS = 8
HEAD_DIM = 64
MLA_HEADS = 16
MLA_NOPE = 64
MLA_ROPE = 32
MLA_V = 64
MLA_Q_RANK = 384
MLA_KV_RANK = 256
SB_CUT = -104.0
RW_CHUNK = 64
RW_SUB = 16


def _cp(*sem):
    return pltpu.CompilerParams(dimension_semantics=sem, vmem_limit_bytes=VMEM_LIMIT)


def _iota(shape, dim):
    return lax.broadcasted_iota(jnp.int32, shape, dim)


def _softplus(z):
    return jnp.maximum(z, 0.0) + jnp.log1p(jnp.exp(-jnp.abs(z)))


def _sigmoid(z):
    return 1.0 / (1.0 + jnp.exp(-z))


def _dot(a, b, precision=None):
    return jnp.dot(a, b, preferred_element_type=F32, precision=precision)


def _dot_nt(a, b, precision=None):
    return lax.dot_general(a, b, (((1,), (1,)), ((), ())), preferred_element_type=F32,
                           precision=precision)


def _dot_tn(a, b, precision=None):
    return lax.dot_general(a, b, (((0,), (0,)), ((), ())), preferred_element_type=F32,
                           precision=precision)


class _Split:
    def __init__(self, x):
        self.x = x
        self.hi = x.astype(BF16)
        self.lo = (x - self.hi.astype(F32)).astype(BF16)
        self._packed = {}

    def left(self, axis):
        key = ("l", axis)
        if key not in self._packed:
            self._packed[key] = jnp.concatenate([self.hi, self.lo, self.hi], axis=axis)
        return self._packed[key]

    def right(self, axis):
        key = ("r", axis)
        if key not in self._packed:
            self._packed[key] = jnp.concatenate([self.hi, self.hi, self.lo], axis=axis)
        return self._packed[key]


def _mm3(a, b):
    return _dot(a.left(1), b.right(0))


def _mm3_nt(a, b):
    return _dot_nt(a.left(1), b.right(1))


def _mm3_tn(a, b):
    return _dot_tn(a.left(0), b.right(0))


def _rms(x, g, eps=NORM_EPS):
    return x * lax.rsqrt(jnp.mean(x * x, axis=-1, keepdims=True) + eps) * g


def _rms_matmul_kernel(x_ref, g_ref, w_ref, o_ref, xn_ref):
    @pl.when(pl.program_id(1) == 0)
    def _():
        xn_ref[...] = _rms(x_ref[...].astype(F32), g_ref[...]).astype(BF16)

    o_ref[...] = _dot(xn_ref[...], w_ref[...]).astype(o_ref.dtype)


def rms_matmul(x, g, w, *, x_col=0, tm=512, tn=512, out_dtype=F32):
    t = x.shape[0]
    k, n = w.shape
    tn = min(tn, n)
    return pl.pallas_call(
        _rms_matmul_kernel,
        grid=(t // tm, n // tn),
        in_specs=[pl.BlockSpec((tm, k), lambda i, j: (i, x_col)),
                  pl.BlockSpec((1, k), lambda i, j: (0, 0)),
                  pl.BlockSpec((k, tn), lambda i, j: (0, j))],
        out_specs=pl.BlockSpec((tm, tn), lambda i, j: (i, j)),
        out_shape=jax.ShapeDtypeStruct((t, n), out_dtype),
        scratch_shapes=[pltpu.VMEM((tm, k), BF16)],
        compiler_params=_cp("parallel", "arbitrary"),
        name="rms_matmul",
    )(x, g.reshape(1, k).astype(F32), w)


def _matmul_res_kernel(a_ref, w_ref, r_ref, o_ref):
    o_ref[...] = r_ref[...] + _dot(a_ref[...].astype(BF16), w_ref[...])


def matmul_residual(res, a, w, *, tm=512, tn=512):
    t, k = a.shape
    n = w.shape[1]
    return pl.pallas_call(
        _matmul_res_kernel,
        grid=(t // tm, n // tn),
        in_specs=[pl.BlockSpec((tm, k), lambda i, j: (i, 0)),
                  pl.BlockSpec((k, tn), lambda i, j: (0, j)),
                  pl.BlockSpec((tm, tn), lambda i, j: (i, j))],
        out_specs=pl.BlockSpec((tm, tn), lambda i, j: (i, j)),
        out_shape=jax.ShapeDtypeStruct((t, n), F32),
        compiler_params=_cp("parallel", "arbitrary"),
        name="matmul_residual",
    )(a, w, res)


def _convglu_out_kernel(u_ref, g_ref, gp_ref, cw_ref, cb_ref, w_ref, r_ref, o_ref, *, tm, seq):
    i = pl.program_id(0)
    g = g_ref[...].astype(F32)
    rows = _iota(g.shape, 0)
    at_start = (i * tm) % seq == 0
    halo = jnp.where(at_start, 0.0, gp_ref[...].astype(F32))
    h1, h2 = halo[HALO - 1:HALO, :], halo[HALO - 2:HALO - 1, :]
    p1 = jnp.where(rows == 0, h1, pltpu.roll(g, 1, axis=0))
    p2 = jnp.where(rows == 0, h2, jnp.where(rows == 1, h1, pltpu.roll(g, 2, axis=0)))
    c = cw_ref[0:1, :] * p2 + cw_ref[1:2, :] * p1 + cw_ref[2:3, :] * g + cb_ref[...]
    act = 0.5 * c * (1.0 + lax.erf(c * (1.0 / math.sqrt(2.0)))) * u_ref[...].astype(F32)
    o_ref[...] = r_ref[...] + _dot(act.astype(BF16), w_ref[...])


def convglu_out(res, ug, conv_w, conv_b, w_out, *, seq, tm=256):
    t = res.shape[0]
    f, n = w_out.shape
    hb = tm // HALO
    return pl.pallas_call(
        functools.partial(_convglu_out_kernel, tm=tm, seq=seq),
        grid=(t // tm,),
        in_specs=[pl.BlockSpec((tm, f), lambda i: (i, 0)),
                  pl.BlockSpec((tm, f), lambda i: (i, 1)),
                  pl.BlockSpec((HALO, f), lambda i: (jnp.maximum(i * hb - 1, 0), 1)),
                  pl.BlockSpec((3, f), lambda i: (0, 0)),
                  pl.BlockSpec((1, f), lambda i: (0, 0)),
                  pl.BlockSpec((f, n), lambda i: (0, 0)),
                  pl.BlockSpec((tm, n), lambda i: (i, 0))],
        out_specs=pl.BlockSpec((tm, n), lambda i: (i, 0)),
        out_shape=jax.ShapeDtypeStruct((t, n), F32),
        compiler_params=_cp("parallel"),
        name="convglu_out",
    )(ug, ug, ug, conv_w, conv_b.reshape(1, f), w_out, res)


def _ple_kernel(h_ref, p_ref, wp_ref, pn_ref, gn_ref, wg_ref, fn_ref, o_ref, *, final):
    h = h_ref[...]
    e = _rms(_dot(p_ref[...].astype(BF16), wp_ref[...]), pn_ref[...])
    gate = _sigmoid(_dot(_rms(h, gn_ref[...]).astype(BF16), wg_ref[...]))
    out = h + gate * e
    if final:
        out = _rms(out, fn_ref[...])
    o_ref[...] = out


def ple(h, p, w_proj, p_norm, g_norm, w_gate, f_norm, *, final, tm=512):
    t, d = h.shape
    pd = p.shape[1]
    row = lambda i: (i, 0)
    fix = lambda i: (0, 0)
    return pl.pallas_call(
        functools.partial(_ple_kernel, final=final),
        grid=(t // tm,),
        in_specs=[pl.BlockSpec((tm, d), row), pl.BlockSpec((tm, pd), row),
                  pl.BlockSpec((pd, d), fix), pl.BlockSpec((1, d), fix), pl.BlockSpec((1, d), fix),
                  pl.BlockSpec((d, d), fix), pl.BlockSpec((1, d), fix)],
        out_specs=pl.BlockSpec((tm, d), row),
        out_shape=jax.ShapeDtypeStruct((t, d), F32),
        compiler_params=_cp("parallel"),
        name="ple",
    )(h, p, w_proj, p_norm.reshape(1, d), g_norm.reshape(1, d), w_gate, f_norm.reshape(1, d))


def _sb_attn_kernel(q_ref, k_ref, v_ref, u_ref, o_ref, *, tq, tk, scale):
    qi = pl.program_id(2)
    nblk = tq // tk
    first = _iota((tq, LANE), 1) < HEAD_DIM
    qf = q_ref[...] * scale
    zero = jnp.zeros_like(qf)
    qs = (jnp.where(first, qf, zero), jnp.where(first, zero, qf))

    def sweep(k_lo, runs, accs, masked):
        parts = []
        for d in reversed(range(nblk)):
            k0 = k_lo + d * tk
            kb = k_ref[pl.ds(k0, tk), :]
            valid = None
            if masked:
                valid = _iota((tq, tk), 1) + d * tk < _iota((tq, tk), 0)
            for h in range(2):
                z = _dot_nt(qs[h], kb)
                sp = _softplus(z)
                lom = -sp
                if masked:
                    lom = jnp.where(valid, lom, 0.0)
                hi = lom.astype(BF16)
                lo = (lom - hi.astype(F32)).astype(BF16)
                sums = _dot(jnp.concatenate([hi, lo], axis=1), u_ref[...])
                parts.append((h, k0, z - sp, sums, valid))
        runs, accs = list(runs), list(accs)
        for h, k0, logsig, sums, valid in parts:
            w = jnp.exp(logsig + runs[h] + sums[:, :tk])
            if masked:
                w = jnp.where(valid, w, 0.0)
            accs[h] = accs[h] + _dot(w.astype(BF16), v_ref[pl.ds(k0, tk), :])
            runs[h] = runs[h] + sums[:, tk:]
        return tuple(runs), tuple(accs)

    runs = (jnp.zeros((tq, tk), F32),) * 2
    accs = (jnp.zeros((tq, LANE), F32),) * 2
    runs, accs = sweep(pl.multiple_of(qi * tq, tq), runs, accs, True)

    def live(runs):
        return jnp.max(jnp.maximum(runs[0], runs[1])) > SB_CUT

    def cond(c):
        return c[0]

    def body(c):
        _, j, runs, accs = c
        runs, accs = sweep(pl.multiple_of(j * tq, tq), runs, accs, False)
        return (j > 0) & live(runs), j - 1, runs, accs

    _, _, _, accs = lax.while_loop(cond, body, ((qi > 0) & live(runs), qi - 1, runs, accs))
    o_ref[...] = jnp.where(first, accs[0], accs[1])


def sb_attention(qkv, *, batch, heads, tq=256):
    t, w3 = qkv.shape
    s = t // batch
    nq = s // tq
    pairs = heads * HEAD_DIM // LANE
    tk = LANE
    jj = jnp.arange(2 * tk)[:, None] % tk
    ss = jnp.arange(2 * tk)[None, :]
    u = jnp.where(ss < tk, jj > ss, True).astype(BF16)
    return pl.pallas_call(
        functools.partial(_sb_attn_kernel, tq=tq, tk=tk, scale=1.0 / math.sqrt(HEAD_DIM)),
        grid=(batch, pairs, nq),
        in_specs=[pl.BlockSpec((tq, LANE), lambda b, p, i: (b * nq + i, p)),
                  pl.BlockSpec((s, LANE), lambda b, p, i: (b, pairs + p)),
                  pl.BlockSpec((s, LANE), lambda b, p, i: (b, 2 * pairs + p)),
                  pl.BlockSpec((2 * tk, 2 * tk), lambda b, p, i: (0, 0))],
        out_specs=pl.BlockSpec((tq, LANE), lambda b, p, i: (b * nq + i, p)),
        out_shape=jax.ShapeDtypeStruct((t, w3 // 3), F32),
        compiler_params=_cp("parallel", "parallel", "arbitrary"),
        name="sb_attention",
    )(qkv, qkv, qkv, u)


def _softmax_attn_kernel(qn_ref, qr_ref, kn_ref, kr_ref, v_ref, o_ref, sa0, sa1, sb0, sb1,
                         *, tq, tk, scale):
    hp = pl.program_id(1)
    qi = pl.program_id(2)
    lane = _iota((tq, LANE), 1)
    qn = qn_ref[...] * (scale * math.log2(math.e))
    qr = qr_ref[...] * (scale * math.log2(math.e))
    qs = []
    for h in range(2):
        nope = jnp.where(lane // MLA_NOPE == h, qn, 0.0)
        rope_ = jnp.where(lane // MLA_ROPE == (hp % 2) * 2 + h, qr, 0.0)
        qs.append(jnp.concatenate([nope, rope_], axis=1).astype(BF16))

    def scores(j, dsts):
        k0 = pl.multiple_of(j * tk, tk)
        kb = jnp.concatenate([kn_ref[pl.ds(k0, tk), :], kr_ref[pl.ds(k0, tk), :]], axis=1)
        mx = []
        for h in range(2):
            s = _dot_nt(qs[h], kb)
            dsts[h][...] = s
            mx.append(jnp.max(s, axis=1, keepdims=True))
        return tuple(mx)

    def consume(srcs, j, carry, mx):
        vb = v_ref[pl.ds(pl.multiple_of(j * tk, tk), tk), :]
        one = jnp.ones_like(vb)
        first = _iota(vb.shape, 1) < MLA_V
        vbs = (jnp.where(first, vb, one), jnp.where(first, one, vb))
        out = []
        for h in range(2):
            m, acc = carry[h]
            s = srcs[h][...]
            if mx is None:
                s = jnp.where(_iota((tq, tk), 1) <= _iota((tq, tk), 0), s, -1e30)
                m_new = jnp.maximum(m, jnp.max(s, axis=1, keepdims=True))
            else:
                m_new = jnp.maximum(m, mx[h])
            p = jnp.exp2(s - m_new)
            acc = jnp.exp2(m - m_new) * acc + _dot(p.astype(BF16), vbs[h])
            out.append((m_new, acc))
        return tuple(out)

    sa, sb = (sa0, sa1), (sb0, sb1)

    def pair(t, carry):
        state, mxa = carry
        mxb = scores(2 * t + 1, sb)
        state = consume(sa, 2 * t, state, mxa)
        mxa = scores(2 * t + 2, sa)
        return consume(sb, 2 * t + 1, state, mxb), mxa

    def tail_odd(carry):
        state, mxa = carry
        scores(qi, sb)
        state = consume(sa, qi - 1, state, mxa)
        return consume(sb, qi, state, None)

    def tail_even(carry):
        return consume(sa, qi, carry[0], None)

    init = ((jnp.full((tq, 1), -1e30, F32), jnp.zeros((tq, LANE), F32)),) * 2
    carry = lax.fori_loop(0, qi // 2, pair, (init, scores(0, sa)))
    (_, acc0), (_, acc1) = lax.cond(qi % 2 == 1, tail_odd, tail_even, carry)
    o_ref[...] = jnp.where(lane < MLA_V, acc0 / pltpu.roll(acc0, MLA_V, axis=1),
                           acc1 / pltpu.roll(acc1, MLA_V, axis=1))


def softmax_attention(q, qrot, kv, krot, scale, *, batch, heads, tq=512):
    tk = tq
    t = q.shape[0]
    s = t // batch
    nq = s // tq
    pairs = heads * MLA_NOPE // LANE
    buf = pltpu.VMEM((tq, tk), F32)
    return pl.pallas_call(
        functools.partial(_softmax_attn_kernel, tq=tq, tk=tk, scale=scale),
        grid=(batch, pairs, nq),
        in_specs=[pl.BlockSpec((tq, LANE), lambda b, p, i: (b * nq + i, p)),
                  pl.BlockSpec((tq, LANE), lambda b, p, i: (b * nq + i, p // 2)),
                  pl.BlockSpec((s, LANE), lambda b, p, i: (b, p)),
                  pl.BlockSpec((s, LANE), lambda b, p, i: (b, 0)),
                  pl.BlockSpec((s, LANE), lambda b, p, i: (b, pairs + p))],
        out_specs=pl.BlockSpec((tq, LANE), lambda b, p, i: (b * nq + i, p)),
        out_shape=jax.ShapeDtypeStruct((t, heads * MLA_V), F32),
        scratch_shapes=[buf, buf, buf, buf],
        compiler_params=_cp("parallel", "parallel", "arbitrary"),
        name="softmax_attention",
    )(q, qrot, kv, krot, kv)


def _rope_kernel(q_ref, k_ref, pos_ref, f_ref, oq_ref, ok_ref):
    half = MLA_ROPE // 2
    ang = pos_ref[...].astype(F32) * f_ref[...]
    cos, sin = jnp.cos(ang), jnp.sin(ang)
    low = _iota(ang.shape, 1) % MLA_ROPE < half

    def rot(x):
        partner = jnp.where(low, -pltpu.roll(x, LANE - half, axis=1), pltpu.roll(x, half, axis=1))
        return x * cos + partner * sin

    for c in range(q_ref.shape[1] // LANE):
        cols = slice(c * LANE, (c + 1) * LANE)
        oq_ref[:, cols] = rot(q_ref[:, cols])
    ok_ref[...] = rot(k_ref[...]).astype(ok_ref.dtype)


def rope(q, c, positions, *, q_col, qw, k_col, tm=512):
    t = q.shape[0]
    half = MLA_ROPE // 2
    inv_freq = ROPE_THETA ** (-jnp.arange(half, dtype=F32) / half)
    freq = jnp.tile(inv_freq, LANE // half).reshape(1, LANE)
    return pl.pallas_call(
        _rope_kernel,
        grid=(t // tm,),
        in_specs=[pl.BlockSpec((tm, qw), lambda i: (i, q_col)), pl.BlockSpec((tm, LANE), lambda i: (i, k_col)),
                  pl.BlockSpec((tm, 1), lambda i: (i, 0)), pl.BlockSpec((1, LANE), lambda i: (0, 0))],
        out_specs=[pl.BlockSpec((tm, qw), lambda i: (i, 0)), pl.BlockSpec((tm, LANE), lambda i: (i, 0))],
        out_shape=[jax.ShapeDtypeStruct((t, qw), F32), jax.ShapeDtypeStruct((t, LANE), BF16)],
        compiler_params=_cp("parallel"),
        name="rope",
    )(q, c, positions.reshape(t, 1), freq)


def _rwkv_prep_kernel(x_ref, xp_ref, mu_ref, w0_ref, a0_ref, kk_ref, ka_ref, rk_ref,
                      w2_ref, a2_ref, g2_ref, tri_ref, hs_ref,
                      at_ref, bt_ref, kt_ref, rt_ref, v_ref, eg_ref, g_ref, bv_ref, *, tm, seq, width):
    i = pl.program_id(0)
    x = x_ref[...]
    rows = _iota(x.shape, 0)
    at_start = (i * tm) % seq == 0
    prev_row = jnp.where(at_start, 0.0, xp_ref[7:8, :])
    prev = jnp.where(rows == 0, prev_row, pltpu.roll(x, 1, axis=0))
    xm = x + (prev - x) * mu_ref[...]
    r = xm[:, :width]
    k = xm[:, width:2 * width]
    v = xm[:, 2 * width:3 * width]
    l1 = xm[:, 3 * width:3 * width + LANE]
    l2 = xm[:, 3 * width + LANE:]
    wpre = w0_ref[...] + _dot(jnp.tanh(l1), w2_ref[...], HI)
    a = _sigmoid(a0_ref[...] + _dot(l1, a2_ref[...], HI))
    g = _dot(_sigmoid(l2), g2_ref[...], HI)
    lw = -jnp.exp(-_softplus(-wpre) - 0.5)
    hs = hs_ref[...]
    kk = k * kk_ref[...]
    kk = kk * lax.rsqrt(jnp.maximum(_dot(kk * kk, hs, HI), 1e-24))
    km = k * (1.0 + (a - 1.0) * ka_ref[...])
    bonus = _dot(r * km * rk_ref[...], hs, HI)
    cum = _dot(tri_ref[...], lw, HI)
    eg = jnp.exp(cum)
    ieg = jnp.exp(-cum)
    at_ref[...] = -kk * jnp.exp(cum - lw)
    bt_ref[...] = kk * a * ieg
    kt_ref[...] = km * ieg
    rt_ref[...] = r * eg
    v_ref[...] = v
    eg_ref[...] = eg
    g_ref[...] = g
    bv_ref[...] = bonus * v


def rwkv_prep(x, mu, w0, a0, k_k, k_a, r_k, w2p, a2p, g2p, *, seq, width, tm=256):
    t, wx = x.shape
    ch = RW_CHUNK
    ti = jnp.arange(tm)
    tri = ((ti[:, None] // ch == ti[None, :] // ch) & (ti[None, :] <= ti[:, None])).astype(F32)
    ci = jnp.arange(width) // HEAD_DIM
    hs = (ci[:, None] == ci[None, :]).astype(F32)
    hb = tm // 8
    row = lambda i: (i, 0)
    fix = lambda i: (0, 0)
    vec = pl.BlockSpec((1, width), fix)
    out = jax.ShapeDtypeStruct((t, width), F32)
    return pl.pallas_call(
        functools.partial(_rwkv_prep_kernel, tm=tm, seq=seq, width=width),
        grid=(t // tm,),
        in_specs=[pl.BlockSpec((tm, wx), row),
                  pl.BlockSpec((8, wx), lambda i: (jnp.maximum(i * hb - 1, 0), 0)),
                  pl.BlockSpec((1, wx), fix), vec, vec, vec, vec, vec,
                  pl.BlockSpec(w2p.shape, fix), pl.BlockSpec(a2p.shape, fix),
                  pl.BlockSpec(g2p.shape, fix), pl.BlockSpec((tm, tm), fix),
                  pl.BlockSpec((width, width), fix)],
        out_specs=[pl.BlockSpec((tm, width), row)] * 8,
        out_shape=[out] * 8,
        compiler_params=_cp("parallel"),
        name="rwkv_prep",
    )(x, x, mu, w0, a0, k_k, k_a, r_k, w2p, a2p, g2p, tri, hs)


def _rwkv_chunk_kernel(at_ref, bt_ref, kt_ref, rt_ref, v_ref, eg_ref,
                       m_ref, n_ref, rq_ref, yl_ref, *, nchunk, unroll):
    ch, n = RW_CHUNK, HEAD_DIM
    ti = _iota((ch, ch), 0)
    si = _iota((ch, ch), 1)
    strict = si < ti
    incl = si <= ti
    same = (ti // RW_SUB) == (si // RW_SUB)
    eye = _iota((n, n), 0) == _iota((n, n), 1)
    each = lambda f, *ls: [f(*t) for t in zip(*ls)]

    def body(it, _):
        cs = [it * unroll + u for u in range(unroll)]
        sls = [pl.ds(pl.multiple_of(c * ch, ch), ch) for c in cs]
        ids = [(h, u) for h in range(2) for u in range(unroll)]
        ld = lambda ref: [ref[sls[u], h * n:(h + 1) * n] for h, u in ids]
        sp = lambda xs: [_Split(t) for t in xs]
        a_, b_, k_, r_, v_ = ld(at_ref), ld(bt_ref), ld(kt_ref), ld(rt_ref), ld(v_ref)
        g_end = [eg_ref[pl.ds(cs[u] * ch + ch - 1, 1), h * n:(h + 1) * n] for h, u in ids]
        ar = sp(each(lambda a, r: jnp.concatenate([a, r], axis=0), a_, r_))
        vs = sp(v_)
        pb = each(_mm3_nt, ar, sp(b_))
        pk = each(_mm3_nt, ar, sp(k_))
        a_ab = [jnp.where(strict, t[:ch], 0.0) for t in pb]
        a_ak = sp([jnp.where(strict, t[:ch], 0.0) for t in pk])
        a_rb = sp([jnp.where(incl, t[ch:], 0.0) for t in pb])
        a_rk = sp([jnp.where(incl, t[ch:], 0.0) for t in pk])
        d1 = [jnp.where(same, t, 0.0) for t in a_ab]
        lb = each(lambda t, d: t - d, a_ab, d1)
        d1 = sp(d1)
        d2 = sp(each(_mm3, d1, d1))
        akv = each(_mm3, a_ak, vs)
        x = each(lambda l, a, t: jnp.concatenate([l, a, t], axis=1), lb, a_, akv)
        d4 = sp(each(_mm3, d2, d2))
        x = each(lambda d, t: t + _mm3(d, _Split(t)), d1, x)
        d8 = sp(each(_mm3, d4, d4))
        x = each(lambda d, t: t + _mm3(d, _Split(t)), d2, x)
        x = each(lambda d, t: t + _mm3(d, _Split(t)), d4, x)
        x = each(lambda d, t: t + _mm3(d, _Split(t)), d8, x)
        e1 = sp([t[:, :ch] for t in x])
        wu = [t[:, ch:] for t in x]
        e2 = sp(each(_mm3, e1, e1))
        wu = each(lambda e, t: t + _mm3(e, _Split(t)), e1, wu)
        wu = sp(each(lambda e, t: t + _mm3(e, _Split(t)), e2, wu))
        ry = each(_mm3, a_rb, wu)
        rkv = each(_mm3, a_rk, vs)
        mn = each(lambda b, g, t: _mm3_tn(_Split(b * g), t), b_, g_end, wu)
        kv = each(lambda k, g, t: _mm3_tn(_Split(k * g), t), k_, g_end, vs)
        rq = each(lambda r, t: r + t[:, :n], r_, ry)
        yl = each(lambda t, w: t[:, n:] + w, ry, rkv)
        for u in range(unroll):
            rq_ref[sls[u], :] = jnp.concatenate([rq[u], rq[unroll + u]], axis=1)
            yl_ref[sls[u], :] = jnp.concatenate([yl[u], yl[unroll + u]], axis=1)
        for i, (h, u) in enumerate(ids):
            m_ref[h, cs[u]] = mn[i][:, :n] + jnp.where(eye, g_end[i], 0.0)
            n_ref[h, cs[u]] = mn[i][:, n:] + kv[i]
        return 0

    lax.fori_loop(0, nchunk // unroll, body, 0)


def rwkv_chunk(at, bt, kt, rt, v, eg, *, batch, nchunk=8, unroll=4):
    t, w = at.shape
    s = t // batch
    ch, n = RW_CHUNK, HEAD_DIM
    pairs = w // LANE
    rows = nchunk * ch
    nr = s // rows
    spec = pl.BlockSpec((rows, LANE), lambda b, p, i: (b * nr + i, p))
    mspec = pl.BlockSpec((2, nchunk, n, n), lambda b, p, i: (b * pairs + p, i, 0, 0))
    return pl.pallas_call(
        functools.partial(_rwkv_chunk_kernel, nchunk=nchunk, unroll=unroll),
        grid=(batch, pairs, nr),
        in_specs=[spec] * 6,
        out_specs=[mspec, mspec, spec, spec],
        out_shape=[jax.ShapeDtypeStruct((batch * w // n, s // ch, n, n), F32)] * 2
        + [jax.ShapeDtypeStruct((t, w), F32)] * 2,
        compiler_params=_cp("parallel", "parallel", "parallel"),
        name="rwkv_chunk",
    )(at, bt, kt, rt, v, eg)


def _rwkv_scan_kernel(m_ref, n_ref, s_ref, st_ref, *, heads, nchunk):
    @pl.when(pl.program_id(1) == 0)
    def _():
        st_ref[...] = jnp.zeros_like(st_ref)

    def body(c, _):
        for hh in range(heads):
            st = st_ref[hh]
            s_ref[hh, c] = st
            st_ref[hh] = _dot(m_ref[hh, c], st, HI) + n_ref[hh, c]
        return 0

    lax.fori_loop(0, nchunk, body, 0)


def rwkv_scan(m, nn, *, heads=4, nchunk=32):
    bh, nc, n, _ = m.shape
    nchunk = min(nchunk, nc)
    spec = pl.BlockSpec((heads, nchunk, n, n), lambda b, i: (b, i, 0, 0))
    return pl.pallas_call(
        functools.partial(_rwkv_scan_kernel, heads=heads, nchunk=nchunk),
        grid=(bh // heads, nc // nchunk),
        in_specs=[spec, spec],
        out_specs=spec,
        out_shape=jax.ShapeDtypeStruct((bh, nc, n, n), F32),
        scratch_shapes=[pltpu.VMEM((heads, n, n), F32)],
        compiler_params=_cp("parallel", "arbitrary"),
        name="rwkv_scan",
    )(m, nn)


def _rwkv_out_kernel(rq_ref, yl_ref, s_ref, lw_ref, lb_ref, avg_ref, o_ref, *, nchunk):
    ch, n = RW_CHUNK, HEAD_DIM
    ys = []
    for h in range(2):
        cols = slice(h * n, (h + 1) * n)
        ys.append(jnp.concatenate(
            [_mm3(_Split(rq_ref[c * ch:(c + 1) * ch, cols]), _Split(s_ref[h, c])) for c in range(nchunk)],
            axis=0))
    y = jnp.concatenate(ys, axis=1) + yl_ref[...]
    avg2 = jnp.concatenate([avg_ref[...]] * 2, axis=0)

    def mean(t):
        ts = _Split(t)
        return _dot(jnp.concatenate([ts.hi, ts.lo], axis=1), avg2)

    d = y - mean(y)
    var = mean(d * d)
    o_ref[...] = d * lax.rsqrt(var + GN_EPS) * lw_ref[...] + lb_ref[...]


def rwkv_out(rq, yl, s0, ln_w, ln_b, *, batch, nchunk=8):
    t, w = rq.shape
    s = t // batch
    n = HEAD_DIM
    pairs = w // LANE
    rows = nchunk * RW_CHUNK
    nr = s // rows
    li = jnp.arange(LANE) // n
    avg = ((li[:, None] == li[None, :]).astype(F32) / n).astype(BF16)
    spec = pl.BlockSpec((rows, LANE), lambda b, p, i: (b * nr + i, p))
    hspec = pl.BlockSpec((1, LANE), lambda b, p, i: (0, p))
    return pl.pallas_call(
        functools.partial(_rwkv_out_kernel, nchunk=nchunk),
        grid=(batch, pairs, nr),
        in_specs=[spec, spec, pl.BlockSpec((2, nchunk, n, n), lambda b, p, i: (b * pairs + p, i, 0, 0)),
                  hspec, hspec, pl.BlockSpec((LANE, LANE), lambda b, p, i: (0, 0))],
        out_specs=spec,
        out_shape=jax.ShapeDtypeStruct((t, w), F32),
        compiler_params=_cp("parallel", "parallel", "parallel"),
        name="rwkv_out",
    )(rq, yl, s0, ln_w.reshape(1, w), ln_b.reshape(1, w), avg)


def _hyb_out_kernel(oa_ref, y_ref, bv_ref, g_ref, wa_ref, wb_ref, r_ref, o_ref):
    ob = (y_ref[...] + bv_ref[...]) * g_ref[...]
    o_ref[...] = (r_ref[...] + _dot(oa_ref[...].astype(BF16), wa_ref[...])
                  + _dot(ob.astype(BF16), wb_ref[...]))


def hyb_out(res, oa, y, bv, g, wa, wb, *, tm=512):
    t, d = res.shape
    row = lambda i: (i, 0)
    fix = lambda i: (0, 0)
    act = lambda a: pl.BlockSpec((tm, a.shape[1]), row)
    return pl.pallas_call(
        _hyb_out_kernel,
        grid=(t // tm,),
        in_specs=[act(oa), act(y), act(bv), act(g), pl.BlockSpec(wa.shape, fix),
                  pl.BlockSpec(wb.shape, fix), pl.BlockSpec((tm, d), row)],
        out_specs=pl.BlockSpec((tm, d), row),
        out_shape=jax.ShapeDtypeStruct((t, d), F32),
        compiler_params=_cp("parallel"),
        name="hyb_out",
    )(oa, y, bv, g, wa, wb, res)


def _pad_cols(w, n):
    return jnp.pad(w, ((0, 0), (0, n - w.shape[1])))


def sb_rwkv_layer(h, b, s, attn_norm, w_in, w_out, mu, w0, w2, a0, a2, g2, k_k, k_a, r_k, ln_w, ln_b):
    sbw = SB_HEADS * HEAD_DIM
    rww = RW_HEADS * HEAD_DIM
    rw_in = w_in.shape[1] - 3 * sbw
    rw_pad = -(-rw_in // LANE) * LANE
    dl, al = w2.shape[0], a2.shape[0]
    gl = g2.shape[0]
    assert dl + al == LANE and 3 * rww % LANE == 0

    qkv = rms_matmul(h, attn_norm, w_in[:, :3 * sbw].astype(BF16), out_dtype=BF16)
    rw = rms_matmul(h, attn_norm, _pad_cols(w_in[:, 3 * sbw:], rw_pad).astype(BF16), tn=rw_pad // 3)
    oa = sb_attention(qkv, batch=b, heads=SB_HEADS)

    w2p = jnp.zeros((LANE, rww), F32).at[:dl].set(w2)
    a2p = jnp.zeros((LANE, rww), F32).at[dl:].set(a2)
    g2p = jnp.zeros((rw_pad - 3 * rww - LANE, rww), F32).at[:gl].set(g2)
    vec = lambda t: t.reshape(1, rww).astype(F32)
    mu_p = _pad_cols(mu.reshape(1, rw_in), rw_pad)
    at, bt, kt, rt, vv, eg, g, bv = rwkv_prep(rw, mu_p, vec(w0), vec(a0), vec(k_k), vec(k_a), vec(r_k),
                                              w2p, a2p, g2p, seq=s, width=rww)
    m, nn, rq, yl = rwkv_chunk(at, bt, kt, rt, vv, eg, batch=b)
    s0 = rwkv_scan(m, nn)
    y = rwkv_out(rq, yl, s0, ln_w, ln_b, batch=b)
    return hyb_out(h, oa, y, bv, g, w_out[:sbw].astype(BF16), w_out[sbw:].astype(BF16))


def mla_layer(h, b, s, positions, attn_norm, w_down, q_norm, kv_norm, w_uq, w_ukv, w_o):
    nh, dn, dr, dv = MLA_HEADS, MLA_NOPE, MLA_ROPE, MLA_V
    qr, kvr = MLA_Q_RANK, MLA_KV_RANK
    assert dn == dv and qr % LANE == 0 and (qr + LANE) % kvr == 0
    wd = jnp.concatenate([w_down[:, :qr], jnp.tile(w_down[:, qr + kvr:], (1, LANE // dr)),
                          w_down[:, qr:qr + kvr]], axis=1)
    c = rms_matmul(h, attn_norm, wd.astype(BF16), tn=wd.shape[1])
    wq = w_uq.reshape(qr, nh, dn + dr)
    wq = jnp.concatenate([wq[:, :, :dn].reshape(qr, nh * dn), wq[:, :, dn:].reshape(qr, nh * dr)], axis=1)
    q = rms_matmul(c, q_norm, wq.astype(BF16), x_col=0)
    wkv = w_ukv.reshape(kvr, nh, dn + dv)
    wkv = jnp.concatenate([wkv[:, :, :dn].reshape(kvr, nh * dn), wkv[:, :, dn:].reshape(kvr, nh * dv)], axis=1)
    kv = rms_matmul(c, kv_norm, wkv.astype(BF16), x_col=(qr + LANE) // kvr, out_dtype=BF16)
    qrot, krot = rope(q, c, positions, q_col=dn // dr, qw=nh * dr, k_col=qr // LANE)
    o = softmax_attention(q, qrot, kv, krot, 1.0 / math.sqrt(dn + dr), batch=b, heads=nh)
    return matmul_residual(h, o, w_o.astype(BF16))


def kernel(x, p, positions, attn_norm, ffn_norm, ffn_w_in, ffn_conv_w, ffn_conv_b, ffn_w_out, ple_w_proj, ple_norm, ple_gate_norm, ple_w_gate, hyb_w_in, hyb_w_out, rw_mu, rw_w0, rw_w2, rw_a0, rw_a2, rw_g2, rw_k_k, rw_k_a, rw_r_k, rw_ln_w, rw_ln_b, mla_w_down, mla_q_norm, mla_kv_norm, mla_w_uq, mla_w_ukv, mla_w_o, final_norm):
    b, s, d = x.shape
    depth = p.shape[0]
    h = x.reshape(b * s, d)
    pos = positions.reshape(b * s)
    for i in range(depth):
        j = i // 2
        if i % 2 == 0:
            h = sb_rwkv_layer(h, b, s, attn_norm[i], hyb_w_in[j], hyb_w_out[j], rw_mu[j], rw_w0[j],
                              rw_w2[j], rw_a0[j], rw_a2[j], rw_g2[j], rw_k_k[j], rw_k_a[j], rw_r_k[j],
                              rw_ln_w[j], rw_ln_b[j])
        else:
            h = mla_layer(h, b, s, pos, attn_norm[i], mla_w_down[j], mla_q_norm[j], mla_kv_norm[j],
                          mla_w_uq[j], mla_w_ukv[j], mla_w_o[j])
        ug = rms_matmul(h, ffn_norm[i], ffn_w_in[i].astype(BF16), tm=1024, out_dtype=BF16)
        h = convglu_out(h, ug, ffn_conv_w[i], ffn_conv_b[i], ffn_w_out[i].astype(BF16), seq=s)
        h = ple(h, p[i].reshape(b * s, -1), ple_w_proj[i].astype(BF16), ple_norm[i], ple_gate_norm[i],
                ple_w_gate[i].astype(BF16), final_norm, final=(i == depth - 1))
    return h.reshape(b, s, d)
```

```python
import functools
import math

import jax
import jax.numpy as jnp
from jax import lax
from jax.experimental import pallas as pl
from jax.experimental.pallas import tpu as pltpu

F32 = jnp.float32
BF16 = jnp.bfloat16
HI = lax.Precision.HIGHEST

NORM_EPS = 1e-6
GN_EPS = 64e-5
ROPE_THETA = 10000.0
LANE = 128
VMEM_LIMIT = 48 * 1024 * 1024
HALO = 16

SB_HEADS = 8
RW_HEADS = 8
HEAD_DIM = 64
MLA_HEADS = 16
MLA_NOPE = 64
MLA_ROPE = 32
MLA_V = 64
MLA_Q_RANK = 384
MLA_KV_RANK = 256
SB_CUT = -104.0
RW_CHUNK = 64
RW_SUB = 16


def _cp(*sem):
    return pltpu.CompilerParams(dimension_semantics=sem, vmem_limit_bytes=VMEM_LIMIT)


def _iota(shape, dim):
    return lax.broadcasted_iota(jnp.int32, shape, dim)


def _softplus(z):
    return jnp.maximum(z, 0.0) + jnp.log1p(jnp.exp(-jnp.abs(z)))


def _sigmoid(z):
    return 1.0 / (1.0 + jnp.exp(-z))


def _dot(a, b, precision=None):
    return jnp.dot(a, b, preferred_element_type=F32, precision=precision)


def _dot_nt(a, b, precision=None):
    return lax.dot_general(a, b, (((1,), (1,)), ((), ())), preferred_element_type=F32,
                           precision=precision)


def _dot_tn(a, b, precision=None):
    return lax.dot_general(a, b, (((0,), (0,)), ((), ())), preferred_element_type=F32,
                           precision=precision)


class _Split:
    def __init__(self, x):
        self.x = x
        self.hi = x.astype(BF16)
        self.lo = (x - self.hi.astype(F32)).astype(BF16)
        self._packed = {}

    def left(self, axis):
        key = ("l", axis)
        if key not in self._packed:
            self._packed[key] = jnp.concatenate([self.hi, self.lo, self.hi], axis=axis)
        return self._packed[key]

    def right(self, axis):
        key = ("r", axis)
        if key not in self._packed:
            self._packed[key] = jnp.concatenate([self.hi, self.hi, self.lo], axis=axis)
        return self._packed[key]


def _mm3(a, b):
    return _dot(a.left(1), b.right(0))


def _mm3_nt(a, b):
    return _dot_nt(a.left(1), b.right(1))


def _mm3_tn(a, b):
    return _dot_tn(a.left(0), b.right(0))


def _rms(x, g, eps=NORM_EPS):
    return x * lax.rsqrt(jnp.mean(x * x, axis=-1, keepdims=True) + eps) * g


def _rms_matmul_kernel(x_ref, g_ref, w_ref, o_ref, xn_ref):
    @pl.when(pl.program_id(1) == 0)
    def _():
        xn_ref[...] = _rms(x_ref[...].astype(F32), g_ref[...]).astype(BF16)

    o_ref[...] = _dot(xn_ref[...], w_ref[...]).astype(o_ref.dtype)


def rms_matmul(x, g, w, *, x_col=0, tm=512, tn=512, out_dtype=F32):
    t = x.shape[0]
    k, n = w.shape
    tn = min(tn, n)
    return pl.pallas_call(
        _rms_matmul_kernel,
        grid=(t // tm, n // tn),
        in_specs=[pl.BlockSpec((tm, k), lambda i, j: (i, x_col)),
                  pl.BlockSpec((1, k), lambda i, j: (0, 0)),
                  pl.BlockSpec((k, tn), lambda i, j: (0, j))],
        out_specs=pl.BlockSpec((tm, tn), lambda i, j: (i, j)),
        out_shape=jax.ShapeDtypeStruct((t, n), out_dtype),
        scratch_shapes=[pltpu.VMEM((tm, k), BF16)],
        compiler_params=_cp("parallel", "arbitrary"),
        name="rms_matmul",
    )(x, g.reshape(1, k).astype(F32), w)


def _matmul_res_kernel(a_ref, w_ref, r_ref, o_ref):
    o_ref[...] = r_ref[...] + _dot(a_ref[...].astype(BF16), w_ref[...])


def matmul_residual(res, a, w, *, tm=512, tn=512):
    t, k = a.shape
    n = w.shape[1]
    return pl.pallas_call(
        _matmul_res_kernel,
        grid=(t // tm, n // tn),
        in_specs=[pl.BlockSpec((tm, k), lambda i, j: (i, 0)),
                  pl.BlockSpec((k, tn), lambda i, j: (0, j)),
                  pl.BlockSpec((tm, tn), lambda i, j: (i, j))],
        out_specs=pl.BlockSpec((tm, tn), lambda i, j: (i, j)),
        out_shape=jax.ShapeDtypeStruct((t, n), F32),
        compiler_params=_cp("parallel", "arbitrary"),
        name="matmul_residual",
    )(a, w, res)


def _convglu_out_kernel(u_ref, g_ref, gp_ref, cw_ref, cb_ref, w_ref, r_ref, o_ref, *, tm, seq):
    i = pl.program_id(0)
    g = g_ref[...].astype(F32)
    at_start = (i * tm) % seq == 0
    halo = jnp.where(at_start, 0.0, gp_ref[...].astype(F32))
    h1, h2 = halo[HALO - 1:HALO, :], halo[HALO - 2:HALO - 1, :]
    r1, r2 = pltpu.roll(g, 1, axis=0), pltpu.roll(g, 2, axis=0)
    top = _iota((8, g.shape[1]), 0)
    p1 = jnp.concatenate([jnp.where(top == 0, h1, r1[:8]), r1[8:]], axis=0)
    p2 = jnp.concatenate([jnp.where(top == 0, h2, jnp.where(top == 1, h1, r2[:8])), r2[8:]], axis=0)
    c = cw_ref[0:1, :] * p2 + cw_ref[1:2, :] * p1 + cw_ref[2:3, :] * g + cb_ref[...]
    act = 0.5 * c * (1.0 + lax.erf(c * (1.0 / math.sqrt(2.0)))) * u_ref[...].astype(F32)
    o_ref[...] = r_ref[...] + _dot(act.astype(BF16), w_ref[...])


def convglu_out(res, ug, conv_w, conv_b, w_out, *, seq, tm=256):
    t = res.shape[0]
    f, n = w_out.shape
    hb = tm // HALO
    return pl.pallas_call(
        functools.partial(_convglu_out_kernel, tm=tm, seq=seq),
        grid=(t // tm,),
        in_specs=[pl.BlockSpec((tm, f), lambda i: (i, 0)),
                  pl.BlockSpec((tm, f), lambda i: (i, 1)),
                  pl.BlockSpec((HALO, f), lambda i: (jnp.maximum(i * hb - 1, 0), 1)),
                  pl.BlockSpec((3, f), lambda i: (0, 0)),
                  pl.BlockSpec((1, f), lambda i: (0, 0)),
                  pl.BlockSpec((f, n), lambda i: (0, 0)),
                  pl.BlockSpec((tm, n), lambda i: (i, 0))],
        out_specs=pl.BlockSpec((tm, n), lambda i: (i, 0)),
        out_shape=jax.ShapeDtypeStruct((t, n), F32),
        compiler_params=_cp("parallel"),
        name="convglu_out",
    )(ug, ug, ug, conv_w, conv_b.reshape(1, f), w_out, res)


def _ple_kernel(h_ref, p_ref, wp_ref, pn_ref, gn_ref, wg_ref, fn_ref, o_ref, *, final):
    h = h_ref[...]
    e = _rms(_dot(p_ref[...].astype(BF16), wp_ref[...]), pn_ref[...])
    gate = _sigmoid(_dot(_rms(h, gn_ref[...]).astype(BF16), wg_ref[...]))
    out = h + gate * e
    if final:
        out = _rms(out, fn_ref[...])
    o_ref[...] = out


def ple(h, p, w_proj, p_norm, g_norm, w_gate, f_norm, *, final, tm=512):
    t, d = h.shape
    pd = p.shape[1]
    row = lambda i: (i, 0)
    fix = lambda i: (0, 0)
    return pl.pallas_call(
        functools.partial(_ple_kernel, final=final),
        grid=(t // tm,),
        in_specs=[pl.BlockSpec((tm, d), row), pl.BlockSpec((tm, pd), row),
                  pl.BlockSpec((pd, d), fix), pl.BlockSpec((1, d), fix), pl.BlockSpec((1, d), fix),
                  pl.BlockSpec((d, d), fix), pl.BlockSpec((1, d), fix)],
        out_specs=pl.BlockSpec((tm, d), row),
        out_shape=jax.ShapeDtypeStruct((t, d), F32),
        compiler_params=_cp("parallel"),
        name="ple",
    )(h, p, w_proj, p_norm.reshape(1, d), g_norm.reshape(1, d), w_gate, f_norm.reshape(1, d))


def _sb_attn_kernel(q_ref, k_ref, v_ref, u_ref, o_ref, *, tq, tk, scale):
    qi = pl.program_id(2)
    nblk = tq // tk
    first = _iota((tq, LANE), 1) < HEAD_DIM
    qf = q_ref[...] * scale
    zero = jnp.zeros_like(qf)
    qs = (jnp.where(first, qf, zero), jnp.where(first, zero, qf))

    def sweep(k_lo, runs, accs, masked):
        parts = []
        for d in reversed(range(nblk)):
            k0 = k_lo + d * tk
            kb = k_ref[pl.ds(k0, tk), :]
            valid = None
            if masked:
                valid = _iota((tq, tk), 1) + d * tk < _iota((tq, tk), 0)
            for h in range(2):
                z = _dot_nt(qs[h], kb)
                sp = _softplus(z)
                lom = -sp
                if masked:
                    lom = jnp.where(valid, lom, 0.0)
                hi = lom.astype(BF16)
                lo = (lom - hi.astype(F32)).astype(BF16)
                sums = _dot(jnp.concatenate([hi, lo], axis=1), u_ref[...])
                parts.append((h, k0, z - sp, sums, valid))
        runs, accs = list(runs), list(accs)
        for h, k0, logsig, sums, valid in parts:
            w = jnp.exp(logsig + runs[h] + sums[:, :tk])
            if masked:
                w = jnp.where(valid, w, 0.0)
            accs[h] = accs[h] + _dot(w.astype(BF16), v_ref[pl.ds(k0, tk), :])
            runs[h] = runs[h] + sums[:, tk:]
        return tuple(runs), tuple(accs)

    runs = (jnp.zeros((tq, tk), F32),) * 2
    accs = (jnp.zeros((tq, LANE), F32),) * 2
    runs, accs = sweep(pl.multiple_of(qi * tq, tq), runs, accs, True)

    def live(runs):
        return jnp.max(jnp.maximum(runs[0], runs[1])) > SB_CUT

    def cond(c):
        return c[0]

    def body(c):
        _, j, runs, accs = c
        runs, accs = sweep(pl.multiple_of(j * tq, tq), runs, accs, False)
        return (j > 0) & live(runs), j - 1, runs, accs

    _, _, _, accs = lax.while_loop(cond, body, ((qi > 0) & live(runs), qi - 1, runs, accs))
    o_ref[...] = jnp.where(first, accs[0], accs[1])


def sb_attention(qkv, *, batch, heads, tq=256):
    t, w3 = qkv.shape
    s = t // batch
    nq = s // tq
    pairs = heads * HEAD_DIM // LANE
    tk = LANE
    jj = jnp.arange(2 * tk)[:, None] % tk
    ss = jnp.arange(2 * tk)[None, :]
    u = jnp.where(ss < tk, jj > ss, True).astype(BF16)
    return pl.pallas_call(
        functools.partial(_sb_attn_kernel, tq=tq, tk=tk, scale=1.0 / math.sqrt(HEAD_DIM)),
        grid=(batch, pairs, nq),
        in_specs=[pl.BlockSpec((tq, LANE), lambda b, p, i: (b * nq + i, p)),
                  pl.BlockSpec((s, LANE), lambda b, p, i: (b, pairs + p)),
                  pl.BlockSpec((s, LANE), lambda b, p, i: (b, 2 * pairs + p)),
                  pl.BlockSpec((2 * tk, 2 * tk), lambda b, p, i: (0, 0))],
        out_specs=pl.BlockSpec((tq, LANE), lambda b, p, i: (b * nq + i, p)),
        out_shape=jax.ShapeDtypeStruct((t, w3 // 3), F32),
        compiler_params=_cp("parallel", "parallel", "arbitrary"),
        name="sb_attention",
    )(qkv, qkv, qkv, u)


def _softmax_attn_kernel(qn_ref, qr_ref, kn_ref, kr_ref, v_ref, o_ref, sa0, sa1, sb0, sb1,
                         *, tq, tk, scale):
    hp = pl.program_id(1)
    qi = pl.program_id(2)
    lane = _iota((tq, LANE), 1)
    qn = qn_ref[...] * (scale * math.log2(math.e))
    qr = qr_ref[...] * (scale * math.log2(math.e))
    qs = []
    for h in range(2):
        nope = jnp.where(lane // MLA_NOPE == h, qn, 0.0)
        rope_ = jnp.where(lane // MLA_ROPE == (hp % 2) * 2 + h, qr, 0.0)
        qs.append(jnp.concatenate([nope, rope_], axis=1).astype(BF16))

    def scores(j, dsts):
        k0 = pl.multiple_of(j * tk, tk)
        kb = jnp.concatenate([kn_ref[pl.ds(k0, tk), :], kr_ref[pl.ds(k0, tk), :]], axis=1)
        mx = []
        for h in range(2):
            s = _dot_nt(qs[h], kb)
            dsts[h][...] = s
            mx.append(jnp.max(s, axis=1, keepdims=True))
        return tuple(mx)

    def consume(srcs, j, carry, mx):
        vb = v_ref[pl.ds(pl.multiple_of(j * tk, tk), tk), :]
        one = jnp.ones_like(vb)
        first = _iota(vb.shape, 1) < MLA_V
        vbs = (jnp.where(first, vb, one), jnp.where(first, one, vb))
        out = []
        for h in range(2):
            m, acc = carry[h]
            s = srcs[h][...]
            if mx is None:
                s = jnp.where(_iota((tq, tk), 1) <= _iota((tq, tk), 0), s, -1e30)
                m_new = jnp.maximum(m, jnp.max(s, axis=1, keepdims=True))
            else:
                m_new = jnp.maximum(m, mx[h])
            p = jnp.exp2(s - m_new)
            acc = jnp.exp2(m - m_new) * acc + _dot(p.astype(BF16), vbs[h])
            out.append((m_new, acc))
        return tuple(out)

    sa, sb = (sa0, sa1), (sb0, sb1)

    def pair(t, carry):
        state, mxa = carry
        mxb = scores(2 * t + 1, sb)
        state = consume(sa, 2 * t, state, mxa)
        mxa = scores(2 * t + 2, sa)
        return consume(sb, 2 * t + 1, state, mxb), mxa

    def tail_odd(carry):
        state, mxa = carry
        scores(qi, sb)
        state = consume(sa, qi - 1, state, mxa)
        return consume(sb, qi, state, None)

    def tail_even(carry):
        return consume(sa, qi, carry[0], None)

    init = ((jnp.full((tq, 1), -1e30, F32), jnp.zeros((tq, LANE), F32)),) * 2
    carry = lax.fori_loop(0, qi // 2, pair, (init, scores(0, sa)))
    (_, acc0), (_, acc1) = lax.cond(qi % 2 == 1, tail_odd, tail_even, carry)
    o_ref[...] = jnp.where(lane < MLA_V, acc0 / pltpu.roll(acc0, MLA_V, axis=1),
                           acc1 / pltpu.roll(acc1, MLA_V, axis=1))


def softmax_attention(q, qrot, kv, krot, scale, *, batch, heads, tq=512):
    tk = tq
    t = q.shape[0]
    s = t // batch
    nq = s // tq
    pairs = heads * MLA_NOPE // LANE
    buf = pltpu.VMEM((tq, tk), F32)
    return pl.pallas_call(
        functools.partial(_softmax_attn_kernel, tq=tq, tk=tk, scale=scale),
        grid=(batch, pairs, nq),
        in_specs=[pl.BlockSpec((tq, LANE), lambda b, p, i: (b * nq + i, p)),
                  pl.BlockSpec((tq, LANE), lambda b, p, i: (b * nq + i, p // 2)),
                  pl.BlockSpec((s, LANE), lambda b, p, i: (b, p)),
                  pl.BlockSpec((s, LANE), lambda b, p, i: (b, 0)),
                  pl.BlockSpec((s, LANE), lambda b, p, i: (b, pairs + p))],
        out_specs=pl.BlockSpec((tq, LANE), lambda b, p, i: (b * nq + i, p)),
        out_shape=jax.ShapeDtypeStruct((t, heads * MLA_V), F32),
        scratch_shapes=[buf, buf, buf, buf],
        compiler_params=_cp("parallel", "parallel", "arbitrary"),
        name="softmax_attention",
    )(q, qrot, kv, krot, kv)


def _rope_kernel(q_ref, k_ref, pos_ref, f_ref, oq_ref, ok_ref):
    half = MLA_ROPE // 2
    ang = pos_ref[...].astype(F32) * f_ref[...]
    cos, sin = jnp.cos(ang), jnp.sin(ang)
    low = _iota(ang.shape, 1) % MLA_ROPE < half

    def rot(x):
        partner = jnp.where(low, -pltpu.roll(x, LANE - half, axis=1), pltpu.roll(x, half, axis=1))
        return x * cos + partner * sin

    for c in range(q_ref.shape[1] // LANE):
        cols = slice(c * LANE, (c + 1) * LANE)
        oq_ref[:, cols] = rot(q_ref[:, cols])
    ok_ref[...] = rot(k_ref[...]).astype(ok_ref.dtype)


def rope(q, c, positions, *, q_col, qw, k_col, tm=512):
    t = q.shape[0]
    half = MLA_ROPE // 2
    inv_freq = ROPE_THETA ** (-jnp.arange(half, dtype=F32) / half)
    freq = jnp.tile(inv_freq, LANE // half).reshape(1, LANE)
    return pl.pallas_call(
        _rope_kernel,
        grid=(t // tm,),
        in_specs=[pl.BlockSpec((tm, qw), lambda i: (i, q_col)), pl.BlockSpec((tm, LANE), lambda i: (i, k_col)),
                  pl.BlockSpec((tm, 1), lambda i: (i, 0)), pl.BlockSpec((1, LANE), lambda i: (0, 0))],
        out_specs=[pl.BlockSpec((tm, qw), lambda i: (i, 0)), pl.BlockSpec((tm, LANE), lambda i: (i, 0))],
        out_shape=[jax.ShapeDtypeStruct((t, qw), F32), jax.ShapeDtypeStruct((t, LANE), BF16)],
        compiler_params=_cp("parallel"),
        name="rope",
    )(q, c, positions.reshape(t, 1), freq)


def _rwkv_prep_kernel(x_ref, xp_ref, mu_ref, w0_ref, a0_ref, kk_ref, ka_ref, rk_ref,
                      w2_ref, a2_ref, g2_ref, tri_ref, hs_ref,
                      at_ref, bt_ref, kt_ref, rt_ref, v_ref, eg_ref, g_ref, bv_ref, *, tm, seq, width):
    i = pl.program_id(0)
    x = x_ref[...]
    at_start = (i * tm) % seq == 0
    prev_row = jnp.where(at_start, 0.0, xp_ref[7:8, :])
    rolled = pltpu.roll(x, 1, axis=0)
    top = _iota((8, x.shape[1]), 0)
    prev = jnp.concatenate([jnp.where(top == 0, prev_row, rolled[:8]), rolled[8:]], axis=0)
    xm = x + (prev - x) * mu_ref[...]
    r = xm[:, :width]
    k = xm[:, width:2 * width]
    v = xm[:, 2 * width:3 * width]
    l1 = xm[:, 3 * width:3 * width + LANE]
    l2 = xm[:, 3 * width + LANE:]
    wpre = w0_ref[...] + _dot(_Split(jnp.tanh(l1)).left(1), w2_ref[...])
    a = _sigmoid(a0_ref[...] + _dot(_Split(l1).left(1), a2_ref[...]))
    g = _dot(_Split(_sigmoid(l2)).left(1), g2_ref[...])

    def hilo(t):
        ts = _Split(t)
        return jnp.concatenate([ts.hi, ts.lo], axis=1)

    lw = -jnp.exp(-_softplus(-wpre) - 0.5)
    hs = hs_ref[...]
    kk = k * kk_ref[...]
    kk = kk * lax.rsqrt(jnp.maximum(_dot(hilo(kk * kk), hs), 1e-24))
    km = k * (1.0 + (a - 1.0) * ka_ref[...])
    bonus = _dot(hilo(r * km * rk_ref[...]), hs)
    lws = _Split(lw)
    cum = _dot(tri_ref[...], jnp.concatenate([lws.hi, lws.lo], axis=0))
    eg = jnp.exp(cum)
    ieg = jnp.exp(-cum)
    at_ref[...] = -kk * jnp.exp(cum - lw)
    bt_ref[...] = kk * a * ieg
    kt_ref[...] = km * ieg
    rt_ref[...] = r * eg
    v_ref[...] = v
    eg_ref[...] = eg
    g_ref[...] = g
    bv_ref[...] = bonus * v


def rwkv_prep(x, mu, w0, a0, k_k, k_a, r_k, w2p, a2p, g2p, *, seq, width, tm=256):
    t, wx = x.shape
    ch = RW_CHUNK
    ti = jnp.arange(tm)
    tri = ((ti[:, None] // ch == ti[None, :] // ch) & (ti[None, :] <= ti[:, None])).astype(BF16)
    tri = jnp.concatenate([tri, tri], axis=1)
    ci = jnp.arange(width) // HEAD_DIM
    hs = (ci[:, None] == ci[None, :]).astype(BF16)
    hs = jnp.concatenate([hs, hs], axis=0)
    w2p, a2p, g2p = (_Split(w).right(0) for w in (w2p, a2p, g2p))
    hb = tm // 8
    row = lambda i: (i, 0)
    fix = lambda i: (0, 0)
    vec = pl.BlockSpec((1, width), fix)
    out = jax.ShapeDtypeStruct((t, width), F32)
    return pl.pallas_call(
        functools.partial(_rwkv_prep_kernel, tm=tm, seq=seq, width=width),
        grid=(t // tm,),
        in_specs=[pl.BlockSpec((tm, wx), row),
                  pl.BlockSpec((8, wx), lambda i: (jnp.maximum(i * hb - 1, 0), 0)),
                  pl.BlockSpec((1, wx), fix), vec, vec, vec, vec, vec,
                  pl.BlockSpec(w2p.shape, fix), pl.BlockSpec(a2p.shape, fix),
                  pl.BlockSpec(g2p.shape, fix), pl.BlockSpec(tri.shape, fix),
                  pl.BlockSpec(hs.shape, fix)],
        out_specs=[pl.BlockSpec((tm, width), row)] * 8,
        out_shape=[out] * 8,
        compiler_params=_cp("parallel"),
        name="rwkv_prep",
    )(x, x, mu, w0, a0, k_k, k_a, r_k, w2p, a2p, g2p, tri, hs)


def _rwkv_chunk_kernel(at_ref, bt_ref, kt_ref, rt_ref, v_ref, eg_ref,
                       m_ref, n_ref, rq_ref, yl_ref, *, nchunk, unroll):
    ch, n = RW_CHUNK, HEAD_DIM
    ti = _iota((ch, ch), 0)
    si = _iota((ch, ch), 1)
    strict = si < ti
    incl = si <= ti
    same = (ti // RW_SUB) == (si // RW_SUB)
    eye = _iota((n, n), 0) == _iota((n, n), 1)
    each = lambda f, *ls: [f(*t) for t in zip(*ls)]

    def body(it, _):
        cs = [it * unroll + u for u in range(unroll)]
        sls = [pl.ds(pl.multiple_of(c * ch, ch), ch) for c in cs]
        ids = [(h, u) for h in range(2) for u in range(unroll)]
        ld = lambda ref: [ref[sls[u], h * n:(h + 1) * n] for h, u in ids]
        sp = lambda xs: [_Split(t) for t in xs]
        a_, b_, k_, r_, v_ = ld(at_ref), ld(bt_ref), ld(kt_ref), ld(rt_ref), ld(v_ref)
        g_end = [eg_ref[pl.ds(cs[u] * ch + ch - 1, 1), h * n:(h + 1) * n] for h, u in ids]
        ar = sp(each(lambda a, r: jnp.concatenate([a, r], axis=0), a_, r_))
        vs = sp(v_)
        pb = each(_mm3_nt, ar, sp(b_))
        pk = each(_mm3_nt, ar, sp(k_))
        a_ab = [jnp.where(strict, t[:ch], 0.0) for t in pb]
        a_ak = sp([jnp.where(strict, t[:ch], 0.0) for t in pk])
        a_rb = sp([jnp.where(incl, t[ch:], 0.0) for t in pb])
        a_rk = sp([jnp.where(incl, t[ch:], 0.0) for t in pk])
        d1 = [jnp.where(same, t, 0.0) for t in a_ab]
        lb = each(lambda t, d: t - d, a_ab, d1)
        d1 = sp(d1)
        d2 = sp(each(_mm3, d1, d1))
        akv = each(_mm3, a_ak, vs)
        x = each(lambda l, a, t: jnp.concatenate([l, a, t], axis=1), lb, a_, akv)
        d4 = sp(each(_mm3, d2, d2))
        x = each(lambda d, t: t + _mm3(d, _Split(t)), d1, x)
        d8 = sp(each(_mm3, d4, d4))
        x = each(lambda d, t: t + _mm3(d, _Split(t)), d2, x)
        x = each(lambda d, t: t + _mm3(d, _Split(t)), d4, x)
        x = each(lambda d, t: t + _mm3(d, _Split(t)), d8, x)
        e1 = sp([t[:, :ch] for t in x])
        wu = [t[:, ch:] for t in x]
        e2 = sp(each(_mm3, e1, e1))
        wu = each(lambda e, t: t + _mm3(e, _Split(t)), e1, wu)
        wu = sp(each(lambda e, t: t + _mm3(e, _Split(t)), e2, wu))
        ry = each(_mm3, a_rb, wu)
        rkv = each(_mm3, a_rk, vs)
        mn = each(lambda b, g, t: _mm3_tn(_Split(b * g), t), b_, g_end, wu)
        kv = each(lambda k, g, t: _mm3_tn(_Split(k * g), t), k_, g_end, vs)
        rq = each(lambda r, t: r + t[:, :n], r_, ry)
        yl = each(lambda t, w: t[:, n:] + w, ry, rkv)
        for u in range(unroll):
            rq_ref[sls[u], :] = jnp.concatenate([rq[u], rq[unroll + u]], axis=1)
            yl_ref[sls[u], :] = jnp.concatenate([yl[u], yl[unroll + u]], axis=1)
        for i, (h, u) in enumerate(ids):
            m_ref[h, cs[u]] = mn[i][:, :n] + jnp.where(eye, g_end[i], 0.0)
            n_ref[h, cs[u]] = mn[i][:, n:] + kv[i]
        return 0

    lax.fori_loop(0, nchunk // unroll, body, 0)


def rwkv_chunk(at, bt, kt, rt, v, eg, *, batch, nchunk=8, unroll=4):
    t, w = at.shape
    s = t // batch
    ch, n = RW_CHUNK, HEAD_DIM
    pairs = w // LANE
    rows = nchunk * ch
    nr = s // rows
    spec = pl.BlockSpec((rows, LANE), lambda b, p, i: (b * nr + i, p))
    mspec = pl.BlockSpec((2, nchunk, n, n), lambda b, p, i: (b * pairs + p, i, 0, 0))
    return pl.pallas_call(
        functools.partial(_rwkv_chunk_kernel, nchunk=nchunk, unroll=unroll),
        grid=(batch, pairs, nr),
        in_specs=[spec] * 6,
        out_specs=[mspec, mspec, spec, spec],
        out_shape=[jax.ShapeDtypeStruct((batch * w // n, s // ch, n, n), F32)] * 2
        + [jax.ShapeDtypeStruct((t, w), F32)] * 2,
        compiler_params=_cp("parallel", "parallel", "parallel"),
        name="rwkv_chunk",
    )(at, bt, kt, rt, v, eg)


def _rwkv_scan_kernel(m_ref, n_ref, s_ref, st_ref, *, heads, nchunk):
    @pl.when(pl.program_id(1) == 0)
    def _():
        st_ref[...] = jnp.zeros_like(st_ref)

    def body(c, _):
        for hh in range(heads):
            st = st_ref[hh]
            s_ref[hh, c] = st
            st_ref[hh] = _dot(m_ref[hh, c], st, HI) + n_ref[hh, c]
        return 0

    lax.fori_loop(0, nchunk, body, 0)


def rwkv_scan(m, nn, *, heads=4, nchunk=32):
    bh, nc, n, _ = m.shape
    nchunk = min(nchunk, nc)
    spec = pl.BlockSpec((heads, nchunk, n, n), lambda b, i: (b, i, 0, 0))
    return pl.pallas_call(
        functools.partial(_rwkv_scan_kernel, heads=heads, nchunk=nchunk),
        grid=(bh // heads, nc // nchunk),
        in_specs=[spec, spec],
        out_specs=spec,
        out_shape=jax.ShapeDtypeStruct((bh, nc, n, n), F32),
        scratch_shapes=[pltpu.VMEM((heads, n, n), F32)],
        compiler_params=_cp("parallel", "arbitrary"),
        name="rwkv_scan",
    )(m, nn)


def _rwkv_out_kernel(rq_ref, yl_ref, s_ref, lw_ref, lb_ref, avg_ref, o_ref, *, nchunk):
    ch, n = RW_CHUNK, HEAD_DIM
    ys = []
    for h in range(2):
        cols = slice(h * n, (h + 1) * n)
        ys.append(jnp.concatenate(
            [_mm3(_Split(rq_ref[c * ch:(c + 1) * ch, cols]), _Split(s_ref[h, c])) for c in range(nchunk)],
            axis=0))
    y = jnp.concatenate(ys, axis=1) + yl_ref[...]
    avg2 = jnp.concatenate([avg_ref[...]] * 2, axis=0)

    def mean(t):
        ts = _Split(t)
        return _dot(jnp.concatenate([ts.hi, ts.lo], axis=1), avg2)

    d = y - mean(y)
    var = mean(d * d)
    o_ref[...] = d * lax.rsqrt(var + GN_EPS) * lw_ref[...] + lb_ref[...]


def rwkv_out(rq, yl, s0, ln_w, ln_b, *, batch, nchunk=8):
    t, w = rq.shape
    s = t // batch
    n = HEAD_DIM
    pairs = w // LANE
    rows = nchunk * RW_CHUNK
    nr = s // rows
    li = jnp.arange(LANE) // n
    avg = ((li[:, None] == li[None, :]).astype(F32) / n).astype(BF16)
    spec = pl.BlockSpec((rows, LANE), lambda b, p, i: (b * nr + i, p))
    hspec = pl.BlockSpec((1, LANE), lambda b, p, i: (0, p))
    return pl.pallas_call(
        functools.partial(_rwkv_out_kernel, nchunk=nchunk),
        grid=(batch, pairs, nr),
        in_specs=[spec, spec, pl.BlockSpec((2, nchunk, n, n), lambda b, p, i: (b * pairs + p, i, 0, 0)),
                  hspec, hspec, pl.BlockSpec((LANE, LANE), lambda b, p, i: (0, 0))],
        out_specs=spec,
        out_shape=jax.ShapeDtypeStruct((t, w), F32),
        compiler_params=_cp("parallel", "parallel", "parallel"),
        name="rwkv_out",
    )(rq, yl, s0, ln_w.reshape(1, w), ln_b.reshape(1, w), avg)


def _hyb_out_kernel(oa_ref, y_ref, bv_ref, g_ref, wa_ref, wb_ref, r_ref, o_ref):
    ob = (y_ref[...] + bv_ref[...]) * g_ref[...]
    o_ref[...] = (r_ref[...] + _dot(oa_ref[...].astype(BF16), wa_ref[...])
                  + _dot(ob.astype(BF16), wb_ref[...]))


def hyb_out(res, oa, y, bv, g, wa, wb, *, tm=512):
    t, d = res.shape
    row = lambda i: (i, 0)
    fix = lambda i: (0, 0)
    act = lambda a: pl.BlockSpec((tm, a.shape[1]), row)
    return pl.pallas_call(
        _hyb_out_kernel,
        grid=(t // tm,),
        in_specs=[act(oa), act(y), act(bv), act(g), pl.BlockSpec(wa.shape, fix),
                  pl.BlockSpec(wb.shape, fix), pl.BlockSpec((tm, d), row)],
        out_specs=pl.BlockSpec((tm, d), row),
        out_shape=jax.ShapeDtypeStruct((t, d), F32),
        compiler_params=_cp("parallel"),
        name="hyb_out",
    )(oa, y, bv, g, wa, wb, res)


def _pad_cols(w, n):
    return jnp.pad(w, ((0, 0), (0, n - w.shape[1])))


def sb_rwkv_layer(h, b, s, attn_norm, w_in, w_out, mu, w0, w2, a0, a2, g2, k_k, k_a, r_k, ln_w, ln_b):
    sbw = SB_HEADS * HEAD_DIM
    rww = RW_HEADS * HEAD_DIM
    rw_in = w_in.shape[1] - 3 * sbw
    rw_pad = -(-rw_in // LANE) * LANE
    dl, al = w2.shape[0], a2.shape[0]
    gl = g2.shape[0]
    assert dl + al == LANE and 3 * rww % LANE == 0

    qkv = rms_matmul(h, attn_norm, w_in[:, :3 * sbw].astype(BF16), tm=1024, out_dtype=BF16)
    rw = rms_matmul(h, attn_norm, _pad_cols(w_in[:, 3 * sbw:], rw_pad).astype(BF16), tm=1024,
                    tn=rw_pad // 3)
    oa = sb_attention(qkv, batch=b, heads=SB_HEADS)

    w2p = jnp.zeros((LANE, rww), F32).at[:dl].set(w2)
    a2p = jnp.zeros((LANE, rww), F32).at[dl:].set(a2)
    g2p = jnp.zeros((rw_pad - 3 * rww - LANE, rww), F32).at[:gl].set(g2)
    vec = lambda t: t.reshape(1, rww).astype(F32)
    mu_p = _pad_cols(mu.reshape(1, rw_in), rw_pad)
    at, bt, kt, rt, vv, eg, g, bv = rwkv_prep(rw, mu_p, vec(w0), vec(a0), vec(k_k), vec(k_a), vec(r_k),
                                              w2p, a2p, g2p, seq=s, width=rww)
    m, nn, rq, yl = rwkv_chunk(at, bt, kt, rt, vv, eg, batch=b)
    s0 = rwkv_scan(m, nn)
    y = rwkv_out(rq, yl, s0, ln_w, ln_b, batch=b)
    return hyb_out(h, oa, y, bv, g, w_out[:sbw].astype(BF16), w_out[sbw:].astype(BF16))


def mla_layer(h, b, s, positions, attn_norm, w_down, q_norm, kv_norm, w_uq, w_ukv, w_o):
    nh, dn, dr, dv = MLA_HEADS, MLA_NOPE, MLA_ROPE, MLA_V
    qr, kvr = MLA_Q_RANK, MLA_KV_RANK
    assert dn == dv and qr % LANE == 0 and (qr + LANE) % kvr == 0
    wd = jnp.concatenate([w_down[:, :qr], jnp.tile(w_down[:, qr + kvr:], (1, LANE // dr)),
                          w_down[:, qr:qr + kvr]], axis=1)
    c = rms_matmul(h, attn_norm, wd.astype(BF16), tn=wd.shape[1])
    wq = w_uq.reshape(qr, nh, dn + dr)
    wq = jnp.concatenate([wq[:, :, :dn].reshape(qr, nh * dn), wq[:, :, dn:].reshape(qr, nh * dr)], axis=1)
    q = rms_matmul(c, q_norm, wq.astype(BF16), x_col=0)
    wkv = w_ukv.reshape(kvr, nh, dn + dv)
    wkv = jnp.concatenate([wkv[:, :, :dn].reshape(kvr, nh * dn), wkv[:, :, dn:].reshape(kvr, nh * dv)], axis=1)
    kv = rms_matmul(c, kv_norm, wkv.astype(BF16), x_col=(qr + LANE) // kvr, out_dtype=BF16)
    qrot, krot = rope(q, c, positions, q_col=dn // dr, qw=nh * dr, k_col=qr // LANE)
    o = softmax_attention(q, qrot, kv, krot, 1.0 / math.sqrt(dn + dr), batch=b, heads=nh)
    return matmul_residual(h, o, w_o.astype(BF16))


def kernel(x, p, positions, attn_norm, ffn_norm, ffn_w_in, ffn_conv_w, ffn_conv_b, ffn_w_out, ple_w_proj, ple_norm, ple_gate_norm, ple_w_gate, hyb_w_in, hyb_w_out, rw_mu, rw_w0, rw_w2, rw_a0, rw_a2, rw_g2, rw_k_k, rw_k_a, rw_r_k, rw_ln_w, rw_ln_b, mla_w_down, mla_q_norm, mla_kv_norm, mla_w_uq, mla_w_ukv, mla_w_o, final_norm):
    b, s, d = x.shape
    depth = p.shape[0]
    h = x.reshape(b * s, d)
    pos = positions.reshape(b * s)
    for i in range(depth):
        j = i // 2
        if i % 2 == 0:
            h = sb_rwkv_layer(h, b, s, attn_norm[i], hyb_w_in[j], hyb_w_out[j], rw_mu[j], rw_w0[j],
                              rw_w2[j], rw_a0[j], rw_a2[j], rw_g2[j], rw_k_k[j], rw_k_a[j], rw_r_k[j],
                              rw_ln_w[j], rw_ln_b[j])
        else:
            h = mla_layer(h, b, s, pos, attn_norm[i], mla_w_down[j], mla_q_norm[j], mla_kv_norm[j],
                          mla_w_uq[j], mla_w_ukv[j], mla_w_o[j])
        ug = rms_matmul(h, ffn_norm[i], ffn_w_in[i].astype(BF16), tm=2048, out_dtype=BF16)
        h = convglu_out(h, ug, ffn_conv_w[i], ffn_conv_b[i], ffn_w_out[i].astype(BF16), seq=s)
        h = ple(h, p[i].reshape(b * s, -1), ple_w_proj[i].astype(BF16), ple_norm[i], ple_gate_norm[i],
                ple_w_gate[i].astype(BF16), final_norm, final=(i == depth - 1))
    return h.reshape(b, s, d)
```

```python
import functools
import math

import jax
import jax.numpy as jnp
from jax import lax
from jax.experimental import pallas as pl
from jax.experimental.pallas import tpu as pltpu

F32 = jnp.float32
BF16 = jnp.bfloat16
HI = lax.Precision.HIGHEST

NORM_EPS = 1e-6
GN_EPS = 64e-5
ROPE_THETA = 10000.0
LANE = 128
VMEM_LIMIT = 48 * 1024 * 1024
HALO = 16
MXU_WIDE_TILE = 512
TILE_VMEM_BUDGET = 36 * 1024 * 1024
RESIDENT_WEIGHT_BYTES = 4 * 1024 * 1024

SB_HEADS = 8
RW_HEADS = 8
HEAD_DIM = 64
MLA_HEADS = 16
MLA_NOPE = 64
MLA_ROPE = 32
MLA_V = 64
MLA_Q_RANK = 384
MLA_KV_RANK = 256
SB_CUT = -104.0
RW_CHUNK = 64
RW_SUB = 16


def _cp(*sem):
    return pltpu.CompilerParams(dimension_semantics=sem, vmem_limit_bytes=VMEM_LIMIT)


def _iota(shape, dim):
    return lax.broadcasted_iota(jnp.int32, shape, dim)


def _softplus(z):
    return jnp.maximum(z, 0.0) + jnp.log1p(jnp.exp(-jnp.abs(z)))


def _sigmoid(z):
    return 1.0 / (1.0 + jnp.exp(-z))


def _dot(a, b, precision=None):
    return jnp.dot(a, b, preferred_element_type=F32, precision=precision)


def _dot_nt(a, b, precision=None):
    return lax.dot_general(a, b, (((1,), (1,)), ((), ())), preferred_element_type=F32,
                           precision=precision)


def _dot_tn(a, b, precision=None):
    return lax.dot_general(a, b, (((0,), (0,)), ((), ())), preferred_element_type=F32,
                           precision=precision)


class _Split:
    def __init__(self, x):
        self.x = x
        self.hi = x.astype(BF16)
        self.lo = (x - self.hi.astype(F32)).astype(BF16)
        self._packed = {}

    def left(self, axis):
        key = ("l", axis)
        if key not in self._packed:
            self._packed[key] = jnp.concatenate([self.hi, self.lo, self.hi], axis=axis)
        return self._packed[key]

    def right(self, axis):
        key = ("r", axis)
        if key not in self._packed:
            self._packed[key] = jnp.concatenate([self.hi, self.hi, self.lo], axis=axis)
        return self._packed[key]


def _mm3(a, b):
    return _dot(a.left(1), b.right(0))


def _mm3_nt(a, b):
    return _dot_nt(a.left(1), b.right(1))


def _mm3_tn(a, b):
    return _dot_tn(a.left(0), b.right(0))


def _rms(x, g, eps=NORM_EPS):
    return x * lax.rsqrt(jnp.mean(x * x, axis=-1, keepdims=True) + eps) * g


def _rms_matmul_kernel(x_ref, g_ref, w_ref, o_ref, xn_ref):
    @pl.when(pl.program_id(1) == 0)
    def _():
        xn_ref[...] = _rms(x_ref[...].astype(F32), g_ref[...]).astype(BF16)

    o_ref[...] = _dot(xn_ref[...], w_ref[...]).astype(o_ref.dtype)


def _pick_tiles(t, k, n, tile_bytes):
    tn = n if k * n * 2 <= RESIDENT_WEIGHT_BYTES else MXU_WIDE_TILE
    for tm in (2048, 1024, 512, 256):
        if t % tm == 0 and tile_bytes(tm, tn) + 2 * k * tn * 2 <= TILE_VMEM_BUDGET:
            return tm, tn
    raise ValueError("no row tile fits")


def rms_matmul(x, g, w, *, x_col=0, out_dtype=F32):
    t = x.shape[0]
    k, n = w.shape
    ob = jnp.dtype(out_dtype).itemsize
    tm, tn = _pick_tiles(t, k, n, lambda tm, tn: 2 * tm * k * x.dtype.itemsize + tm * k * 2 + 2 * tm * tn * ob)
    return pl.pallas_call(
        _rms_matmul_kernel,
        grid=(t // tm, n // tn),
        in_specs=[pl.BlockSpec((tm, k), lambda i, j: (i, x_col)),
                  pl.BlockSpec((1, k), lambda i, j: (0, 0)),
                  pl.BlockSpec((k, tn), lambda i, j: (0, j))],
        out_specs=pl.BlockSpec((tm, tn), lambda i, j: (i, j)),
        out_shape=jax.ShapeDtypeStruct((t, n), out_dtype),
        scratch_shapes=[pltpu.VMEM((tm, k), BF16)],
        compiler_params=_cp("parallel", "arbitrary"),
        name="rms_matmul",
    )(x, g.reshape(1, k).astype(F32), w)


def _matmul_res_kernel(a_ref, w_ref, r_ref, o_ref):
    o_ref[...] = r_ref[...] + _dot(a_ref[...].astype(BF16), w_ref[...])


def matmul_residual(res, a, w):
    t, k = a.shape
    n = w.shape[1]
    tm, tn = _pick_tiles(t, k, n, lambda tm, tn: 2 * tm * k * a.dtype.itemsize + 4 * tm * tn * 4)
    return pl.pallas_call(
        _matmul_res_kernel,
        grid=(t // tm, n // tn),
        in_specs=[pl.BlockSpec((tm, k), lambda i, j: (i, 0)),
                  pl.BlockSpec((k, tn), lambda i, j: (0, j)),
                  pl.BlockSpec((tm, tn), lambda i, j: (i, j))],
        out_specs=pl.BlockSpec((tm, tn), lambda i, j: (i, j)),
        out_shape=jax.ShapeDtypeStruct((t, n), F32),
        compiler_params=_cp("parallel", "arbitrary"),
        name="matmul_residual",
    )(a, w, res)


def _convglu_out_kernel(u_ref, g_ref, gp_ref, cw_ref, cb_ref, w_ref, r_ref, o_ref, *, tm, seq):
    i = pl.program_id(0)
    g = g_ref[...].astype(F32)
    at_start = (i * tm) % seq == 0
    halo = jnp.where(at_start, 0.0, gp_ref[...].astype(F32))
    h1, h2 = halo[HALO - 1:HALO, :], halo[HALO - 2:HALO - 1, :]
    r1, r2 = pltpu.roll(g, 1, axis=0), pltpu.roll(g, 2, axis=0)
    top = _iota((8, g.shape[1]), 0)
    p1 = jnp.concatenate([jnp.where(top == 0, h1, r1[:8]), r1[8:]], axis=0)
    p2 = jnp.concatenate([jnp.where(top == 0, h2, jnp.where(top == 1, h1, r2[:8])), r2[8:]], axis=0)
    c = cw_ref[0:1, :] * p2 + cw_ref[1:2, :] * p1 + cw_ref[2:3, :] * g + cb_ref[...]
    act = 0.5 * c * (1.0 + lax.erf(c * (1.0 / math.sqrt(2.0)))) * u_ref[...].astype(F32)
    o_ref[...] = r_ref[...] + _dot(act.astype(BF16), w_ref[...])


def convglu_out(res, ug, conv_w, conv_b, w_out, *, seq, tm=256):
    t = res.shape[0]
    f, n = w_out.shape
    hb = tm // HALO
    return pl.pallas_call(
        functools.partial(_convglu_out_kernel, tm=tm, seq=seq),
        grid=(t // tm,),
        in_specs=[pl.BlockSpec((tm, f), lambda i: (i, 0)),
                  pl.BlockSpec((tm, f), lambda i: (i, 1)),
                  pl.BlockSpec((HALO, f), lambda i: (jnp.maximum(i * hb - 1, 0), 1)),
                  pl.BlockSpec((3, f), lambda i: (0, 0)),
                  pl.BlockSpec((1, f), lambda i: (0, 0)),
                  pl.BlockSpec((f, n), lambda i: (0, 0)),
                  pl.BlockSpec((tm, n), lambda i: (i, 0))],
        out_specs=pl.BlockSpec((tm, n), lambda i: (i, 0)),
        out_shape=jax.ShapeDtypeStruct((t, n), F32),
        compiler_params=_cp("parallel"),
        name="convglu_out",
    )(ug, ug, ug, conv_w, conv_b.reshape(1, f), w_out, res)


def _ple_kernel(h_ref, p_ref, wp_ref, pn_ref, gn_ref, wg_ref, fn_ref, o_ref, *, final):
    h = h_ref[...]
    e = _rms(_dot(p_ref[...].astype(BF16), wp_ref[...]), pn_ref[...])
    gate = _sigmoid(_dot(_rms(h, gn_ref[...]).astype(BF16), wg_ref[...]))
    out = h + gate * e
    if final:
        out = _rms(out, fn_ref[...])
    o_ref[...] = out


def ple(h, p, w_proj, p_norm, g_norm, w_gate, f_norm, *, final, tm=1024):
    t, d = h.shape
    pd = p.shape[1]
    row = lambda i: (i, 0)
    fix = lambda i: (0, 0)
    return pl.pallas_call(
        functools.partial(_ple_kernel, final=final),
        grid=(t // tm,),
        in_specs=[pl.BlockSpec((tm, d), row), pl.BlockSpec((tm, pd), row),
                  pl.BlockSpec((pd, d), fix), pl.BlockSpec((1, d), fix), pl.BlockSpec((1, d), fix),
                  pl.BlockSpec((d, d), fix), pl.BlockSpec((1, d), fix)],
        out_specs=pl.BlockSpec((tm, d), row),
        out_shape=jax.ShapeDtypeStruct((t, d), F32),
        compiler_params=_cp("parallel"),
        name="ple",
    )(h, p, w_proj, p_norm.reshape(1, d), g_norm.reshape(1, d), w_gate, f_norm.reshape(1, d))


def _sb_attn_kernel(q_ref, k_ref, v_ref, u_ref, o_ref, *, tq, tk, scale):
    qi = pl.program_id(2)
    nblk = tq // tk
    first = _iota((tq, LANE), 1) < HEAD_DIM
    qf = q_ref[...] * scale
    zero = jnp.zeros_like(qf)
    qs = (jnp.where(first, qf, zero), jnp.where(first, zero, qf))

    def sweep(k_lo, runs, accs, masked):
        parts = []
        for d in reversed(range(nblk)):
            k0 = k_lo + d * tk
            kb = k_ref[pl.ds(k0, tk), :]
            valid = None
            if masked:
                valid = _iota((tq, tk), 1) + d * tk < _iota((tq, tk), 0)
            for h in range(2):
                z = _dot_nt(qs[h], kb)
                sp = _softplus(z)
                lom = -sp
                if masked:
                    lom = jnp.where(valid, lom, 0.0)
                hi = lom.astype(BF16)
                lo = (lom - hi.astype(F32)).astype(BF16)
                sums = _dot(jnp.concatenate([hi, lo], axis=1), u_ref[...])
                parts.append((h, k0, z - sp, sums, valid))
        runs, accs = list(runs), list(accs)
        for h, k0, logsig, sums, valid in parts:
            w = jnp.exp(logsig + runs[h] + sums[:, :tk])
            if masked:
                w = jnp.where(valid, w, 0.0)
            accs[h] = accs[h] + _dot(w.astype(BF16), v_ref[pl.ds(k0, tk), :])
            runs[h] = runs[h] + sums[:, tk:]
        return tuple(runs), tuple(accs)

    runs = (jnp.zeros((tq, tk), F32),) * 2
    accs = (jnp.zeros((tq, LANE), F32),) * 2
    runs, accs = sweep(pl.multiple_of(qi * tq, tq), runs, accs, True)

    def live(runs):
        return jnp.max(jnp.maximum(runs[0], runs[1])) > SB_CUT

    def cond(c):
        return c[0]

    def body(c):
        _, j, runs, accs = c
        runs, accs = sweep(pl.multiple_of(j * tq, tq), runs, accs, False)
        return (j > 0) & live(runs), j - 1, runs, accs

    _, _, _, accs = lax.while_loop(cond, body, ((qi > 0) & live(runs), qi - 1, runs, accs))
    o_ref[...] = jnp.where(first, accs[0], accs[1])


def sb_attention(qkv, *, batch, heads, tq=256):
    t, w3 = qkv.shape
    s = t // batch
    nq = s // tq
    pairs = heads * HEAD_DIM // LANE
    tk = LANE
    jj = jnp.arange(2 * tk)[:, None] % tk
    ss = jnp.arange(2 * tk)[None, :]
    u = jnp.where(ss < tk, jj > ss, True).astype(BF16)
    return pl.pallas_call(
        functools.partial(_sb_attn_kernel, tq=tq, tk=tk, scale=1.0 / math.sqrt(HEAD_DIM)),
        grid=(batch, pairs, nq),
        in_specs=[pl.BlockSpec((tq, LANE), lambda b, p, i: (b * nq + i, p)),
                  pl.BlockSpec((s, LANE), lambda b, p, i: (b, pairs + p)),
                  pl.BlockSpec((s, LANE), lambda b, p, i: (b, 2 * pairs + p)),
                  pl.BlockSpec((2 * tk, 2 * tk), lambda b, p, i: (0, 0))],
        out_specs=pl.BlockSpec((tq, LANE), lambda b, p, i: (b * nq + i, p)),
        out_shape=jax.ShapeDtypeStruct((t, w3 // 3), F32),
        compiler_params=_cp("parallel", "parallel", "arbitrary"),
        name="sb_attention",
    )(qkv, qkv, qkv, u)


def _softmax_attn_kernel(qn_ref, qr_ref, kn_ref, kr_ref, v_ref, o_ref, sa0, sa1, sb0, sb1,
                         *, tq, tk, scale):
    hp = pl.program_id(1)
    qi = pl.program_id(2)
    lane = _iota((tq, LANE), 1)
    qn = qn_ref[...] * (scale * math.log2(math.e))
    qr = qr_ref[...] * (scale * math.log2(math.e))
    qs = []
    for h in range(2):
        nope = jnp.where(lane // MLA_NOPE == h, qn, 0.0)
        rope_ = jnp.where(lane // MLA_ROPE == (hp % 2) * 2 + h, qr, 0.0)
        qs.append(jnp.concatenate([nope, rope_], axis=1).astype(BF16))

    def scores(j, dsts):
        k0 = pl.multiple_of(j * tk, tk)
        kb = jnp.concatenate([kn_ref[pl.ds(k0, tk), :], kr_ref[pl.ds(k0, tk), :]], axis=1)
        mx = []
        for h in range(2):
            s = _dot_nt(qs[h], kb)
            dsts[h][...] = s
            mx.append(jnp.max(s, axis=1, keepdims=True))
        return tuple(mx)

    def consume(srcs, j, carry, mx):
        vb = v_ref[pl.ds(pl.multiple_of(j * tk, tk), tk), :]
        one = jnp.ones_like(vb)
        first = _iota(vb.shape, 1) < MLA_V
        vbs = (jnp.where(first, vb, one), jnp.where(first, one, vb))
        out = []
        for h in range(2):
            m, acc = carry[h]
            s = srcs[h][...]
            if mx is None:
                s = jnp.where(_iota((tq, tk), 1) <= _iota((tq, tk), 0), s, -1e30)
                m_new = jnp.maximum(m, jnp.max(s, axis=1, keepdims=True))
            else:
                m_new = jnp.maximum(m, mx[h])
            p = jnp.exp2(s - m_new)
            acc = jnp.exp2(m - m_new) * acc + _dot(p.astype(BF16), vbs[h])
            out.append((m_new, acc))
        return tuple(out)

    sa, sb = (sa0, sa1), (sb0, sb1)

    def pair(t, carry):
        state, mxa = carry
        mxb = scores(2 * t + 1, sb)
        state = consume(sa, 2 * t, state, mxa)
        mxa = scores(2 * t + 2, sa)
        return consume(sb, 2 * t + 1, state, mxb), mxa

    def tail_odd(carry):
        state, mxa = carry
        scores(qi, sb)
        state = consume(sa, qi - 1, state, mxa)
        return consume(sb, qi, state, None)

    def tail_even(carry):
        return consume(sa, qi, carry[0], None)

    init = ((jnp.full((tq, 1), -1e30, F32), jnp.zeros((tq, LANE), F32)),) * 2
    carry = lax.fori_loop(0, qi // 2, pair, (init, scores(0, sa)))
    (_, acc0), (_, acc1) = lax.cond(qi % 2 == 1, tail_odd, tail_even, carry)
    o_ref[...] = jnp.where(lane < MLA_V, acc0 / pltpu.roll(acc0, MLA_V, axis=1),
                           acc1 / pltpu.roll(acc1, MLA_V, axis=1))


def softmax_attention(q, qrot, kv, krot, scale, *, batch, heads, tq=512):
    tk = tq
    t = q.shape[0]
    s = t // batch
    nq = s // tq
    pairs = heads * MLA_NOPE // LANE
    buf = pltpu.VMEM((tq, tk), F32)
    return pl.pallas_call(
        functools.partial(_softmax_attn_kernel, tq=tq, tk=tk, scale=scale),
        grid=(batch, pairs, nq),
        in_specs=[pl.BlockSpec((tq, LANE), lambda b, p, i: (b * nq + i, p)),
                  pl.BlockSpec((tq, LANE), lambda b, p, i: (b * nq + i, p // 2)),
                  pl.BlockSpec((s, LANE), lambda b, p, i: (b, p)),
                  pl.BlockSpec((s, LANE), lambda b, p, i: (b, 0)),
                  pl.BlockSpec((s, LANE), lambda b, p, i: (b, pairs + p))],
        out_specs=pl.BlockSpec((tq, LANE), lambda b, p, i: (b * nq + i, p)),
        out_shape=jax.ShapeDtypeStruct((t, heads * MLA_V), F32),
        scratch_shapes=[buf, buf, buf, buf],
        compiler_params=_cp("parallel", "parallel", "arbitrary"),
        name="softmax_attention",
    )(q, qrot, kv, krot, kv)


def _rope_kernel(q_ref, k_ref, pos_ref, f_ref, oq_ref, ok_ref):
    half = MLA_ROPE // 2
    ang = pos_ref[...].astype(F32) * f_ref[...]
    cos, sin = jnp.cos(ang), jnp.sin(ang)
    low = _iota(ang.shape, 1) % MLA_ROPE < half

    def rot(x):
        partner = jnp.where(low, -pltpu.roll(x, LANE - half, axis=1), pltpu.roll(x, half, axis=1))
        return x * cos + partner * sin

    for c in range(q_ref.shape[1] // LANE):
        cols = slice(c * LANE, (c + 1) * LANE)
        oq_ref[:, cols] = rot(q_ref[:, cols])
    ok_ref[...] = rot(k_ref[...]).astype(ok_ref.dtype)


def rope(q, c, positions, *, q_col, qw, k_col, tm=512):
    t = q.shape[0]
    half = MLA_ROPE // 2
    inv_freq = ROPE_THETA ** (-jnp.arange(half, dtype=F32) / half)
    freq = jnp.tile(inv_freq, LANE // half).reshape(1, LANE)
    return pl.pallas_call(
        _rope_kernel,
        grid=(t // tm,),
        in_specs=[pl.BlockSpec((tm, qw), lambda i: (i, q_col)), pl.BlockSpec((tm, LANE), lambda i: (i, k_col)),
                  pl.BlockSpec((tm, 1), lambda i: (i, 0)), pl.BlockSpec((1, LANE), lambda i: (0, 0))],
        out_specs=[pl.BlockSpec((tm, qw), lambda i: (i, 0)), pl.BlockSpec((tm, LANE), lambda i: (i, 0))],
        out_shape=[jax.ShapeDtypeStruct((t, qw), F32), jax.ShapeDtypeStruct((t, LANE), BF16)],
        compiler_params=_cp("parallel"),
        name="rope",
    )(q, c, positions.reshape(t, 1), freq)


def _rwkv_prep_kernel(x_ref, xp_ref, mu_ref, w0_ref, a0_ref, kk_ref, ka_ref, rk_ref,
                      w2_ref, a2_ref, g2_ref, tri_ref, hs_ref,
                      at_ref, bt_ref, kt_ref, rt_ref, v_ref, eg_ref, g_ref, bv_ref, *, tm, seq, width):
    i = pl.program_id(0)
    x = x_ref[...]
    at_start = (i * tm) % seq == 0
    prev_row = jnp.where(at_start, 0.0, xp_ref[7:8, :])
    rolled = pltpu.roll(x, 1, axis=0)
    top = _iota((8, x.shape[1]), 0)
    prev = jnp.concatenate([jnp.where(top == 0, prev_row, rolled[:8]), rolled[8:]], axis=0)
    xm = x + (prev - x) * mu_ref[...]
    r = xm[:, :width]
    k = xm[:, width:2 * width]
    v = xm[:, 2 * width:3 * width]
    l1 = xm[:, 3 * width:3 * width + LANE]
    l2 = xm[:, 3 * width + LANE:]
    wpre = w0_ref[...] + _dot(_Split(jnp.tanh(l1)).left(1), w2_ref[...])
    a = _sigmoid(a0_ref[...] + _dot(_Split(l1).left(1), a2_ref[...]))
    g = _dot(_Split(_sigmoid(l2)).left(1), g2_ref[...])

    def hilo(t):
        ts = _Split(t)
        return jnp.concatenate([ts.hi, ts.lo], axis=1)

    lw = -jnp.exp(-_softplus(-wpre) - 0.5)
    hs = hs_ref[...]
    kk = k * kk_ref[...]
    kk = kk * lax.rsqrt(jnp.maximum(_dot(hilo(kk * kk), hs), 1e-24))
    km = k * (1.0 + (a - 1.0) * ka_ref[...])
    bonus = _dot(hilo(r * km * rk_ref[...]), hs)
    lws = _Split(lw)
    cum = _dot(tri_ref[...], jnp.concatenate([lws.hi, lws.lo], axis=0))
    eg = jnp.exp(cum)
    ieg = jnp.exp(-cum)
    at_ref[...] = -kk * jnp.exp(cum - lw)
    bt_ref[...] = kk * a * ieg
    kt_ref[...] = km * ieg
    rt_ref[...] = r * eg
    v_ref[...] = v
    eg_ref[...] = eg
    g_ref[...] = g
    bv_ref[...] = bonus * v


def rwkv_prep(x, mu, w0, a0, k_k, k_a, r_k, w2p, a2p, g2p, *, seq, width, tm=256):
    t, wx = x.shape
    ch = RW_CHUNK
    ti = jnp.arange(tm)
    tri = ((ti[:, None] // ch == ti[None, :] // ch) & (ti[None, :] <= ti[:, None])).astype(BF16)
    tri = jnp.concatenate([tri, tri], axis=1)
    ci = jnp.arange(width) // HEAD_DIM
    hs = (ci[:, None] == ci[None, :]).astype(BF16)
    hs = jnp.concatenate([hs, hs], axis=0)
    w2p, a2p, g2p = (_Split(w).right(0) for w in (w2p, a2p, g2p))
    hb = tm // 8
    row = lambda i: (i, 0)
    fix = lambda i: (0, 0)
    vec = pl.BlockSpec((1, width), fix)
    out = jax.ShapeDtypeStruct((t, width), F32)
    return pl.pallas_call(
        functools.partial(_rwkv_prep_kernel, tm=tm, seq=seq, width=width),
        grid=(t // tm,),
        in_specs=[pl.BlockSpec((tm, wx), row),
                  pl.BlockSpec((8, wx), lambda i: (jnp.maximum(i * hb - 1, 0), 0)),
                  pl.BlockSpec((1, wx), fix), vec, vec, vec, vec, vec,
                  pl.BlockSpec(w2p.shape, fix), pl.BlockSpec(a2p.shape, fix),
                  pl.BlockSpec(g2p.shape, fix), pl.BlockSpec(tri.shape, fix),
                  pl.BlockSpec(hs.shape, fix)],
        out_specs=[pl.BlockSpec((tm, width), row)] * 8,
        out_shape=[out] * 8,
        compiler_params=_cp("parallel"),
        name="rwkv_prep",
    )(x, x, mu, w0, a0, k_k, k_a, r_k, w2p, a2p, g2p, tri, hs)


def _rwkv_chunk_kernel(at_ref, bt_ref, kt_ref, rt_ref, v_ref, eg_ref,
                       m_ref, n_ref, rq_ref, yl_ref, *, nchunk, unroll):
    ch, n = RW_CHUNK, HEAD_DIM
    ti = _iota((ch, ch), 0)
    si = _iota((ch, ch), 1)
    strict = si < ti
    incl = si <= ti
    same = (ti // RW_SUB) == (si // RW_SUB)
    eye = _iota((n, n), 0) == _iota((n, n), 1)
    each = lambda f, *ls: [f(*t) for t in zip(*ls)]

    def body(it, _):
        cs = [it * unroll + u for u in range(unroll)]
        sls = [pl.ds(pl.multiple_of(c * ch, ch), ch) for c in cs]
        ids = [(h, u) for h in range(2) for u in range(unroll)]
        ld = lambda ref: [ref[sls[u], h * n:(h + 1) * n] for h, u in ids]
        sp = lambda xs: [_Split(t) for t in xs]
        a_, b_, k_, r_, v_ = ld(at_ref), ld(bt_ref), ld(kt_ref), ld(rt_ref), ld(v_ref)
        g_end = [eg_ref[pl.ds(cs[u] * ch + ch - 1, 1), h * n:(h + 1) * n] for h, u in ids]
        ar = sp(each(lambda a, r: jnp.concatenate([a, r], axis=0), a_, r_))
        vs = sp(v_)
        pb = each(_mm3_nt, ar, sp(b_))
        pk = each(_mm3_nt, ar, sp(k_))
        a_ab = [jnp.where(strict, t[:ch], 0.0) for t in pb]
        a_ak = sp([jnp.where(strict, t[:ch], 0.0) for t in pk])
        a_rb = sp([jnp.where(incl, t[ch:], 0.0) for t in pb])
        a_rk = sp([jnp.where(incl, t[ch:], 0.0) for t in pk])
        d1 = [jnp.where(same, t, 0.0) for t in a_ab]
        lb = each(lambda t, d: t - d, a_ab, d1)
        d1 = sp(d1)
        d2 = sp(each(_mm3, d1, d1))
        akv = each(_mm3, a_ak, vs)
        x = each(lambda l, a, t: jnp.concatenate([l, a, t], axis=1), lb, a_, akv)
        d4 = sp(each(_mm3, d2, d2))
        x = each(lambda d, t: t + _mm3(d, _Split(t)), d1, x)
        d8 = sp(each(_mm3, d4, d4))
        x = each(lambda d, t: t + _mm3(d, _Split(t)), d2, x)
        x = each(lambda d, t: t + _mm3(d, _Split(t)), d4, x)
        x = each(lambda d, t: t + _mm3(d, _Split(t)), d8, x)
        e1 = sp([t[:, :ch] for t in x])
        wu = [t[:, ch:] for t in x]
        e2 = sp(each(_mm3, e1, e1))
        wu = each(lambda e, t: t + _mm3(e, _Split(t)), e1, wu)
        wu = sp(each(lambda e, t: t + _mm3(e, _Split(t)), e2, wu))
        ry = each(_mm3, a_rb, wu)
        rkv = each(_mm3, a_rk, vs)
        mn = each(lambda b, g, t: _mm3_tn(_Split(b * g), t), b_, g_end, wu)
        kv = each(lambda k, g, t: _mm3_tn(_Split(k * g), t), k_, g_end, vs)
        rq = each(lambda r, t: r + t[:, :n], r_, ry)
        yl = each(lambda t, w: t[:, n:] + w, ry, rkv)
        for u in range(unroll):
            rq_ref[sls[u], :] = jnp.concatenate([rq[u], rq[unroll + u]], axis=1)
            yl_ref[sls[u], :] = jnp.concatenate([yl[u], yl[unroll + u]], axis=1)
        for i, (h, u) in enumerate(ids):
            m_ref[h, cs[u]] = mn[i][:, :n] + jnp.where(eye, g_end[i], 0.0)
            n_ref[h, cs[u]] = mn[i][:, n:] + kv[i]
        return 0

    lax.fori_loop(0, nchunk // unroll, body, 0)


def rwkv_chunk(at, bt, kt, rt, v, eg, *, batch, nchunk=8, unroll=4):
    t, w = at.shape
    s = t // batch
    ch, n = RW_CHUNK, HEAD_DIM
    pairs = w // LANE
    rows = nchunk * ch
    nr = s // rows
    spec = pl.BlockSpec((rows, LANE), lambda b, p, i: (b * nr + i, p))
    mspec = pl.BlockSpec((2, nchunk, n, n), lambda b, p, i: (b * pairs + p, i, 0, 0))
    return pl.pallas_call(
        functools.partial(_rwkv_chunk_kernel, nchunk=nchunk, unroll=unroll),
        grid=(batch, pairs, nr),
        in_specs=[spec] * 6,
        out_specs=[mspec, mspec, spec, spec],
        out_shape=[jax.ShapeDtypeStruct((batch * w // n, s // ch, n, n), F32)] * 2
        + [jax.ShapeDtypeStruct((t, w), F32)] * 2,
        compiler_params=_cp("parallel", "parallel", "parallel"),
        name="rwkv_chunk",
    )(at, bt, kt, rt, v, eg)


def _rwkv_scan_kernel(m_ref, n_ref, s_ref, st_ref, *, heads, nchunk):
    @pl.when(pl.program_id(1) == 0)
    def _():
        st_ref[...] = jnp.zeros_like(st_ref)

    def body(c, _):
        sts = [st_ref[hh] for hh in range(heads)]
        for hh in range(heads):
            s_ref[hh, c] = sts[hh]
        ms = [_Split(m_ref[hh, c]) for hh in range(heads)]
        ss = [_Split(st) for st in sts]
        new = [_mm3(a, b) for a, b in zip(ms, ss)]
        for hh in range(heads):
            st_ref[hh] = new[hh] + n_ref[hh, c]
        return 0

    lax.fori_loop(0, nchunk, body, 0)


def rwkv_scan(m, nn, *, heads=16, nchunk=8):
    bh, nc, n, _ = m.shape
    heads = min(heads, bh)
    nchunk = min(nchunk, nc)
    spec = pl.BlockSpec((heads, nchunk, n, n), lambda b, i: (b, i, 0, 0))
    return pl.pallas_call(
        functools.partial(_rwkv_scan_kernel, heads=heads, nchunk=nchunk),
        grid=(bh // heads, nc // nchunk),
        in_specs=[spec, spec],
        out_specs=spec,
        out_shape=jax.ShapeDtypeStruct((bh, nc, n, n), F32),
        scratch_shapes=[pltpu.VMEM((heads, n, n), F32)],
        compiler_params=_cp("parallel", "arbitrary"),
        name="rwkv_scan",
    )(m, nn)


def _rwkv_out_kernel(rq_ref, yl_ref, s_ref, lw_ref, lb_ref, avg_ref, o_ref, *, nchunk):
    ch, n = RW_CHUNK, HEAD_DIM
    ys = []
    for h in range(2):
        cols = slice(h * n, (h + 1) * n)
        ys.append(jnp.concatenate(
            [_mm3(_Split(rq_ref[c * ch:(c + 1) * ch, cols]), _Split(s_ref[h, c])) for c in range(nchunk)],
            axis=0))
    y = jnp.concatenate(ys, axis=1) + yl_ref[...]
    avg2 = jnp.concatenate([avg_ref[...]] * 2, axis=0)

    def mean(t):
        ts = _Split(t)
        return _dot(jnp.concatenate([ts.hi, ts.lo], axis=1), avg2)

    d = y - mean(y)
    var = mean(d * d)
    o_ref[...] = d * lax.rsqrt(var + GN_EPS) * lw_ref[...] + lb_ref[...]


def rwkv_out(rq, yl, s0, ln_w, ln_b, *, batch, nchunk=8):
    t, w = rq.shape
    s = t // batch
    n = HEAD_DIM
    pairs = w // LANE
    rows = nchunk * RW_CHUNK
    nr = s // rows
    li = jnp.arange(LANE) // n
    avg = ((li[:, None] == li[None, :]).astype(F32) / n).astype(BF16)
    spec = pl.BlockSpec((rows, LANE), lambda b, p, i: (b * nr + i, p))
    hspec = pl.BlockSpec((1, LANE), lambda b, p, i: (0, p))
    return pl.pallas_call(
        functools.partial(_rwkv_out_kernel, nchunk=nchunk),
        grid=(batch, pairs, nr),
        in_specs=[spec, spec, pl.BlockSpec((2, nchunk, n, n), lambda b, p, i: (b * pairs + p, i, 0, 0)),
                  hspec, hspec, pl.BlockSpec((LANE, LANE), lambda b, p, i: (0, 0))],
        out_specs=spec,
        out_shape=jax.ShapeDtypeStruct((t, w), F32),
        compiler_params=_cp("parallel", "parallel", "parallel"),
        name="rwkv_out",
    )(rq, yl, s0, ln_w.reshape(1, w), ln_b.reshape(1, w), avg)


def _hyb_out_kernel(oa_ref, y_ref, bv_ref, g_ref, wa_ref, wb_ref, r_ref, o_ref):
    ob = (y_ref[...] + bv_ref[...]) * g_ref[...]
    o_ref[...] = (r_ref[...] + _dot(oa_ref[...].astype(BF16), wa_ref[...])
                  + _dot(ob.astype(BF16), wb_ref[...]))


def hyb_out(res, oa, y, bv, g, wa, wb, *, tm=1024):
    t, d = res.shape
    row = lambda i: (i, 0)
    fix = lambda i: (0, 0)
    act = lambda a: pl.BlockSpec((tm, a.shape[1]), row)
    return pl.pallas_call(
        _hyb_out_kernel,
        grid=(t // tm,),
        in_specs=[act(oa), act(y), act(bv), act(g), pl.BlockSpec(wa.shape, fix),
                  pl.BlockSpec(wb.shape, fix), pl.BlockSpec((tm, d), row)],
        out_specs=pl.BlockSpec((tm, d), row),
        out_shape=jax.ShapeDtypeStruct((t, d), F32),
        compiler_params=_cp("parallel"),
        name="hyb_out",
    )(oa, y, bv, g, wa, wb, res)


def _pad_cols(w, n):
    return jnp.pad(w, ((0, 0), (0, n - w.shape[1])))


def sb_rwkv_layer(h, b, s, attn_norm, w_in, w_out, mu, w0, w2, a0, a2, g2, k_k, k_a, r_k, ln_w, ln_b):
    sbw = SB_HEADS * HEAD_DIM
    rww = RW_HEADS * HEAD_DIM
    rw_in = w_in.shape[1] - 3 * sbw
    rw_pad = -(-rw_in // LANE) * LANE
    dl, al = w2.shape[0], a2.shape[0]
    gl = g2.shape[0]
    assert dl + al == LANE and 3 * rww % LANE == 0

    qkv = rms_matmul(h, attn_norm, w_in[:, :3 * sbw].astype(BF16), out_dtype=BF16)
    rw = rms_matmul(h, attn_norm, _pad_cols(w_in[:, 3 * sbw:], rw_pad).astype(BF16))
    oa = sb_attention(qkv, batch=b, heads=SB_HEADS)

    w2p = jnp.zeros((LANE, rww), F32).at[:dl].set(w2)
    a2p = jnp.zeros((LANE, rww), F32).at[dl:].set(a2)
    g2p = jnp.zeros((rw_pad - 3 * rww - LANE, rww), F32).at[:gl].set(g2)
    vec = lambda t: t.reshape(1, rww).astype(F32)
    mu_p = _pad_cols(mu.reshape(1, rw_in), rw_pad)
    at, bt, kt, rt, vv, eg, g, bv = rwkv_prep(rw, mu_p, vec(w0), vec(a0), vec(k_k), vec(k_a), vec(r_k),
                                              w2p, a2p, g2p, seq=s, width=rww)
    m, nn, rq, yl = rwkv_chunk(at, bt, kt, rt, vv, eg, batch=b)
    s0 = rwkv_scan(m, nn)
    y = rwkv_out(rq, yl, s0, ln_w, ln_b, batch=b)
    return hyb_out(h, oa, y, bv, g, w_out[:sbw].astype(BF16), w_out[sbw:].astype(BF16))


def mla_layer(h, b, s, positions, attn_norm, w_down, q_norm, kv_norm, w_uq, w_ukv, w_o):
    nh, dn, dr, dv = MLA_HEADS, MLA_NOPE, MLA_ROPE, MLA_V
    qr, kvr = MLA_Q_RANK, MLA_KV_RANK
    assert dn == dv and qr % LANE == 0 and (qr + LANE) % kvr == 0
    wd = jnp.concatenate([w_down[:, :qr], jnp.tile(w_down[:, qr + kvr:], (1, LANE // dr)),
                          w_down[:, qr:qr + kvr]], axis=1)
    c = rms_matmul(h, attn_norm, wd.astype(BF16))
    wq = w_uq.reshape(qr, nh, dn + dr)
    wq = jnp.concatenate([wq[:, :, :dn].reshape(qr, nh * dn), wq[:, :, dn:].reshape(qr, nh * dr)], axis=1)
    q = rms_matmul(c, q_norm, wq.astype(BF16), x_col=0)
    wkv = w_ukv.reshape(kvr, nh, dn + dv)
    wkv = jnp.concatenate([wkv[:, :, :dn].reshape(kvr, nh * dn), wkv[:, :, dn:].reshape(kvr, nh * dv)], axis=1)
    kv = rms_matmul(c, kv_norm, wkv.astype(BF16), x_col=(qr + LANE) // kvr, out_dtype=BF16)
    qrot, krot = rope(q, c, positions, q_col=dn // dr, qw=nh * dr, k_col=qr // LANE)
    o = softmax_attention(q, qrot, kv, krot, 1.0 / math.sqrt(dn + dr), batch=b, heads=nh)
    return matmul_residual(h, o, w_o.astype(BF16))


def kernel(x, p, positions, attn_norm, ffn_norm, ffn_w_in, ffn_conv_w, ffn_conv_b, ffn_w_out, ple_w_proj, ple_norm, ple_gate_norm, ple_w_gate, hyb_w_in, hyb_w_out, rw_mu, rw_w0, rw_w2, rw_a0, rw_a2, rw_g2, rw_k_k, rw_k_a, rw_r_k, rw_ln_w, rw_ln_b, mla_w_down, mla_q_norm, mla_kv_norm, mla_w_uq, mla_w_ukv, mla_w_o, final_norm):
    b, s, d = x.shape
    depth = p.shape[0]
    h = x.reshape(b * s, d)
    pos = positions.reshape(b * s)
    for i in range(depth):
        j = i // 2
        if i % 2 == 0:
            h = sb_rwkv_layer(h, b, s, attn_norm[i], hyb_w_in[j], hyb_w_out[j], rw_mu[j], rw_w0[j],
                              rw_w2[j], rw_a0[j], rw_a2[j], rw_g2[j], rw_k_k[j], rw_k_a[j], rw_r_k[j],
                              rw_ln_w[j], rw_ln_b[j])
        else:
            h = mla_layer(h, b, s, pos, attn_norm[i], mla_w_down[j], mla_q_norm[j], mla_kv_norm[j],
                          mla_w_uq[j], mla_w_ukv[j], mla_w_o[j])
        ug = rms_matmul(h, ffn_norm[i], ffn_w_in[i].astype(BF16), out_dtype=BF16)
        h = convglu_out(h, ug, ffn_conv_w[i], ffn_conv_b[i], ffn_w_out[i].astype(BF16), seq=s)
        h = ple(h, p[i].reshape(b * s, -1), ple_w_proj[i].astype(BF16), ple_norm[i], ple_gate_norm[i],
                ple_w_gate[i].astype(BF16), final_norm, final=(i == depth - 1))
    return h.reshape(b, s, d)
```

```python
import functools
import math

import jax
import jax.numpy as jnp
from jax import lax
from jax.experimental import pallas as pl
from jax.experimental.pallas import tpu as pltpu

F32 = jnp.float32
BF16 = jnp.bfloat16
HI = lax.Precision.HIGHEST

NORM_EPS = 1e-6
GN_EPS = 64e-5
ROPE_THETA = 10000.0
LANE = 128
VMEM_LIMIT = 48 * 1024 * 1024
HALO = 16
MXU_WIDE_TILE = 512
TILE_VMEM_BUDGET = 36 * 1024 * 1024
RESIDENT_WEIGHT_BYTES = 4 * 1024 * 1024

SB_HEADS = 8
RW_HEADS = 8
HEAD_DIM = 64
MLA_HEADS = 16
MLA_NOPE = 64
MLA_ROPE = 32
MLA_V = 64
MLA_Q_RANK = 384
MLA_KV_RANK = 256
SB_CUT = -151.0
RW_CHUNK = 64
RW_SUB = 16


def _cp(*sem):
    return pltpu.CompilerParams(dimension_semantics=sem, vmem_limit_bytes=VMEM_LIMIT)


def _iota(shape, dim):
    return lax.broadcasted_iota(jnp.int32, shape, dim)


def _softplus(z):
    return jnp.maximum(z, 0.0) + jnp.log1p(jnp.exp(-jnp.abs(z)))


def _sigmoid(z):
    return 1.0 / (1.0 + jnp.exp(-z))


def _dot(a, b, precision=None):
    return jnp.dot(a, b, preferred_element_type=F32, precision=precision)


def _dot_nt(a, b, precision=None):
    return lax.dot_general(a, b, (((1,), (1,)), ((), ())), preferred_element_type=F32,
                           precision=precision)


def _dot_tn(a, b, precision=None):
    return lax.dot_general(a, b, (((0,), (0,)), ((), ())), preferred_element_type=F32,
                           precision=precision)


class _Split:
    def __init__(self, x):
        self.x = x
        self.hi = x.astype(BF16)
        self.lo = (x - self.hi.astype(F32)).astype(BF16)
        self._packed = {}

    def left(self, axis):
        key = ("l", axis)
        if key not in self._packed:
            self._packed[key] = jnp.concatenate([self.hi, self.lo, self.hi], axis=axis)
        return self._packed[key]

    def right(self, axis):
        key = ("r", axis)
        if key not in self._packed:
            self._packed[key] = jnp.concatenate([self.hi, self.hi, self.lo], axis=axis)
        return self._packed[key]


def _mm3(a, b):
    return _dot(a.left(1), b.right(0))


def _mm3_nt(a, b):
    return _dot_nt(a.left(1), b.right(1))


def _mm3_tn(a, b):
    return _dot_tn(a.left(0), b.right(0))


def _rms(x, g, eps=NORM_EPS):
    return x * lax.rsqrt(jnp.mean(x * x, axis=-1, keepdims=True) + eps) * g


def _rms_matmul_kernel(x_ref, g_ref, w_ref, o_ref, xn_ref):
    @pl.when(pl.program_id(1) == 0)
    def _():
        xn_ref[...] = _rms(x_ref[...].astype(F32), g_ref[...]).astype(BF16)

    o_ref[...] = _dot(xn_ref[...], w_ref[...]).astype(o_ref.dtype)


def _pick_tiles(t, k, n, tile_bytes):
    tn = n if k * n * 2 <= RESIDENT_WEIGHT_BYTES else MXU_WIDE_TILE
    for tm in (2048, 1024, 512, 256):
        if t % tm == 0 and tile_bytes(tm, tn) + 2 * k * tn * 2 <= TILE_VMEM_BUDGET:
            return tm, tn
    raise ValueError("no row tile fits")


def rms_matmul(x, g, w, *, x_col=0, out_dtype=F32):
    t = x.shape[0]
    k, n = w.shape
    ob = jnp.dtype(out_dtype).itemsize
    tm, tn = _pick_tiles(t, k, n, lambda tm, tn: 2 * tm * k * x.dtype.itemsize + tm * k * 2 + 2 * tm * tn * ob)
    return pl.pallas_call(
        _rms_matmul_kernel,
        grid=(t // tm, n // tn),
        in_specs=[pl.BlockSpec((tm, k), lambda i, j: (i, x_col)),
                  pl.BlockSpec((1, k), lambda i, j: (0, 0)),
                  pl.BlockSpec((k, tn), lambda i, j: (0, j))],
        out_specs=pl.BlockSpec((tm, tn), lambda i, j: (i, j)),
        out_shape=jax.ShapeDtypeStruct((t, n), out_dtype),
        scratch_shapes=[pltpu.VMEM((tm, k), BF16)],
        compiler_params=_cp("parallel", "arbitrary"),
        name="rms_matmul",
    )(x, g.reshape(1, k).astype(F32), w)


def _matmul_res_kernel(a_ref, w_ref, r_ref, o_ref):
    o_ref[...] = r_ref[...] + _dot(a_ref[...].astype(BF16), w_ref[...])


def matmul_residual(res, a, w):
    t, k = a.shape
    n = w.shape[1]
    tm, tn = _pick_tiles(t, k, n, lambda tm, tn: 2 * tm * k * a.dtype.itemsize + 4 * tm * tn * 4)
    return pl.pallas_call(
        _matmul_res_kernel,
        grid=(t // tm, n // tn),
        in_specs=[pl.BlockSpec((tm, k), lambda i, j: (i, 0)),
                  pl.BlockSpec((k, tn), lambda i, j: (0, j)),
                  pl.BlockSpec((tm, tn), lambda i, j: (i, j))],
        out_specs=pl.BlockSpec((tm, tn), lambda i, j: (i, j)),
        out_shape=jax.ShapeDtypeStruct((t, n), F32),
        compiler_params=_cp("parallel", "arbitrary"),
        name="matmul_residual",
    )(a, w, res)


def _convglu_out_kernel(u_ref, g_ref, gp_ref, cw_ref, cb_ref, w_ref, r_ref, o_ref, *, tm, seq):
    i = pl.program_id(0)
    g = g_ref[...].astype(F32)
    at_start = (i * tm) % seq == 0
    halo = jnp.where(at_start, 0.0, gp_ref[...].astype(F32))
    h1, h2 = halo[HALO - 1:HALO, :], halo[HALO - 2:HALO - 1, :]
    r1, r2 = pltpu.roll(g, 1, axis=0), pltpu.roll(g, 2, axis=0)
    top = _iota((8, g.shape[1]), 0)
    p1 = jnp.concatenate([jnp.where(top == 0, h1, r1[:8]), r1[8:]], axis=0)
    p2 = jnp.concatenate([jnp.where(top == 0, h2, jnp.where(top == 1, h1, r2[:8])), r2[8:]], axis=0)
    c = cw_ref[0:1, :] * p2 + cw_ref[1:2, :] * p1 + cw_ref[2:3, :] * g + cb_ref[...]
    act = 0.5 * c * (1.0 + lax.erf(c * (1.0 / math.sqrt(2.0)))) * u_ref[...].astype(F32)
    o_ref[...] = r_ref[...] + _dot(act.astype(BF16), w_ref[...])


def convglu_out(res, ug, conv_w, conv_b, w_out, *, seq, tm=256):
    t = res.shape[0]
    f, n = w_out.shape
    hb = tm // HALO
    return pl.pallas_call(
        functools.partial(_convglu_out_kernel, tm=tm, seq=seq),
        grid=(t // tm,),
        in_specs=[pl.BlockSpec((tm, f), lambda i: (i, 0)),
                  pl.BlockSpec((tm, f), lambda i: (i, 1)),
                  pl.BlockSpec((HALO, f), lambda i: (jnp.maximum(i * hb - 1, 0), 1)),
                  pl.BlockSpec((3, f), lambda i: (0, 0)),
                  pl.BlockSpec((1, f), lambda i: (0, 0)),
                  pl.BlockSpec((f, n), lambda i: (0, 0)),
                  pl.BlockSpec((tm, n), lambda i: (i, 0))],
        out_specs=pl.BlockSpec((tm, n), lambda i: (i, 0)),
        out_shape=jax.ShapeDtypeStruct((t, n), F32),
        compiler_params=_cp("parallel"),
        name="convglu_out",
    )(ug, ug, ug, conv_w, conv_b.reshape(1, f), w_out, res)


def _ple_kernel(h_ref, p_ref, wp_ref, pn_ref, gn_ref, wg_ref, fn_ref, o_ref, *, final):
    h = h_ref[...]
    e = _rms(_dot(p_ref[...].astype(BF16), wp_ref[...]), pn_ref[...])
    gate = _sigmoid(_dot(_rms(h, gn_ref[...]).astype(BF16), wg_ref[...]))
    out = h + gate * e
    if final:
        out = _rms(out, fn_ref[...])
    o_ref[...] = out


def ple(h, p, w_proj, p_norm, g_norm, w_gate, f_norm, *, final, tm=1024):
    t, d = h.shape
    pd = p.shape[1]
    row = lambda i: (i, 0)
    fix = lambda i: (0, 0)
    return pl.pallas_call(
        functools.partial(_ple_kernel, final=final),
        grid=(t // tm,),
        in_specs=[pl.BlockSpec((tm, d), row), pl.BlockSpec((tm, pd), row),
                  pl.BlockSpec((pd, d), fix), pl.BlockSpec((1, d), fix), pl.BlockSpec((1, d), fix),
                  pl.BlockSpec((d, d), fix), pl.BlockSpec((1, d), fix)],
        out_specs=pl.BlockSpec((tm, d), row),
        out_shape=jax.ShapeDtypeStruct((t, d), F32),
        compiler_params=_cp("parallel"),
        name="ple",
    )(h, p, w_proj, p_norm.reshape(1, d), g_norm.reshape(1, d), w_gate, f_norm.reshape(1, d))


def _sb_attn_kernel(q_ref, k_ref, v_ref, u_ref, o_ref, *, tq, tk, scale):
    qi = pl.program_id(2)
    nblk = tq // tk
    first = _iota((tq, LANE), 1) < HEAD_DIM
    qf = q_ref[...] * scale
    zero = jnp.zeros_like(qf)
    qs = (jnp.where(first, qf, zero), jnp.where(first, zero, qf))

    def sweep(k_lo, runs, accs, masked):
        parts = []
        for d in reversed(range(nblk)):
            k0 = k_lo + d * tk
            kb = k_ref[pl.ds(k0, tk), :]
            valid = None
            if masked:
                valid = _iota((tq, tk), 1) + d * tk < _iota((tq, tk), 0)
            for h in range(2):
                t = _dot_nt(qs[h], kb) * math.log2(math.e)
                sp = jnp.maximum(t, 0.0) + jnp.log2(1.0 + jnp.exp2(-jnp.abs(t)))
                lom = -sp
                if masked:
                    lom = jnp.where(valid, lom, 0.0)
                hi = lom.astype(BF16)
                lo = (lom - hi.astype(F32)).astype(BF16)
                sums = _dot(jnp.concatenate([hi, lo], axis=1), u_ref[...])
                parts.append((h, k0, t - sp, sums, valid))
        runs, accs = list(runs), list(accs)
        for h, k0, logsig, sums, valid in parts:
            w = jnp.exp2(logsig + runs[h] + sums[:, :tk])
            if masked:
                w = jnp.where(valid, w, 0.0)
            accs[h] = accs[h] + _dot(w.astype(BF16), v_ref[pl.ds(k0, tk), :])
            runs[h] = runs[h] + sums[:, tk:]
        return tuple(runs), tuple(accs)

    runs = (jnp.zeros((tq, tk), F32),) * 2
    accs = (jnp.zeros((tq, LANE), F32),) * 2
    runs, accs = sweep(pl.multiple_of(qi * tq, tq), runs, accs, True)

    def live(runs):
        return jnp.max(jnp.maximum(runs[0], runs[1])) > SB_CUT

    def cond(c):
        return c[0]

    def body(c):
        _, j, runs, accs = c
        runs, accs = sweep(pl.multiple_of(j * tq, tq), runs, accs, False)
        return (j > 0) & live(runs), j - 1, runs, accs

    _, _, _, accs = lax.while_loop(cond, body, ((qi > 0) & live(runs), qi - 1, runs, accs))
    o_ref[...] = jnp.where(first, accs[0], accs[1])


def sb_attention(qkv, *, batch, heads, tq=256):
    t, w3 = qkv.shape
    s = t // batch
    nq = s // tq
    pairs = heads * HEAD_DIM // LANE
    tk = LANE
    jj = jnp.arange(2 * tk)[:, None] % tk
    ss = jnp.arange(2 * tk)[None, :]
    u = jnp.where(ss < tk, jj > ss, True).astype(BF16)
    return pl.pallas_call(
        functools.partial(_sb_attn_kernel, tq=tq, tk=tk, scale=1.0 / math.sqrt(HEAD_DIM)),
        grid=(batch, pairs, nq),
        in_specs=[pl.BlockSpec((tq, LANE), lambda b, p, i: (b * nq + i, p)),
                  pl.BlockSpec((s, LANE), lambda b, p, i: (b, pairs + p)),
                  pl.BlockSpec((s, LANE), lambda b, p, i: (b, 2 * pairs + p)),
                  pl.BlockSpec((2 * tk, 2 * tk), lambda b, p, i: (0, 0))],
        out_specs=pl.BlockSpec((tq, LANE), lambda b, p, i: (b * nq + i, p)),
        out_shape=jax.ShapeDtypeStruct((t, w3 // 3), F32),
        compiler_params=_cp("parallel", "parallel", "arbitrary"),
        name="sb_attention",
    )(qkv, qkv, qkv, u)


def _softmax_attn_kernel(qn_ref, qr_ref, kn_ref, kr_ref, v_ref, o_ref, sa0, sa1, sb0, sb1,
                         *, tq, tk, scale):
    hp = pl.program_id(1)
    qi = pl.program_id(2)
    lane = _iota((tq, LANE), 1)
    qn = qn_ref[...] * (scale * math.log2(math.e))
    qr = qr_ref[...] * (scale * math.log2(math.e))
    qs = []
    for h in range(2):
        nope = jnp.where(lane // MLA_NOPE == h, qn, 0.0)
        rope_ = jnp.where(lane // MLA_ROPE == (hp % 2) * 2 + h, qr, 0.0)
        qs.append(jnp.concatenate([nope, rope_], axis=1).astype(BF16))

    def scores(j, dsts):
        k0 = pl.multiple_of(j * tk, tk)
        kb = jnp.concatenate([kn_ref[pl.ds(k0, tk), :], kr_ref[pl.ds(k0, tk), :]], axis=1)
        mx = []
        for h in range(2):
            s = _dot_nt(qs[h], kb)
            dsts[h][...] = s
            mx.append(jnp.max(s, axis=1, keepdims=True))
        return tuple(mx)

    def consume(srcs, j, carry, mx):
        vb = v_ref[pl.ds(pl.multiple_of(j * tk, tk), tk), :]
        one = jnp.ones_like(vb)
        first = _iota(vb.shape, 1) < MLA_V
        vbs = (jnp.where(first, vb, one), jnp.where(first, one, vb))
        out = []
        for h in range(2):
            m, acc = carry[h]
            s = srcs[h][...]
            if mx is None:
                s = jnp.where(_iota((tq, tk), 1) <= _iota((tq, tk), 0), s, -1e30)
                m_new = jnp.maximum(m, jnp.max(s, axis=1, keepdims=True))
            else:
                m_new = jnp.maximum(m, mx[h])
            p = jnp.exp2(s - m_new)
            acc = jnp.exp2(m - m_new) * acc + _dot(p.astype(BF16), vbs[h])
            out.append((m_new, acc))
        return tuple(out)

    sa, sb = (sa0, sa1), (sb0, sb1)

    def pair(t, carry):
        state, mxa = carry
        mxb = scores(2 * t + 1, sb)
        state = consume(sa, 2 * t, state, mxa)
        mxa = scores(2 * t + 2, sa)
        return consume(sb, 2 * t + 1, state, mxb), mxa

    def tail_odd(carry):
        state, mxa = carry
        scores(qi, sb)
        state = consume(sa, qi - 1, state, mxa)
        return consume(sb, qi, state, None)

    def tail_even(carry):
        return consume(sa, qi, carry[0], None)

    init = ((jnp.full((tq, 1), -1e30, F32), jnp.zeros((tq, LANE), F32)),) * 2
    carry = lax.fori_loop(0, qi // 2, pair, (init, scores(0, sa)))
    (_, acc0), (_, acc1) = lax.cond(qi % 2 == 1, tail_odd, tail_even, carry)
    o_ref[...] = jnp.where(lane < MLA_V, acc0 / pltpu.roll(acc0, MLA_V, axis=1),
                           acc1 / pltpu.roll(acc1, MLA_V, axis=1))


def softmax_attention(q, qrot, kv, krot, scale, *, batch, heads, tq=512):
    tk = tq
    t = q.shape[0]
    s = t // batch
    nq = s // tq
    pairs = heads * MLA_NOPE // LANE
    buf = pltpu.VMEM((tq, tk), F32)
    return pl.pallas_call(
        functools.partial(_softmax_attn_kernel, tq=tq, tk=tk, scale=scale),
        grid=(batch, pairs, nq),
        in_specs=[pl.BlockSpec((tq, LANE), lambda b, p, i: (b * nq + i, p)),
                  pl.BlockSpec((tq, LANE), lambda b, p, i: (b * nq + i, p // 2)),
                  pl.BlockSpec((s, LANE), lambda b, p, i: (b, p)),
                  pl.BlockSpec((s, LANE), lambda b, p, i: (b, 0)),
                  pl.BlockSpec((s, LANE), lambda b, p, i: (b, pairs + p))],
        out_specs=pl.BlockSpec((tq, LANE), lambda b, p, i: (b * nq + i, p)),
        out_shape=jax.ShapeDtypeStruct((t, heads * MLA_V), F32),
        scratch_shapes=[buf, buf, buf, buf],
        compiler_params=_cp("parallel", "parallel", "arbitrary"),
        name="softmax_attention",
    )(q, qrot, kv, krot, kv)


def _rope_kernel(q_ref, k_ref, pos_ref, f_ref, oq_ref, ok_ref):
    half = MLA_ROPE // 2
    ang = pos_ref[...].astype(F32) * f_ref[...]
    cos, sin = jnp.cos(ang), jnp.sin(ang)
    low = _iota(ang.shape, 1) % MLA_ROPE < half

    def rot(x):
        partner = jnp.where(low, -pltpu.roll(x, LANE - half, axis=1), pltpu.roll(x, half, axis=1))
        return x * cos + partner * sin

    for c in range(q_ref.shape[1] // LANE):
        cols = slice(c * LANE, (c + 1) * LANE)
        oq_ref[:, cols] = rot(q_ref[:, cols])
    ok_ref[...] = rot(k_ref[...]).astype(ok_ref.dtype)


def rope(q, c, positions, *, q_col, qw, k_col, tm=512):
    t = q.shape[0]
    half = MLA_ROPE // 2
    inv_freq = ROPE_THETA ** (-jnp.arange(half, dtype=F32) / half)
    freq = jnp.tile(inv_freq, LANE // half).reshape(1, LANE)
    return pl.pallas_call(
        _rope_kernel,
        grid=(t // tm,),
        in_specs=[pl.BlockSpec((tm, qw), lambda i: (i, q_col)), pl.BlockSpec((tm, LANE), lambda i: (i, k_col)),
                  pl.BlockSpec((tm, 1), lambda i: (i, 0)), pl.BlockSpec((1, LANE), lambda i: (0, 0))],
        out_specs=[pl.BlockSpec((tm, qw), lambda i: (i, 0)), pl.BlockSpec((tm, LANE), lambda i: (i, 0))],
        out_shape=[jax.ShapeDtypeStruct((t, qw), F32), jax.ShapeDtypeStruct((t, LANE), BF16)],
        compiler_params=_cp("parallel"),
        name="rope",
    )(q, c, positions.reshape(t, 1), freq)


def _rwkv_prep_kernel(x_ref, xp_ref, mu_ref, w0_ref, a0_ref, kk_ref, ka_ref, rk_ref,
                      w2_ref, a2_ref, g2_ref, tri_ref, hs_ref,
                      at_ref, bt_ref, kt_ref, rt_ref, v_ref, eg_ref, g_ref, bv_ref, *, tm, seq, width):
    i = pl.program_id(0)
    x = x_ref[...]
    at_start = (i * tm) % seq == 0
    prev_row = jnp.where(at_start, 0.0, xp_ref[7:8, :])
    rolled = pltpu.roll(x, 1, axis=0)
    top = _iota((8, x.shape[1]), 0)
    prev = jnp.concatenate([jnp.where(top == 0, prev_row, rolled[:8]), rolled[8:]], axis=0)
    xm = x + (prev - x) * mu_ref[...]
    r = xm[:, :width]
    k = xm[:, width:2 * width]
    v = xm[:, 2 * width:3 * width]
    l1 = xm[:, 3 * width:3 * width + LANE]
    l2 = xm[:, 3 * width + LANE:]
    wpre = w0_ref[...] + _dot(_Split(jnp.tanh(l1)).left(1), w2_ref[...])
    a = _sigmoid(a0_ref[...] + _dot(_Split(l1).left(1), a2_ref[...]))
    g = _dot(_Split(_sigmoid(l2)).left(1), g2_ref[...])

    def hilo(t):
        ts = _Split(t)
        return jnp.concatenate([ts.hi, ts.lo], axis=1)

    lw = -jnp.exp(-_softplus(-wpre) - 0.5)
    hs = hs_ref[...]
    kk = k * kk_ref[...]
    kk = kk * lax.rsqrt(jnp.maximum(_dot(hilo(kk * kk), hs), 1e-24))
    km = k * (1.0 + (a - 1.0) * ka_ref[...])
    bonus = _dot(hilo(r * km * rk_ref[...]), hs)
    lws = _Split(lw)
    cum = _dot(tri_ref[...], jnp.concatenate([lws.hi, lws.lo], axis=0))
    eg = jnp.exp(cum)
    ieg = jnp.exp(-cum)
    at_ref[...] = -kk * jnp.exp(cum - lw)
    bt_ref[...] = kk * a * ieg
    kt_ref[...] = km * ieg
    rt_ref[...] = r * eg
    v_ref[...] = v
    eg_ref[...] = eg
    g_ref[...] = g
    bv_ref[...] = bonus * v


def rwkv_prep(x, mu, w0, a0, k_k, k_a, r_k, w2p, a2p, g2p, *, seq, width, tm=256):
    t, wx = x.shape
    ch = RW_CHUNK
    ti = jnp.arange(tm)
    tri = ((ti[:, None] // ch == ti[None, :] // ch) & (ti[None, :] <= ti[:, None])).astype(BF16)
    tri = jnp.concatenate([tri, tri], axis=1)
    ci = jnp.arange(width) // HEAD_DIM
    hs = (ci[:, None] == ci[None, :]).astype(BF16)
    hs = jnp.concatenate([hs, hs], axis=0)
    w2p, a2p, g2p = (_Split(w).right(0) for w in (w2p, a2p, g2p))
    hb = tm // 8
    row = lambda i: (i, 0)
    fix = lambda i: (0, 0)
    vec = pl.BlockSpec((1, width), fix)
    out = jax.ShapeDtypeStruct((t, width), F32)
    return pl.pallas_call(
        functools.partial(_rwkv_prep_kernel, tm=tm, seq=seq, width=width),
        grid=(t // tm,),
        in_specs=[pl.BlockSpec((tm, wx), row),
                  pl.BlockSpec((8, wx), lambda i: (jnp.maximum(i * hb - 1, 0), 0)),
                  pl.BlockSpec((1, wx), fix), vec, vec, vec, vec, vec,
                  pl.BlockSpec(w2p.shape, fix), pl.BlockSpec(a2p.shape, fix),
                  pl.BlockSpec(g2p.shape, fix), pl.BlockSpec(tri.shape, fix),
                  pl.BlockSpec(hs.shape, fix)],
        out_specs=[pl.BlockSpec((tm, width), row)] * 8,
        out_shape=[out] * 8,
        compiler_params=_cp("parallel"),
        name="rwkv_prep",
    )(x, x, mu, w0, a0, k_k, k_a, r_k, w2p, a2p, g2p, tri, hs)


def _rwkv_chunk_kernel(at_ref, bt_ref, kt_ref, rt_ref, v_ref, eg_ref,
                       m_ref, n_ref, rq_ref, yl_ref, *, nchunk, unroll):
    ch, n = RW_CHUNK, HEAD_DIM
    ti = _iota((ch, ch), 0)
    si = _iota((ch, ch), 1)
    strict = si < ti
    incl = si <= ti
    same = (ti // RW_SUB) == (si // RW_SUB)
    eye = _iota((n, n), 0) == _iota((n, n), 1)
    each = lambda f, *ls: [f(*t) for t in zip(*ls)]

    def body(it, _):
        cs = [it * unroll + u for u in range(unroll)]
        sls = [pl.ds(pl.multiple_of(c * ch, ch), ch) for c in cs]
        ids = [(h, u) for h in range(2) for u in range(unroll)]
        ld = lambda ref: [ref[sls[u], h * n:(h + 1) * n] for h, u in ids]
        sp = lambda xs: [_Split(t) for t in xs]
        a_, b_, k_, r_, v_ = ld(at_ref), ld(bt_ref), ld(kt_ref), ld(rt_ref), ld(v_ref)
        g_end = [eg_ref[pl.ds(cs[u] * ch + ch - 1, 1), h * n:(h + 1) * n] for h, u in ids]
        ar = sp(each(lambda a, r: jnp.concatenate([a, r], axis=0), a_, r_))
        vs = sp(v_)
        pb = each(_mm3_nt, ar, sp(b_))
        pk = each(_mm3_nt, ar, sp(k_))
        a_ab = [jnp.where(strict, t[:ch], 0.0) for t in pb]
        a_ak = sp([jnp.where(strict, t[:ch], 0.0) for t in pk])
        a_rb = sp([jnp.where(incl, t[ch:], 0.0) for t in pb])
        a_rk = sp([jnp.where(incl, t[ch:], 0.0) for t in pk])
        d1 = [jnp.where(same, t, 0.0) for t in a_ab]
        lb = each(lambda t, d: t - d, a_ab, d1)
        d1 = sp(d1)
        d2 = sp(each(_mm3, d1, d1))
        akv = each(_mm3, a_ak, vs)
        x = each(lambda l, a, t: jnp.concatenate([l, a, t], axis=1), lb, a_, akv)
        d4 = sp(each(_mm3, d2, d2))
        x = each(lambda d, t: t + _mm3(d, _Split(t)), d1, x)
        d8 = sp(each(_mm3, d4, d4))
        x = each(lambda d, t: t + _mm3(d, _Split(t)), d2, x)
        x = each(lambda d, t: t + _mm3(d, _Split(t)), d4, x)
        x = each(lambda d, t: t + _mm3(d, _Split(t)), d8, x)
        e1 = sp([t[:, :ch] for t in x])
        wu = [t[:, ch:] for t in x]
        e2 = sp(each(_mm3, e1, e1))
        wu = each(lambda e, t: t + _mm3(e, _Split(t)), e1, wu)
        wu = sp(each(lambda e, t: t + _mm3(e, _Split(t)), e2, wu))
        ry = each(_mm3, a_rb, wu)
        rkv = each(_mm3, a_rk, vs)
        mn = each(lambda b, g, t: _mm3_tn(_Split(b * g), t), b_, g_end, wu)
        kv = each(lambda k, g, t: _mm3_tn(_Split(k * g), t), k_, g_end, vs)
        rq = each(lambda r, t: r + t[:, :n], r_, ry)
        yl = each(lambda t, w: t[:, n:] + w, ry, rkv)
        for u in range(unroll):
            rq_ref[sls[u], :] = jnp.concatenate([rq[u], rq[unroll + u]], axis=1)
            yl_ref[sls[u], :] = jnp.concatenate([yl[u], yl[unroll + u]], axis=1)
        for i, (h, u) in enumerate(ids):
            m_ref[h, cs[u]] = mn[i][:, :n] + jnp.where(eye, g_end[i], 0.0)
            n_ref[h, cs[u]] = mn[i][:, n:] + kv[i]
        return 0

    lax.fori_loop(0, nchunk // unroll, body, 0)


def rwkv_chunk(at, bt, kt, rt, v, eg, *, batch, nchunk=8, unroll=8):
    t, w = at.shape
    s = t // batch
    ch, n = RW_CHUNK, HEAD_DIM
    pairs = w // LANE
    rows = nchunk * ch
    nr = s // rows
    spec = pl.BlockSpec((rows, LANE), lambda b, p, i: (b * nr + i, p))
    mspec = pl.BlockSpec((2, nchunk, n, n), lambda b, p, i: (b * pairs + p, i, 0, 0))
    return pl.pallas_call(
        functools.partial(_rwkv_chunk_kernel, nchunk=nchunk, unroll=unroll),
        grid=(batch, pairs, nr),
        in_specs=[spec] * 6,
        out_specs=[mspec, mspec, spec, spec],
        out_shape=[jax.ShapeDtypeStruct((batch * w // n, s // ch, n, n), F32)] * 2
        + [jax.ShapeDtypeStruct((t, w), F32)] * 2,
        compiler_params=_cp("parallel", "parallel", "parallel"),
        name="rwkv_chunk",
    )(at, bt, kt, rt, v, eg)


def _rwkv_scan_kernel(m_ref, n_ref, s_ref, st_ref, *, heads, nchunk):
    @pl.when(pl.program_id(1) == 0)
    def _():
        st_ref[...] = jnp.zeros_like(st_ref)

    def body(c, _):
        sts = [st_ref[hh] for hh in range(heads)]
        for hh in range(heads):
            s_ref[hh, c] = sts[hh]
        ms = [_Split(m_ref[hh, c]) for hh in range(heads)]
        ss = [_Split(st) for st in sts]
        new = [_mm3(a, b) for a, b in zip(ms, ss)]
        for hh in range(heads):
            st_ref[hh] = new[hh] + n_ref[hh, c]
        return 0

    lax.fori_loop(0, nchunk, body, 0)


def rwkv_scan(m, nn, *, heads=16, nchunk=8):
    bh, nc, n, _ = m.shape
    heads = min(heads, bh)
    nchunk = min(nchunk, nc)
    spec = pl.BlockSpec((heads, nchunk, n, n), lambda b, i: (b, i, 0, 0))
    return pl.pallas_call(
        functools.partial(_rwkv_scan_kernel, heads=heads, nchunk=nchunk),
        grid=(bh // heads, nc // nchunk),
        in_specs=[spec, spec],
        out_specs=spec,
        out_shape=jax.ShapeDtypeStruct((bh, nc, n, n), F32),
        scratch_shapes=[pltpu.VMEM((heads, n, n), F32)],
        compiler_params=_cp("parallel", "arbitrary"),
        name="rwkv_scan",
    )(m, nn)


def _rwkv_out_kernel(rq_ref, yl_ref, s_ref, lw_ref, lb_ref, avg_ref, o_ref, *, nchunk):
    ch, n = RW_CHUNK, HEAD_DIM
    ys = []
    for h in range(2):
        cols = slice(h * n, (h + 1) * n)
        ys.append(jnp.concatenate(
            [_mm3(_Split(rq_ref[c * ch:(c + 1) * ch, cols]), _Split(s_ref[h, c])) for c in range(nchunk)],
            axis=0))
    y = jnp.concatenate(ys, axis=1) + yl_ref[...]
    avg2 = jnp.concatenate([avg_ref[...]] * 2, axis=0)

    def mean(t):
        ts = _Split(t)
        return _dot(jnp.concatenate([ts.hi, ts.lo], axis=1), avg2)

    d = y - mean(y)
    var = mean(d * d)
    o_ref[...] = d * lax.rsqrt(var + GN_EPS) * lw_ref[...] + lb_ref[...]


def rwkv_out(rq, yl, s0, ln_w, ln_b, *, batch, nchunk=8):
    t, w = rq.shape
    s = t // batch
    n = HEAD_DIM
    pairs = w // LANE
    rows = nchunk * RW_CHUNK
    nr = s // rows
    li = jnp.arange(LANE) // n
    avg = ((li[:, None] == li[None, :]).astype(F32) / n).astype(BF16)
    spec = pl.BlockSpec((rows, LANE), lambda b, p, i: (b * nr + i, p))
    hspec = pl.BlockSpec((1, LANE), lambda b, p, i: (0, p))
    return pl.pallas_call(
        functools.partial(_rwkv_out_kernel, nchunk=nchunk),
        grid=(batch, pairs, nr),
        in_specs=[spec, spec, pl.BlockSpec((2, nchunk, n, n), lambda b, p, i: (b * pairs + p, i, 0, 0)),
                  hspec, hspec, pl.BlockSpec((LANE, LANE), lambda b, p, i: (0, 0))],
        out_specs=spec,
        out_shape=jax.ShapeDtypeStruct((t, w), F32),
        compiler_params=_cp("parallel", "parallel", "parallel"),
        name="rwkv_out",
    )(rq, yl, s0, ln_w.reshape(1, w), ln_b.reshape(1, w), avg)


def _hyb_out_kernel(oa_ref, y_ref, bv_ref, g_ref, wa_ref, wb_ref, r_ref, o_ref):
    ob = (y_ref[...] + bv_ref[...]) * g_ref[...]
    o_ref[...] = (r_ref[...] + _dot(oa_ref[...].astype(BF16), wa_ref[...])
                  + _dot(ob.astype(BF16), wb_ref[...]))


def hyb_out(res, oa, y, bv, g, wa, wb, *, tm=1024):
    t, d = res.shape
    row = lambda i: (i, 0)
    fix = lambda i: (0, 0)
    act = lambda a: pl.BlockSpec((tm, a.shape[1]), row)
    return pl.pallas_call(
        _hyb_out_kernel,
        grid=(t // tm,),
        in_specs=[act(oa), act(y), act(bv), act(g), pl.BlockSpec(wa.shape, fix),
                  pl.BlockSpec(wb.shape, fix), pl.BlockSpec((tm, d), row)],
        out_specs=pl.BlockSpec((tm, d), row),
        out_shape=jax.ShapeDtypeStruct((t, d), F32),
        compiler_params=_cp("parallel"),
        name="hyb_out",
    )(oa, y, bv, g, wa, wb, res)


def _pad_cols(w, n):
    return jnp.pad(w, ((0, 0), (0, n - w.shape[1])))


def sb_rwkv_layer(h, b, s, attn_norm, w_in, w_out, mu, w0, w2, a0, a2, g2, k_k, k_a, r_k, ln_w, ln_b):
    sbw = SB_HEADS * HEAD_DIM
    rww = RW_HEADS * HEAD_DIM
    rw_in = w_in.shape[1] - 3 * sbw
    rw_pad = -(-rw_in // LANE) * LANE
    dl, al = w2.shape[0], a2.shape[0]
    gl = g2.shape[0]
    assert dl + al == LANE and 3 * rww % LANE == 0

    qkv = rms_matmul(h, attn_norm, w_in[:, :3 * sbw].astype(BF16), out_dtype=BF16)
    rw = rms_matmul(h, attn_norm, _pad_cols(w_in[:, 3 * sbw:], rw_pad).astype(BF16))
    oa = sb_attention(qkv, batch=b, heads=SB_HEADS)

    w2p = jnp.zeros((LANE, rww), F32).at[:dl].set(w2)
    a2p = jnp.zeros((LANE, rww), F32).at[dl:].set(a2)
    g2p = jnp.zeros((rw_pad - 3 * rww - LANE, rww), F32).at[:gl].set(g2)
    vec = lambda t: t.reshape(1, rww).astype(F32)
    mu_p = _pad_cols(mu.reshape(1, rw_in), rw_pad)
    at, bt, kt, rt, vv, eg, g, bv = rwkv_prep(rw, mu_p, vec(w0), vec(a0), vec(k_k), vec(k_a), vec(r_k),
                                              w2p, a2p, g2p, seq=s, width=rww)
    m, nn, rq, yl = rwkv_chunk(at, bt, kt, rt, vv, eg, batch=b)
    s0 = rwkv_scan(m, nn)
    y = rwkv_out(rq, yl, s0, ln_w, ln_b, batch=b)
    return hyb_out(h, oa, y, bv, g, w_out[:sbw].astype(BF16), w_out[sbw:].astype(BF16))


def mla_layer(h, b, s, positions, attn_norm, w_down, q_norm, kv_norm, w_uq, w_ukv, w_o):
    nh, dn, dr, dv = MLA_HEADS, MLA_NOPE, MLA_ROPE, MLA_V
    qr, kvr = MLA_Q_RANK, MLA_KV_RANK
    assert dn == dv and qr % LANE == 0 and (qr + LANE) % kvr == 0
    wd = jnp.concatenate([w_down[:, :qr], jnp.tile(w_down[:, qr + kvr:], (1, LANE // dr)),
                          w_down[:, qr:qr + kvr]], axis=1)
    c = rms_matmul(h, attn_norm, wd.astype(BF16))
    wq = w_uq.reshape(qr, nh, dn + dr)
    wq = jnp.concatenate([wq[:, :, :dn].reshape(qr, nh * dn), wq[:, :, dn:].reshape(qr, nh * dr)], axis=1)
    q = rms_matmul(c, q_norm, wq.astype(BF16), x_col=0)
    wkv = w_ukv.reshape(kvr, nh, dn + dv)
    wkv = jnp.concatenate([wkv[:, :, :dn].reshape(kvr, nh * dn), wkv[:, :, dn:].reshape(kvr, nh * dv)], axis=1)
    kv = rms_matmul(c, kv_norm, wkv.astype(BF16), x_col=(qr + LANE) // kvr, out_dtype=BF16)
    qrot, krot = rope(q, c, positions, q_col=dn // dr, qw=nh * dr, k_col=qr // LANE)
    o = softmax_attention(q, qrot, kv, krot, 1.0 / math.sqrt(dn + dr), batch=b, heads=nh)
    return matmul_residual(h, o, w_o.astype(BF16))


def kernel(x, p, positions, attn_norm, ffn_norm, ffn_w_in, ffn_conv_w, ffn_conv_b, ffn_w_out, ple_w_proj, ple_norm, ple_gate_norm, ple_w_gate, hyb_w_in, hyb_w_out, rw_mu, rw_w0, rw_w2, rw_a0, rw_a2, rw_g2, rw_k_k, rw_k_a, rw_r_k, rw_ln_w, rw_ln_b, mla_w_down, mla_q_norm, mla_kv_norm, mla_w_uq, mla_w_ukv, mla_w_o, final_norm):
    b, s, d = x.shape
    depth = p.shape[0]
    h = x.reshape(b * s, d)
    pos = positions.reshape(b * s)
    for i in range(depth):
        j = i // 2
        if i % 2 == 0:
            h = sb_rwkv_layer(h, b, s, attn_norm[i], hyb_w_in[j], hyb_w_out[j], rw_mu[j], rw_w0[j],
                              rw_w2[j], rw_a0[j], rw_a2[j], rw_g2[j], rw_k_k[j], rw_k_a[j], rw_r_k[j],
                              rw_ln_w[j], rw_ln_b[j])
        else:
            h = mla_layer(h, b, s, pos, attn_norm[i], mla_w_down[j], mla_q_norm[j], mla_kv_norm[j],
                          mla_w_uq[j], mla_w_ukv[j], mla_w_o[j])
        ug = rms_matmul(h, ffn_norm[i], ffn_w_in[i].astype(BF16), out_dtype=BF16)
        h = convglu_out(h, ug, ffn_conv_w[i], ffn_conv_b[i], ffn_w_out[i].astype(BF16), seq=s)
        h = ple(h, p[i].reshape(b * s, -1), ple_w_proj[i].astype(BF16), ple_norm[i], ple_gate_norm[i],
                ple_w_gate[i].astype(BF16), final_norm, final=(i == depth - 1))
    return h.reshape(b, s, d)
```

```python
import functools
import math

import jax
import jax.numpy as jnp
from jax import lax
from jax.experimental import pallas as pl
from jax.experimental.pallas import tpu as pltpu

F32 = jnp.float32
BF16 = jnp.bfloat16
HI = lax.Precision.HIGHEST

NORM_EPS = 1e-6
GN_EPS = 64e-5
ROPE_THETA = 10000.0
LANE = 128
VMEM_LIMIT = 48 * 1024 * 1024
HALO = 16
MXU_WIDE_TILE = 512
TILE_VMEM_BUDGET = 36 * 1024 * 1024
RESIDENT_WEIGHT_BYTES = 4 * 1024 * 1024

SB_HEADS = 8
RW_HEADS = 8
HEAD_DIM = 64
MLA_HEADS = 16
MLA_NOPE = 64
MLA_ROPE = 32
MLA_V = 64
MLA_Q_RANK = 384
MLA_KV_RANK = 256
SB_CUT = -151.0
RW_CHUNK = 64
RW_SUB = 16


def _cp(*sem):
    return pltpu.CompilerParams(dimension_semantics=sem, vmem_limit_bytes=VMEM_LIMIT)


def _iota(shape, dim):
    return lax.broadcasted_iota(jnp.int32, shape, dim)


def _softplus(z):
    return jnp.maximum(z, 0.0) + jnp.log1p(jnp.exp(-jnp.abs(z)))


def _sigmoid(z):
    return 1.0 / (1.0 + jnp.exp(-z))


def _dot(a, b, precision=None):
    return jnp.dot(a, b, preferred_element_type=F32, precision=precision)


def _dot_nt(a, b, precision=None):
    return lax.dot_general(a, b, (((1,), (1,)), ((), ())), preferred_element_type=F32,
                           precision=precision)


def _dot_tn(a, b, precision=None):
    return lax.dot_general(a, b, (((0,), (0,)), ((), ())), preferred_element_type=F32,
                           precision=precision)


class _Split:
    def __init__(self, x):
        self.x = x
        self.hi = x.astype(BF16)
        self.lo = (x - self.hi.astype(F32)).astype(BF16)
        self._packed = {}

    def left(self, axis):
        key = ("l", axis)
        if key not in self._packed:
            self._packed[key] = jnp.concatenate([self.hi, self.lo, self.hi], axis=axis)
        return self._packed[key]

    def right(self, axis):
        key = ("r", axis)
        if key not in self._packed:
            self._packed[key] = jnp.concatenate([self.hi, self.hi, self.lo], axis=axis)
        return self._packed[key]


def _mm3(a, b):
    return _dot(a.left(1), b.right(0))


def _mm3_nt(a, b):
    return _dot_nt(a.left(1), b.right(1))


def _mm3_tn(a, b):
    return _dot_tn(a.left(0), b.right(0))


def _rms(x, g, eps=NORM_EPS):
    return x * lax.rsqrt(jnp.mean(x * x, axis=-1, keepdims=True) + eps) * g


def _rms_matmul_kernel(x_ref, g_ref, w_ref, o_ref, xn_ref):
    @pl.when(pl.program_id(1) == 0)
    def _():
        xn_ref[...] = _rms(x_ref[...].astype(F32), g_ref[...]).astype(BF16)

    o_ref[...] = _dot(xn_ref[...], w_ref[...]).astype(o_ref.dtype)


def _pick_tiles(t, k, n, tile_bytes):
    tn = n if k * n * 2 <= RESIDENT_WEIGHT_BYTES else MXU_WIDE_TILE
    for tm in (2048, 1024, 512, 256):
        if t % tm == 0 and tile_bytes(tm, tn) + 2 * k * tn * 2 <= TILE_VMEM_BUDGET:
            return tm, tn
    raise ValueError("no row tile fits")


def rms_matmul(x, g, w, *, x_col=0, out_dtype=F32):
    t = x.shape[0]
    k, n = w.shape
    ob = jnp.dtype(out_dtype).itemsize
    tm, tn = _pick_tiles(t, k, n, lambda tm, tn: 2 * tm * k * x.dtype.itemsize + tm * k * 2 + 2 * tm * tn * ob)
    return pl.pallas_call(
        _rms_matmul_kernel,
        grid=(t // tm, n // tn),
        in_specs=[pl.BlockSpec((tm, k), lambda i, j: (i, x_col)),
                  pl.BlockSpec((1, k), lambda i, j: (0, 0)),
                  pl.BlockSpec((k, tn), lambda i, j: (0, j))],
        out_specs=pl.BlockSpec((tm, tn), lambda i, j: (i, j)),
        out_shape=jax.ShapeDtypeStruct((t, n), out_dtype),
        scratch_shapes=[pltpu.VMEM((tm, k), BF16)],
        compiler_params=_cp("parallel", "arbitrary"),
        name="rms_matmul",
    )(x, g.reshape(1, k).astype(F32), w)


def _matmul_res_kernel(a_ref, w_ref, r_ref, o_ref):
    o_ref[...] = r_ref[...] + _dot(a_ref[...].astype(BF16), w_ref[...])


def matmul_residual(res, a, w):
    t, k = a.shape
    n = w.shape[1]
    tm, tn = _pick_tiles(t, k, n, lambda tm, tn: 2 * tm * k * a.dtype.itemsize + 4 * tm * tn * 4)
    return pl.pallas_call(
        _matmul_res_kernel,
        grid=(t // tm, n // tn),
        in_specs=[pl.BlockSpec((tm, k), lambda i, j: (i, 0)),
                  pl.BlockSpec((k, tn), lambda i, j: (0, j)),
                  pl.BlockSpec((tm, tn), lambda i, j: (i, j))],
        out_specs=pl.BlockSpec((tm, tn), lambda i, j: (i, j)),
        out_shape=jax.ShapeDtypeStruct((t, n), F32),
        compiler_params=_cp("parallel", "arbitrary"),
        name="matmul_residual",
    )(a, w, res)


def _convglu_out_kernel(u_ref, g_ref, gp_ref, cw_ref, cb_ref, w_ref, r_ref, o_ref, *, tm, seq):
    i = pl.program_id(0)
    g = g_ref[...].astype(F32)
    at_start = (i * tm) % seq == 0
    halo = jnp.where(at_start, 0.0, gp_ref[...].astype(F32))
    h1, h2 = halo[HALO - 1:HALO, :], halo[HALO - 2:HALO - 1, :]
    r1, r2 = pltpu.roll(g, 1, axis=0), pltpu.roll(g, 2, axis=0)
    top = _iota((8, g.shape[1]), 0)
    p1 = jnp.concatenate([jnp.where(top == 0, h1, r1[:8]), r1[8:]], axis=0)
    p2 = jnp.concatenate([jnp.where(top == 0, h2, jnp.where(top == 1, h1, r2[:8])), r2[8:]], axis=0)
    c = cw_ref[0:1, :] * p2 + cw_ref[1:2, :] * p1 + cw_ref[2:3, :] * g + cb_ref[...]
    act = 0.5 * c * (1.0 + lax.erf(c * (1.0 / math.sqrt(2.0)))) * u_ref[...].astype(F32)
    o_ref[...] = r_ref[...] + _dot(act.astype(BF16), w_ref[...])


def convglu_out(res, ug, conv_w, conv_b, w_out, *, seq, tm=256):
    t = res.shape[0]
    f, n = w_out.shape
    hb = tm // HALO
    return pl.pallas_call(
        functools.partial(_convglu_out_kernel, tm=tm, seq=seq),
        grid=(t // tm,),
        in_specs=[pl.BlockSpec((tm, f), lambda i: (i, 0)),
                  pl.BlockSpec((tm, f), lambda i: (i, 1)),
                  pl.BlockSpec((HALO, f), lambda i: (jnp.maximum(i * hb - 1, 0), 1)),
                  pl.BlockSpec((3, f), lambda i: (0, 0)),
                  pl.BlockSpec((1, f), lambda i: (0, 0)),
                  pl.BlockSpec((f, n), lambda i: (0, 0)),
                  pl.BlockSpec((tm, n), lambda i: (i, 0))],
        out_specs=pl.BlockSpec((tm, n), lambda i: (i, 0)),
        out_shape=jax.ShapeDtypeStruct((t, n), F32),
        compiler_params=_cp("parallel"),
        name="convglu_out",
    )(ug, ug, ug, conv_w, conv_b.reshape(1, f), w_out, res)


def _ple_kernel(h_ref, p_ref, wp_ref, pn_ref, gn_ref, wg_ref, fn_ref, o_ref, *, final):
    h = h_ref[...]
    e = _rms(_dot(p_ref[...].astype(BF16), wp_ref[...]), pn_ref[...])
    gate = _sigmoid(_dot(_rms(h, gn_ref[...]).astype(BF16), wg_ref[...]))
    out = h + gate * e
    if final:
        out = _rms(out, fn_ref[...])
    o_ref[...] = out


def ple(h, p, w_proj, p_norm, g_norm, w_gate, f_norm, *, final, tm=1024):
    t, d = h.shape
    pd = p.shape[1]
    row = lambda i: (i, 0)
    fix = lambda i: (0, 0)
    return pl.pallas_call(
        functools.partial(_ple_kernel, final=final),
        grid=(t // tm,),
        in_specs=[pl.BlockSpec((tm, d), row), pl.BlockSpec((tm, pd), row),
                  pl.BlockSpec((pd, d), fix), pl.BlockSpec((1, d), fix), pl.BlockSpec((1, d), fix),
                  pl.BlockSpec((d, d), fix), pl.BlockSpec((1, d), fix)],
        out_specs=pl.BlockSpec((tm, d), row),
        out_shape=jax.ShapeDtypeStruct((t, d), F32),
        compiler_params=_cp("parallel"),
        name="ple",
    )(h, p, w_proj, p_norm.reshape(1, d), g_norm.reshape(1, d), w_gate, f_norm.reshape(1, d))


def _sb_attn_kernel(q_ref, k_ref, v_ref, u_ref, o_ref, *, tq, tk, nh, scale):
    qi = pl.program_id(2)
    nblk = tq // tk
    width = nh * HEAD_DIM
    qf = q_ref[...] * scale
    qlane = _iota((tq, width), 1) // HEAD_DIM
    qs = [jnp.where(qlane == h, qf, jnp.zeros_like(qf)) for h in range(nh)]
    vlane = _iota((tk, width), 1) // HEAD_DIM

    def sweep(k_lo, runs, acc, masked):
        parts = []
        for d in reversed(range(nblk)):
            k0 = k_lo + d * tk
            kb = k_ref[pl.ds(k0, tk), :]
            valid = None
            if masked:
                valid = _iota((tq, tk), 1) + d * tk < _iota((tq, tk), 0)
            for h in range(nh):
                t = _dot_nt(qs[h], kb) * math.log2(math.e)
                sp = jnp.maximum(t, 0.0) + jnp.log2(1.0 + jnp.exp2(-jnp.abs(t)))
                lom = -sp
                if masked:
                    lom = jnp.where(valid, lom, 0.0)
                hi = lom.astype(BF16)
                lo = (lom - hi.astype(F32)).astype(BF16)
                sums = _dot(jnp.concatenate([hi, lo], axis=1), u_ref[...])
                parts.append((h, k0, t - sp, sums, valid))
        runs = list(runs)
        for h, k0, logsig, sums, valid in parts:
            w = jnp.exp2(logsig + runs[h] + sums[:, :tk])
            if masked:
                w = jnp.where(valid, w, 0.0)
            vb = v_ref[pl.ds(k0, tk), :]
            acc = acc + _dot(w.astype(BF16), jnp.where(vlane == h, vb, jnp.zeros_like(vb)))
            runs[h] = runs[h] + sums[:, tk:]
        return tuple(runs), acc

    runs = (jnp.zeros((tq, tk), F32),) * nh
    acc = jnp.zeros((tq, width), F32)
    runs, acc = sweep(pl.multiple_of(qi * tq, tq), runs, acc, True)

    def live(runs):
        return jnp.max(functools.reduce(jnp.maximum, runs)) > SB_CUT

    def cond(c):
        return c[0]

    def body(c):
        _, j, runs, acc = c
        runs, acc = sweep(pl.multiple_of(j * tq, tq), runs, acc, False)
        return (j > 0) & live(runs), j - 1, runs, acc

    _, _, _, acc = lax.while_loop(cond, body, ((qi > 0) & live(runs), qi - 1, runs, acc))
    o_ref[...] = acc


def sb_attention(qkv, *, batch, heads, tq=256, nh=4):
    t, w3 = qkv.shape
    s = t // batch
    nq = s // tq
    width = nh * HEAD_DIM
    groups = heads // nh
    tk = LANE
    jj = jnp.arange(2 * tk)[:, None] % tk
    ss = jnp.arange(2 * tk)[None, :]
    u = jnp.where(ss < tk, jj > ss, True).astype(BF16)
    return pl.pallas_call(
        functools.partial(_sb_attn_kernel, tq=tq, tk=tk, nh=nh, scale=1.0 / math.sqrt(HEAD_DIM)),
        grid=(batch, groups, nq),
        in_specs=[pl.BlockSpec((tq, width), lambda b, p, i: (b * nq + i, p)),
                  pl.BlockSpec((s, width), lambda b, p, i: (b, groups + p)),
                  pl.BlockSpec((s, width), lambda b, p, i: (b, 2 * groups + p)),
                  pl.BlockSpec((2 * tk, 2 * tk), lambda b, p, i: (0, 0))],
        out_specs=pl.BlockSpec((tq, width), lambda b, p, i: (b * nq + i, p)),
        out_shape=jax.ShapeDtypeStruct((t, w3 // 3), F32),
        compiler_params=_cp("parallel", "parallel", "arbitrary"),
        name="sb_attention",
    )(qkv, qkv, qkv, u)


def _softmax_attn_kernel(qn_ref, qr_ref, kn_ref, kr_ref, v_ref, o_ref, sa0, sa1, sb0, sb1,
                         *, tq, tk, scale):
    hp = pl.program_id(1)
    qi = pl.program_id(2)
    lane = _iota((tq, LANE), 1)
    qn = qn_ref[...] * (scale * math.log2(math.e))
    qr = qr_ref[...] * (scale * math.log2(math.e))
    qs = []
    for h in range(2):
        nope = jnp.where(lane // MLA_NOPE == h, qn, 0.0)
        rope_ = jnp.where(lane // MLA_ROPE == (hp % 2) * 2 + h, qr, 0.0)
        qs.append(jnp.concatenate([nope, rope_], axis=1).astype(BF16))

    def scores(j, dsts):
        k0 = pl.multiple_of(j * tk, tk)
        kb = jnp.concatenate([kn_ref[pl.ds(k0, tk), :], kr_ref[pl.ds(k0, tk), :]], axis=1)
        mx = []
        for h in range(2):
            s = _dot_nt(qs[h], kb)
            dsts[h][...] = s
            mx.append(jnp.max(s, axis=1, keepdims=True))
        return tuple(mx)

    def consume(srcs, j, carry, mx):
        vb = v_ref[pl.ds(pl.multiple_of(j * tk, tk), tk), :]
        one = jnp.ones_like(vb)
        first = _iota(vb.shape, 1) < MLA_V
        vbs = (jnp.where(first, vb, one), jnp.where(first, one, vb))
        out = []
        for h in range(2):
            m, acc = carry[h]
            s = srcs[h][...]
            if mx is None:
                s = jnp.where(_iota((tq, tk), 1) <= _iota((tq, tk), 0), s, -1e30)
                m_new = jnp.maximum(m, jnp.max(s, axis=1, keepdims=True))
            else:
                m_new = jnp.maximum(m, mx[h])
            p = jnp.exp2(s - m_new)
            acc = jnp.exp2(m - m_new) * acc + _dot(p.astype(BF16), vbs[h])
            out.append((m_new, acc))
        return tuple(out)

    sa, sb = (sa0, sa1), (sb0, sb1)

    def pair(t, carry):
        state, mxa = carry
        mxb = scores(2 * t + 1, sb)
        state = consume(sa, 2 * t, state, mxa)
        mxa = scores(2 * t + 2, sa)
        return consume(sb, 2 * t + 1, state, mxb), mxa

    def tail_odd(carry):
        state, mxa = carry
        scores(qi, sb)
        state = consume(sa, qi - 1, state, mxa)
        return consume(sb, qi, state, None)

    def tail_even(carry):
        return consume(sa, qi, carry[0], None)

    init = ((jnp.full((tq, 1), -1e30, F32), jnp.zeros((tq, LANE), F32)),) * 2
    carry = lax.fori_loop(0, qi // 2, pair, (init, scores(0, sa)))
    (_, acc0), (_, acc1) = lax.cond(qi % 2 == 1, tail_odd, tail_even, carry)
    o_ref[...] = jnp.where(lane < MLA_V, acc0 / pltpu.roll(acc0, MLA_V, axis=1),
                           acc1 / pltpu.roll(acc1, MLA_V, axis=1))


def softmax_attention(q, qrot, kv, krot, scale, *, batch, heads, tq=512):
    tk = tq
    t = q.shape[0]
    s = t // batch
    nq = s // tq
    pairs = heads * MLA_NOPE // LANE
    buf = pltpu.VMEM((tq, tk), F32)
    return pl.pallas_call(
        functools.partial(_softmax_attn_kernel, tq=tq, tk=tk, scale=scale),
        grid=(batch, pairs, nq),
        in_specs=[pl.BlockSpec((tq, LANE), lambda b, p, i: (b * nq + i, p)),
                  pl.BlockSpec((tq, LANE), lambda b, p, i: (b * nq + i, p // 2)),
                  pl.BlockSpec((s, LANE), lambda b, p, i: (b, p)),
                  pl.BlockSpec((s, LANE), lambda b, p, i: (b, 0)),
                  pl.BlockSpec((s, LANE), lambda b, p, i: (b, pairs + p))],
        out_specs=pl.BlockSpec((tq, LANE), lambda b, p, i: (b * nq + i, p)),
        out_shape=jax.ShapeDtypeStruct((t, heads * MLA_V), F32),
        scratch_shapes=[buf, buf, buf, buf],
        compiler_params=_cp("parallel", "parallel", "arbitrary"),
        name="softmax_attention",
    )(q, qrot, kv, krot, kv)


def _rope_kernel(q_ref, k_ref, pos_ref, f_ref, oq_ref, ok_ref):
    half = MLA_ROPE // 2
    ang = pos_ref[...].astype(F32) * f_ref[...]
    cos, sin = jnp.cos(ang), jnp.sin(ang)
    low = _iota(ang.shape, 1) % MLA_ROPE < half

    def rot(x):
        partner = jnp.where(low, -pltpu.roll(x, LANE - half, axis=1), pltpu.roll(x, half, axis=1))
        return x * cos + partner * sin

    for c in range(q_ref.shape[1] // LANE):
        cols = slice(c * LANE, (c + 1) * LANE)
        oq_ref[:, cols] = rot(q_ref[:, cols])
    ok_ref[...] = rot(k_ref[...]).astype(ok_ref.dtype)


def rope(q, c, positions, *, q_col, qw, k_col, tm=512):
    t = q.shape[0]
    half = MLA_ROPE // 2
    inv_freq = ROPE_THETA ** (-jnp.arange(half, dtype=F32) / half)
    freq = jnp.tile(inv_freq, LANE // half).reshape(1, LANE)
    return pl.pallas_call(
        _rope_kernel,
        grid=(t // tm,),
        in_specs=[pl.BlockSpec((tm, qw), lambda i: (i, q_col)), pl.BlockSpec((tm, LANE), lambda i: (i, k_col)),
                  pl.BlockSpec((tm, 1), lambda i: (i, 0)), pl.BlockSpec((1, LANE), lambda i: (0, 0))],
        out_specs=[pl.BlockSpec((tm, qw), lambda i: (i, 0)), pl.BlockSpec((tm, LANE), lambda i: (i, 0))],
        out_shape=[jax.ShapeDtypeStruct((t, qw), F32), jax.ShapeDtypeStruct((t, LANE), BF16)],
        compiler_params=_cp("parallel"),
        name="rope",
    )(q, c, positions.reshape(t, 1), freq)


def _rwkv_prep_kernel(x_ref, xp_ref, mu_ref, w0_ref, a0_ref, kk_ref, ka_ref, rk_ref,
                      w2_ref, a2_ref, g2_ref, tri_ref, hs_ref,
                      at_ref, bt_ref, kt_ref, rt_ref, v_ref, eg_ref, g_ref, bv_ref, *, tm, seq, width):
    i = pl.program_id(0)
    x = x_ref[...]
    at_start = (i * tm) % seq == 0
    prev_row = jnp.where(at_start, 0.0, xp_ref[7:8, :])
    rolled = pltpu.roll(x, 1, axis=0)
    top = _iota((8, x.shape[1]), 0)
    prev = jnp.concatenate([jnp.where(top == 0, prev_row, rolled[:8]), rolled[8:]], axis=0)
    xm = x + (prev - x) * mu_ref[...]
    r = xm[:, :width]
    k = xm[:, width:2 * width]
    v = xm[:, 2 * width:3 * width]
    l1 = xm[:, 3 * width:3 * width + LANE]
    l2 = xm[:, 3 * width + LANE:]
    wpre = w0_ref[...] + _dot(_Split(jnp.tanh(l1)).left(1), w2_ref[...])
    a = _sigmoid(a0_ref[...] + _dot(_Split(l1).left(1), a2_ref[...]))
    g = _dot(_Split(_sigmoid(l2)).left(1), g2_ref[...])

    def hilo(t):
        ts = _Split(t)
        return jnp.concatenate([ts.hi, ts.lo], axis=1)

    lw = -jnp.exp(-_softplus(-wpre) - 0.5)
    hs = hs_ref[...]
    kk = k * kk_ref[...]
    kk = kk * lax.rsqrt(jnp.maximum(_dot(hilo(kk * kk), hs), 1e-24))
    km = k * (1.0 + (a - 1.0) * ka_ref[...])
    bonus = _dot(hilo(r * km * rk_ref[...]), hs)
    lws = _Split(lw)
    cum = _dot(tri_ref[...], jnp.concatenate([lws.hi, lws.lo], axis=0))
    eg = jnp.exp(cum)
    ieg = jnp.exp(-cum)
    at_ref[...] = -kk * jnp.exp(cum - lw)
    bt_ref[...] = kk * a * ieg
    kt_ref[...] = km * ieg
    rt_ref[...] = r * eg
    v_ref[...] = v
    eg_ref[...] = eg
    g_ref[...] = g
    bv_ref[...] = bonus * v


def rwkv_prep(x, mu, w0, a0, k_k, k_a, r_k, w2p, a2p, g2p, *, seq, width, tm=256):
    t, wx = x.shape
    ch = RW_CHUNK
    ti = jnp.arange(tm)
    tri = ((ti[:, None] // ch == ti[None, :] // ch) & (ti[None, :] <= ti[:, None])).astype(BF16)
    tri = jnp.concatenate([tri, tri], axis=1)
    ci = jnp.arange(width) // HEAD_DIM
    hs = (ci[:, None] == ci[None, :]).astype(BF16)
    hs = jnp.concatenate([hs, hs], axis=0)
    w2p, a2p, g2p = (_Split(w).right(0) for w in (w2p, a2p, g2p))
    hb = tm // 8
    row = lambda i: (i, 0)
    fix = lambda i: (0, 0)
    vec = pl.BlockSpec((1, width), fix)
    out = jax.ShapeDtypeStruct((t, width), F32)
    return pl.pallas_call(
        functools.partial(_rwkv_prep_kernel, tm=tm, seq=seq, width=width),
        grid=(t // tm,),
        in_specs=[pl.BlockSpec((tm, wx), row),
                  pl.BlockSpec((8, wx), lambda i: (jnp.maximum(i * hb - 1, 0), 0)),
                  pl.BlockSpec((1, wx), fix), vec, vec, vec, vec, vec,
                  pl.BlockSpec(w2p.shape, fix), pl.BlockSpec(a2p.shape, fix),
                  pl.BlockSpec(g2p.shape, fix), pl.BlockSpec(tri.shape, fix),
                  pl.BlockSpec(hs.shape, fix)],
        out_specs=[pl.BlockSpec((tm, width), row)] * 8,
        out_shape=[out] * 8,
        compiler_params=_cp("parallel"),
        name="rwkv_prep",
    )(x, x, mu, w0, a0, k_k, k_a, r_k, w2p, a2p, g2p, tri, hs)


def _rwkv_chunk_kernel(at_ref, bt_ref, kt_ref, rt_ref, v_ref, eg_ref,
                       m_ref, n_ref, rq_ref, yl_ref, *, nchunk, unroll):
    ch, n = RW_CHUNK, HEAD_DIM
    ti = _iota((ch, ch), 0)
    si = _iota((ch, ch), 1)
    strict = si < ti
    incl = si <= ti
    same = (ti // RW_SUB) == (si // RW_SUB)
    eye = _iota((n, n), 0) == _iota((n, n), 1)
    each = lambda f, *ls: [f(*t) for t in zip(*ls)]

    def body(it, _):
        cs = [it * unroll + u for u in range(unroll)]
        sls = [pl.ds(pl.multiple_of(c * ch, ch), ch) for c in cs]
        ids = [(h, u) for h in range(2) for u in range(unroll)]
        ld = lambda ref: [ref[sls[u], h * n:(h + 1) * n] for h, u in ids]
        sp = lambda xs: [_Split(t) for t in xs]
        a_, b_, k_, r_, v_ = ld(at_ref), ld(bt_ref), ld(kt_ref), ld(rt_ref), ld(v_ref)
        g_end = [eg_ref[pl.ds(cs[u] * ch + ch - 1, 1), h * n:(h + 1) * n] for h, u in ids]
        ar = sp(each(lambda a, r: jnp.concatenate([a, r], axis=0), a_, r_))
        vs = sp(v_)
        pb = each(_mm3_nt, ar, sp(b_))
        pk = each(_mm3_nt, ar, sp(k_))
        a_ab = [jnp.where(strict, t[:ch], 0.0) for t in pb]
        a_ak = sp([jnp.where(strict, t[:ch], 0.0) for t in pk])
        a_rb = sp([jnp.where(incl, t[ch:], 0.0) for t in pb])
        a_rk = sp([jnp.where(incl, t[ch:], 0.0) for t in pk])
        d1 = [jnp.where(same, t, 0.0) for t in a_ab]
        lb = each(lambda t, d: t - d, a_ab, d1)
        d1 = sp(d1)
        d2 = sp(each(_mm3, d1, d1))
        akv = each(_mm3, a_ak, vs)
        x = each(lambda l, a, t: jnp.concatenate([l, a, t], axis=1), lb, a_, akv)
        d4 = sp(each(_mm3, d2, d2))
        x = each(lambda d, t: t + _mm3(d, _Split(t)), d1, x)
        d8 = sp(each(_mm3, d4, d4))
        x = each(lambda d, t: t + _mm3(d, _Split(t)), d2, x)
        x = each(lambda d, t: t + _mm3(d, _Split(t)), d4, x)
        x = each(lambda d, t: t + _mm3(d, _Split(t)), d8, x)
        e1 = sp([t[:, :ch] for t in x])
        wu = [t[:, ch:] for t in x]
        e2 = sp(each(_mm3, e1, e1))
        wu = each(lambda e, t: t + _mm3(e, _Split(t)), e1, wu)
        wu = sp(each(lambda e, t: t + _mm3(e, _Split(t)), e2, wu))
        ry = each(_mm3, a_rb, wu)
        rkv = each(_mm3, a_rk, vs)
        mn = each(lambda b, g, t: _mm3_tn(_Split(b * g), t), b_, g_end, wu)
        kv = each(lambda k, g, t: _mm3_tn(_Split(k * g), t), k_, g_end, vs)
        rq = each(lambda r, t: r + t[:, :n], r_, ry)
        yl = each(lambda t, w: t[:, n:] + w, ry, rkv)
        for u in range(unroll):
            rq_ref[sls[u], :] = jnp.concatenate([rq[u], rq[unroll + u]], axis=1)
            yl_ref[sls[u], :] = jnp.concatenate([yl[u], yl[unroll + u]], axis=1)
        for i, (h, u) in enumerate(ids):
            m_ref[h, cs[u]] = mn[i][:, :n] + jnp.where(eye, g_end[i], 0.0)
            n_ref[h, cs[u]] = mn[i][:, n:] + kv[i]
        return 0

    lax.fori_loop(0, nchunk // unroll, body, 0)


def rwkv_chunk(at, bt, kt, rt, v, eg, *, batch, nchunk=8, unroll=8):
    t, w = at.shape
    s = t // batch
    ch, n = RW_CHUNK, HEAD_DIM
    pairs = w // LANE
    rows = nchunk * ch
    nr = s // rows
    spec = pl.BlockSpec((rows, LANE), lambda b, p, i: (b * nr + i, p))
    mspec = pl.BlockSpec((2, nchunk, n, n), lambda b, p, i: (b * pairs + p, i, 0, 0))
    return pl.pallas_call(
        functools.partial(_rwkv_chunk_kernel, nchunk=nchunk, unroll=unroll),
        grid=(batch, pairs, nr),
        in_specs=[spec] * 6,
        out_specs=[mspec, mspec, spec, spec],
        out_shape=[jax.ShapeDtypeStruct((batch * w // n, s // ch, n, n), F32)] * 2
        + [jax.ShapeDtypeStruct((t, w), F32)] * 2,
        compiler_params=_cp("parallel", "parallel", "parallel"),
        name="rwkv_chunk",
    )(at, bt, kt, rt, v, eg)


def _rwkv_scan_kernel(m_ref, n_ref, s_ref, st_ref, *, heads, nchunk):
    @pl.when(pl.program_id(1) == 0)
    def _():
        st_ref[...] = jnp.zeros_like(st_ref)

    def body(c, _):
        sts = [st_ref[hh] for hh in range(heads)]
        for hh in range(heads):
            s_ref[hh, c] = sts[hh]
        ms = [_Split(m_ref[hh, c]) for hh in range(heads)]
        ss = [_Split(st) for st in sts]
        new = [_mm3(a, b) for a, b in zip(ms, ss)]
        for hh in range(heads):
            st_ref[hh] = new[hh] + n_ref[hh, c]
        return 0

    lax.fori_loop(0, nchunk, body, 0)


def rwkv_scan(m, nn, *, heads=16, nchunk=8):
    bh, nc, n, _ = m.shape
    heads = min(heads, bh)
    nchunk = min(nchunk, nc)
    spec = pl.BlockSpec((heads, nchunk, n, n), lambda b, i: (b, i, 0, 0))
    return pl.pallas_call(
        functools.partial(_rwkv_scan_kernel, heads=heads, nchunk=nchunk),
        grid=(bh // heads, nc // nchunk),
        in_specs=[spec, spec],
        out_specs=spec,
        out_shape=jax.ShapeDtypeStruct((bh, nc, n, n), F32),
        scratch_shapes=[pltpu.VMEM((heads, n, n), F32)],
        compiler_params=_cp("parallel", "arbitrary"),
        name="rwkv_scan",
    )(m, nn)


def _rwkv_out_kernel(rq_ref, yl_ref, s_ref, lw_ref, lb_ref, avg_ref, o_ref, *, nchunk):
    ch, n = RW_CHUNK, HEAD_DIM
    ys = []
    for h in range(2):
        cols = slice(h * n, (h + 1) * n)
        ys.append(jnp.concatenate(
            [_mm3(_Split(rq_ref[c * ch:(c + 1) * ch, cols]), _Split(s_ref[h, c])) for c in range(nchunk)],
            axis=0))
    y = jnp.concatenate(ys, axis=1) + yl_ref[...]
    avg2 = jnp.concatenate([avg_ref[...]] * 2, axis=0)

    def mean(t):
        ts = _Split(t)
        return _dot(jnp.concatenate([ts.hi, ts.lo], axis=1), avg2)

    d = y - mean(y)
    var = mean(d * d)
    o_ref[...] = d * lax.rsqrt(var + GN_EPS) * lw_ref[...] + lb_ref[...]


def rwkv_out(rq, yl, s0, ln_w, ln_b, *, batch, nchunk=8):
    t, w = rq.shape
    s = t // batch
    n = HEAD_DIM
    pairs = w // LANE
    rows = nchunk * RW_CHUNK
    nr = s // rows
    li = jnp.arange(LANE) // n
    avg = ((li[:, None] == li[None, :]).astype(F32) / n).astype(BF16)
    spec = pl.BlockSpec((rows, LANE), lambda b, p, i: (b * nr + i, p))
    hspec = pl.BlockSpec((1, LANE), lambda b, p, i: (0, p))
    return pl.pallas_call(
        functools.partial(_rwkv_out_kernel, nchunk=nchunk),
        grid=(batch, pairs, nr),
        in_specs=[spec, spec, pl.BlockSpec((2, nchunk, n, n), lambda b, p, i: (b * pairs + p, i, 0, 0)),
                  hspec, hspec, pl.BlockSpec((LANE, LANE), lambda b, p, i: (0, 0))],
        out_specs=spec,
        out_shape=jax.ShapeDtypeStruct((t, w), F32),
        compiler_params=_cp("parallel", "parallel", "parallel"),
        name="rwkv_out",
    )(rq, yl, s0, ln_w.reshape(1, w), ln_b.reshape(1, w), avg)


def _hyb_out_kernel(oa_ref, y_ref, bv_ref, g_ref, wa_ref, wb_ref, r_ref, o_ref):
    ob = (y_ref[...] + bv_ref[...]) * g_ref[...]
    o_ref[...] = (r_ref[...] + _dot(oa_ref[...].astype(BF16), wa_ref[...])
                  + _dot(ob.astype(BF16), wb_ref[...]))


def hyb_out(res, oa, y, bv, g, wa, wb, *, tm=1024):
    t, d = res.shape
    row = lambda i: (i, 0)
    fix = lambda i: (0, 0)
    act = lambda a: pl.BlockSpec((tm, a.shape[1]), row)
    return pl.pallas_call(
        _hyb_out_kernel,
        grid=(t // tm,),
        in_specs=[act(oa), act(y), act(bv), act(g), pl.BlockSpec(wa.shape, fix),
                  pl.BlockSpec(wb.shape, fix), pl.BlockSpec((tm, d), row)],
        out_specs=pl.BlockSpec((tm, d), row),
        out_shape=jax.ShapeDtypeStruct((t, d), F32),
        compiler_params=_cp("parallel"),
        name="hyb_out",
    )(oa, y, bv, g, wa, wb, res)


def _pad_cols(w, n):
    return jnp.pad(w, ((0, 0), (0, n - w.shape[1])))


def sb_rwkv_layer(h, b, s, attn_norm, w_in, w_out, mu, w0, w2, a0, a2, g2, k_k, k_a, r_k, ln_w, ln_b):
    sbw = SB_HEADS * HEAD_DIM
    rww = RW_HEADS * HEAD_DIM
    rw_in = w_in.shape[1] - 3 * sbw
    rw_pad = -(-rw_in // LANE) * LANE
    dl, al = w2.shape[0], a2.shape[0]
    gl = g2.shape[0]
    assert dl + al == LANE and 3 * rww % LANE == 0

    qkv = rms_matmul(h, attn_norm, w_in[:, :3 * sbw].astype(BF16), out_dtype=BF16)
    rw = rms_matmul(h, attn_norm, _pad_cols(w_in[:, 3 * sbw:], rw_pad).astype(BF16))
    oa = sb_attention(qkv, batch=b, heads=SB_HEADS)

    w2p = jnp.zeros((LANE, rww), F32).at[:dl].set(w2)
    a2p = jnp.zeros((LANE, rww), F32).at[dl:].set(a2)
    g2p = jnp.zeros((rw_pad - 3 * rww - LANE, rww), F32).at[:gl].set(g2)
    vec = lambda t: t.reshape(1, rww).astype(F32)
    mu_p = _pad_cols(mu.reshape(1, rw_in), rw_pad)
    at, bt, kt, rt, vv, eg, g, bv = rwkv_prep(rw, mu_p, vec(w0), vec(a0), vec(k_k), vec(k_a), vec(r_k),
                                              w2p, a2p, g2p, seq=s, width=rww)
    m, nn, rq, yl = rwkv_chunk(at, bt, kt, rt, vv, eg, batch=b)
    s0 = rwkv_scan(m, nn)
    y = rwkv_out(rq, yl, s0, ln_w, ln_b, batch=b)
    return hyb_out(h, oa, y, bv, g, w_out[:sbw].astype(BF16), w_out[sbw:].astype(BF16))


def mla_layer(h, b, s, positions, attn_norm, w_down, q_norm, kv_norm, w_uq, w_ukv, w_o):
    nh, dn, dr, dv = MLA_HEADS, MLA_NOPE, MLA_ROPE, MLA_V
    qr, kvr = MLA_Q_RANK, MLA_KV_RANK
    assert dn == dv and qr % LANE == 0 and (qr + LANE) % kvr == 0
    wd = jnp.concatenate([w_down[:, :qr], jnp.tile(w_down[:, qr + kvr:], (1, LANE // dr)),
                          w_down[:, qr:qr + kvr]], axis=1)
    c = rms_matmul(h, attn_norm, wd.astype(BF16))
    wq = w_uq.reshape(qr, nh, dn + dr)
    wq = jnp.concatenate([wq[:, :, :dn].reshape(qr, nh * dn), wq[:, :, dn:].reshape(qr, nh * dr)], axis=1)
    q = rms_matmul(c, q_norm, wq.astype(BF16), x_col=0)
    wkv = w_ukv.reshape(kvr, nh, dn + dv)
    wkv = jnp.concatenate([wkv[:, :, :dn].reshape(kvr, nh * dn), wkv[:, :, dn:].reshape(kvr, nh * dv)], axis=1)
    kv = rms_matmul(c, kv_norm, wkv.astype(BF16), x_col=(qr + LANE) // kvr, out_dtype=BF16)
    qrot, krot = rope(q, c, positions, q_col=dn // dr, qw=nh * dr, k_col=qr // LANE)
    o = softmax_attention(q, qrot, kv, krot, 1.0 / math.sqrt(dn + dr), batch=b, heads=nh)
    return matmul_residual(h, o, w_o.astype(BF16))


def kernel(x, p, positions, attn_norm, ffn_norm, ffn_w_in, ffn_conv_w, ffn_conv_b, ffn_w_out, ple_w_proj, ple_norm, ple_gate_norm, ple_w_gate, hyb_w_in, hyb_w_out, rw_mu, rw_w0, rw_w2, rw_a0, rw_a2, rw_g2, rw_k_k, rw_k_a, rw_r_k, rw_ln_w, rw_ln_b, mla_w_down, mla_q_norm, mla_kv_norm, mla_w_uq, mla_w_ukv, mla_w_o, final_norm):
    b, s, d = x.shape
    depth = p.shape[0]
    h = x.reshape(b * s, d)
    pos = positions.reshape(b * s)
    for i in range(depth):
        j = i // 2
        if i % 2 == 0:
            h = sb_rwkv_layer(h, b, s, attn_norm[i], hyb_w_in[j], hyb_w_out[j], rw_mu[j], rw_w0[j],
                              rw_w2[j], rw_a0[j], rw_a2[j], rw_g2[j], rw_k_k[j], rw_k_a[j], rw_r_k[j],
                              rw_ln_w[j], rw_ln_b[j])
        else:
            h = mla_layer(h, b, s, pos, attn_norm[i], mla_w_down[j], mla_q_norm[j], mla_kv_norm[j],
                          mla_w_uq[j], mla_w_ukv[j], mla_w_o[j])
        ug = rms_matmul(h, ffn_norm[i], ffn_w_in[i].astype(BF16), out_dtype=BF16)
        h = convglu_out(h, ug, ffn_conv_w[i], ffn_conv_b[i], ffn_w_out[i].astype(BF16), seq=s)
        h = ple(h, p[i].reshape(b * s, -1), ple_w_proj[i].astype(BF16), ple_norm[i], ple_gate_norm[i],
                ple_w_gate[i].astype(BF16), final_norm, final=(i == depth - 1))
    return h.reshape(b, s, d)
```

```python
import functools
import math

import jax
import jax.numpy as jnp
from jax import lax
from jax.experimental import pallas as pl
from jax.experimental.pallas import tpu as pltpu

F32 = jnp.float32
BF16 = jnp.bfloat16
HI = lax.Precision.HIGHEST

NORM_EPS = 1e-6
GN_EPS = 64e-5
ROPE_THETA = 10000.0
LANE = 128
VMEM_LIMIT = 48 * 1024 * 1024
HALO = 16
MXU_WIDE_TILE = 512
TILE_VMEM_BUDGET = 36 * 1024 * 1024
RESIDENT_WEIGHT_BYTES = 4 * 1024 * 1024

SB_HEADS = 8
RW_HEADS = 8
HEAD_DIM = 64
MLA_HEADS = 16
MLA_NOPE = 64
MLA_ROPE = 32
MLA_V = 64
MLA_Q_RANK = 384
MLA_KV_RANK = 256
SB_CUT = -151.0
RW_CHUNK = 64
RW_SUB = 16


def _cp(*sem):
    return pltpu.CompilerParams(dimension_semantics=sem, vmem_limit_bytes=VMEM_LIMIT)


def _iota(shape, dim):
    return lax.broadcasted_iota(jnp.int32, shape, dim)


def _softplus(z):
    return jnp.maximum(z, 0.0) + jnp.log1p(jnp.exp(-jnp.abs(z)))


def _sigmoid(z):
    return 1.0 / (1.0 + jnp.exp(-z))


def _dot(a, b, precision=None):
    return jnp.dot(a, b, preferred_element_type=F32, precision=precision)


def _dot_nt(a, b, precision=None):
    return lax.dot_general(a, b, (((1,), (1,)), ((), ())), preferred_element_type=F32,
                           precision=precision)


def _dot_tn(a, b, precision=None):
    return lax.dot_general(a, b, (((0,), (0,)), ((), ())), preferred_element_type=F32,
                           precision=precision)


class _Split:
    def __init__(self, x):
        self.x = x
        self.hi = x.astype(BF16)
        self.lo = (x - self.hi.astype(F32)).astype(BF16)
        self._packed = {}

    def left(self, axis):
        key = ("l", axis)
        if key not in self._packed:
            self._packed[key] = jnp.concatenate([self.hi, self.lo, self.hi], axis=axis)
        return self._packed[key]

    def right(self, axis):
        key = ("r", axis)
        if key not in self._packed:
            self._packed[key] = jnp.concatenate([self.hi, self.hi, self.lo], axis=axis)
        return self._packed[key]


def _mm3(a, b):
    return _dot(a.left(1), b.right(0))


def _mm3_nt(a, b):
    return _dot_nt(a.left(1), b.right(1))


def _mm3_tn(a, b):
    return _dot_tn(a.left(0), b.right(0))


def _rms(x, g, eps=NORM_EPS):
    return x * lax.rsqrt(jnp.mean(x * x, axis=-1, keepdims=True) + eps) * g


def _rms_matmul_kernel(x_ref, g_ref, w_ref, o_ref, xn_ref):
    @pl.when(pl.program_id(1) == 0)
    def _():
        xn_ref[...] = _rms(x_ref[...].astype(F32), g_ref[...]).astype(BF16)

    o_ref[...] = _dot(xn_ref[...], w_ref[...]).astype(o_ref.dtype)


def _pick_tiles(t, k, n, tile_bytes):
    tn = n if k * n * 2 <= RESIDENT_WEIGHT_BYTES else MXU_WIDE_TILE
    for tm in (2048, 1024, 512, 256):
        if t % tm == 0 and tile_bytes(tm, tn) + 2 * k * tn * 2 <= TILE_VMEM_BUDGET:
            return tm, tn
    raise ValueError("no row tile fits")


def rms_matmul(x, g, w, *, x_col=0, out_dtype=F32):
    t = x.shape[0]
    k, n = w.shape
    ob = jnp.dtype(out_dtype).itemsize
    tm, tn = _pick_tiles(t, k, n, lambda tm, tn: 2 * tm * k * x.dtype.itemsize + tm * k * 2 + 2 * tm * tn * ob)
    return pl.pallas_call(
        _rms_matmul_kernel,
        grid=(t // tm, n // tn),
        in_specs=[pl.BlockSpec((tm, k), lambda i, j: (i, x_col)),
                  pl.BlockSpec((1, k), lambda i, j: (0, 0)),
                  pl.BlockSpec((k, tn), lambda i, j: (0, j))],
        out_specs=pl.BlockSpec((tm, tn), lambda i, j: (i, j)),
        out_shape=jax.ShapeDtypeStruct((t, n), out_dtype),
        scratch_shapes=[pltpu.VMEM((tm, k), BF16)],
        compiler_params=_cp("parallel", "arbitrary"),
        name="rms_matmul",
    )(x, g.reshape(1, k).astype(F32), w)


def _matmul_res_kernel(a_ref, w_ref, r_ref, o_ref):
    o_ref[...] = r_ref[...] + _dot(a_ref[...].astype(BF16), w_ref[...])


def matmul_residual(res, a, w):
    t, k = a.shape
    n = w.shape[1]
    tm, tn = _pick_tiles(t, k, n, lambda tm, tn: 2 * tm * k * a.dtype.itemsize + 4 * tm * tn * 4)
    return pl.pallas_call(
        _matmul_res_kernel,
        grid=(t // tm, n // tn),
        in_specs=[pl.BlockSpec((tm, k), lambda i, j: (i, 0)),
                  pl.BlockSpec((k, tn), lambda i, j: (0, j)),
                  pl.BlockSpec((tm, tn), lambda i, j: (i, j))],
        out_specs=pl.BlockSpec((tm, tn), lambda i, j: (i, j)),
        out_shape=jax.ShapeDtypeStruct((t, n), F32),
        compiler_params=_cp("parallel", "arbitrary"),
        name="matmul_residual",
    )(a, w, res)


def _convglu_out_kernel(u_ref, g_ref, gp_ref, cw_ref, cb_ref, w_ref, r_ref, o_ref, *, tm, seq):
    i = pl.program_id(0)
    g = g_ref[...].astype(F32)
    at_start = (i * tm) % seq == 0
    halo = jnp.where(at_start, 0.0, gp_ref[...].astype(F32))
    h1, h2 = halo[HALO - 1:HALO, :], halo[HALO - 2:HALO - 1, :]
    r1, r2 = pltpu.roll(g, 1, axis=0), pltpu.roll(g, 2, axis=0)
    top = _iota((8, g.shape[1]), 0)
    p1 = jnp.concatenate([jnp.where(top == 0, h1, r1[:8]), r1[8:]], axis=0)
    p2 = jnp.concatenate([jnp.where(top == 0, h2, jnp.where(top == 1, h1, r2[:8])), r2[8:]], axis=0)
    c = cw_ref[0:1, :] * p2 + cw_ref[1:2, :] * p1 + cw_ref[2:3, :] * g + cb_ref[...]
    act = c * (1.0 + lax.erf(c)) * u_ref[...].astype(F32)
    o_ref[...] = r_ref[...] + _dot(act.astype(BF16), w_ref[...])


def convglu_out(res, ug, conv_w, conv_b, w_out, *, seq, tm=256):
    t = res.shape[0]
    f, n = w_out.shape
    hb = tm // HALO
    rs2 = 1.0 / math.sqrt(2.0)
    conv_w, conv_b, w_out = conv_w * rs2, conv_b * rs2, (w_out * rs2).astype(BF16)
    return pl.pallas_call(
        functools.partial(_convglu_out_kernel, tm=tm, seq=seq),
        grid=(t // tm,),
        in_specs=[pl.BlockSpec((tm, f), lambda i: (i, 0)),
                  pl.BlockSpec((tm, f), lambda i: (i, 1)),
                  pl.BlockSpec((HALO, f), lambda i: (jnp.maximum(i * hb - 1, 0), 1)),
                  pl.BlockSpec((3, f), lambda i: (0, 0)),
                  pl.BlockSpec((1, f), lambda i: (0, 0)),
                  pl.BlockSpec((f, n), lambda i: (0, 0)),
                  pl.BlockSpec((tm, n), lambda i: (i, 0))],
        out_specs=pl.BlockSpec((tm, n), lambda i: (i, 0)),
        out_shape=jax.ShapeDtypeStruct((t, n), F32),
        compiler_params=_cp("parallel"),
        name="convglu_out",
    )(ug, ug, ug, conv_w, conv_b.reshape(1, f), w_out, res)


def _ple_kernel(h_ref, p_ref, wp_ref, pn_ref, gn_ref, wg_ref, fn_ref, o_ref, *, final):
    h = h_ref[...]
    e = _rms(_dot(p_ref[...].astype(BF16), wp_ref[...]), pn_ref[...])
    gate = _sigmoid(_dot(_rms(h, gn_ref[...]).astype(BF16), wg_ref[...]))
    out = h + gate * e
    if final:
        out = _rms(out, fn_ref[...])
    o_ref[...] = out


def ple(h, p, w_proj, p_norm, g_norm, w_gate, f_norm, *, final, tm=1024):
    t, d = h.shape
    pd = p.shape[1]
    row = lambda i: (i, 0)
    fix = lambda i: (0, 0)
    return pl.pallas_call(
        functools.partial(_ple_kernel, final=final),
        grid=(t // tm,),
        in_specs=[pl.BlockSpec((tm, d), row), pl.BlockSpec((tm, pd), row),
                  pl.BlockSpec((pd, d), fix), pl.BlockSpec((1, d), fix), pl.BlockSpec((1, d), fix),
                  pl.BlockSpec((d, d), fix), pl.BlockSpec((1, d), fix)],
        out_specs=pl.BlockSpec((tm, d), row),
        out_shape=jax.ShapeDtypeStruct((t, d), F32),
        compiler_params=_cp("parallel"),
        name="ple",
    )(h, p, w_proj, p_norm.reshape(1, d), g_norm.reshape(1, d), w_gate, f_norm.reshape(1, d))


def _sb_attn_kernel(q_ref, k_ref, v_ref, u_ref, o_ref, *, tq, tk, nh, scale):
    qi = pl.program_id(2)
    nblk = tq // tk
    width = nh * HEAD_DIM
    qf = q_ref[...] * scale
    qlane = _iota((tq, width), 1) // HEAD_DIM
    qs = [jnp.where(qlane == h, qf, jnp.zeros_like(qf)) for h in range(nh)]
    vlane = _iota((tk, width), 1) // HEAD_DIM

    def sweep(k_lo, runs, acc, masked):
        parts = []
        for d in reversed(range(nblk)):
            k0 = k_lo + d * tk
            kb = k_ref[pl.ds(k0, tk), :]
            valid = None
            if masked:
                valid = _iota((tq, tk), 1) + d * tk < _iota((tq, tk), 0)
            for h in range(nh):
                t = _dot_nt(qs[h], kb) * math.log2(math.e)
                sp = jnp.maximum(t, 0.0) + jnp.log2(1.0 + jnp.exp2(-jnp.abs(t)))
                lom = -sp
                if masked:
                    lom = jnp.where(valid, lom, 0.0)
                hi = lom.astype(BF16)
                lo = (lom - hi.astype(F32)).astype(BF16)
                sums = _dot(jnp.concatenate([hi, lo], axis=1), u_ref[...])
                parts.append((h, k0, t - sp, sums, valid))
        runs = list(runs)
        for h, k0, logsig, sums, valid in parts:
            w = jnp.exp2(logsig + runs[h] + sums[:, :tk])
            if masked:
                w = jnp.where(valid, w, 0.0)
            vb = v_ref[pl.ds(k0, tk), :]
            acc = acc + _dot(w.astype(BF16), jnp.where(vlane == h, vb, jnp.zeros_like(vb)))
            runs[h] = runs[h] + sums[:, tk:]
        return tuple(runs), acc

    runs = (jnp.zeros((tq, tk), F32),) * nh
    acc = jnp.zeros((tq, width), F32)
    runs, acc = sweep(pl.multiple_of(qi * tq, tq), runs, acc, True)

    def live(runs):
        return jnp.max(functools.reduce(jnp.maximum, runs)) > SB_CUT

    def cond(c):
        return c[0]

    def body(c):
        _, j, runs, acc = c
        runs, acc = sweep(pl.multiple_of(j * tq, tq), runs, acc, False)
        return (j > 0) & live(runs), j - 1, runs, acc

    _, _, _, acc = lax.while_loop(cond, body, ((qi > 0) & live(runs), qi - 1, runs, acc))
    o_ref[...] = acc


def sb_attention(qkv, *, batch, heads, tq=256, nh=4):
    t, w3 = qkv.shape
    s = t // batch
    nq = s // tq
    width = nh * HEAD_DIM
    groups = heads // nh
    tk = LANE
    jj = jnp.arange(2 * tk)[:, None] % tk
    ss = jnp.arange(2 * tk)[None, :]
    u = jnp.where(ss < tk, jj > ss, True).astype(BF16)
    return pl.pallas_call(
        functools.partial(_sb_attn_kernel, tq=tq, tk=tk, nh=nh, scale=1.0 / math.sqrt(HEAD_DIM)),
        grid=(batch, groups, nq),
        in_specs=[pl.BlockSpec((tq, width), lambda b, p, i: (b * nq + i, p)),
                  pl.BlockSpec((s, width), lambda b, p, i: (b, groups + p)),
                  pl.BlockSpec((s, width), lambda b, p, i: (b, 2 * groups + p)),
                  pl.BlockSpec((2 * tk, 2 * tk), lambda b, p, i: (0, 0))],
        out_specs=pl.BlockSpec((tq, width), lambda b, p, i: (b * nq + i, p)),
        out_shape=jax.ShapeDtypeStruct((t, w3 // 3), F32),
        compiler_params=_cp("parallel", "parallel", "arbitrary"),
        name="sb_attention",
    )(qkv, qkv, qkv, u)


def _softmax_attn_kernel(qn_ref, qr_ref, kn_ref, kr_ref, v_ref, o_ref, *bufs, tq, tk, npair, scale):
    qi = pl.program_id(2)
    g = pl.program_id(1)
    nh = 2 * npair
    lane = _iota((tq, LANE), 1)
    qr = qr_ref[...] * (scale * math.log2(math.e))
    slots = LANE // MLA_ROPE
    qs = []
    for h in range(nh):
        cols = slice(h // 2 * LANE, (h // 2 + 1) * LANE)
        qn = qn_ref[:, cols] * (scale * math.log2(math.e))
        nope = jnp.where(lane // MLA_NOPE == h % 2, qn, 0.0)
        rope_ = jnp.where(lane // MLA_ROPE == (g * nh + h) % slots, qr, 0.0)
        qs.append(jnp.concatenate([nope, rope_], axis=1).astype(BF16))

    def scores(j, dsts):
        k0 = pl.multiple_of(j * tk, tk)
        mx = []
        for h in range(nh):
            cols = slice(h // 2 * LANE, (h // 2 + 1) * LANE)
            kb = jnp.concatenate([kn_ref[pl.ds(k0, tk), cols], kr_ref[pl.ds(k0, tk), :]], axis=1)
            s = _dot_nt(qs[h], kb)
            dsts[h][...] = s
            mx.append(jnp.max(s, axis=1, keepdims=True))
        return tuple(mx)

    def consume(srcs, j, carry, mx):
        out = []
        for h in range(nh):
            cols = slice(h // 2 * LANE, (h // 2 + 1) * LANE)
            vb = v_ref[pl.ds(pl.multiple_of(j * tk, tk), tk), cols]
            mine = (_iota(vb.shape, 1) // MLA_V) == h % 2
            vbh = jnp.where(mine, vb, jnp.ones_like(vb))
            m, acc = carry[h]
            s = srcs[h][...]
            if mx is None:
                s = jnp.where(_iota((tq, tk), 1) <= _iota((tq, tk), 0), s, -1e30)
                m_new = jnp.maximum(m, jnp.max(s, axis=1, keepdims=True))
            else:
                m_new = jnp.maximum(m, mx[h])
            p = jnp.exp2(s - m_new)
            acc = jnp.exp2(m - m_new) * acc + _dot(p.astype(BF16), vbh)
            out.append((m_new, acc))
        return tuple(out)

    sa, sb = bufs[:nh], bufs[nh:]

    def pair(t, carry):
        state, mxa = carry
        mxb = scores(2 * t + 1, sb)
        state = consume(sa, 2 * t, state, mxa)
        mxa = scores(2 * t + 2, sa)
        return consume(sb, 2 * t + 1, state, mxb), mxa

    def tail_odd(carry):
        state, mxa = carry
        scores(qi, sb)
        state = consume(sa, qi - 1, state, mxa)
        return consume(sb, qi, state, None)

    def tail_even(carry):
        return consume(sa, qi, carry[0], None)

    init = ((jnp.full((tq, 1), -1e30, F32), jnp.zeros((tq, LANE), F32)),) * nh
    carry = lax.fori_loop(0, qi // 2, pair, (init, scores(0, sa)))
    state = lax.cond(qi % 2 == 1, tail_odd, tail_even, carry)
    for pr in range(npair):
        acc0, acc1 = state[2 * pr][1], state[2 * pr + 1][1]
        o_ref[:, pr * LANE:(pr + 1) * LANE] = jnp.where(
            lane < MLA_V, acc0 / pltpu.roll(acc0, MLA_V, axis=1), acc1 / pltpu.roll(acc1, MLA_V, axis=1))


def softmax_attention(q, qrot, kv, krot, scale, *, batch, heads, tq=512, npair=2):
    tk = tq
    t = q.shape[0]
    s = t // batch
    nq = s // tq
    width = npair * LANE
    groups = heads * MLA_NOPE // width
    rope_per_group = LANE // (2 * npair * MLA_ROPE)
    buf = pltpu.VMEM((tq, tk), F32)
    return pl.pallas_call(
        functools.partial(_softmax_attn_kernel, tq=tq, tk=tk, npair=npair, scale=scale),
        grid=(batch, groups, nq),
        in_specs=[pl.BlockSpec((tq, width), lambda b, p, i: (b * nq + i, p)),
                  pl.BlockSpec((tq, LANE), lambda b, p, i: (b * nq + i, p // rope_per_group)),
                  pl.BlockSpec((s, width), lambda b, p, i: (b, p)),
                  pl.BlockSpec((s, LANE), lambda b, p, i: (b, 0)),
                  pl.BlockSpec((s, width), lambda b, p, i: (b, groups + p))],
        out_specs=pl.BlockSpec((tq, width), lambda b, p, i: (b * nq + i, p)),
        out_shape=jax.ShapeDtypeStruct((t, heads * MLA_V), F32),
        scratch_shapes=[buf] * (4 * npair),
        compiler_params=_cp("parallel", "parallel", "arbitrary"),
        name="softmax_attention",
    )(q, qrot, kv, krot, kv)


def _rope_kernel(q_ref, k_ref, pos_ref, f_ref, oq_ref, ok_ref):
    half = MLA_ROPE // 2
    ang = pos_ref[...].astype(F32) * f_ref[...]
    cos, sin = jnp.cos(ang), jnp.sin(ang)
    low = _iota(ang.shape, 1) % MLA_ROPE < half

    def rot(x):
        partner = jnp.where(low, -pltpu.roll(x, LANE - half, axis=1), pltpu.roll(x, half, axis=1))
        return x * cos + partner * sin

    for c in range(q_ref.shape[1] // LANE):
        cols = slice(c * LANE, (c + 1) * LANE)
        oq_ref[:, cols] = rot(q_ref[:, cols])
    ok_ref[...] = rot(k_ref[...]).astype(ok_ref.dtype)


def rope(q, c, positions, *, q_col, qw, k_col, tm=512):
    t = q.shape[0]
    half = MLA_ROPE // 2
    inv_freq = ROPE_THETA ** (-jnp.arange(half, dtype=F32) / half)
    freq = jnp.tile(inv_freq, LANE // half).reshape(1, LANE)
    return pl.pallas_call(
        _rope_kernel,
        grid=(t // tm,),
        in_specs=[pl.BlockSpec((tm, qw), lambda i: (i, q_col)), pl.BlockSpec((tm, LANE), lambda i: (i, k_col)),
                  pl.BlockSpec((tm, 1), lambda i: (i, 0)), pl.BlockSpec((1, LANE), lambda i: (0, 0))],
        out_specs=[pl.BlockSpec((tm, qw), lambda i: (i, 0)), pl.BlockSpec((tm, LANE), lambda i: (i, 0))],
        out_shape=[jax.ShapeDtypeStruct((t, qw), F32), jax.ShapeDtypeStruct((t, LANE), BF16)],
        compiler_params=_cp("parallel"),
        name="rope",
    )(q, c, positions.reshape(t, 1), freq)


def _rwkv_prep_kernel(x_ref, xp_ref, mu_ref, w0_ref, a0_ref, kk_ref, ka_ref, rk_ref,
                      w2_ref, a2_ref, g2_ref, tri_ref, hs_ref,
                      at_ref, bt_ref, kt_ref, rt_ref, v_ref, eg_ref, g_ref, bv_ref, *, tm, seq, width):
    i = pl.program_id(0)
    x = x_ref[...]
    at_start = (i * tm) % seq == 0
    prev_row = jnp.where(at_start, 0.0, xp_ref[7:8, :])
    rolled = pltpu.roll(x, 1, axis=0)
    top = _iota((8, x.shape[1]), 0)
    prev = jnp.concatenate([jnp.where(top == 0, prev_row, rolled[:8]), rolled[8:]], axis=0)
    xm = x + (prev - x) * mu_ref[...]
    r = xm[:, :width]
    k = xm[:, width:2 * width]
    v = xm[:, 2 * width:3 * width]
    l1 = xm[:, 3 * width:3 * width + LANE]
    l2 = xm[:, 3 * width + LANE:]
    wpre = w0_ref[...] + _dot(_Split(jnp.tanh(l1)).left(1), w2_ref[...])
    a = _sigmoid(a0_ref[...] + _dot(_Split(l1).left(1), a2_ref[...]))
    g = _dot(_Split(_sigmoid(l2)).left(1), g2_ref[...])

    def hilo(t):
        ts = _Split(t)
        return jnp.concatenate([ts.hi, ts.lo], axis=1)

    lw = -jnp.exp(-_softplus(-wpre) - 0.5)
    hs = hs_ref[...]
    kk = k * kk_ref[...]
    kk = kk * lax.rsqrt(jnp.maximum(_dot(hilo(kk * kk), hs), 1e-24))
    km = k * (1.0 + (a - 1.0) * ka_ref[...])
    bonus = _dot(hilo(r * km * rk_ref[...]), hs)
    lws = _Split(lw)
    cum = _dot(tri_ref[...], jnp.concatenate([lws.hi, lws.lo], axis=0))
    eg = jnp.exp(cum)
    ieg = jnp.exp(-cum)
    at_ref[...] = -kk * jnp.exp(cum - lw)
    bt_ref[...] = kk * a * ieg
    kt_ref[...] = km * ieg
    rt_ref[...] = r * eg
    v_ref[...] = v
    eg_ref[...] = eg
    g_ref[...] = g
    bv_ref[...] = bonus * v


def rwkv_prep(x, mu, w0, a0, k_k, k_a, r_k, w2p, a2p, g2p, *, seq, width, tm=256):
    t, wx = x.shape
    ch = RW_CHUNK
    ti = jnp.arange(tm)
    tri = ((ti[:, None] // ch == ti[None, :] // ch) & (ti[None, :] <= ti[:, None])).astype(BF16)
    tri = jnp.concatenate([tri, tri], axis=1)
    ci = jnp.arange(width) // HEAD_DIM
    hs = (ci[:, None] == ci[None, :]).astype(BF16)
    hs = jnp.concatenate([hs, hs], axis=0)
    w2p, a2p, g2p = (_Split(w).right(0) for w in (w2p, a2p, g2p))
    hb = tm // 8
    row = lambda i: (i, 0)
    fix = lambda i: (0, 0)
    vec = pl.BlockSpec((1, width), fix)
    out = jax.ShapeDtypeStruct((t, width), F32)
    return pl.pallas_call(
        functools.partial(_rwkv_prep_kernel, tm=tm, seq=seq, width=width),
        grid=(t // tm,),
        in_specs=[pl.BlockSpec((tm, wx), row),
                  pl.BlockSpec((8, wx), lambda i: (jnp.maximum(i * hb - 1, 0), 0)),
                  pl.BlockSpec((1, wx), fix), vec, vec, vec, vec, vec,
                  pl.BlockSpec(w2p.shape, fix), pl.BlockSpec(a2p.shape, fix),
                  pl.BlockSpec(g2p.shape, fix), pl.BlockSpec(tri.shape, fix),
                  pl.BlockSpec(hs.shape, fix)],
        out_specs=[pl.BlockSpec((tm, width), row)] * 8,
        out_shape=[out] * 8,
        compiler_params=_cp("parallel"),
        name="rwkv_prep",
    )(x, x, mu, w0, a0, k_k, k_a, r_k, w2p, a2p, g2p, tri, hs)


def _rwkv_chunk_kernel(at_ref, bt_ref, kt_ref, rt_ref, v_ref, eg_ref,
                       m_ref, n_ref, rq_ref, yl_ref, *, nchunk, unroll):
    ch, n = RW_CHUNK, HEAD_DIM
    ti = _iota((ch, ch), 0)
    si = _iota((ch, ch), 1)
    strict = si < ti
    incl = si <= ti
    same = (ti // RW_SUB) == (si // RW_SUB)
    eye = _iota((n, n), 0) == _iota((n, n), 1)
    each = lambda f, *ls: [f(*t) for t in zip(*ls)]

    def body(it, _):
        cs = [it * unroll + u for u in range(unroll)]
        sls = [pl.ds(pl.multiple_of(c * ch, ch), ch) for c in cs]
        ids = [(h, u) for h in range(2) for u in range(unroll)]
        ld = lambda ref: [ref[sls[u], h * n:(h + 1) * n] for h, u in ids]
        sp = lambda xs: [_Split(t) for t in xs]
        a_, b_, k_, r_, v_ = ld(at_ref), ld(bt_ref), ld(kt_ref), ld(rt_ref), ld(v_ref)
        g_end = [eg_ref[pl.ds(cs[u] * ch + ch - 1, 1), h * n:(h + 1) * n] for h, u in ids]
        ar = sp(each(lambda a, r: jnp.concatenate([a, r], axis=0), a_, r_))
        vs = sp(v_)
        pb = each(_mm3_nt, ar, sp(b_))
        pk = each(_mm3_nt, ar, sp(k_))
        a_ab = [jnp.where(strict, t[:ch], 0.0) for t in pb]
        a_ak = sp([jnp.where(strict, t[:ch], 0.0) for t in pk])
        a_rb = sp([jnp.where(incl, t[ch:], 0.0) for t in pb])
        a_rk = sp([jnp.where(incl, t[ch:], 0.0) for t in pk])
        d1 = [jnp.where(same, t, 0.0) for t in a_ab]
        lb = each(lambda t, d: t - d, a_ab, d1)
        d1 = sp(d1)
        d2 = sp(each(_mm3, d1, d1))
        akv = each(_mm3, a_ak, vs)
        x = each(lambda l, a, t: jnp.concatenate([l, a, t], axis=1), lb, a_, akv)
        d4 = sp(each(_mm3, d2, d2))
        x = each(lambda d, t: t + _mm3(d, _Split(t)), d1, x)
        d8 = sp(each(_mm3, d4, d4))
        x = each(lambda d, t: t + _mm3(d, _Split(t)), d2, x)
        x = each(lambda d, t: t + _mm3(d, _Split(t)), d4, x)
        x = each(lambda d, t: t + _mm3(d, _Split(t)), d8, x)
        e1 = sp([t[:, :ch] for t in x])
        wu = [t[:, ch:] for t in x]
        e2 = sp(each(_mm3, e1, e1))
        wu = each(lambda e, t: t + _mm3(e, _Split(t)), e1, wu)
        wu = sp(each(lambda e, t: t + _mm3(e, _Split(t)), e2, wu))
        ry = each(_mm3, a_rb, wu)
        rkv = each(_mm3, a_rk, vs)
        mn = each(lambda b, g, t: _mm3_tn(_Split(b * g), t), b_, g_end, wu)
        kv = each(lambda k, g, t: _mm3_tn(_Split(k * g), t), k_, g_end, vs)
        rq = each(lambda r, t: r + t[:, :n], r_, ry)
        yl = each(lambda t, w: t[:, n:] + w, ry, rkv)
        for u in range(unroll):
            rq_ref[sls[u], :] = jnp.concatenate([rq[u], rq[unroll + u]], axis=1)
            yl_ref[sls[u], :] = jnp.concatenate([yl[u], yl[unroll + u]], axis=1)
        for i, (h, u) in enumerate(ids):
            m_ref[h, cs[u]] = mn[i][:, :n] + jnp.where(eye, g_end[i], 0.0)
            n_ref[h, cs[u]] = mn[i][:, n:] + kv[i]
        return 0

    lax.fori_loop(0, nchunk // unroll, body, 0)


def rwkv_chunk(at, bt, kt, rt, v, eg, *, batch, nchunk=8, unroll=8):
    t, w = at.shape
    s = t // batch
    ch, n = RW_CHUNK, HEAD_DIM
    pairs = w // LANE
    rows = nchunk * ch
    nr = s // rows
    spec = pl.BlockSpec((rows, LANE), lambda b, p, i: (b * nr + i, p))
    mspec = pl.BlockSpec((2, nchunk, n, n), lambda b, p, i: (b * pairs + p, i, 0, 0))
    return pl.pallas_call(
        functools.partial(_rwkv_chunk_kernel, nchunk=nchunk, unroll=unroll),
        grid=(batch, pairs, nr),
        in_specs=[spec] * 6,
        out_specs=[mspec, mspec, spec, spec],
        out_shape=[jax.ShapeDtypeStruct((batch * w // n, s // ch, n, n), F32)] * 2
        + [jax.ShapeDtypeStruct((t, w), F32)] * 2,
        compiler_params=_cp("parallel", "parallel", "parallel"),
        name="rwkv_chunk",
    )(at, bt, kt, rt, v, eg)


def _rwkv_scan_kernel(m_ref, n_ref, s_ref, st_ref, *, heads, nchunk):
    @pl.when(pl.program_id(1) == 0)
    def _():
        st_ref[...] = jnp.zeros_like(st_ref)

    def body(c, _):
        sts = [st_ref[hh] for hh in range(heads)]
        for hh in range(heads):
            s_ref[hh, c] = sts[hh]
        ms = [_Split(m_ref[hh, c]) for hh in range(heads)]
        ss = [_Split(st) for st in sts]
        new = [_mm3(a, b) for a, b in zip(ms, ss)]
        for hh in range(heads):
            st_ref[hh] = new[hh] + n_ref[hh, c]
        return 0

    lax.fori_loop(0, nchunk, body, 0)


def rwkv_scan(m, nn, *, heads=16, nchunk=8):
    bh, nc, n, _ = m.shape
    heads = min(heads, bh)
    nchunk = min(nchunk, nc)
    spec = pl.BlockSpec((heads, nchunk, n, n), lambda b, i: (b, i, 0, 0))
    return pl.pallas_call(
        functools.partial(_rwkv_scan_kernel, heads=heads, nchunk=nchunk),
        grid=(bh // heads, nc // nchunk),
        in_specs=[spec, spec],
        out_specs=spec,
        out_shape=jax.ShapeDtypeStruct((bh, nc, n, n), F32),
        scratch_shapes=[pltpu.VMEM((heads, n, n), F32)],
        compiler_params=_cp("parallel", "arbitrary"),
        name="rwkv_scan",
    )(m, nn)


def _rwkv_out_kernel(rq_ref, yl_ref, s_ref, lw_ref, lb_ref, avg_ref, o_ref, *, nchunk):
    ch, n = RW_CHUNK, HEAD_DIM
    ys = []
    for h in range(2):
        cols = slice(h * n, (h + 1) * n)
        ys.append(jnp.concatenate(
            [_mm3(_Split(rq_ref[c * ch:(c + 1) * ch, cols]), _Split(s_ref[h, c])) for c in range(nchunk)],
            axis=0))
    y = jnp.concatenate(ys, axis=1) + yl_ref[...]
    avg2 = jnp.concatenate([avg_ref[...]] * 2, axis=0)

    def mean(t):
        ts = _Split(t)
        return _dot(jnp.concatenate([ts.hi, ts.lo], axis=1), avg2)

    d = y - mean(y)
    var = mean(d * d)
    o_ref[...] = d * lax.rsqrt(var + GN_EPS) * lw_ref[...] + lb_ref[...]


def rwkv_out(rq, yl, s0, ln_w, ln_b, *, batch, nchunk=8):
    t, w = rq.shape
    s = t // batch
    n = HEAD_DIM
    pairs = w // LANE
    rows = nchunk * RW_CHUNK
    nr = s // rows
    li = jnp.arange(LANE) // n
    avg = ((li[:, None] == li[None, :]).astype(F32) / n).astype(BF16)
    spec = pl.BlockSpec((rows, LANE), lambda b, p, i: (b * nr + i, p))
    hspec = pl.BlockSpec((1, LANE), lambda b, p, i: (0, p))
    return pl.pallas_call(
        functools.partial(_rwkv_out_kernel, nchunk=nchunk),
        grid=(batch, pairs, nr),
        in_specs=[spec, spec, pl.BlockSpec((2, nchunk, n, n), lambda b, p, i: (b * pairs + p, i, 0, 0)),
                  hspec, hspec, pl.BlockSpec((LANE, LANE), lambda b, p, i: (0, 0))],
        out_specs=spec,
        out_shape=jax.ShapeDtypeStruct((t, w), F32),
        compiler_params=_cp("parallel", "parallel", "parallel"),
        name="rwkv_out",
    )(rq, yl, s0, ln_w.reshape(1, w), ln_b.reshape(1, w), avg)


def _hyb_out_kernel(oa_ref, y_ref, bv_ref, g_ref, wa_ref, wb_ref, r_ref, o_ref):
    ob = (y_ref[...] + bv_ref[...]) * g_ref[...]
    o_ref[...] = (r_ref[...] + _dot(oa_ref[...].astype(BF16), wa_ref[...])
                  + _dot(ob.astype(BF16), wb_ref[...]))


def hyb_out(res, oa, y, bv, g, wa, wb, *, tm=1024):
    t, d = res.shape
    row = lambda i: (i, 0)
    fix = lambda i: (0, 0)
    act = lambda a: pl.BlockSpec((tm, a.shape[1]), row)
    return pl.pallas_call(
        _hyb_out_kernel,
        grid=(t // tm,),
        in_specs=[act(oa), act(y), act(bv), act(g), pl.BlockSpec(wa.shape, fix),
                  pl.BlockSpec(wb.shape, fix), pl.BlockSpec((tm, d), row)],
        out_specs=pl.BlockSpec((tm, d), row),
        out_shape=jax.ShapeDtypeStruct((t, d), F32),
        compiler_params=_cp("parallel"),
        name="hyb_out",
    )(oa, y, bv, g, wa, wb, res)


def _pad_cols(w, n):
    return jnp.pad(w, ((0, 0), (0, n - w.shape[1])))


def sb_rwkv_layer(h, b, s, attn_norm, w_in, w_out, mu, w0, w2, a0, a2, g2, k_k, k_a, r_k, ln_w, ln_b):
    sbw = SB_HEADS * HEAD_DIM
    rww = RW_HEADS * HEAD_DIM
    rw_in = w_in.shape[1] - 3 * sbw
    rw_pad = -(-rw_in // LANE) * LANE
    dl, al = w2.shape[0], a2.shape[0]
    gl = g2.shape[0]
    assert dl + al == LANE and 3 * rww % LANE == 0

    qkv = rms_matmul(h, attn_norm, w_in[:, :3 * sbw].astype(BF16), out_dtype=BF16)
    rw = rms_matmul(h, attn_norm, _pad_cols(w_in[:, 3 * sbw:], rw_pad).astype(BF16))
    oa = sb_attention(qkv, batch=b, heads=SB_HEADS)

    w2p = jnp.zeros((LANE, rww), F32).at[:dl].set(w2)
    a2p = jnp.zeros((LANE, rww), F32).at[dl:].set(a2)
    g2p = jnp.zeros((rw_pad - 3 * rww - LANE, rww), F32).at[:gl].set(g2)
    vec = lambda t: t.reshape(1, rww).astype(F32)
    mu_p = _pad_cols(mu.reshape(1, rw_in), rw_pad)
    at, bt, kt, rt, vv, eg, g, bv = rwkv_prep(rw, mu_p, vec(w0), vec(a0), vec(k_k), vec(k_a), vec(r_k),
                                              w2p, a2p, g2p, seq=s, width=rww)
    m, nn, rq, yl = rwkv_chunk(at, bt, kt, rt, vv, eg, batch=b)
    s0 = rwkv_scan(m, nn)
    y = rwkv_out(rq, yl, s0, ln_w, ln_b, batch=b)
    return hyb_out(h, oa, y, bv, g, w_out[:sbw].astype(BF16), w_out[sbw:].astype(BF16))


def mla_layer(h, b, s, positions, attn_norm, w_down, q_norm, kv_norm, w_uq, w_ukv, w_o):
    nh, dn, dr, dv = MLA_HEADS, MLA_NOPE, MLA_ROPE, MLA_V
    qr, kvr = MLA_Q_RANK, MLA_KV_RANK
    assert dn == dv and qr % LANE == 0 and (qr + LANE) % kvr == 0
    wd = jnp.concatenate([w_down[:, :qr], jnp.tile(w_down[:, qr + kvr:], (1, LANE // dr)),
                          w_down[:, qr:qr + kvr]], axis=1)
    c = rms_matmul(h, attn_norm, wd.astype(BF16))
    wq = w_uq.reshape(qr, nh, dn + dr)
    wq = jnp.concatenate([wq[:, :, :dn].reshape(qr, nh * dn), wq[:, :, dn:].reshape(qr, nh * dr)], axis=1)
    q = rms_matmul(c, q_norm, wq.astype(BF16), x_col=0)
    wkv = w_ukv.reshape(kvr, nh, dn + dv)
    wkv = jnp.concatenate([wkv[:, :, :dn].reshape(kvr, nh * dn), wkv[:, :, dn:].reshape(kvr, nh * dv)], axis=1)
    kv = rms_matmul(c, kv_norm, wkv.astype(BF16), x_col=(qr + LANE) // kvr, out_dtype=BF16)
    qrot, krot = rope(q, c, positions, q_col=dn // dr, qw=nh * dr, k_col=qr // LANE)
    o = softmax_attention(q, qrot, kv, krot, 1.0 / math.sqrt(dn + dr), batch=b, heads=nh)
    return matmul_residual(h, o, w_o.astype(BF16))


def kernel(x, p, positions, attn_norm, ffn_norm, ffn_w_in, ffn_conv_w, ffn_conv_b, ffn_w_out, ple_w_proj, ple_norm, ple_gate_norm, ple_w_gate, hyb_w_in, hyb_w_out, rw_mu, rw_w0, rw_w2, rw_a0, rw_a2, rw_g2, rw_k_k, rw_k_a, rw_r_k, rw_ln_w, rw_ln_b, mla_w_down, mla_q_norm, mla_kv_norm, mla_w_uq, mla_w_ukv, mla_w_o, final_norm):
    b, s, d = x.shape
    depth = p.shape[0]
    h = x.reshape(b * s, d)
    pos = positions.reshape(b * s)
    for i in range(depth):
        j = i // 2
        if i % 2 == 0:
            h = sb_rwkv_layer(h, b, s, attn_norm[i], hyb_w_in[j], hyb_w_out[j], rw_mu[j], rw_w0[j],
                              rw_w2[j], rw_a0[j], rw_a2[j], rw_g2[j], rw_k_k[j], rw_k_a[j], rw_r_k[j],
                              rw_ln_w[j], rw_ln_b[j])
        else:
            h = mla_layer(h, b, s, pos, attn_norm[i], mla_w_down[j], mla_q_norm[j], mla_kv_norm[j],
                          mla_w_uq[j], mla_w_ukv[j], mla_w_o[j])
        ug = rms_matmul(h, ffn_norm[i], ffn_w_in[i].astype(BF16), out_dtype=BF16)
        h = convglu_out(h, ug, ffn_conv_w[i], ffn_conv_b[i], ffn_w_out[i], seq=s)
        h = ple(h, p[i].reshape(b * s, -1), ple_w_proj[i].astype(BF16), ple_norm[i], ple_gate_norm[i],
                ple_w_gate[i].astype(BF16), final_norm, final=(i == depth - 1))
    return h.reshape(b, s, d)
```

```python
import functools
import math

import jax
import jax.numpy as jnp
from jax import lax
from jax.experimental import pallas as pl
from jax.experimental.pallas import tpu as pltpu

F32 = jnp.float32
BF16 = jnp.bfloat16
HI = lax.Precision.HIGHEST

NORM_EPS = 1e-6
GN_EPS = 64e-5
ROPE_THETA = 10000.0
LANE = 128
VMEM_LIMIT = 48 * 1024 * 1024
HALO = 16
MXU_WIDTH = 256
TILE_VMEM_BUDGET = 36 * 1024 * 1024
WEIGHT_TILE_BYTES = 6 * 1024 * 1024

SB_HEADS = 8
RW_HEADS = 8
HEAD_DIM = 64
MLA_HEADS = 16
MLA_NOPE = 64
MLA_ROPE = 32
MLA_V = 64
MLA_Q_RANK = 384
MLA_KV_RANK = 256
SB_CUT = -151.0
RW_CHUNK = 64
RW_SUB = 16


def _cp(*sem):
    return pltpu.CompilerParams(dimension_semantics=sem, vmem_limit_bytes=VMEM_LIMIT)


def _iota(shape, dim):
    return lax.broadcasted_iota(jnp.int32, shape, dim)


def _softplus(z):
    return jnp.maximum(z, 0.0) + jnp.log1p(jnp.exp(-jnp.abs(z)))


def _sigmoid(z):
    return 1.0 / (1.0 + jnp.exp(-z))


def _dot(a, b, precision=None):
    return jnp.dot(a, b, preferred_element_type=F32, precision=precision)


def _dot_nt(a, b, precision=None):
    return lax.dot_general(a, b, (((1,), (1,)), ((), ())), preferred_element_type=F32,
                           precision=precision)


def _dot_tn(a, b, precision=None):
    return lax.dot_general(a, b, (((0,), (0,)), ((), ())), preferred_element_type=F32,
                           precision=precision)


class _Split:
    def __init__(self, x):
        self.x = x
        self.hi = x.astype(BF16)
        self.lo = (x - self.hi.astype(F32)).astype(BF16)
        self._packed = {}

    def left(self, axis):
        key = ("l", axis)
        if key not in self._packed:
            self._packed[key] = jnp.concatenate([self.hi, self.lo, self.hi], axis=axis)
        return self._packed[key]

    def right(self, axis):
        key = ("r", axis)
        if key not in self._packed:
            self._packed[key] = jnp.concatenate([self.hi, self.hi, self.lo], axis=axis)
        return self._packed[key]


def _mm3(a, b):
    return _dot(a.left(1), b.right(0))


def _mm3_nt(a, b):
    return _dot_nt(a.left(1), b.right(1))


def _mm3_tn(a, b):
    return _dot_tn(a.left(0), b.right(0))


def _rms(x, g, eps=NORM_EPS):
    return x * lax.rsqrt(jnp.mean(x * x, axis=-1, keepdims=True) + eps) * g


def _rms_matmul_kernel(x_ref, g_ref, w_ref, o_ref, xn_ref):
    @pl.when(pl.program_id(1) == 0)
    def _():
        xn_ref[...] = _rms(x_ref[...].astype(F32), g_ref[...]).astype(BF16)

    o_ref[...] = _dot(xn_ref[...], w_ref[...]).astype(o_ref.dtype)


def _pick_tiles(t, k, n, tile_bytes):
    tn = max(c for c in range(LANE, n + 1, LANE)
             if n % c == 0 and (c == n or c % MXU_WIDTH == 0) and k * c * 2 <= WEIGHT_TILE_BYTES)
    for tm in (2048, 1024, 512, 256):
        if t % tm == 0 and tile_bytes(tm, tn) + 2 * k * tn * 2 <= TILE_VMEM_BUDGET:
            return tm, tn
    raise ValueError("no row tile fits")


def rms_matmul(x, g, w, *, x_col=0, out_dtype=F32):
    t = x.shape[0]
    k, n = w.shape
    ob = jnp.dtype(out_dtype).itemsize
    tm, tn = _pick_tiles(t, k, n, lambda tm, tn: 2 * tm * k * x.dtype.itemsize + tm * k * 2 + 2 * tm * tn * ob)
    return pl.pallas_call(
        _rms_matmul_kernel,
        grid=(t // tm, n // tn),
        in_specs=[pl.BlockSpec((tm, k), lambda i, j: (i, x_col)),
                  pl.BlockSpec((1, k), lambda i, j: (0, 0)),
                  pl.BlockSpec((k, tn), lambda i, j: (0, j))],
        out_specs=pl.BlockSpec((tm, tn), lambda i, j: (i, j)),
        out_shape=jax.ShapeDtypeStruct((t, n), out_dtype),
        scratch_shapes=[pltpu.VMEM((tm, k), BF16)],
        compiler_params=_cp("parallel", "arbitrary"),
        name="rms_matmul",
    )(x, g.reshape(1, k).astype(F32), w)


def _matmul_res_kernel(a_ref, w_ref, r_ref, o_ref):
    o_ref[...] = r_ref[...] + _dot(a_ref[...].astype(BF16), w_ref[...])


def matmul_residual(res, a, w):
    t, k = a.shape
    n = w.shape[1]
    tm, tn = _pick_tiles(t, k, n, lambda tm, tn: 2 * tm * k * a.dtype.itemsize + 4 * tm * tn * 4)
    return pl.pallas_call(
        _matmul_res_kernel,
        grid=(t // tm, n // tn),
        in_specs=[pl.BlockSpec((tm, k), lambda i, j: (i, 0)),
                  pl.BlockSpec((k, tn), lambda i, j: (0, j)),
                  pl.BlockSpec((tm, tn), lambda i, j: (i, j))],
        out_specs=pl.BlockSpec((tm, tn), lambda i, j: (i, j)),
        out_shape=jax.ShapeDtypeStruct((t, n), F32),
        compiler_params=_cp("parallel", "arbitrary"),
        name="matmul_residual",
    )(a, w, res)


def _convglu_out_kernel(u_ref, g_ref, gp_ref, cw_ref, cb_ref, w_ref, r_ref, o_ref, *, tm, seq):
    i = pl.program_id(0)
    g = g_ref[...].astype(F32)
    at_start = (i * tm) % seq == 0
    halo = jnp.where(at_start, 0.0, gp_ref[...].astype(F32))
    h1, h2 = halo[HALO - 1:HALO, :], halo[HALO - 2:HALO - 1, :]
    r1, r2 = pltpu.roll(g, 1, axis=0), pltpu.roll(g, 2, axis=0)
    top = _iota((8, g.shape[1]), 0)
    p1 = jnp.concatenate([jnp.where(top == 0, h1, r1[:8]), r1[8:]], axis=0)
    p2 = jnp.concatenate([jnp.where(top == 0, h2, jnp.where(top == 1, h1, r2[:8])), r2[8:]], axis=0)
    c = cw_ref[0:1, :] * p2 + cw_ref[1:2, :] * p1 + cw_ref[2:3, :] * g + cb_ref[...]
    act = c * (1.0 + lax.erf(c)) * u_ref[...].astype(F32)
    o_ref[...] = r_ref[...] + _dot(act.astype(BF16), w_ref[...])


def convglu_out(res, ug, conv_w, conv_b, w_out, *, seq, tm=256):
    t = res.shape[0]
    f, n = w_out.shape
    hb = tm // HALO
    rs2 = 1.0 / math.sqrt(2.0)
    conv_w, conv_b, w_out = conv_w * rs2, conv_b * rs2, (w_out * rs2).astype(BF16)
    return pl.pallas_call(
        functools.partial(_convglu_out_kernel, tm=tm, seq=seq),
        grid=(t // tm,),
        in_specs=[pl.BlockSpec((tm, f), lambda i: (i, 0)),
                  pl.BlockSpec((tm, f), lambda i: (i, 1)),
                  pl.BlockSpec((HALO, f), lambda i: (jnp.maximum(i * hb - 1, 0), 1)),
                  pl.BlockSpec((3, f), lambda i: (0, 0)),
                  pl.BlockSpec((1, f), lambda i: (0, 0)),
                  pl.BlockSpec((f, n), lambda i: (0, 0)),
                  pl.BlockSpec((tm, n), lambda i: (i, 0))],
        out_specs=pl.BlockSpec((tm, n), lambda i: (i, 0)),
        out_shape=jax.ShapeDtypeStruct((t, n), F32),
        compiler_params=_cp("parallel"),
        name="convglu_out",
    )(ug, ug, ug, conv_w, conv_b.reshape(1, f), w_out, res)


def _ple_kernel(h_ref, p_ref, wp_ref, pn_ref, gn_ref, wg_ref, fn_ref, o_ref, *, final):
    h = h_ref[...]
    e = _rms(_dot(p_ref[...].astype(BF16), wp_ref[...]), pn_ref[...])
    gate = _sigmoid(_dot(_rms(h, gn_ref[...]).astype(BF16), wg_ref[...]))
    out = h + gate * e
    if final:
        out = _rms(out, fn_ref[...])
    o_ref[...] = out


def ple(h, p, w_proj, p_norm, g_norm, w_gate, f_norm, *, final, tm=1024):
    t, d = h.shape
    pd = p.shape[1]
    row = lambda i: (i, 0)
    fix = lambda i: (0, 0)
    return pl.pallas_call(
        functools.partial(_ple_kernel, final=final),
        grid=(t // tm,),
        in_specs=[pl.BlockSpec((tm, d), row), pl.BlockSpec((tm, pd), row),
                  pl.BlockSpec((pd, d), fix), pl.BlockSpec((1, d), fix), pl.BlockSpec((1, d), fix),
                  pl.BlockSpec((d, d), fix), pl.BlockSpec((1, d), fix)],
        out_specs=pl.BlockSpec((tm, d), row),
        out_shape=jax.ShapeDtypeStruct((t, d), F32),
        compiler_params=_cp("parallel"),
        name="ple",
    )(h, p, w_proj, p_norm.reshape(1, d), g_norm.reshape(1, d), w_gate, f_norm.reshape(1, d))


def _sb_attn_kernel(q_ref, k_ref, v_ref, u_ref, o_ref, *, tq, tk, nh, scale):
    qi = pl.program_id(2)
    nblk = tq // tk
    width = nh * HEAD_DIM
    qf = q_ref[...] * scale
    qlane = _iota((tq, width), 1) // HEAD_DIM
    qs = jnp.concatenate([jnp.where(qlane == h, qf, jnp.zeros_like(qf)) for h in range(nh)], axis=0)
    vlane = _iota((tk, width), 1) // HEAD_DIM
    rows = lambda x, h: x[h * tq:(h + 1) * tq]

    def sweep(k_lo, run, acc, masked):
        parts = []
        for d in reversed(range(nblk)):
            k0 = k_lo + d * tk
            t = _dot_nt(qs, k_ref[pl.ds(k0, tk), :]) * math.log2(math.e)
            sp = jnp.maximum(t, 0.0) + jnp.log2(1.0 + jnp.exp2(-jnp.abs(t)))
            lom = -sp
            valid = None
            if masked:
                valid = _iota((nh * tq, tk), 1) + d * tk < _iota((nh * tq, tk), 0) % tq
                lom = jnp.where(valid, lom, 0.0)
            hi = lom.astype(BF16)
            lo = (lom - hi.astype(F32)).astype(BF16)
            sums = _dot(jnp.concatenate([hi, lo], axis=1), u_ref[...])
            parts.append((k0, t - sp, sums, valid))
        for k0, logsig, sums, valid in parts:
            w = jnp.exp2(logsig + run + sums[:, :tk])
            if masked:
                w = jnp.where(valid, w, 0.0)
            w = w.astype(BF16)
            vb = v_ref[pl.ds(k0, tk), :]
            wcat = jnp.concatenate([rows(w, h) for h in range(nh)], axis=1)
            vcat = jnp.concatenate([jnp.where(vlane == h, vb, jnp.zeros_like(vb)) for h in range(nh)], axis=0)
            acc = acc + _dot(wcat, vcat)
            run = run + sums[:, tk:]
        return run, acc

    run = jnp.zeros((nh * tq, tk), F32)
    acc = jnp.zeros((tq, width), F32)
    run, acc = sweep(pl.multiple_of(qi * tq, tq), run, acc, True)

    def live(run):
        return jnp.max(run) > SB_CUT

    def cond(c):
        return c[0]

    def body(c):
        _, j, run, acc = c
        run, acc = sweep(pl.multiple_of(j * tq, tq), run, acc, False)
        return (j > 0) & live(run), j - 1, run, acc

    _, _, _, acc = lax.while_loop(cond, body, ((qi > 0) & live(run), qi - 1, run, acc))
    o_ref[...] = acc


def sb_attention(qkv, *, batch, heads, tq=256, nh=4):
    t, w3 = qkv.shape
    s = t // batch
    nq = s // tq
    width = nh * HEAD_DIM
    groups = heads // nh
    tk = LANE
    jj = jnp.arange(2 * tk)[:, None] % tk
    ss = jnp.arange(2 * tk)[None, :]
    u = jnp.where(ss < tk, jj > ss, True).astype(BF16)
    return pl.pallas_call(
        functools.partial(_sb_attn_kernel, tq=tq, tk=tk, nh=nh, scale=1.0 / math.sqrt(HEAD_DIM)),
        grid=(batch, groups, nq),
        in_specs=[pl.BlockSpec((tq, width), lambda b, p, i: (b * nq + i, p)),
                  pl.BlockSpec((s, width), lambda b, p, i: (b, groups + p)),
                  pl.BlockSpec((s, width), lambda b, p, i: (b, 2 * groups + p)),
                  pl.BlockSpec((2 * tk, 2 * tk), lambda b, p, i: (0, 0))],
        out_specs=pl.BlockSpec((tq, width), lambda b, p, i: (b * nq + i, p)),
        out_shape=jax.ShapeDtypeStruct((t, w3 // 3), F32),
        compiler_params=_cp("parallel", "parallel", "arbitrary"),
        name="sb_attention",
    )(qkv, qkv, qkv, u)


def _softmax_attn_kernel(qn_ref, qr_ref, kn_ref, kr_ref, v_ref, o_ref, *bufs, tq, tk, npair, scale):
    qi = pl.program_id(2)
    g = pl.program_id(1)
    nh = 2 * npair
    lane = _iota((tq, LANE), 1)
    qr = qr_ref[...] * (scale * math.log2(math.e))
    slots = LANE // MLA_ROPE
    qs = []
    for h in range(nh):
        cols = slice(h // 2 * LANE, (h // 2 + 1) * LANE)
        qn = qn_ref[:, cols] * (scale * math.log2(math.e))
        nope = jnp.where(lane // MLA_NOPE == h % 2, qn, 0.0)
        rope_ = jnp.where(lane // MLA_ROPE == (g * nh + h) % slots, qr, 0.0)
        qs.append(jnp.concatenate([nope, rope_], axis=1).astype(BF16))

    def scores(j, dsts):
        k0 = pl.multiple_of(j * tk, tk)
        mx = []
        for h in range(nh):
            cols = slice(h // 2 * LANE, (h // 2 + 1) * LANE)
            kb = jnp.concatenate([kn_ref[pl.ds(k0, tk), cols], kr_ref[pl.ds(k0, tk), :]], axis=1)
            s = _dot_nt(qs[h], kb)
            dsts[h][...] = s
            mx.append(jnp.max(s, axis=1, keepdims=True))
        return tuple(mx)

    def consume(srcs, j, carry, mx):
        out = []
        for h in range(nh):
            cols = slice(h // 2 * LANE, (h // 2 + 1) * LANE)
            vb = v_ref[pl.ds(pl.multiple_of(j * tk, tk), tk), cols]
            mine = (_iota(vb.shape, 1) // MLA_V) == h % 2
            vbh = jnp.where(mine, vb, jnp.ones_like(vb))
            m, acc = carry[h]
            s = srcs[h][...]
            if mx is None:
                s = jnp.where(_iota((tq, tk), 1) <= _iota((tq, tk), 0), s, -1e30)
                m_new = jnp.maximum(m, jnp.max(s, axis=1, keepdims=True))
            else:
                m_new = jnp.maximum(m, mx[h])
            p = jnp.exp2(s - m_new)
            acc = jnp.exp2(m - m_new) * acc + _dot(p.astype(BF16), vbh)
            out.append((m_new, acc))
        return tuple(out)

    sa, sb = bufs[:nh], bufs[nh:]

    def pair(t, carry):
        state, mxa = carry
        mxb = scores(2 * t + 1, sb)
        state = consume(sa, 2 * t, state, mxa)
        mxa = scores(2 * t + 2, sa)
        return consume(sb, 2 * t + 1, state, mxb), mxa

    def tail_odd(carry):
        state, mxa = carry
        scores(qi, sb)
        state = consume(sa, qi - 1, state, mxa)
        return consume(sb, qi, state, None)

    def tail_even(carry):
        return consume(sa, qi, carry[0], None)

    init = ((jnp.full((tq, 1), -1e30, F32), jnp.zeros((tq, LANE), F32)),) * nh
    carry = lax.fori_loop(0, qi // 2, pair, (init, scores(0, sa)))
    state = lax.cond(qi % 2 == 1, tail_odd, tail_even, carry)
    for pr in range(npair):
        acc0, acc1 = state[2 * pr][1], state[2 * pr + 1][1]
        o_ref[:, pr * LANE:(pr + 1) * LANE] = jnp.where(
            lane < MLA_V, acc0 / pltpu.roll(acc0, MLA_V, axis=1), acc1 / pltpu.roll(acc1, MLA_V, axis=1))


def softmax_attention(q, qrot, kv, krot, scale, *, batch, heads, tq=512, npair=2):
    tk = tq
    t = q.shape[0]
    s = t // batch
    nq = s // tq
    width = npair * LANE
    groups = heads * MLA_NOPE // width
    rope_per_group = LANE // (2 * npair * MLA_ROPE)
    buf = pltpu.VMEM((tq, tk), F32)
    return pl.pallas_call(
        functools.partial(_softmax_attn_kernel, tq=tq, tk=tk, npair=npair, scale=scale),
        grid=(batch, groups, nq),
        in_specs=[pl.BlockSpec((tq, width), lambda b, p, i: (b * nq + i, p)),
                  pl.BlockSpec((tq, LANE), lambda b, p, i: (b * nq + i, p // rope_per_group)),
                  pl.BlockSpec((s, width), lambda b, p, i: (b, p)),
                  pl.BlockSpec((s, LANE), lambda b, p, i: (b, 0)),
                  pl.BlockSpec((s, width), lambda b, p, i: (b, groups + p))],
        out_specs=pl.BlockSpec((tq, width), lambda b, p, i: (b * nq + i, p)),
        out_shape=jax.ShapeDtypeStruct((t, heads * MLA_V), F32),
        scratch_shapes=[buf] * (4 * npair),
        compiler_params=_cp("parallel", "parallel", "arbitrary"),
        name="softmax_attention",
    )(q, qrot, kv, krot, kv)


def _rope_kernel(q_ref, k_ref, pos_ref, f_ref, oq_ref, ok_ref):
    half = MLA_ROPE // 2
    ang = pos_ref[...].astype(F32) * f_ref[...]
    cos, sin = jnp.cos(ang), jnp.sin(ang)
    low = _iota(ang.shape, 1) % MLA_ROPE < half

    def rot(x):
        partner = jnp.where(low, -pltpu.roll(x, LANE - half, axis=1), pltpu.roll(x, half, axis=1))
        return x * cos + partner * sin

    for c in range(q_ref.shape[1] // LANE):
        cols = slice(c * LANE, (c + 1) * LANE)
        oq_ref[:, cols] = rot(q_ref[:, cols])
    ok_ref[...] = rot(k_ref[...]).astype(ok_ref.dtype)


def rope(q, c, positions, *, q_col, qw, k_col, tm=512):
    t = q.shape[0]
    half = MLA_ROPE // 2
    inv_freq = ROPE_THETA ** (-jnp.arange(half, dtype=F32) / half)
    freq = jnp.tile(inv_freq, LANE // half).reshape(1, LANE)
    return pl.pallas_call(
        _rope_kernel,
        grid=(t // tm,),
        in_specs=[pl.BlockSpec((tm, qw), lambda i: (i, q_col)), pl.BlockSpec((tm, LANE), lambda i: (i, k_col)),
                  pl.BlockSpec((tm, 1), lambda i: (i, 0)), pl.BlockSpec((1, LANE), lambda i: (0, 0))],
        out_specs=[pl.BlockSpec((tm, qw), lambda i: (i, 0)), pl.BlockSpec((tm, LANE), lambda i: (i, 0))],
        out_shape=[jax.ShapeDtypeStruct((t, qw), F32), jax.ShapeDtypeStruct((t, LANE), BF16)],
        compiler_params=_cp("parallel"),
        name="rope",
    )(q, c, positions.reshape(t, 1), freq)


def _rwkv_prep_kernel(x_ref, xp_ref, mu_ref, w0_ref, a0_ref, kk_ref, ka_ref, rk_ref,
                      w2_ref, a2_ref, g2_ref, tri_ref, hs_ref,
                      at_ref, bt_ref, kt_ref, rt_ref, v_ref, eg_ref, g_ref, bv_ref, *, tm, seq, width):
    i = pl.program_id(0)
    x = x_ref[...]
    at_start = (i * tm) % seq == 0
    prev_row = jnp.where(at_start, 0.0, xp_ref[7:8, :])
    rolled = pltpu.roll(x, 1, axis=0)
    top = _iota((8, x.shape[1]), 0)
    prev = jnp.concatenate([jnp.where(top == 0, prev_row, rolled[:8]), rolled[8:]], axis=0)
    xm = x + (prev - x) * mu_ref[...]
    r = xm[:, :width]
    k = xm[:, width:2 * width]
    v = xm[:, 2 * width:3 * width]
    l1 = xm[:, 3 * width:3 * width + LANE]
    l2 = xm[:, 3 * width + LANE:]
    wpre = w0_ref[...] + _dot(_Split(jnp.tanh(l1)).left(1), w2_ref[...])
    a = _sigmoid(a0_ref[...] + _dot(_Split(l1).left(1), a2_ref[...]))
    g = _dot(_Split(_sigmoid(l2)).left(1), g2_ref[...])

    def hilo(t):
        ts = _Split(t)
        return jnp.concatenate([ts.hi, ts.lo], axis=1)

    lw = -jnp.exp(-_softplus(-wpre) - 0.5)
    hs = hs_ref[...]
    kk = k * kk_ref[...]
    kk = kk * lax.rsqrt(jnp.maximum(_dot(hilo(kk * kk), hs), 1e-24))
    km = k * (1.0 + (a - 1.0) * ka_ref[...])
    bonus = _dot(hilo(r * km * rk_ref[...]), hs)
    lws = _Split(lw)
    cum = _dot(tri_ref[...], jnp.concatenate([lws.hi, lws.lo], axis=0))
    eg = jnp.exp(cum)
    ieg = jnp.exp(-cum)
    at_ref[...] = -kk * jnp.exp(cum - lw)
    bt_ref[...] = kk * a * ieg
    kt_ref[...] = km * ieg
    rt_ref[...] = r * eg
    v_ref[...] = v
    eg_ref[...] = eg
    g_ref[...] = g
    bv_ref[...] = bonus * v


def rwkv_prep(x, mu, w0, a0, k_k, k_a, r_k, w2p, a2p, g2p, *, seq, width, tm=256):
    t, wx = x.shape
    ch = RW_CHUNK
    ti = jnp.arange(tm)
    tri = ((ti[:, None] // ch == ti[None, :] // ch) & (ti[None, :] <= ti[:, None])).astype(BF16)
    tri = jnp.concatenate([tri, tri], axis=1)
    ci = jnp.arange(width) // HEAD_DIM
    hs = (ci[:, None] == ci[None, :]).astype(BF16)
    hs = jnp.concatenate([hs, hs], axis=0)
    w2p, a2p, g2p = (_Split(w).right(0) for w in (w2p, a2p, g2p))
    hb = tm // 8
    row = lambda i: (i, 0)
    fix = lambda i: (0, 0)
    vec = pl.BlockSpec((1, width), fix)
    out = jax.ShapeDtypeStruct((t, width), F32)
    return pl.pallas_call(
        functools.partial(_rwkv_prep_kernel, tm=tm, seq=seq, width=width),
        grid=(t // tm,),
        in_specs=[pl.BlockSpec((tm, wx), row),
                  pl.BlockSpec((8, wx), lambda i: (jnp.maximum(i * hb - 1, 0), 0)),
                  pl.BlockSpec((1, wx), fix), vec, vec, vec, vec, vec,
                  pl.BlockSpec(w2p.shape, fix), pl.BlockSpec(a2p.shape, fix),
                  pl.BlockSpec(g2p.shape, fix), pl.BlockSpec(tri.shape, fix),
                  pl.BlockSpec(hs.shape, fix)],
        out_specs=[pl.BlockSpec((tm, width), row)] * 8,
        out_shape=[out] * 8,
        compiler_params=_cp("parallel"),
        name="rwkv_prep",
    )(x, x, mu, w0, a0, k_k, k_a, r_k, w2p, a2p, g2p, tri, hs)


def _rwkv_chunk_kernel(at_ref, bt_ref, kt_ref, rt_ref, v_ref, eg_ref,
                       m_ref, n_ref, rq_ref, yl_ref, *, nchunk, unroll):
    ch, n = RW_CHUNK, HEAD_DIM
    ti = _iota((ch, ch), 0)
    si = _iota((ch, ch), 1)
    strict = si < ti
    incl = si <= ti
    same = (ti // RW_SUB) == (si // RW_SUB)
    eye = _iota((n, n), 0) == _iota((n, n), 1)
    each = lambda f, *ls: [f(*t) for t in zip(*ls)]

    def body(it, _):
        cs = [it * unroll + u for u in range(unroll)]
        sls = [pl.ds(pl.multiple_of(c * ch, ch), ch) for c in cs]
        ids = [(h, u) for h in range(2) for u in range(unroll)]
        ld = lambda ref: [ref[sls[u], h * n:(h + 1) * n] for h, u in ids]
        sp = lambda xs: [_Split(t) for t in xs]
        a_, b_, k_, r_, v_ = ld(at_ref), ld(bt_ref), ld(kt_ref), ld(rt_ref), ld(v_ref)
        g_end = [eg_ref[pl.ds(cs[u] * ch + ch - 1, 1), h * n:(h + 1) * n] for h, u in ids]
        ar = sp(each(lambda a, r: jnp.concatenate([a, r], axis=0), a_, r_))
        vs = sp(v_)
        pb = each(_mm3_nt, ar, sp(b_))
        pk = each(_mm3_nt, ar, sp(k_))
        a_ab = [jnp.where(strict, t[:ch], 0.0) for t in pb]
        a_ak = sp([jnp.where(strict, t[:ch], 0.0) for t in pk])
        a_rb = sp([jnp.where(incl, t[ch:], 0.0) for t in pb])
        a_rk = sp([jnp.where(incl, t[ch:], 0.0) for t in pk])
        d1 = [jnp.where(same, t, 0.0) for t in a_ab]
        lb = each(lambda t, d: t - d, a_ab, d1)
        d1 = sp(d1)
        d2 = sp(each(_mm3, d1, d1))
        akv = each(_mm3, a_ak, vs)
        x = each(lambda l, a, t: jnp.concatenate([l, a, t], axis=1), lb, a_, akv)
        d4 = sp(each(_mm3, d2, d2))
        x = each(lambda d, t: t + _mm3(d, _Split(t)), d1, x)
        d8 = sp(each(_mm3, d4, d4))
        x = each(lambda d, t: t + _mm3(d, _Split(t)), d2, x)
        x = each(lambda d, t: t + _mm3(d, _Split(t)), d4, x)
        x = each(lambda d, t: t + _mm3(d, _Split(t)), d8, x)
        e1 = sp([t[:, :ch] for t in x])
        wu = [t[:, ch:] for t in x]
        e2 = sp(each(_mm3, e1, e1))
        wu = each(lambda e, t: t + _mm3(e, _Split(t)), e1, wu)
        wu = sp(each(lambda e, t: t + _mm3(e, _Split(t)), e2, wu))
        ry = each(_mm3, a_rb, wu)
        rkv = each(_mm3, a_rk, vs)
        mn = each(lambda b, g, t: _mm3_tn(_Split(b * g), t), b_, g_end, wu)
        kv = each(lambda k, g, t: _mm3_tn(_Split(k * g), t), k_, g_end, vs)
        rq = each(lambda r, t: r + t[:, :n], r_, ry)
        yl = each(lambda t, w: t[:, n:] + w, ry, rkv)
        for u in range(unroll):
            rq_ref[sls[u], :] = jnp.concatenate([rq[u], rq[unroll + u]], axis=1)
            yl_ref[sls[u], :] = jnp.concatenate([yl[u], yl[unroll + u]], axis=1)
        for i, (h, u) in enumerate(ids):
            m_ref[h, cs[u]] = mn[i][:, :n] + jnp.where(eye, g_end[i], 0.0)
            n_ref[h, cs[u]] = mn[i][:, n:] + kv[i]
        return 0

    lax.fori_loop(0, nchunk // unroll, body, 0)


def rwkv_chunk(at, bt, kt, rt, v, eg, *, batch, nchunk=8, unroll=8):
    t, w = at.shape
    s = t // batch
    ch, n = RW_CHUNK, HEAD_DIM
    pairs = w // LANE
    rows = nchunk * ch
    nr = s // rows
    spec = pl.BlockSpec((rows, LANE), lambda b, p, i: (b * nr + i, p))
    mspec = pl.BlockSpec((2, nchunk, n, n), lambda b, p, i: (b * pairs + p, i, 0, 0))
    return pl.pallas_call(
        functools.partial(_rwkv_chunk_kernel, nchunk=nchunk, unroll=unroll),
        grid=(batch, pairs, nr),
        in_specs=[spec] * 6,
        out_specs=[mspec, mspec, spec, spec],
        out_shape=[jax.ShapeDtypeStruct((batch * w // n, s // ch, n, n), F32)] * 2
        + [jax.ShapeDtypeStruct((t, w), F32)] * 2,
        compiler_params=_cp("parallel", "parallel", "parallel"),
        name="rwkv_chunk",
    )(at, bt, kt, rt, v, eg)


def _rwkv_scan_kernel(m_ref, n_ref, s_ref, st_ref, *, heads, nchunk):
    @pl.when(pl.program_id(1) == 0)
    def _():
        st_ref[...] = jnp.zeros_like(st_ref)

    def body(c, _):
        sts = [st_ref[hh] for hh in range(heads)]
        for hh in range(heads):
            s_ref[hh, c] = sts[hh]
        ms = [_Split(m_ref[hh, c]) for hh in range(heads)]
        ss = [_Split(st) for st in sts]
        new = [_mm3(a, b) for a, b in zip(ms, ss)]
        for hh in range(heads):
            st_ref[hh] = new[hh] + n_ref[hh, c]
        return 0

    lax.fori_loop(0, nchunk, body, 0)


def rwkv_scan(m, nn, *, heads=16, nchunk=8):
    bh, nc, n, _ = m.shape
    heads = min(heads, bh)
    nchunk = min(nchunk, nc)
    spec = pl.BlockSpec((heads, nchunk, n, n), lambda b, i: (b, i, 0, 0))
    return pl.pallas_call(
        functools.partial(_rwkv_scan_kernel, heads=heads, nchunk=nchunk),
        grid=(bh // heads, nc // nchunk),
        in_specs=[spec, spec],
        out_specs=spec,
        out_shape=jax.ShapeDtypeStruct((bh, nc, n, n), F32),
        scratch_shapes=[pltpu.VMEM((heads, n, n), F32)],
        compiler_params=_cp("parallel", "arbitrary"),
        name="rwkv_scan",
    )(m, nn)


def _rwkv_out_kernel(rq_ref, yl_ref, s_ref, lw_ref, lb_ref, avg_ref, o_ref, *, nchunk):
    ch, n = RW_CHUNK, HEAD_DIM
    ys = []
    for h in range(2):
        cols = slice(h * n, (h + 1) * n)
        ys.append(jnp.concatenate(
            [_mm3(_Split(rq_ref[c * ch:(c + 1) * ch, cols]), _Split(s_ref[h, c])) for c in range(nchunk)],
            axis=0))
    y = jnp.concatenate(ys, axis=1) + yl_ref[...]
    avg2 = jnp.concatenate([avg_ref[...]] * 2, axis=0)

    def mean(t):
        ts = _Split(t)
        return _dot(jnp.concatenate([ts.hi, ts.lo], axis=1), avg2)

    d = y - mean(y)
    var = mean(d * d)
    o_ref[...] = d * lax.rsqrt(var + GN_EPS) * lw_ref[...] + lb_ref[...]


def rwkv_out(rq, yl, s0, ln_w, ln_b, *, batch, nchunk=8):
    t, w = rq.shape
    s = t // batch
    n = HEAD_DIM
    pairs = w // LANE
    rows = nchunk * RW_CHUNK
    nr = s // rows
    li = jnp.arange(LANE) // n
    avg = ((li[:, None] == li[None, :]).astype(F32) / n).astype(BF16)
    spec = pl.BlockSpec((rows, LANE), lambda b, p, i: (b * nr + i, p))
    hspec = pl.BlockSpec((1, LANE), lambda b, p, i: (0, p))
    return pl.pallas_call(
        functools.partial(_rwkv_out_kernel, nchunk=nchunk),
        grid=(batch, pairs, nr),
        in_specs=[spec, spec, pl.BlockSpec((2, nchunk, n, n), lambda b, p, i: (b * pairs + p, i, 0, 0)),
                  hspec, hspec, pl.BlockSpec((LANE, LANE), lambda b, p, i: (0, 0))],
        out_specs=spec,
        out_shape=jax.ShapeDtypeStruct((t, w), F32),
        compiler_params=_cp("parallel", "parallel", "parallel"),
        name="rwkv_out",
    )(rq, yl, s0, ln_w.reshape(1, w), ln_b.reshape(1, w), avg)


def _hyb_out_kernel(oa_ref, y_ref, bv_ref, g_ref, wa_ref, wb_ref, r_ref, o_ref):
    ob = (y_ref[...] + bv_ref[...]) * g_ref[...]
    o_ref[...] = (r_ref[...] + _dot(oa_ref[...].astype(BF16), wa_ref[...])
                  + _dot(ob.astype(BF16), wb_ref[...]))


def hyb_out(res, oa, y, bv, g, wa, wb, *, tm=1024):
    t, d = res.shape
    row = lambda i: (i, 0)
    fix = lambda i: (0, 0)
    act = lambda a: pl.BlockSpec((tm, a.shape[1]), row)
    return pl.pallas_call(
        _hyb_out_kernel,
        grid=(t // tm,),
        in_specs=[act(oa), act(y), act(bv), act(g), pl.BlockSpec(wa.shape, fix),
                  pl.BlockSpec(wb.shape, fix), pl.BlockSpec((tm, d), row)],
        out_specs=pl.BlockSpec((tm, d), row),
        out_shape=jax.ShapeDtypeStruct((t, d), F32),
        compiler_params=_cp("parallel"),
        name="hyb_out",
    )(oa, y, bv, g, wa, wb, res)


def _pad_cols(w, n):
    return jnp.pad(w, ((0, 0), (0, n - w.shape[1])))


def sb_rwkv_layer(h, b, s, attn_norm, w_in, w_out, mu, w0, w2, a0, a2, g2, k_k, k_a, r_k, ln_w, ln_b):
    sbw = SB_HEADS * HEAD_DIM
    rww = RW_HEADS * HEAD_DIM
    rw_in = w_in.shape[1] - 3 * sbw
    rw_pad = -(-rw_in // LANE) * LANE
    dl, al = w2.shape[0], a2.shape[0]
    gl = g2.shape[0]
    assert dl + al == LANE and 3 * rww % LANE == 0

    qkv = rms_matmul(h, attn_norm, w_in[:, :3 * sbw].astype(BF16), out_dtype=BF16)
    rw = rms_matmul(h, attn_norm, _pad_cols(w_in[:, 3 * sbw:], rw_pad).astype(BF16))
    oa = sb_attention(qkv, batch=b, heads=SB_HEADS)

    w2p = jnp.zeros((LANE, rww), F32).at[:dl].set(w2)
    a2p = jnp.zeros((LANE, rww), F32).at[dl:].set(a2)
    g2p = jnp.zeros((rw_pad - 3 * rww - LANE, rww), F32).at[:gl].set(g2)
    vec = lambda t: t.reshape(1, rww).astype(F32)
    mu_p = _pad_cols(mu.reshape(1, rw_in), rw_pad)
    at, bt, kt, rt, vv, eg, g, bv = rwkv_prep(rw, mu_p, vec(w0), vec(a0), vec(k_k), vec(k_a), vec(r_k),
                                              w2p, a2p, g2p, seq=s, width=rww)
    m, nn, rq, yl = rwkv_chunk(at, bt, kt, rt, vv, eg, batch=b)
    s0 = rwkv_scan(m, nn)
    y = rwkv_out(rq, yl, s0, ln_w, ln_b, batch=b)
    return hyb_out(h, oa, y, bv, g, w_out[:sbw].astype(BF16), w_out[sbw:].astype(BF16))


def mla_layer(h, b, s, positions, attn_norm, w_down, q_norm, kv_norm, w_uq, w_ukv, w_o):
    nh, dn, dr, dv = MLA_HEADS, MLA_NOPE, MLA_ROPE, MLA_V
    qr, kvr = MLA_Q_RANK, MLA_KV_RANK
    assert dn == dv and qr % LANE == 0 and (qr + LANE) % kvr == 0
    wd = jnp.concatenate([w_down[:, :qr], jnp.tile(w_down[:, qr + kvr:], (1, LANE // dr)),
                          w_down[:, qr:qr + kvr]], axis=1)
    c = rms_matmul(h, attn_norm, wd.astype(BF16))
    wq = w_uq.reshape(qr, nh, dn + dr)
    wq = jnp.concatenate([wq[:, :, :dn].reshape(qr, nh * dn), wq[:, :, dn:].reshape(qr, nh * dr)], axis=1)
    q = rms_matmul(c, q_norm, wq.astype(BF16), x_col=0)
    wkv = w_ukv.reshape(kvr, nh, dn + dv)
    wkv = jnp.concatenate([wkv[:, :, :dn].reshape(kvr, nh * dn), wkv[:, :, dn:].reshape(kvr, nh * dv)], axis=1)
    kv = rms_matmul(c, kv_norm, wkv.astype(BF16), x_col=(qr + LANE) // kvr, out_dtype=BF16)
    qrot, krot = rope(q, c, positions, q_col=dn // dr, qw=nh * dr, k_col=qr // LANE)
    o = softmax_attention(q, qrot, kv, krot, 1.0 / math.sqrt(dn + dr), batch=b, heads=nh)
    return matmul_residual(h, o, w_o.astype(BF16))


def kernel(x, p, positions, attn_norm, ffn_norm, ffn_w_in, ffn_conv_w, ffn_conv_b, ffn_w_out, ple_w_proj, ple_norm, ple_gate_norm, ple_w_gate, hyb_w_in, hyb_w_out, rw_mu, rw_w0, rw_w2, rw_a0, rw_a2, rw_g2, rw_k_k, rw_k_a, rw_r_k, rw_ln_w, rw_ln_b, mla_w_down, mla_q_norm, mla_kv_norm, mla_w_uq, mla_w_ukv, mla_w_o, final_norm):
    b, s, d = x.shape
    depth = p.shape[0]
    h = x.reshape(b * s, d)
    pos = positions.reshape(b * s)
    for i in range(depth):
        j = i // 2
        if i % 2 == 0:
            h = sb_rwkv_layer(h, b, s, attn_norm[i], hyb_w_in[j], hyb_w_out[j], rw_mu[j], rw_w0[j],
                              rw_w2[j], rw_a0[j], rw_a2[j], rw_g2[j], rw_k_k[j], rw_k_a[j], rw_r_k[j],
                              rw_ln_w[j], rw_ln_b[j])
        else:
            h = mla_layer(h, b, s, pos, attn_norm[i], mla_w_down[j], mla_q_norm[j], mla_kv_norm[j],
                          mla_w_uq[j], mla_w_ukv[j], mla_w_o[j])
        ug = rms_matmul(h, ffn_norm[i], ffn_w_in[i].astype(BF16), out_dtype=BF16)
        h = convglu_out(h, ug, ffn_conv_w[i], ffn_conv_b[i], ffn_w_out[i], seq=s)
        h = ple(h, p[i].reshape(b * s, -1), ple_w_proj[i].astype(BF16), ple_norm[i], ple_gate_norm[i],
                ple_w_gate[i].astype(BF16), final_norm, final=(i == depth - 1))
    return h.reshape(b, s, d)
```

```python
import functools
import math

import jax
import jax.numpy as jnp
from jax import lax
from jax.experimental import pallas as pl
from jax.experimental.pallas import tpu as pltpu

F32 = jnp.float32
BF16 = jnp.bfloat16
HI = lax.Precision.HIGHEST

NORM_EPS = 1e-6
GN_EPS = 64e-5
ROPE_THETA = 10000.0
LANE = 128
VMEM_LIMIT = 48 * 1024 * 1024
HALO = 16
MXU_WIDTH = 256
TILE_VMEM_BUDGET = 36 * 1024 * 1024
WEIGHT_TILE_BYTES = 6 * 1024 * 1024

SB_HEADS = 8
RW_HEADS = 8
HEAD_DIM = 64
MLA_HEADS = 16
MLA_NOPE = 64
MLA_ROPE = 32
MLA_V = 64
MLA_Q_RANK = 384
MLA_KV_RANK = 256
SB_CUT = -151.0
RW_CHUNK = 64
RW_SUB = 16


def _cp(*sem):
    return pltpu.CompilerParams(dimension_semantics=sem, vmem_limit_bytes=VMEM_LIMIT)


def _iota(shape, dim):
    return lax.broadcasted_iota(jnp.int32, shape, dim)


def _softplus(z):
    return jnp.maximum(z, 0.0) + jnp.log1p(jnp.exp(-jnp.abs(z)))


def _sigmoid(z):
    return 1.0 / (1.0 + jnp.exp(-z))


def _dot(a, b, precision=None):
    return jnp.dot(a, b, preferred_element_type=F32, precision=precision)


def _dot_nt(a, b, precision=None):
    return lax.dot_general(a, b, (((1,), (1,)), ((), ())), preferred_element_type=F32,
                           precision=precision)


def _dot_tn(a, b, precision=None):
    return lax.dot_general(a, b, (((0,), (0,)), ((), ())), preferred_element_type=F32,
                           precision=precision)


class _Split:
    def __init__(self, x):
        self.x = x
        self.hi = x.astype(BF16)
        self.lo = (x - self.hi.astype(F32)).astype(BF16)
        self._packed = {}

    def left(self, axis):
        key = ("l", axis)
        if key not in self._packed:
            self._packed[key] = jnp.concatenate([self.hi, self.lo, self.hi], axis=axis)
        return self._packed[key]

    def right(self, axis):
        key = ("r", axis)
        if key not in self._packed:
            self._packed[key] = jnp.concatenate([self.hi, self.hi, self.lo], axis=axis)
        return self._packed[key]


def _mm3(a, b):
    return _dot(a.left(1), b.right(0))


def _mm3_nt(a, b):
    return _dot_nt(a.left(1), b.right(1))


def _mm3_tn(a, b):
    return _dot_tn(a.left(0), b.right(0))


def _rms(x, g, eps=NORM_EPS):
    return x * lax.rsqrt(jnp.mean(x * x, axis=-1, keepdims=True) + eps) * g


def _rms_matmul_kernel(x_ref, g_ref, w_ref, o_ref, xn_ref):
    @pl.when(pl.program_id(1) == 0)
    def _():
        xn_ref[...] = _rms(x_ref[...].astype(F32), g_ref[...]).astype(BF16)

    o_ref[...] = _dot(xn_ref[...], w_ref[...]).astype(o_ref.dtype)


def _pick_tiles(t, k, n, tile_bytes):
    tn = max(c for c in range(LANE, n + 1, LANE)
             if n % c == 0 and (c == n or c % MXU_WIDTH == 0) and k * c * 2 <= WEIGHT_TILE_BYTES)
    for tm in (2048, 1024, 512, 256):
        if t % tm == 0 and tile_bytes(tm, tn) + 2 * k * tn * 2 <= TILE_VMEM_BUDGET:
            return tm, tn
    raise ValueError("no row tile fits")


def rms_matmul(x, g, w, *, x_col=0, out_dtype=F32):
    t = x.shape[0]
    k, n = w.shape
    ob = jnp.dtype(out_dtype).itemsize
    tm, tn = _pick_tiles(t, k, n, lambda tm, tn: 2 * tm * k * x.dtype.itemsize + tm * k * 2 + 2 * tm * tn * ob)
    return pl.pallas_call(
        _rms_matmul_kernel,
        grid=(t // tm, n // tn),
        in_specs=[pl.BlockSpec((tm, k), lambda i, j: (i, x_col)),
                  pl.BlockSpec((1, k), lambda i, j: (0, 0)),
                  pl.BlockSpec((k, tn), lambda i, j: (0, j))],
        out_specs=pl.BlockSpec((tm, tn), lambda i, j: (i, j)),
        out_shape=jax.ShapeDtypeStruct((t, n), out_dtype),
        scratch_shapes=[pltpu.VMEM((tm, k), BF16)],
        compiler_params=_cp("parallel", "arbitrary"),
        name="rms_matmul",
    )(x, g.reshape(1, k).astype(F32), w)


def _matmul_res_kernel(a_ref, w_ref, r_ref, o_ref):
    o_ref[...] = r_ref[...] + _dot(a_ref[...].astype(BF16), w_ref[...])


def matmul_residual(res, a, w):
    t, k = a.shape
    n = w.shape[1]
    tm, tn = _pick_tiles(t, k, n, lambda tm, tn: 2 * tm * k * a.dtype.itemsize + 4 * tm * tn * 4)
    return pl.pallas_call(
        _matmul_res_kernel,
        grid=(t // tm, n // tn),
        in_specs=[pl.BlockSpec((tm, k), lambda i, j: (i, 0)),
                  pl.BlockSpec((k, tn), lambda i, j: (0, j)),
                  pl.BlockSpec((tm, tn), lambda i, j: (i, j))],
        out_specs=pl.BlockSpec((tm, tn), lambda i, j: (i, j)),
        out_shape=jax.ShapeDtypeStruct((t, n), F32),
        compiler_params=_cp("parallel", "arbitrary"),
        name="matmul_residual",
    )(a, w, res)


def _convglu_ple_kernel(u_ref, g_ref, gp_ref, cw_ref, cb_ref, w_ref, r_ref,
                        p_ref, wp_ref, pn_ref, gn_ref, wg_ref, fn_ref, o_ref, *, tm, seq, final):
    i = pl.program_id(0)
    g = g_ref[...].astype(F32)
    at_start = (i * tm) % seq == 0
    halo = jnp.where(at_start, 0.0, gp_ref[...].astype(F32))
    h1, h2 = halo[HALO - 1:HALO, :], halo[HALO - 2:HALO - 1, :]
    r1, r2 = pltpu.roll(g, 1, axis=0), pltpu.roll(g, 2, axis=0)
    top = _iota((8, g.shape[1]), 0)
    p1 = jnp.concatenate([jnp.where(top == 0, h1, r1[:8]), r1[8:]], axis=0)
    p2 = jnp.concatenate([jnp.where(top == 0, h2, jnp.where(top == 1, h1, r2[:8])), r2[8:]], axis=0)
    c = cw_ref[0:1, :] * p2 + cw_ref[1:2, :] * p1 + cw_ref[2:3, :] * g + cb_ref[...]
    act = c * (1.0 + lax.erf(c)) * u_ref[...].astype(F32)
    h = r_ref[...] + _dot(act.astype(BF16), w_ref[...])
    e = _rms(_dot(p_ref[...].astype(BF16), wp_ref[...]), pn_ref[...])
    gate = _sigmoid(_dot(_rms(h, gn_ref[...]).astype(BF16), wg_ref[...]))
    out = h + gate * e
    if final:
        out = _rms(out, fn_ref[...])
    o_ref[...] = out


def convglu_ple(res, ug, conv_w, conv_b, w_out, p, w_proj, p_norm, g_norm, w_gate, f_norm,
                *, seq, final, tm=256):
    t, d = res.shape
    f = w_out.shape[0]
    pd = p.shape[1]
    hb = tm // HALO
    rs2 = 1.0 / math.sqrt(2.0)
    row = lambda i: (i, 0)
    fix = lambda i: (0, 0)
    vec = pl.BlockSpec((1, d), fix)
    return pl.pallas_call(
        functools.partial(_convglu_ple_kernel, tm=tm, seq=seq, final=final),
        grid=(t // tm,),
        in_specs=[pl.BlockSpec((tm, f), row),
                  pl.BlockSpec((tm, f), lambda i: (i, 1)),
                  pl.BlockSpec((HALO, f), lambda i: (jnp.maximum(i * hb - 1, 0), 1)),
                  pl.BlockSpec((3, f), fix), pl.BlockSpec((1, f), fix), pl.BlockSpec((f, d), fix),
                  pl.BlockSpec((tm, d), row), pl.BlockSpec((tm, pd), row), pl.BlockSpec((pd, d), fix),
                  vec, vec, pl.BlockSpec((d, d), fix), vec],
        out_specs=pl.BlockSpec((tm, d), row),
        out_shape=jax.ShapeDtypeStruct((t, d), F32),
        compiler_params=_cp("parallel"),
        name="convglu_ple",
    )(ug, ug, ug, conv_w * rs2, (conv_b * rs2).reshape(1, f), (w_out * rs2).astype(BF16), res,
      p, w_proj.astype(BF16), p_norm.reshape(1, d), g_norm.reshape(1, d), w_gate.astype(BF16),
      f_norm.reshape(1, d))


def _sb_attn_kernel(q_ref, k_ref, v_ref, u_ref, o_ref, *, tq, tk, nh, scale):
    qi = pl.program_id(2)
    nblk = tq // tk
    width = nh * HEAD_DIM
    qf = q_ref[...] * scale
    qlane = _iota((tq, width), 1) // HEAD_DIM
    qs = jnp.concatenate([jnp.where(qlane == h, qf, jnp.zeros_like(qf)) for h in range(nh)], axis=0)
    vlane = _iota((tk, width), 1) // HEAD_DIM
    rows = lambda x, h: x[h * tq:(h + 1) * tq]

    def sweep(k_lo, run, acc, masked):
        parts = []
        for d in reversed(range(nblk)):
            k0 = k_lo + d * tk
            t = _dot_nt(qs, k_ref[pl.ds(k0, tk), :]) * math.log2(math.e)
            sp = jnp.maximum(t, 0.0) + jnp.log2(1.0 + jnp.exp2(-jnp.abs(t)))
            lom = -sp
            valid = None
            if masked:
                valid = _iota((nh * tq, tk), 1) + d * tk < _iota((nh * tq, tk), 0) % tq
                lom = jnp.where(valid, lom, 0.0)
            hi = lom.astype(BF16)
            lo = (lom - hi.astype(F32)).astype(BF16)
            sums = _dot(jnp.concatenate([hi, lo], axis=1), u_ref[...])
            parts.append((k0, t - sp, sums, valid))
        for k0, logsig, sums, valid in parts:
            w = jnp.exp2(logsig + run + sums[:, :tk])
            if masked:
                w = jnp.where(valid, w, 0.0)
            w = w.astype(BF16)
            vb = v_ref[pl.ds(k0, tk), :]
            wcat = jnp.concatenate([rows(w, h) for h in range(nh)], axis=1)
            vcat = jnp.concatenate([jnp.where(vlane == h, vb, jnp.zeros_like(vb)) for h in range(nh)], axis=0)
            acc = acc + _dot(wcat, vcat)
            run = run + sums[:, tk:]
        return run, acc

    run = jnp.zeros((nh * tq, tk), F32)
    acc = jnp.zeros((tq, width), F32)
    run, acc = sweep(pl.multiple_of(qi * tq, tq), run, acc, True)

    def live(run):
        return jnp.max(run) > SB_CUT

    def cond(c):
        return c[0]

    def body(c):
        _, j, run, acc = c
        run, acc = sweep(pl.multiple_of(j * tq, tq), run, acc, False)
        return (j > 0) & live(run), j - 1, run, acc

    _, _, _, acc = lax.while_loop(cond, body, ((qi > 0) & live(run), qi - 1, run, acc))
    o_ref[...] = acc


def sb_attention(qkv, *, batch, heads, tq=256, nh=4):
    t, w3 = qkv.shape
    s = t // batch
    nq = s // tq
    width = nh * HEAD_DIM
    groups = heads // nh
    tk = LANE
    jj = jnp.arange(2 * tk)[:, None] % tk
    ss = jnp.arange(2 * tk)[None, :]
    u = jnp.where(ss < tk, jj > ss, True).astype(BF16)
    return pl.pallas_call(
        functools.partial(_sb_attn_kernel, tq=tq, tk=tk, nh=nh, scale=1.0 / math.sqrt(HEAD_DIM)),
        grid=(batch, groups, nq),
        in_specs=[pl.BlockSpec((tq, width), lambda b, p, i: (b * nq + i, p)),
                  pl.BlockSpec((s, width), lambda b, p, i: (b, groups + p)),
                  pl.BlockSpec((s, width), lambda b, p, i: (b, 2 * groups + p)),
                  pl.BlockSpec((2 * tk, 2 * tk), lambda b, p, i: (0, 0))],
        out_specs=pl.BlockSpec((tq, width), lambda b, p, i: (b * nq + i, p)),
        out_shape=jax.ShapeDtypeStruct((t, w3 // 3), F32),
        compiler_params=_cp("parallel", "parallel", "arbitrary"),
        name="sb_attention",
    )(qkv, qkv, qkv, u)


def _softmax_attn_kernel(qn_ref, qr_ref, kn_ref, kr_ref, v_ref, o_ref, *bufs, tq, tk, npair, scale):
    qi = pl.program_id(2)
    g = pl.program_id(1)
    nh = 2 * npair
    lane = _iota((tq, LANE), 1)
    qr = qr_ref[...] * (scale * math.log2(math.e))
    slots = LANE // MLA_ROPE
    qs = []
    for h in range(nh):
        cols = slice(h // 2 * LANE, (h // 2 + 1) * LANE)
        qn = qn_ref[:, cols] * (scale * math.log2(math.e))
        nope = jnp.where(lane // MLA_NOPE == h % 2, qn, 0.0)
        rope_ = jnp.where(lane // MLA_ROPE == (g * nh + h) % slots, qr, 0.0)
        qs.append(jnp.concatenate([nope, rope_], axis=1).astype(BF16))

    def scores(j, dsts):
        k0 = pl.multiple_of(j * tk, tk)
        mx = []
        for h in range(nh):
            cols = slice(h // 2 * LANE, (h // 2 + 1) * LANE)
            kb = jnp.concatenate([kn_ref[pl.ds(k0, tk), cols], kr_ref[pl.ds(k0, tk), :]], axis=1)
            s = _dot_nt(qs[h], kb)
            dsts[h][...] = s
            mx.append(jnp.max(s, axis=1, keepdims=True))
        return tuple(mx)

    def consume(srcs, j, carry, mx):
        out = []
        for h in range(nh):
            cols = slice(h // 2 * LANE, (h // 2 + 1) * LANE)
            vb = v_ref[pl.ds(pl.multiple_of(j * tk, tk), tk), cols]
            mine = (_iota(vb.shape, 1) // MLA_V) == h % 2
            vbh = jnp.where(mine, vb, jnp.ones_like(vb))
            m, acc = carry[h]
            s = srcs[h][...]
            if mx is None:
                s = jnp.where(_iota((tq, tk), 1) <= _iota((tq, tk), 0), s, -1e30)
                m_new = jnp.maximum(m, jnp.max(s, axis=1, keepdims=True))
            else:
                m_new = jnp.maximum(m, mx[h])
            p = jnp.exp2(s - m_new)
            acc = jnp.exp2(m - m_new) * acc + _dot(p.astype(BF16), vbh)
            out.append((m_new, acc))
        return tuple(out)

    sa, sb = bufs[:nh], bufs[nh:]

    def pair(t, carry):
        state, mxa = carry
        mxb = scores(2 * t + 1, sb)
        state = consume(sa, 2 * t, state, mxa)
        mxa = scores(2 * t + 2, sa)
        return consume(sb, 2 * t + 1, state, mxb), mxa

    def tail_odd(carry):
        state, mxa = carry
        scores(qi, sb)
        state = consume(sa, qi - 1, state, mxa)
        return consume(sb, qi, state, None)

    def tail_even(carry):
        return consume(sa, qi, carry[0], None)

    init = ((jnp.full((tq, 1), -1e30, F32), jnp.zeros((tq, LANE), F32)),) * nh
    carry = lax.fori_loop(0, qi // 2, pair, (init, scores(0, sa)))
    state = lax.cond(qi % 2 == 1, tail_odd, tail_even, carry)
    for pr in range(npair):
        acc0, acc1 = state[2 * pr][1], state[2 * pr + 1][1]
        o_ref[:, pr * LANE:(pr + 1) * LANE] = jnp.where(
            lane < MLA_V, acc0 / pltpu.roll(acc0, MLA_V, axis=1), acc1 / pltpu.roll(acc1, MLA_V, axis=1))


def softmax_attention(q, qrot, kv, krot, scale, *, batch, heads, tq=512, npair=2):
    tk = tq
    t = q.shape[0]
    s = t // batch
    nq = s // tq
    width = npair * LANE
    groups = heads * MLA_NOPE // width
    rope_per_group = LANE // (2 * npair * MLA_ROPE)
    buf = pltpu.VMEM((tq, tk), F32)
    return pl.pallas_call(
        functools.partial(_softmax_attn_kernel, tq=tq, tk=tk, npair=npair, scale=scale),
        grid=(batch, groups, nq),
        in_specs=[pl.BlockSpec((tq, width), lambda b, p, i: (b * nq + i, p)),
                  pl.BlockSpec((tq, LANE), lambda b, p, i: (b * nq + i, p // rope_per_group)),
                  pl.BlockSpec((s, width), lambda b, p, i: (b, p)),
                  pl.BlockSpec((s, LANE), lambda b, p, i: (b, 0)),
                  pl.BlockSpec((s, width), lambda b, p, i: (b, groups + p))],
        out_specs=pl.BlockSpec((tq, width), lambda b, p, i: (b * nq + i, p)),
        out_shape=jax.ShapeDtypeStruct((t, heads * MLA_V), F32),
        scratch_shapes=[buf] * (4 * npair),
        compiler_params=_cp("parallel", "parallel", "arbitrary"),
        name="softmax_attention",
    )(q, qrot, kv, krot, kv)


def _rope_kernel(q_ref, k_ref, pos_ref, f_ref, oq_ref, ok_ref):
    half = MLA_ROPE // 2
    ang = pos_ref[...].astype(F32) * f_ref[...]
    cos, sin = jnp.cos(ang), jnp.sin(ang)
    low = _iota(ang.shape, 1) % MLA_ROPE < half

    def rot(x):
        partner = jnp.where(low, -pltpu.roll(x, LANE - half, axis=1), pltpu.roll(x, half, axis=1))
        return x * cos + partner * sin

    for c in range(q_ref.shape[1] // LANE):
        cols = slice(c * LANE, (c + 1) * LANE)
        oq_ref[:, cols] = rot(q_ref[:, cols])
    ok_ref[...] = rot(k_ref[...]).astype(ok_ref.dtype)


def rope(q, c, positions, *, q_col, qw, k_col, tm=512):
    t = q.shape[0]
    half = MLA_ROPE // 2
    inv_freq = ROPE_THETA ** (-jnp.arange(half, dtype=F32) / half)
    freq = jnp.tile(inv_freq, LANE // half).reshape(1, LANE)
    return pl.pallas_call(
        _rope_kernel,
        grid=(t // tm,),
        in_specs=[pl.BlockSpec((tm, qw), lambda i: (i, q_col)), pl.BlockSpec((tm, LANE), lambda i: (i, k_col)),
                  pl.BlockSpec((tm, 1), lambda i: (i, 0)), pl.BlockSpec((1, LANE), lambda i: (0, 0))],
        out_specs=[pl.BlockSpec((tm, qw), lambda i: (i, 0)), pl.BlockSpec((tm, LANE), lambda i: (i, 0))],
        out_shape=[jax.ShapeDtypeStruct((t, qw), F32), jax.ShapeDtypeStruct((t, LANE), BF16)],
        compiler_params=_cp("parallel"),
        name="rope",
    )(q, c, positions.reshape(t, 1), freq)


def _rwkv_prep_kernel(x_ref, xp_ref, mu_ref, w0_ref, a0_ref, kk_ref, ka_ref, rk_ref,
                      w2_ref, a2_ref, g2_ref, tri_ref, hs_ref,
                      at_ref, bt_ref, kt_ref, rt_ref, v_ref, eg_ref, g_ref, bv_ref, *, tm, seq, width):
    i = pl.program_id(0)
    x = x_ref[...]
    at_start = (i * tm) % seq == 0
    prev_row = jnp.where(at_start, 0.0, xp_ref[7:8, :])
    rolled = pltpu.roll(x, 1, axis=0)
    top = _iota((8, x.shape[1]), 0)
    prev = jnp.concatenate([jnp.where(top == 0, prev_row, rolled[:8]), rolled[8:]], axis=0)
    xm = x + (prev - x) * mu_ref[...]
    r = xm[:, :width]
    k = xm[:, width:2 * width]
    v = xm[:, 2 * width:3 * width]
    l1 = xm[:, 3 * width:3 * width + LANE]
    l2 = xm[:, 3 * width + LANE:]
    wpre = w0_ref[...] + _dot(_Split(jnp.tanh(l1)).left(1), w2_ref[...])
    a = _sigmoid(a0_ref[...] + _dot(_Split(l1).left(1), a2_ref[...]))
    g = _dot(_Split(_sigmoid(l2)).left(1), g2_ref[...])

    def hilo(t):
        ts = _Split(t)
        return jnp.concatenate([ts.hi, ts.lo], axis=1)

    lw = -jnp.exp(-_softplus(-wpre) - 0.5)
    hs = hs_ref[...]
    kk = k * kk_ref[...]
    kk = kk * lax.rsqrt(jnp.maximum(_dot(hilo(kk * kk), hs), 1e-24))
    km = k * (1.0 + (a - 1.0) * ka_ref[...])
    bonus = _dot(hilo(r * km * rk_ref[...]), hs)
    lws = _Split(lw)
    cum = _dot(tri_ref[...], jnp.concatenate([lws.hi, lws.lo], axis=0))
    eg = jnp.exp(cum)
    ieg = jnp.exp(-cum)
    at_ref[...] = -kk * jnp.exp(cum - lw)
    bt_ref[...] = kk * a * ieg
    kt_ref[...] = km * ieg
    rt_ref[...] = r * eg
    v_ref[...] = v
    eg_ref[...] = eg
    g_ref[...] = g
    bv_ref[...] = bonus * v


def rwkv_prep(x, mu, w0, a0, k_k, k_a, r_k, w2p, a2p, g2p, *, seq, width, tm=256):
    t, wx = x.shape
    ch = RW_CHUNK
    ti = jnp.arange(tm)
    tri = ((ti[:, None] // ch == ti[None, :] // ch) & (ti[None, :] <= ti[:, None])).astype(BF16)
    tri = jnp.concatenate([tri, tri], axis=1)
    ci = jnp.arange(width) // HEAD_DIM
    hs = (ci[:, None] == ci[None, :]).astype(BF16)
    hs = jnp.concatenate([hs, hs], axis=0)
    w2p, a2p, g2p = (_Split(w).right(0) for w in (w2p, a2p, g2p))
    hb = tm // 8
    row = lambda i: (i, 0)
    fix = lambda i: (0, 0)
    vec = pl.BlockSpec((1, width), fix)
    out = jax.ShapeDtypeStruct((t, width), F32)
    return pl.pallas_call(
        functools.partial(_rwkv_prep_kernel, tm=tm, seq=seq, width=width),
        grid=(t // tm,),
        in_specs=[pl.BlockSpec((tm, wx), row),
                  pl.BlockSpec((8, wx), lambda i: (jnp.maximum(i * hb - 1, 0), 0)),
                  pl.BlockSpec((1, wx), fix), vec, vec, vec, vec, vec,
                  pl.BlockSpec(w2p.shape, fix), pl.BlockSpec(a2p.shape, fix),
                  pl.BlockSpec(g2p.shape, fix), pl.BlockSpec(tri.shape, fix),
                  pl.BlockSpec(hs.shape, fix)],
        out_specs=[pl.BlockSpec((tm, width), row)] * 8,
        out_shape=[out] * 8,
        compiler_params=_cp("parallel"),
        name="rwkv_prep",
    )(x, x, mu, w0, a0, k_k, k_a, r_k, w2p, a2p, g2p, tri, hs)


def _rwkv_chunk_kernel(at_ref, bt_ref, kt_ref, rt_ref, v_ref, eg_ref,
                       m_ref, n_ref, rq_ref, yl_ref, *, nchunk, unroll):
    ch, n = RW_CHUNK, HEAD_DIM
    ti = _iota((ch, ch), 0)
    si = _iota((ch, ch), 1)
    strict = si < ti
    incl = si <= ti
    same = (ti // RW_SUB) == (si // RW_SUB)
    eye = _iota((n, n), 0) == _iota((n, n), 1)
    each = lambda f, *ls: [f(*t) for t in zip(*ls)]

    def body(it, _):
        cs = [it * unroll + u for u in range(unroll)]
        sls = [pl.ds(pl.multiple_of(c * ch, ch), ch) for c in cs]
        ids = [(h, u) for h in range(2) for u in range(unroll)]
        ld = lambda ref: [ref[sls[u], h * n:(h + 1) * n] for h, u in ids]
        sp = lambda xs: [_Split(t) for t in xs]
        a_, b_, k_, r_, v_ = ld(at_ref), ld(bt_ref), ld(kt_ref), ld(rt_ref), ld(v_ref)
        g_end = [eg_ref[pl.ds(cs[u] * ch + ch - 1, 1), h * n:(h + 1) * n] for h, u in ids]
        ar = sp(each(lambda a, r: jnp.concatenate([a, r], axis=0), a_, r_))
        vs = sp(v_)
        pb = each(_mm3_nt, ar, sp(b_))
        pk = each(_mm3_nt, ar, sp(k_))
        a_ab = [jnp.where(strict, t[:ch], 0.0) for t in pb]
        a_ak = sp([jnp.where(strict, t[:ch], 0.0) for t in pk])
        a_rb = sp([jnp.where(incl, t[ch:], 0.0) for t in pb])
        a_rk = sp([jnp.where(incl, t[ch:], 0.0) for t in pk])
        d1 = [jnp.where(same, t, 0.0) for t in a_ab]
        lb = each(lambda t, d: t - d, a_ab, d1)
        d1 = sp(d1)
        d2 = sp(each(_mm3, d1, d1))
        akv = each(_mm3, a_ak, vs)
        x = each(lambda l, a, t: jnp.concatenate([l, a, t], axis=1), lb, a_, akv)
        d4 = sp(each(_mm3, d2, d2))
        x = each(lambda d, t: t + _mm3(d, _Split(t)), d1, x)
        d8 = sp(each(_mm3, d4, d4))
        x = each(lambda d, t: t + _mm3(d, _Split(t)), d2, x)
        x = each(lambda d, t: t + _mm3(d, _Split(t)), d4, x)
        x = each(lambda d, t: t + _mm3(d, _Split(t)), d8, x)
        e1 = sp([t[:, :ch] for t in x])
        wu = [t[:, ch:] for t in x]
        e2 = sp(each(_mm3, e1, e1))
        wu = each(lambda e, t: t + _mm3(e, _Split(t)), e1, wu)
        wu = sp(each(lambda e, t: t + _mm3(e, _Split(t)), e2, wu))
        ry = each(_mm3, a_rb, wu)
        rkv = each(_mm3, a_rk, vs)
        mn = each(lambda b, g, t: _mm3_tn(_Split(b * g), t), b_, g_end, wu)
        kv = each(lambda k, g, t: _mm3_tn(_Split(k * g), t), k_, g_end, vs)
        rq = each(lambda r, t: r + t[:, :n], r_, ry)
        yl = each(lambda t, w: t[:, n:] + w, ry, rkv)
        for u in range(unroll):
            rq_ref[sls[u], :] = jnp.concatenate([rq[u], rq[unroll + u]], axis=1)
            yl_ref[sls[u], :] = jnp.concatenate([yl[u], yl[unroll + u]], axis=1)
        for i, (h, u) in enumerate(ids):
            m_ref[h, cs[u]] = mn[i][:, :n] + jnp.where(eye, g_end[i], 0.0)
            n_ref[h, cs[u]] = mn[i][:, n:] + kv[i]
        return 0

    lax.fori_loop(0, nchunk // unroll, body, 0)


def rwkv_chunk(at, bt, kt, rt, v, eg, *, batch, nchunk=8, unroll=8):
    t, w = at.shape
    s = t // batch
    ch, n = RW_CHUNK, HEAD_DIM
    pairs = w // LANE
    rows = nchunk * ch
    nr = s // rows
    spec = pl.BlockSpec((rows, LANE), lambda b, p, i: (b * nr + i, p))
    mspec = pl.BlockSpec((2, nchunk, n, n), lambda b, p, i: (b * pairs + p, i, 0, 0))
    return pl.pallas_call(
        functools.partial(_rwkv_chunk_kernel, nchunk=nchunk, unroll=unroll),
        grid=(batch, pairs, nr),
        in_specs=[spec] * 6,
        out_specs=[mspec, mspec, spec, spec],
        out_shape=[jax.ShapeDtypeStruct((batch * w // n, s // ch, n, n), F32)] * 2
        + [jax.ShapeDtypeStruct((t, w), F32)] * 2,
        compiler_params=_cp("parallel", "parallel", "parallel"),
        name="rwkv_chunk",
    )(at, bt, kt, rt, v, eg)


def _rwkv_scan_kernel(m_ref, n_ref, s_ref, st_ref, *, heads, nchunk):
    @pl.when(pl.program_id(1) == 0)
    def _():
        st_ref[...] = jnp.zeros_like(st_ref)

    def body(c, _):
        sts = [st_ref[hh] for hh in range(heads)]
        for hh in range(heads):
            s_ref[hh, c] = sts[hh]
        ms = [_Split(m_ref[hh, c]) for hh in range(heads)]
        ss = [_Split(st) for st in sts]
        new = [_mm3(a, b) for a, b in zip(ms, ss)]
        for hh in range(heads):
            st_ref[hh] = new[hh] + n_ref[hh, c]
        return 0

    lax.fori_loop(0, nchunk, body, 0)


def rwkv_scan(m, nn, *, heads=16, nchunk=8):
    bh, nc, n, _ = m.shape
    heads = min(heads, bh)
    nchunk = min(nchunk, nc)
    spec = pl.BlockSpec((heads, nchunk, n, n), lambda b, i: (b, i, 0, 0))
    return pl.pallas_call(
        functools.partial(_rwkv_scan_kernel, heads=heads, nchunk=nchunk),
        grid=(bh // heads, nc // nchunk),
        in_specs=[spec, spec],
        out_specs=spec,
        out_shape=jax.ShapeDtypeStruct((bh, nc, n, n), F32),
        scratch_shapes=[pltpu.VMEM((heads, n, n), F32)],
        compiler_params=_cp("parallel", "arbitrary"),
        name="rwkv_scan",
    )(m, nn)


def _rwkv_out_kernel(rq_ref, yl_ref, s_ref, lw_ref, lb_ref, avg_ref, o_ref, *, nchunk):
    ch, n = RW_CHUNK, HEAD_DIM
    ys = []
    for h in range(2):
        cols = slice(h * n, (h + 1) * n)
        ys.append(jnp.concatenate(
            [_mm3(_Split(rq_ref[c * ch:(c + 1) * ch, cols]), _Split(s_ref[h, c])) for c in range(nchunk)],
            axis=0))
    y = jnp.concatenate(ys, axis=1) + yl_ref[...]
    avg2 = jnp.concatenate([avg_ref[...]] * 2, axis=0)

    def mean(t):
        ts = _Split(t)
        return _dot(jnp.concatenate([ts.hi, ts.lo], axis=1), avg2)

    d = y - mean(y)
    var = mean(d * d)
    o_ref[...] = d * lax.rsqrt(var + GN_EPS) * lw_ref[...] + lb_ref[...]


def rwkv_out(rq, yl, s0, ln_w, ln_b, *, batch, nchunk=8):
    t, w = rq.shape
    s = t // batch
    n = HEAD_DIM
    pairs = w // LANE
    rows = nchunk * RW_CHUNK
    nr = s // rows
    li = jnp.arange(LANE) // n
    avg = ((li[:, None] == li[None, :]).astype(F32) / n).astype(BF16)
    spec = pl.BlockSpec((rows, LANE), lambda b, p, i: (b * nr + i, p))
    hspec = pl.BlockSpec((1, LANE), lambda b, p, i: (0, p))
    return pl.pallas_call(
        functools.partial(_rwkv_out_kernel, nchunk=nchunk),
        grid=(batch, pairs, nr),
        in_specs=[spec, spec, pl.BlockSpec((2, nchunk, n, n), lambda b, p, i: (b * pairs + p, i, 0, 0)),
                  hspec, hspec, pl.BlockSpec((LANE, LANE), lambda b, p, i: (0, 0))],
        out_specs=spec,
        out_shape=jax.ShapeDtypeStruct((t, w), F32),
        compiler_params=_cp("parallel", "parallel", "parallel"),
        name="rwkv_out",
    )(rq, yl, s0, ln_w.reshape(1, w), ln_b.reshape(1, w), avg)


def _hyb_out_kernel(oa_ref, y_ref, bv_ref, g_ref, wa_ref, wb_ref, r_ref, o_ref):
    ob = (y_ref[...] + bv_ref[...]) * g_ref[...]
    o_ref[...] = (r_ref[...] + _dot(oa_ref[...].astype(BF16), wa_ref[...])
                  + _dot(ob.astype(BF16), wb_ref[...]))


def hyb_out(res, oa, y, bv, g, wa, wb, *, tm=1024):
    t, d = res.shape
    row = lambda i: (i, 0)
    fix = lambda i: (0, 0)
    act = lambda a: pl.BlockSpec((tm, a.shape[1]), row)
    return pl.pallas_call(
        _hyb_out_kernel,
        grid=(t // tm,),
        in_specs=[act(oa), act(y), act(bv), act(g), pl.BlockSpec(wa.shape, fix),
                  pl.BlockSpec(wb.shape, fix), pl.BlockSpec((tm, d), row)],
        out_specs=pl.BlockSpec((tm, d), row),
        out_shape=jax.ShapeDtypeStruct((t, d), F32),
        compiler_params=_cp("parallel"),
        name="hyb_out",
    )(oa, y, bv, g, wa, wb, res)


def _pad_cols(w, n):
    return jnp.pad(w, ((0, 0), (0, n - w.shape[1])))


def sb_rwkv_layer(h, b, s, attn_norm, w_in, w_out, mu, w0, w2, a0, a2, g2, k_k, k_a, r_k, ln_w, ln_b):
    sbw = SB_HEADS * HEAD_DIM
    rww = RW_HEADS * HEAD_DIM
    rw_in = w_in.shape[1] - 3 * sbw
    rw_pad = -(-rw_in // LANE) * LANE
    dl, al = w2.shape[0], a2.shape[0]
    gl = g2.shape[0]
    assert dl + al == LANE and 3 * rww % LANE == 0

    qkv = rms_matmul(h, attn_norm, w_in[:, :3 * sbw].astype(BF16), out_dtype=BF16)
    rw = rms_matmul(h, attn_norm, _pad_cols(w_in[:, 3 * sbw:], rw_pad).astype(BF16))
    oa = sb_attention(qkv, batch=b, heads=SB_HEADS)

    w2p = jnp.zeros((LANE, rww), F32).at[:dl].set(w2)
    a2p = jnp.zeros((LANE, rww), F32).at[dl:].set(a2)
    g2p = jnp.zeros((rw_pad - 3 * rww - LANE, rww), F32).at[:gl].set(g2)
    vec = lambda t: t.reshape(1, rww).astype(F32)
    mu_p = _pad_cols(mu.reshape(1, rw_in), rw_pad)
    at, bt, kt, rt, vv, eg, g, bv = rwkv_prep(rw, mu_p, vec(w0), vec(a0), vec(k_k), vec(k_a), vec(r_k),
                                              w2p, a2p, g2p, seq=s, width=rww)
    m, nn, rq, yl = rwkv_chunk(at, bt, kt, rt, vv, eg, batch=b)
    s0 = rwkv_scan(m, nn)
    y = rwkv_out(rq, yl, s0, ln_w, ln_b, batch=b)
    return hyb_out(h, oa, y, bv, g, w_out[:sbw].astype(BF16), w_out[sbw:].astype(BF16))


def mla_layer(h, b, s, positions, attn_norm, w_down, q_norm, kv_norm, w_uq, w_ukv, w_o):
    nh, dn, dr, dv = MLA_HEADS, MLA_NOPE, MLA_ROPE, MLA_V
    qr, kvr = MLA_Q_RANK, MLA_KV_RANK
    assert dn == dv and qr % LANE == 0 and (qr + LANE) % kvr == 0
    wd = jnp.concatenate([w_down[:, :qr], jnp.tile(w_down[:, qr + kvr:], (1, LANE // dr)),
                          w_down[:, qr:qr + kvr]], axis=1)
    c = rms_matmul(h, attn_norm, wd.astype(BF16))
    wq = w_uq.reshape(qr, nh, dn + dr)
    wq = jnp.concatenate([wq[:, :, :dn].reshape(qr, nh * dn), wq[:, :, dn:].reshape(qr, nh * dr)], axis=1)
    q = rms_matmul(c, q_norm, wq.astype(BF16), x_col=0)
    wkv = w_ukv.reshape(kvr, nh, dn + dv)
    wkv = jnp.concatenate([wkv[:, :, :dn].reshape(kvr, nh * dn), wkv[:, :, dn:].reshape(kvr, nh * dv)], axis=1)
    kv = rms_matmul(c, kv_norm, wkv.astype(BF16), x_col=(qr + LANE) // kvr, out_dtype=BF16)
    qrot, krot = rope(q, c, positions, q_col=dn // dr, qw=nh * dr, k_col=qr // LANE)
    o = softmax_attention(q, qrot, kv, krot, 1.0 / math.sqrt(dn + dr), batch=b, heads=nh)
    return matmul_residual(h, o, w_o.astype(BF16))


def kernel(x, p, positions, attn_norm, ffn_norm, ffn_w_in, ffn_conv_w, ffn_conv_b, ffn_w_out, ple_w_proj, ple_norm, ple_gate_norm, ple_w_gate, hyb_w_in, hyb_w_out, rw_mu, rw_w0, rw_w2, rw_a0, rw_a2, rw_g2, rw_k_k, rw_k_a, rw_r_k, rw_ln_w, rw_ln_b, mla_w_down, mla_q_norm, mla_kv_norm, mla_w_uq, mla_w_ukv, mla_w_o, final_norm):
    b, s, d = x.shape
    depth = p.shape[0]
    h = x.reshape(b * s, d)
    pos = positions.reshape(b * s)
    for i in range(depth):
        j = i // 2
        if i % 2 == 0:
            h = sb_rwkv_layer(h, b, s, attn_norm[i], hyb_w_in[j], hyb_w_out[j], rw_mu[j], rw_w0[j],
                              rw_w2[j], rw_a0[j], rw_a2[j], rw_g2[j], rw_k_k[j], rw_k_a[j], rw_r_k[j],
                              rw_ln_w[j], rw_ln_b[j])
        else:
            h = mla_layer(h, b, s, pos, attn_norm[i], mla_w_down[j], mla_q_norm[j], mla_kv_norm[j],
                          mla_w_uq[j], mla_w_ukv[j], mla_w_o[j])
        ug = rms_matmul(h, ffn_norm[i], ffn_w_in[i].astype(BF16), out_dtype=BF16)
        h = convglu_ple(h, ug, ffn_conv_w[i], ffn_conv_b[i], ffn_w_out[i], p[i].reshape(b * s, -1),
                        ple_w_proj[i], ple_norm[i], ple_gate_norm[i], ple_w_gate[i], final_norm,
                        seq=s, final=(i == depth - 1))
    return h.reshape(b, s, d)
```

```python
import functools
import math

import jax
import jax.numpy as jnp
from jax import lax
from jax.experimental import pallas as pl
from jax.experimental.pallas import tpu as pltpu

F32 = jnp.float32
BF16 = jnp.bfloat16
HI = lax.Precision.HIGHEST

NORM_EPS = 1e-6
GN_EPS = 64e-5
ROPE_THETA = 10000.0
LANE = 128
VMEM_LIMIT = 48 * 1024 * 1024
HALO = 16
MXU_WIDTH = 256
TILE_VMEM_BUDGET = 36 * 1024 * 1024
WEIGHT_TILE_BYTES = 6 * 1024 * 1024

SB_HEADS = 8
RW_HEADS = 8
HEAD_DIM = 64
MLA_HEADS = 16
MLA_NOPE = 64
MLA_ROPE = 32
MLA_V = 64
MLA_Q_RANK = 384
MLA_KV_RANK = 256
SB_CUT = -151.0
RW_CHUNK = 64
RW_SUB = 16


def _cp(*sem):
    return pltpu.CompilerParams(dimension_semantics=sem, vmem_limit_bytes=VMEM_LIMIT)


def _iota(shape, dim):
    return lax.broadcasted_iota(jnp.int32, shape, dim)


def _softplus(z):
    return jnp.maximum(z, 0.0) + jnp.log1p(jnp.exp(-jnp.abs(z)))


def _sigmoid(z):
    return 1.0 / (1.0 + jnp.exp(-z))


def _dot(a, b, precision=None):
    return jnp.dot(a, b, preferred_element_type=F32, precision=precision)


def _dot_nt(a, b, precision=None):
    return lax.dot_general(a, b, (((1,), (1,)), ((), ())), preferred_element_type=F32,
                           precision=precision)


def _dot_tn(a, b, precision=None):
    return lax.dot_general(a, b, (((0,), (0,)), ((), ())), preferred_element_type=F32,
                           precision=precision)


class _Split:
    def __init__(self, x):
        self.x = x
        self.hi = x.astype(BF16)
        self.lo = (x - self.hi.astype(F32)).astype(BF16)
        self._packed = {}

    def left(self, axis):
        key = ("l", axis)
        if key not in self._packed:
            self._packed[key] = jnp.concatenate([self.hi, self.lo, self.hi], axis=axis)
        return self._packed[key]

    def right(self, axis):
        key = ("r", axis)
        if key not in self._packed:
            self._packed[key] = jnp.concatenate([self.hi, self.hi, self.lo], axis=axis)
        return self._packed[key]


def _mm3(a, b):
    return _dot(a.left(1), b.right(0))


def _mm3_nt(a, b):
    return _dot_nt(a.left(1), b.right(1))


def _mm3_tn(a, b):
    return _dot_tn(a.left(0), b.right(0))


class _SplitDup:
    def __init__(self, xd):
        hi = xd.astype(BF16)
        lo = (xd - hi.astype(F32)).astype(BF16)
        self.left = jnp.concatenate([jnp.where(_iota(xd.shape, 1) < HEAD_DIM, hi, lo), hi], axis=1)
        self.right = _rows_packed(hi, lo)


def _rows_packed(hi, lo):
    return jnp.concatenate([hi, hi, lo, jnp.zeros_like(hi)], axis=0)


def _mm3_dup(a, b):
    bs = _Split(b)
    return _dot(a.left, _rows_packed(bs.hi, bs.lo))


def _rms(x, g, eps=NORM_EPS):
    return x * lax.rsqrt(jnp.mean(x * x, axis=-1, keepdims=True) + eps) * g


def _rms_matmul_kernel(x_ref, g_ref, w_ref, o_ref, xn_ref):
    @pl.when(pl.program_id(1) == 0)
    def _():
        xn_ref[...] = _rms(x_ref[...].astype(F32), g_ref[...]).astype(BF16)

    o_ref[...] = _dot(xn_ref[...], w_ref[...]).astype(o_ref.dtype)


def _pick_tiles(t, k, n, tile_bytes):
    tn = max(c for c in range(LANE, n + 1, LANE)
             if n % c == 0 and (c == n or c % MXU_WIDTH == 0) and k * c * 2 <= WEIGHT_TILE_BYTES)
    for tm in (2048, 1024, 512, 256):
        if t % tm == 0 and tile_bytes(tm, tn) + 2 * k * tn * 2 <= TILE_VMEM_BUDGET:
            return tm, tn
    raise ValueError("no row tile fits")


def rms_matmul(x, g, w, *, x_col=0, out_dtype=F32):
    t = x.shape[0]
    k, n = w.shape
    ob = jnp.dtype(out_dtype).itemsize
    tm, tn = _pick_tiles(t, k, n, lambda tm, tn: 2 * tm * k * x.dtype.itemsize + tm * k * 2 + 2 * tm * tn * ob)
    return pl.pallas_call(
        _rms_matmul_kernel,
        grid=(t // tm, n // tn),
        in_specs=[pl.BlockSpec((tm, k), lambda i, j: (i, x_col)),
                  pl.BlockSpec((1, k), lambda i, j: (0, 0)),
                  pl.BlockSpec((k, tn), lambda i, j: (0, j))],
        out_specs=pl.BlockSpec((tm, tn), lambda i, j: (i, j)),
        out_shape=jax.ShapeDtypeStruct((t, n), out_dtype),
        scratch_shapes=[pltpu.VMEM((tm, k), BF16)],
        compiler_params=_cp("parallel", "arbitrary"),
        name="rms_matmul",
    )(x, g.reshape(1, k).astype(F32), w)


def _matmul_res_kernel(a_ref, w_ref, r_ref, o_ref):
    o_ref[...] = r_ref[...] + _dot(a_ref[...].astype(BF16), w_ref[...])


def matmul_residual(res, a, w):
    t, k = a.shape
    n = w.shape[1]
    tm, tn = _pick_tiles(t, k, n, lambda tm, tn: 2 * tm * k * a.dtype.itemsize + 4 * tm * tn * 4)
    return pl.pallas_call(
        _matmul_res_kernel,
        grid=(t // tm, n // tn),
        in_specs=[pl.BlockSpec((tm, k), lambda i, j: (i, 0)),
                  pl.BlockSpec((k, tn), lambda i, j: (0, j)),
                  pl.BlockSpec((tm, tn), lambda i, j: (i, j))],
        out_specs=pl.BlockSpec((tm, tn), lambda i, j: (i, j)),
        out_shape=jax.ShapeDtypeStruct((t, n), F32),
        compiler_params=_cp("parallel", "arbitrary"),
        name="matmul_residual",
    )(a, w, res)


def _convglu_ple_kernel(u_ref, g_ref, gp_ref, cw_ref, cb_ref, w_ref, r_ref,
                        p_ref, wp_ref, pn_ref, gn_ref, wg_ref, fn_ref, o_ref, *, tm, seq, final):
    i = pl.program_id(0)
    g = g_ref[...].astype(F32)
    at_start = (i * tm) % seq == 0
    halo = jnp.where(at_start, 0.0, gp_ref[...].astype(F32))
    h1, h2 = halo[HALO - 1:HALO, :], halo[HALO - 2:HALO - 1, :]
    r1, r2 = pltpu.roll(g, 1, axis=0), pltpu.roll(g, 2, axis=0)
    top = _iota((8, g.shape[1]), 0)
    p1 = jnp.concatenate([jnp.where(top == 0, h1, r1[:8]), r1[8:]], axis=0)
    p2 = jnp.concatenate([jnp.where(top == 0, h2, jnp.where(top == 1, h1, r2[:8])), r2[8:]], axis=0)
    c = cw_ref[0:1, :] * p2 + cw_ref[1:2, :] * p1 + cw_ref[2:3, :] * g + cb_ref[...]
    act = c * (1.0 + lax.erf(c)) * u_ref[...].astype(F32)
    h = r_ref[...] + _dot(act.astype(BF16), w_ref[...])
    e = _rms(_dot(p_ref[...].astype(BF16), wp_ref[...]), pn_ref[...])
    gate = _sigmoid(_dot(_rms(h, gn_ref[...]).astype(BF16), wg_ref[...]))
    out = h + gate * e
    if final:
        out = _rms(out, fn_ref[...])
    o_ref[...] = out


def convglu_ple(res, ug, conv_w, conv_b, w_out, p, w_proj, p_norm, g_norm, w_gate, f_norm,
                *, seq, final, tm=256):
    t, d = res.shape
    f = w_out.shape[0]
    pd = p.shape[1]
    hb = tm // HALO
    rs2 = 1.0 / math.sqrt(2.0)
    row = lambda i: (i, 0)
    fix = lambda i: (0, 0)
    vec = pl.BlockSpec((1, d), fix)
    return pl.pallas_call(
        functools.partial(_convglu_ple_kernel, tm=tm, seq=seq, final=final),
        grid=(t // tm,),
        in_specs=[pl.BlockSpec((tm, f), row),
                  pl.BlockSpec((tm, f), lambda i: (i, 1)),
                  pl.BlockSpec((HALO, f), lambda i: (jnp.maximum(i * hb - 1, 0), 1)),
                  pl.BlockSpec((3, f), fix), pl.BlockSpec((1, f), fix), pl.BlockSpec((f, d), fix),
                  pl.BlockSpec((tm, d), row), pl.BlockSpec((tm, pd), row), pl.BlockSpec((pd, d), fix),
                  vec, vec, pl.BlockSpec((d, d), fix), vec],
        out_specs=pl.BlockSpec((tm, d), row),
        out_shape=jax.ShapeDtypeStruct((t, d), F32),
        compiler_params=_cp("parallel"),
        name="convglu_ple",
    )(ug, ug, ug, conv_w * rs2, (conv_b * rs2).reshape(1, f), (w_out * rs2).astype(BF16), res,
      p, w_proj.astype(BF16), p_norm.reshape(1, d), g_norm.reshape(1, d), w_gate.astype(BF16),
      f_norm.reshape(1, d))


def _sb_attn_kernel(q_ref, k_ref, v_ref, u_ref, o_ref, *, tq, tk, nh, scale):
    qi = pl.program_id(2)
    nblk = tq // tk
    width = nh * HEAD_DIM
    qf = q_ref[...] * scale
    qlane = _iota((tq, width), 1) // HEAD_DIM
    qs = jnp.concatenate([jnp.where(qlane == h, qf, jnp.zeros_like(qf)) for h in range(nh)], axis=0)
    vlane = _iota((tk, width), 1) // HEAD_DIM
    rows = lambda x, h: x[h * tq:(h + 1) * tq]

    def sweep(k_lo, run, acc, masked):
        parts = []
        for d in reversed(range(nblk)):
            k0 = k_lo + d * tk
            t = _dot_nt(qs, k_ref[pl.ds(k0, tk), :]) * math.log2(math.e)
            sp = jnp.maximum(t, 0.0) + jnp.log2(1.0 + jnp.exp2(-jnp.abs(t)))
            lom = -sp
            valid = None
            if masked:
                valid = _iota((nh * tq, tk), 1) + d * tk < _iota((nh * tq, tk), 0) % tq
                lom = jnp.where(valid, lom, 0.0)
            hi = lom.astype(BF16)
            lo = (lom - hi.astype(F32)).astype(BF16)
            sums = _dot(jnp.concatenate([hi, lo], axis=1), u_ref[...])
            parts.append((k0, t - sp, sums, valid))
        for k0, logsig, sums, valid in parts:
            w = jnp.exp2(logsig + run + sums[:, :tk])
            if masked:
                w = jnp.where(valid, w, 0.0)
            w = w.astype(BF16)
            vb = v_ref[pl.ds(k0, tk), :]
            wcat = jnp.concatenate([rows(w, h) for h in range(nh)], axis=1)
            vcat = jnp.concatenate([jnp.where(vlane == h, vb, jnp.zeros_like(vb)) for h in range(nh)], axis=0)
            acc = acc + _dot(wcat, vcat)
            run = run + sums[:, tk:]
        return run, acc

    run = jnp.zeros((nh * tq, tk), F32)
    acc = jnp.zeros((tq, width), F32)
    run, acc = sweep(pl.multiple_of(qi * tq, tq), run, acc, True)

    def live(run):
        return jnp.max(run) > SB_CUT

    def cond(c):
        return c[0]

    def body(c):
        _, j, run, acc = c
        run, acc = sweep(pl.multiple_of(j * tq, tq), run, acc, False)
        return (j > 0) & live(run), j - 1, run, acc

    _, _, _, acc = lax.while_loop(cond, body, ((qi > 0) & live(run), qi - 1, run, acc))
    o_ref[...] = acc


def sb_attention(qkv, *, batch, heads, tq=256, nh=4):
    t, w3 = qkv.shape
    s = t // batch
    nq = s // tq
    width = nh * HEAD_DIM
    groups = heads // nh
    tk = LANE
    jj = jnp.arange(2 * tk)[:, None] % tk
    ss = jnp.arange(2 * tk)[None, :]
    u = jnp.where(ss < tk, jj > ss, True).astype(BF16)
    return pl.pallas_call(
        functools.partial(_sb_attn_kernel, tq=tq, tk=tk, nh=nh, scale=1.0 / math.sqrt(HEAD_DIM)),
        grid=(batch, groups, nq),
        in_specs=[pl.BlockSpec((tq, width), lambda b, p, i: (b * nq + i, p)),
                  pl.BlockSpec((s, width), lambda b, p, i: (b, groups + p)),
                  pl.BlockSpec((s, width), lambda b, p, i: (b, 2 * groups + p)),
                  pl.BlockSpec((2 * tk, 2 * tk), lambda b, p, i: (0, 0))],
        out_specs=pl.BlockSpec((tq, width), lambda b, p, i: (b * nq + i, p)),
        out_shape=jax.ShapeDtypeStruct((t, w3 // 3), F32),
        compiler_params=_cp("parallel", "parallel", "arbitrary"),
        name="sb_attention",
    )(qkv, qkv, qkv, u)


def _softmax_attn_kernel(qn_ref, qr_ref, kn_ref, kr_ref, v_ref, o_ref, *bufs, tq, tk, npair, scale):
    qi = pl.program_id(2)
    g = pl.program_id(1)
    nh = 2 * npair
    lane = _iota((tq, LANE), 1)
    qr = qr_ref[...] * (scale * math.log2(math.e))
    slots = LANE // MLA_ROPE
    qs = []
    for h in range(nh):
        cols = slice(h // 2 * LANE, (h // 2 + 1) * LANE)
        qn = qn_ref[:, cols] * (scale * math.log2(math.e))
        nope = jnp.where(lane // MLA_NOPE == h % 2, qn, 0.0)
        rope_ = jnp.where(lane // MLA_ROPE == (g * nh + h) % slots, qr, 0.0)
        qs.append(jnp.concatenate([nope, rope_], axis=1).astype(BF16))

    def scores(j, dsts):
        k0 = pl.multiple_of(j * tk, tk)
        mx = []
        for h in range(nh):
            cols = slice(h // 2 * LANE, (h // 2 + 1) * LANE)
            kb = jnp.concatenate([kn_ref[pl.ds(k0, tk), cols], kr_ref[pl.ds(k0, tk), :]], axis=1)
            s = _dot_nt(qs[h], kb)
            dsts[h][...] = s
            mx.append(jnp.max(s, axis=1, keepdims=True))
        return tuple(mx)

    def consume(srcs, j, carry, mx):
        out = []
        for h in range(nh):
            cols = slice(h // 2 * LANE, (h // 2 + 1) * LANE)
            vb = v_ref[pl.ds(pl.multiple_of(j * tk, tk), tk), cols]
            mine = (_iota(vb.shape, 1) // MLA_V) == h % 2
            vbh = jnp.where(mine, vb, jnp.ones_like(vb))
            m, acc = carry[h]
            s = srcs[h][...]
            if mx is None:
                s = jnp.where(_iota((tq, tk), 1) <= _iota((tq, tk), 0), s, -1e30)
                m_new = jnp.maximum(m, jnp.max(s, axis=1, keepdims=True))
            else:
                m_new = jnp.maximum(m, mx[h])
            p = jnp.exp2(s - m_new)
            acc = jnp.exp2(m - m_new) * acc + _dot(p.astype(BF16), vbh)
            out.append((m_new, acc))
        return tuple(out)

    sa, sb = bufs[:nh], bufs[nh:]

    def pair(t, carry):
        state, mxa = carry
        mxb = scores(2 * t + 1, sb)
        state = consume(sa, 2 * t, state, mxa)
        mxa = scores(2 * t + 2, sa)
        return consume(sb, 2 * t + 1, state, mxb), mxa

    def tail_odd(carry):
        state, mxa = carry
        scores(qi, sb)
        state = consume(sa, qi - 1, state, mxa)
        return consume(sb, qi, state, None)

    def tail_even(carry):
        return consume(sa, qi, carry[0], None)

    init = ((jnp.full((tq, 1), -1e30, F32), jnp.zeros((tq, LANE), F32)),) * nh
    carry = lax.fori_loop(0, qi // 2, pair, (init, scores(0, sa)))
    state = lax.cond(qi % 2 == 1, tail_odd, tail_even, carry)
    for pr in range(npair):
        acc0, acc1 = state[2 * pr][1], state[2 * pr + 1][1]
        o_ref[:, pr * LANE:(pr + 1) * LANE] = jnp.where(
            lane < MLA_V, acc0 / pltpu.roll(acc0, MLA_V, axis=1), acc1 / pltpu.roll(acc1, MLA_V, axis=1))


def softmax_attention(q, qrot, kv, krot, scale, *, batch, heads, tq=512, npair=2):
    tk = tq
    t = q.shape[0]
    s = t // batch
    nq = s // tq
    width = npair * LANE
    groups = heads * MLA_NOPE // width
    rope_per_group = LANE // (2 * npair * MLA_ROPE)
    buf = pltpu.VMEM((tq, tk), F32)
    return pl.pallas_call(
        functools.partial(_softmax_attn_kernel, tq=tq, tk=tk, npair=npair, scale=scale),
        grid=(batch, groups, nq),
        in_specs=[pl.BlockSpec((tq, width), lambda b, p, i: (b * nq + i, p)),
                  pl.BlockSpec((tq, LANE), lambda b, p, i: (b * nq + i, p // rope_per_group)),
                  pl.BlockSpec((s, width), lambda b, p, i: (b, p)),
                  pl.BlockSpec((s, LANE), lambda b, p, i: (b, 0)),
                  pl.BlockSpec((s, width), lambda b, p, i: (b, groups + p))],
        out_specs=pl.BlockSpec((tq, width), lambda b, p, i: (b * nq + i, p)),
        out_shape=jax.ShapeDtypeStruct((t, heads * MLA_V), F32),
        scratch_shapes=[buf] * (4 * npair),
        compiler_params=_cp("parallel", "parallel", "arbitrary"),
        name="softmax_attention",
    )(q, qrot, kv, krot, kv)


def _rope_kernel(q_ref, k_ref, pos_ref, f_ref, oq_ref, ok_ref):
    half = MLA_ROPE // 2
    ang = pos_ref[...].astype(F32) * f_ref[...]
    cos, sin = jnp.cos(ang), jnp.sin(ang)
    low = _iota(ang.shape, 1) % MLA_ROPE < half

    def rot(x):
        partner = jnp.where(low, -pltpu.roll(x, LANE - half, axis=1), pltpu.roll(x, half, axis=1))
        return x * cos + partner * sin

    for c in range(q_ref.shape[1] // LANE):
        cols = slice(c * LANE, (c + 1) * LANE)
        oq_ref[:, cols] = rot(q_ref[:, cols])
    ok_ref[...] = rot(k_ref[...]).astype(ok_ref.dtype)


def rope(q, c, positions, *, q_col, qw, k_col, tm=512):
    t = q.shape[0]
    half = MLA_ROPE // 2
    inv_freq = ROPE_THETA ** (-jnp.arange(half, dtype=F32) / half)
    freq = jnp.tile(inv_freq, LANE // half).reshape(1, LANE)
    return pl.pallas_call(
        _rope_kernel,
        grid=(t // tm,),
        in_specs=[pl.BlockSpec((tm, qw), lambda i: (i, q_col)), pl.BlockSpec((tm, LANE), lambda i: (i, k_col)),
                  pl.BlockSpec((tm, 1), lambda i: (i, 0)), pl.BlockSpec((1, LANE), lambda i: (0, 0))],
        out_specs=[pl.BlockSpec((tm, qw), lambda i: (i, 0)), pl.BlockSpec((tm, LANE), lambda i: (i, 0))],
        out_shape=[jax.ShapeDtypeStruct((t, qw), F32), jax.ShapeDtypeStruct((t, LANE), BF16)],
        compiler_params=_cp("parallel"),
        name="rope",
    )(q, c, positions.reshape(t, 1), freq)


def _rwkv_prep_kernel(x_ref, xp_ref, mu_ref, w0_ref, a0_ref, kk_ref, ka_ref, rk_ref,
                      w2_ref, a2_ref, g2_ref, tri_ref, hs_ref,
                      at_ref, bt_ref, kt_ref, rt_ref, v_ref, eg_ref, g_ref, bv_ref, *, tm, seq, width):
    i = pl.program_id(0)
    x = x_ref[...]
    at_start = (i * tm) % seq == 0
    prev_row = jnp.where(at_start, 0.0, xp_ref[7:8, :])
    rolled = pltpu.roll(x, 1, axis=0)
    top = _iota((8, x.shape[1]), 0)
    prev = jnp.concatenate([jnp.where(top == 0, prev_row, rolled[:8]), rolled[8:]], axis=0)
    xm = x + (prev - x) * mu_ref[...]
    r = xm[:, :width]
    k = xm[:, width:2 * width]
    v = xm[:, 2 * width:3 * width]
    l1 = xm[:, 3 * width:3 * width + LANE]
    l2 = xm[:, 3 * width + LANE:]
    wpre = w0_ref[...] + _dot(_Split(jnp.tanh(l1)).left(1), w2_ref[...])
    a = _sigmoid(a0_ref[...] + _dot(_Split(l1).left(1), a2_ref[...]))
    g = _dot(_Split(_sigmoid(l2)).left(1), g2_ref[...])

    def hilo(t):
        ts = _Split(t)
        return jnp.concatenate([ts.hi, ts.lo], axis=1)

    lw = -jnp.exp(-_softplus(-wpre) - 0.5)
    hs = hs_ref[...]
    kk = k * kk_ref[...]
    kk = kk * lax.rsqrt(jnp.maximum(_dot(hilo(kk * kk), hs), 1e-24))
    km = k * (1.0 + (a - 1.0) * ka_ref[...])
    bonus = _dot(hilo(r * km * rk_ref[...]), hs)
    lws = _Split(lw)
    cum = _dot(tri_ref[...], jnp.concatenate([lws.hi, lws.lo], axis=0))
    eg = jnp.exp(cum)
    ieg = jnp.exp(-cum)
    at_ref[...] = -kk * jnp.exp(cum - lw)
    bt_ref[...] = kk * a * ieg
    kt_ref[...] = km * ieg
    rt_ref[...] = r * eg
    v_ref[...] = v
    eg_ref[...] = eg
    g_ref[...] = g
    bv_ref[...] = bonus * v


def rwkv_prep(x, mu, w0, a0, k_k, k_a, r_k, w2p, a2p, g2p, *, seq, width, tm=256):
    t, wx = x.shape
    ch = RW_CHUNK
    ti = jnp.arange(tm)
    tri = ((ti[:, None] // ch == ti[None, :] // ch) & (ti[None, :] <= ti[:, None])).astype(BF16)
    tri = jnp.concatenate([tri, tri], axis=1)
    ci = jnp.arange(width) // HEAD_DIM
    hs = (ci[:, None] == ci[None, :]).astype(BF16)
    hs = jnp.concatenate([hs, hs], axis=0)
    w2p, a2p, g2p = (_Split(w).right(0) for w in (w2p, a2p, g2p))
    hb = tm // 8
    row = lambda i: (i, 0)
    fix = lambda i: (0, 0)
    vec = pl.BlockSpec((1, width), fix)
    out = jax.ShapeDtypeStruct((t, width), F32)
    return pl.pallas_call(
        functools.partial(_rwkv_prep_kernel, tm=tm, seq=seq, width=width),
        grid=(t // tm,),
        in_specs=[pl.BlockSpec((tm, wx), row),
                  pl.BlockSpec((8, wx), lambda i: (jnp.maximum(i * hb - 1, 0), 0)),
                  pl.BlockSpec((1, wx), fix), vec, vec, vec, vec, vec,
                  pl.BlockSpec(w2p.shape, fix), pl.BlockSpec(a2p.shape, fix),
                  pl.BlockSpec(g2p.shape, fix), pl.BlockSpec(tri.shape, fix),
                  pl.BlockSpec(hs.shape, fix)],
        out_specs=[pl.BlockSpec((tm, width), row)] * 8,
        out_shape=[out] * 8,
        compiler_params=_cp("parallel"),
        name="rwkv_prep",
    )(x, x, mu, w0, a0, k_k, k_a, r_k, w2p, a2p, g2p, tri, hs)


def _rwkv_chunk_kernel(at_ref, bt_ref, kt_ref, rt_ref, v_ref, eg_ref,
                       m_ref, n_ref, rq_ref, yl_ref, *, nchunk, unroll):
    ch, n = RW_CHUNK, HEAD_DIM
    ti = _iota((ch, 2 * ch), 0)
    si = _iota((ch, 2 * ch), 1) % ch
    strict = si < ti
    incl = si <= ti
    same = (ti // RW_SUB) == (si // RW_SUB)
    eye = _iota((n, n), 0) == _iota((n, n), 1)
    each = lambda f, *ls: [f(*t) for t in zip(*ls)]

    def body(it, _):
        cs = [it * unroll + u for u in range(unroll)]
        sls = [pl.ds(pl.multiple_of(c * ch, ch), ch) for c in cs]
        ids = [(h, u) for h in range(2) for u in range(unroll)]
        ld = lambda ref: [ref[sls[u], h * n:(h + 1) * n] for h, u in ids]
        sp = lambda xs: [_Split(t) for t in xs]
        dup = lambda xs: [_SplitDup(t) for t in xs]
        twice = lambda t: jnp.concatenate([t, t], axis=0)
        a_, b_, k_, r_, v_ = ld(at_ref), ld(bt_ref), ld(kt_ref), ld(rt_ref), ld(v_ref)
        g_end = [eg_ref[pl.ds(cs[u] * ch + ch - 1, 1), h * n:(h + 1) * n] for h, u in ids]
        ar = sp(each(lambda a, r: jnp.concatenate([a, r], axis=0), a_, r_))
        pb = each(lambda x, y: _dot_nt(x.left(1), twice(y.right(1))), ar, sp(b_))
        pk = each(lambda x, y: _dot_nt(x.left(1), twice(y.right(1))), ar, sp(k_))
        a_ab = [jnp.where(strict, t[:ch], 0.0) for t in pb]
        a_ak = dup([jnp.where(strict, t[:ch], 0.0) for t in pk])
        a_rb = dup([jnp.where(incl, t[ch:], 0.0) for t in pb])
        a_rk = dup([jnp.where(incl, t[ch:], 0.0) for t in pk])
        d1 = [jnp.where(same, t, 0.0) for t in a_ab]
        lb = each(lambda t, d: t - d, a_ab, d1)
        d1 = dup(d1)
        d2 = dup([_dot(d.left, d.right) for d in d1])
        akv = each(_mm3_dup, a_ak, v_)
        x = each(lambda l, a, t: jnp.concatenate([l, a, t], axis=1), lb, a_, akv)
        d4 = dup([_dot(d.left, d.right) for d in d2])
        x = each(lambda d, t: t + _mm3_dup(d, t), d1, x)
        d8 = dup([_dot(d.left, d.right) for d in d4])
        x = each(lambda d, t: t + _mm3_dup(d, t), d2, x)
        x = each(lambda d, t: t + _mm3_dup(d, t), d4, x)
        x = each(lambda d, t: t + _mm3_dup(d, t), d8, x)
        e1 = dup([t[:, :2 * ch] for t in x])
        wu = [t[:, 2 * ch:] for t in x]
        e2 = dup([_dot(e.left, e.right) for e in e1])
        wu = each(lambda e, t: t + _mm3_dup(e, t), e1, wu)
        wu = each(lambda e, t: t + _mm3_dup(e, t), e2, wu)
        ry = each(_mm3_dup, a_rb, wu)
        rkv = each(_mm3_dup, a_rk, v_)
        wus, vs = sp(wu), sp(v_)
        mn = each(lambda b, g, t: _mm3_tn(_Split(b * g), t), b_, g_end, wus)
        kv = each(lambda k, g, t: _mm3_tn(_Split(k * g), t), k_, g_end, vs)
        rq = each(lambda r, t: r + t[:, :n], r_, ry)
        yl = each(lambda t, w: t[:, n:] + w, ry, rkv)
        for u in range(unroll):
            rq_ref[sls[u], :] = jnp.concatenate([rq[u], rq[unroll + u]], axis=1)
            yl_ref[sls[u], :] = jnp.concatenate([yl[u], yl[unroll + u]], axis=1)
        for i, (h, u) in enumerate(ids):
            m_ref[h, cs[u]] = mn[i][:, :n] + jnp.where(eye, g_end[i], 0.0)
            n_ref[h, cs[u]] = mn[i][:, n:] + kv[i]
        return 0

    lax.fori_loop(0, nchunk // unroll, body, 0)


def rwkv_chunk(at, bt, kt, rt, v, eg, *, batch, nchunk=8, unroll=8):
    t, w = at.shape
    s = t // batch
    ch, n = RW_CHUNK, HEAD_DIM
    pairs = w // LANE
    rows = nchunk * ch
    nr = s // rows
    spec = pl.BlockSpec((rows, LANE), lambda b, p, i: (b * nr + i, p))
    mspec = pl.BlockSpec((2, nchunk, n, n), lambda b, p, i: (b * pairs + p, i, 0, 0))
    return pl.pallas_call(
        functools.partial(_rwkv_chunk_kernel, nchunk=nchunk, unroll=unroll),
        grid=(batch, pairs, nr),
        in_specs=[spec] * 6,
        out_specs=[mspec, mspec, spec, spec],
        out_shape=[jax.ShapeDtypeStruct((batch * w // n, s // ch, n, n), F32)] * 2
        + [jax.ShapeDtypeStruct((t, w), F32)] * 2,
        compiler_params=_cp("parallel", "parallel", "parallel"),
        name="rwkv_chunk",
    )(at, bt, kt, rt, v, eg)


def _rwkv_scan_kernel(m_ref, n_ref, s_ref, st_ref, *, heads, nchunk):
    @pl.when(pl.program_id(1) == 0)
    def _():
        st_ref[...] = jnp.zeros_like(st_ref)

    def body(c, _):
        sts = [st_ref[hh] for hh in range(heads)]
        for hh in range(heads):
            s_ref[hh, c] = sts[hh]
        ms = [_Split(m_ref[hh, c]) for hh in range(heads)]
        ss = [_Split(st) for st in sts]
        new = [_mm3(a, b) for a, b in zip(ms, ss)]
        for hh in range(heads):
            st_ref[hh] = new[hh] + n_ref[hh, c]
        return 0

    lax.fori_loop(0, nchunk, body, 0)


def rwkv_scan(m, nn, *, heads=16, nchunk=8):
    bh, nc, n, _ = m.shape
    heads = min(heads, bh)
    nchunk = min(nchunk, nc)
    spec = pl.BlockSpec((heads, nchunk, n, n), lambda b, i: (b, i, 0, 0))
    return pl.pallas_call(
        functools.partial(_rwkv_scan_kernel, heads=heads, nchunk=nchunk),
        grid=(bh // heads, nc // nchunk),
        in_specs=[spec, spec],
        out_specs=spec,
        out_shape=jax.ShapeDtypeStruct((bh, nc, n, n), F32),
        scratch_shapes=[pltpu.VMEM((heads, n, n), F32)],
        compiler_params=_cp("parallel", "arbitrary"),
        name="rwkv_scan",
    )(m, nn)


def _rwkv_out_kernel(rq_ref, yl_ref, s_ref, lw_ref, lb_ref, avg_ref, o_ref, *, nchunk):
    ch, n = RW_CHUNK, HEAD_DIM
    ys = []
    for h in range(2):
        cols = slice(h * n, (h + 1) * n)
        ys.append(jnp.concatenate(
            [_mm3(_Split(rq_ref[c * ch:(c + 1) * ch, cols]), _Split(s_ref[h, c])) for c in range(nchunk)],
            axis=0))
    y = jnp.concatenate(ys, axis=1) + yl_ref[...]
    avg2 = jnp.concatenate([avg_ref[...]] * 2, axis=0)

    def mean(t):
        ts = _Split(t)
        return _dot(jnp.concatenate([ts.hi, ts.lo], axis=1), avg2)

    d = y - mean(y)
    var = mean(d * d)
    o_ref[...] = d * lax.rsqrt(var + GN_EPS) * lw_ref[...] + lb_ref[...]


def rwkv_out(rq, yl, s0, ln_w, ln_b, *, batch, nchunk=8):
    t, w = rq.shape
    s = t // batch
    n = HEAD_DIM
    pairs = w // LANE
    rows = nchunk * RW_CHUNK
    nr = s // rows
    li = jnp.arange(LANE) // n
    avg = ((li[:, None] == li[None, :]).astype(F32) / n).astype(BF16)
    spec = pl.BlockSpec((rows, LANE), lambda b, p, i: (b * nr + i, p))
    hspec = pl.BlockSpec((1, LANE), lambda b, p, i: (0, p))
    return pl.pallas_call(
        functools.partial(_rwkv_out_kernel, nchunk=nchunk),
        grid=(batch, pairs, nr),
        in_specs=[spec, spec, pl.BlockSpec((2, nchunk, n, n), lambda b, p, i: (b * pairs + p, i, 0, 0)),
                  hspec, hspec, pl.BlockSpec((LANE, LANE), lambda b, p, i: (0, 0))],
        out_specs=spec,
        out_shape=jax.ShapeDtypeStruct((t, w), F32),
        compiler_params=_cp("parallel", "parallel", "parallel"),
        name="rwkv_out",
    )(rq, yl, s0, ln_w.reshape(1, w), ln_b.reshape(1, w), avg)


def _hyb_out_kernel(oa_ref, y_ref, bv_ref, g_ref, wa_ref, wb_ref, r_ref, o_ref):
    ob = (y_ref[...] + bv_ref[...]) * g_ref[...]
    o_ref[...] = (r_ref[...] + _dot(oa_ref[...].astype(BF16), wa_ref[...])
                  + _dot(ob.astype(BF16), wb_ref[...]))


def hyb_out(res, oa, y, bv, g, wa, wb, *, tm=1024):
    t, d = res.shape
    row = lambda i: (i, 0)
    fix = lambda i: (0, 0)
    act = lambda a: pl.BlockSpec((tm, a.shape[1]), row)
    return pl.pallas_call(
        _hyb_out_kernel,
        grid=(t // tm,),
        in_specs=[act(oa), act(y), act(bv), act(g), pl.BlockSpec(wa.shape, fix),
                  pl.BlockSpec(wb.shape, fix), pl.BlockSpec((tm, d), row)],
        out_specs=pl.BlockSpec((tm, d), row),
        out_shape=jax.ShapeDtypeStruct((t, d), F32),
        compiler_params=_cp("parallel"),
        name="hyb_out",
    )(oa, y, bv, g, wa, wb, res)


def _pad_cols(w, n):
    return jnp.pad(w, ((0, 0), (0, n - w.shape[1])))


def sb_rwkv_layer(h, b, s, attn_norm, w_in, w_out, mu, w0, w2, a0, a2, g2, k_k, k_a, r_k, ln_w, ln_b):
    sbw = SB_HEADS * HEAD_DIM
    rww = RW_HEADS * HEAD_DIM
    rw_in = w_in.shape[1] - 3 * sbw
    rw_pad = -(-rw_in // LANE) * LANE
    dl, al = w2.shape[0], a2.shape[0]
    gl = g2.shape[0]
    assert dl + al == LANE and 3 * rww % LANE == 0

    qkv = rms_matmul(h, attn_norm, w_in[:, :3 * sbw].astype(BF16), out_dtype=BF16)
    rw = rms_matmul(h, attn_norm, _pad_cols(w_in[:, 3 * sbw:], rw_pad).astype(BF16))
    oa = sb_attention(qkv, batch=b, heads=SB_HEADS)

    w2p = jnp.zeros((LANE, rww), F32).at[:dl].set(w2)
    a2p = jnp.zeros((LANE, rww), F32).at[dl:].set(a2)
    g2p = jnp.zeros((rw_pad - 3 * rww - LANE, rww), F32).at[:gl].set(g2)
    vec = lambda t: t.reshape(1, rww).astype(F32)
    mu_p = _pad_cols(mu.reshape(1, rw_in), rw_pad)
    at, bt, kt, rt, vv, eg, g, bv = rwkv_prep(rw, mu_p, vec(w0), vec(a0), vec(k_k), vec(k_a), vec(r_k),
                                              w2p, a2p, g2p, seq=s, width=rww)
    m, nn, rq, yl = rwkv_chunk(at, bt, kt, rt, vv, eg, batch=b)
    s0 = rwkv_scan(m, nn)
    y = rwkv_out(rq, yl, s0, ln_w, ln_b, batch=b)
    return hyb_out(h, oa, y, bv, g, w_out[:sbw].astype(BF16), w_out[sbw:].astype(BF16))


def mla_layer(h, b, s, positions, attn_norm, w_down, q_norm, kv_norm, w_uq, w_ukv, w_o):
    nh, dn, dr, dv = MLA_HEADS, MLA_NOPE, MLA_ROPE, MLA_V
    qr, kvr = MLA_Q_RANK, MLA_KV_RANK
    assert dn == dv and qr % LANE == 0 and (qr + LANE) % kvr == 0
    wd = jnp.concatenate([w_down[:, :qr], jnp.tile(w_down[:, qr + kvr:], (1, LANE // dr)),
                          w_down[:, qr:qr + kvr]], axis=1)
    c = rms_matmul(h, attn_norm, wd.astype(BF16))
    wq = w_uq.reshape(qr, nh, dn + dr)
    wq = jnp.concatenate([wq[:, :, :dn].reshape(qr, nh * dn), wq[:, :, dn:].reshape(qr, nh * dr)], axis=1)
    q = rms_matmul(c, q_norm, wq.astype(BF16), x_col=0)
    wkv = w_ukv.reshape(kvr, nh, dn + dv)
    wkv = jnp.concatenate([wkv[:, :, :dn].reshape(kvr, nh * dn), wkv[:, :, dn:].reshape(kvr, nh * dv)], axis=1)
    kv = rms_matmul(c, kv_norm, wkv.astype(BF16), x_col=(qr + LANE) // kvr, out_dtype=BF16)
    qrot, krot = rope(q, c, positions, q_col=dn // dr, qw=nh * dr, k_col=qr // LANE)
    o = softmax_attention(q, qrot, kv, krot, 1.0 / math.sqrt(dn + dr), batch=b, heads=nh)
    return matmul_residual(h, o, w_o.astype(BF16))


def kernel(x, p, positions, attn_norm, ffn_norm, ffn_w_in, ffn_conv_w, ffn_conv_b, ffn_w_out, ple_w_proj, ple_norm, ple_gate_norm, ple_w_gate, hyb_w_in, hyb_w_out, rw_mu, rw_w0, rw_w2, rw_a0, rw_a2, rw_g2, rw_k_k, rw_k_a, rw_r_k, rw_ln_w, rw_ln_b, mla_w_down, mla_q_norm, mla_kv_norm, mla_w_uq, mla_w_ukv, mla_w_o, final_norm):
    b, s, d = x.shape
    depth = p.shape[0]
    h = x.reshape(b * s, d)
    pos = positions.reshape(b * s)
    for i in range(depth):
        j = i // 2
        if i % 2 == 0:
            h = sb_rwkv_layer(h, b, s, attn_norm[i], hyb_w_in[j], hyb_w_out[j], rw_mu[j], rw_w0[j],
                              rw_w2[j], rw_a0[j], rw_a2[j], rw_g2[j], rw_k_k[j], rw_k_a[j], rw_r_k[j],
                              rw_ln_w[j], rw_ln_b[j])
        else:
            h = mla_layer(h, b, s, pos, attn_norm[i], mla_w_down[j], mla_q_norm[j], mla_kv_norm[j],
                          mla_w_uq[j], mla_w_ukv[j], mla_w_o[j])
        ug = rms_matmul(h, ffn_norm[i], ffn_w_in[i].astype(BF16), out_dtype=BF16)
        h = convglu_ple(h, ug, ffn_conv_w[i], ffn_conv_b[i], ffn_w_out[i], p[i].reshape(b * s, -1),
                        ple_w_proj[i], ple_norm[i], ple_gate_norm[i], ple_w_gate[i], final_norm,
                        seq=s, final=(i == depth - 1))
    return h.reshape(b, s, d)
```

```python
import functools
import math

import jax
import jax.numpy as jnp
from jax import lax
from jax.experimental import pallas as pl
from jax.experimental.pallas import tpu as pltpu

F32 = jnp.float32
BF16 = jnp.bfloat16
HI = lax.Precision.HIGHEST

NORM_EPS = 1e-6
GN_EPS = 64e-5
ROPE_THETA = 10000.0
LANE = 128
VMEM_LIMIT = 48 * 1024 * 1024
HALO = 16
MXU_WIDTH = 256
TILE_VMEM_BUDGET = 36 * 1024 * 1024
WEIGHT_TILE_BYTES = 6 * 1024 * 1024

SB_HEADS = 8
RW_HEADS = 8
HEAD_DIM = 64
MLA_HEADS = 16
MLA_NOPE = 64
MLA_ROPE = 32
MLA_V = 64
MLA_Q_RANK = 384
MLA_KV_RANK = 256
SB_CUT = -151.0
RW_CHUNK = 64
RW_SUB = 16


def _cp(*sem):
    return pltpu.CompilerParams(dimension_semantics=sem, vmem_limit_bytes=VMEM_LIMIT)


def _iota(shape, dim):
    return lax.broadcasted_iota(jnp.int32, shape, dim)


def _softplus(z):
    return jnp.maximum(z, 0.0) + jnp.log1p(jnp.exp(-jnp.abs(z)))


def _sigmoid(z):
    return 1.0 / (1.0 + jnp.exp(-z))


def _dot(a, b, precision=None):
    return jnp.dot(a, b, preferred_element_type=F32, precision=precision)


def _dot_nt(a, b, precision=None):
    return lax.dot_general(a, b, (((1,), (1,)), ((), ())), preferred_element_type=F32,
                           precision=precision)


def _dot_tn(a, b, precision=None):
    return lax.dot_general(a, b, (((0,), (0,)), ((), ())), preferred_element_type=F32,
                           precision=precision)


class _Split:
    def __init__(self, x):
        self.x = x
        self.hi = x.astype(BF16)
        self.lo = (x - self.hi.astype(F32)).astype(BF16)
        self._packed = {}

    def left(self, axis):
        key = ("l", axis)
        if key not in self._packed:
            self._packed[key] = jnp.concatenate([self.hi, self.lo, self.hi], axis=axis)
        return self._packed[key]

    def right(self, axis):
        key = ("r", axis)
        if key not in self._packed:
            self._packed[key] = jnp.concatenate([self.hi, self.hi, self.lo], axis=axis)
        return self._packed[key]


def _mm3(a, b):
    return _dot(a.left(1), b.right(0))


def _mm3_nt(a, b):
    return _dot_nt(a.left(1), b.right(1))


def _mm3_tn(a, b):
    return _dot_tn(a.left(0), b.right(0))


class _SplitDup:
    def __init__(self, xd):
        hi = xd.astype(BF16)
        lo = (xd - hi.astype(F32)).astype(BF16)
        self.left = jnp.concatenate([jnp.where(_iota(xd.shape, 1) < HEAD_DIM, hi, lo), hi], axis=1)
        self.right = _rows_packed(hi, lo)


def _rows_packed(hi, lo):
    return jnp.concatenate([hi, hi, lo, jnp.zeros_like(hi)], axis=0)


def _mm3_dup(a, b):
    bs = _Split(b)
    return _dot(a.left, _rows_packed(bs.hi, bs.lo))


def _rms(x, g, eps=NORM_EPS):
    return x * lax.rsqrt(jnp.mean(x * x, axis=-1, keepdims=True) + eps) * g


def _rms_matmul_kernel(x_ref, g_ref, w_ref, o_ref, xn_ref):
    @pl.when(pl.program_id(1) == 0)
    def _():
        xn_ref[...] = _rms(x_ref[...].astype(F32), g_ref[...]).astype(BF16)

    o_ref[...] = _dot(xn_ref[...], w_ref[...]).astype(o_ref.dtype)


def _pick_tiles(t, k, n, tile_bytes):
    tn = max(c for c in range(LANE, n + 1, LANE)
             if n % c == 0 and (c == n or c % MXU_WIDTH == 0) and k * c * 2 <= WEIGHT_TILE_BYTES)
    for tm in (2048, 1024, 512, 256):
        if t % tm == 0 and tile_bytes(tm, tn) + 2 * k * tn * 2 <= TILE_VMEM_BUDGET:
            return tm, tn
    raise ValueError("no row tile fits")


def rms_matmul(x, g, w, *, x_col=0, out_dtype=F32):
    t = x.shape[0]
    k, n = w.shape
    ob = jnp.dtype(out_dtype).itemsize
    tm, tn = _pick_tiles(t, k, n, lambda tm, tn: 2 * tm * k * x.dtype.itemsize + tm * k * 2 + 2 * tm * tn * ob)
    return pl.pallas_call(
        _rms_matmul_kernel,
        grid=(t // tm, n // tn),
        in_specs=[pl.BlockSpec((tm, k), lambda i, j: (i, x_col)),
                  pl.BlockSpec((1, k), lambda i, j: (0, 0)),
                  pl.BlockSpec((k, tn), lambda i, j: (0, j))],
        out_specs=pl.BlockSpec((tm, tn), lambda i, j: (i, j)),
        out_shape=jax.ShapeDtypeStruct((t, n), out_dtype),
        scratch_shapes=[pltpu.VMEM((tm, k), BF16)],
        compiler_params=_cp("parallel", "arbitrary"),
        name="rms_matmul",
    )(x, g.reshape(1, k).astype(F32), w)


def _matmul_res_kernel(a_ref, w_ref, r_ref, o_ref):
    o_ref[...] = r_ref[...] + _dot(a_ref[...].astype(BF16), w_ref[...])


def matmul_residual(res, a, w):
    t, k = a.shape
    n = w.shape[1]
    tm, tn = _pick_tiles(t, k, n, lambda tm, tn: 2 * tm * k * a.dtype.itemsize + 4 * tm * tn * 4)
    return pl.pallas_call(
        _matmul_res_kernel,
        grid=(t // tm, n // tn),
        in_specs=[pl.BlockSpec((tm, k), lambda i, j: (i, 0)),
                  pl.BlockSpec((k, tn), lambda i, j: (0, j)),
                  pl.BlockSpec((tm, tn), lambda i, j: (i, j))],
        out_specs=pl.BlockSpec((tm, tn), lambda i, j: (i, j)),
        out_shape=jax.ShapeDtypeStruct((t, n), F32),
        compiler_params=_cp("parallel", "arbitrary"),
        name="matmul_residual",
    )(a, w, res)


def _convglu_ple_kernel(u_ref, g_ref, gp_ref, cw_ref, cb_ref, w_ref, r_ref,
                        p_ref, wp_ref, pn_ref, gn_ref, wg_ref, fn_ref, o_ref, *, tm, seq, final):
    i = pl.program_id(0)
    g = g_ref[...].astype(F32)
    at_start = (i * tm) % seq == 0
    halo = jnp.where(at_start, 0.0, gp_ref[...].astype(F32))
    h1, h2 = halo[HALO - 1:HALO, :], halo[HALO - 2:HALO - 1, :]
    r1, r2 = pltpu.roll(g, 1, axis=0), pltpu.roll(g, 2, axis=0)
    top = _iota((8, g.shape[1]), 0)
    p1 = jnp.concatenate([jnp.where(top == 0, h1, r1[:8]), r1[8:]], axis=0)
    p2 = jnp.concatenate([jnp.where(top == 0, h2, jnp.where(top == 1, h1, r2[:8])), r2[8:]], axis=0)
    c = cw_ref[0:1, :] * p2 + cw_ref[1:2, :] * p1 + cw_ref[2:3, :] * g + cb_ref[...]
    act = c * (1.0 + lax.erf(c)) * u_ref[...].astype(F32)
    h = r_ref[...] + _dot(act.astype(BF16), w_ref[...])
    e = _rms(_dot(p_ref[...].astype(BF16), wp_ref[...]), pn_ref[...])
    gate = _sigmoid(_dot(_rms(h, gn_ref[...]).astype(BF16), wg_ref[...]))
    out = h + gate * e
    if final:
        out = _rms(out, fn_ref[...])
    o_ref[...] = out


def convglu_ple(res, ug, conv_w, conv_b, w_out, p, w_proj, p_norm, g_norm, w_gate, f_norm,
                *, seq, final, tm=256):
    t, d = res.shape
    f = w_out.shape[0]
    pd = p.shape[1]
    hb = tm // HALO
    rs2 = 1.0 / math.sqrt(2.0)
    row = lambda i: (i, 0)
    fix = lambda i: (0, 0)
    vec = pl.BlockSpec((1, d), fix)
    return pl.pallas_call(
        functools.partial(_convglu_ple_kernel, tm=tm, seq=seq, final=final),
        grid=(t // tm,),
        in_specs=[pl.BlockSpec((tm, f), row),
                  pl.BlockSpec((tm, f), lambda i: (i, 1)),
                  pl.BlockSpec((HALO, f), lambda i: (jnp.maximum(i * hb - 1, 0), 1)),
                  pl.BlockSpec((3, f), fix), pl.BlockSpec((1, f), fix), pl.BlockSpec((f, d), fix),
                  pl.BlockSpec((tm, d), row), pl.BlockSpec((tm, pd), row), pl.BlockSpec((pd, d), fix),
                  vec, vec, pl.BlockSpec((d, d), fix), vec],
        out_specs=pl.BlockSpec((tm, d), row),
        out_shape=jax.ShapeDtypeStruct((t, d), F32),
        compiler_params=_cp("parallel"),
        name="convglu_ple",
    )(ug, ug, ug, conv_w * rs2, (conv_b * rs2).reshape(1, f), (w_out * rs2).astype(BF16), res,
      p, w_proj.astype(BF16), p_norm.reshape(1, d), g_norm.reshape(1, d), w_gate.astype(BF16),
      f_norm.reshape(1, d))


def _sb_attn_kernel(q_ref, k_ref, v_ref, u_ref, o_ref, *, tq, tk, nh, scale):
    qi = pl.program_id(2)
    nblk = tq // tk
    width = nh * HEAD_DIM
    qf = q_ref[...] * scale
    qlane = _iota((tq, width), 1) // HEAD_DIM
    qs = jnp.concatenate([jnp.where(qlane == h, qf, jnp.zeros_like(qf)) for h in range(nh)], axis=0)
    vlane = _iota((tk, width), 1) // HEAD_DIM
    rows = lambda x, h: x[h * tq:(h + 1) * tq]

    def sweep(k_lo, run, acc, masked):
        parts = []
        for d in reversed(range(nblk)):
            k0 = k_lo + d * tk
            t = _dot_nt(qs, k_ref[pl.ds(k0, tk), :]) * math.log2(math.e)
            sp = jnp.maximum(t, 0.0) + jnp.log2(1.0 + jnp.exp2(-jnp.abs(t)))
            lom = -sp
            valid = None
            if masked:
                valid = _iota((nh * tq, tk), 1) + d * tk < _iota((nh * tq, tk), 0) % tq
                lom = jnp.where(valid, lom, 0.0)
            hi = lom.astype(BF16)
            lo = (lom - hi.astype(F32)).astype(BF16)
            sums = _dot(jnp.concatenate([hi, lo], axis=1), u_ref[...])
            parts.append((k0, t - sp, sums, valid))
        for k0, logsig, sums, valid in parts:
            w = jnp.exp2(logsig + run + sums[:, :tk])
            if masked:
                w = jnp.where(valid, w, 0.0)
            w = w.astype(BF16)
            vb = v_ref[pl.ds(k0, tk), :]
            wcat = jnp.concatenate([rows(w, h) for h in range(nh)], axis=1)
            vcat = jnp.concatenate([jnp.where(vlane == h, vb, jnp.zeros_like(vb)) for h in range(nh)], axis=0)
            acc = acc + _dot(wcat, vcat)
            run = run + sums[:, tk:]
        return run, acc

    run = jnp.zeros((nh * tq, tk), F32)
    acc = jnp.zeros((tq, width), F32)
    run, acc = sweep(pl.multiple_of(qi * tq, tq), run, acc, True)

    def live(run):
        return jnp.max(run) > SB_CUT

    def cond(c):
        return c[0]

    def body(c):
        _, j, run, acc = c
        run, acc = sweep(pl.multiple_of(j * tq, tq), run, acc, False)
        return (j > 0) & live(run), j - 1, run, acc

    _, _, _, acc = lax.while_loop(cond, body, ((qi > 0) & live(run), qi - 1, run, acc))
    o_ref[...] = acc


def sb_attention(qkv, *, batch, heads, tq=256, nh=4):
    t, w3 = qkv.shape
    s = t // batch
    nq = s // tq
    width = nh * HEAD_DIM
    groups = heads // nh
    tk = LANE
    jj = jnp.arange(2 * tk)[:, None] % tk
    ss = jnp.arange(2 * tk)[None, :]
    u = jnp.where(ss < tk, jj > ss, True).astype(BF16)
    return pl.pallas_call(
        functools.partial(_sb_attn_kernel, tq=tq, tk=tk, nh=nh, scale=1.0 / math.sqrt(HEAD_DIM)),
        grid=(batch, groups, nq),
        in_specs=[pl.BlockSpec((tq, width), lambda b, p, i: (b * nq + i, p)),
                  pl.BlockSpec((s, width), lambda b, p, i: (b, groups + p)),
                  pl.BlockSpec((s, width), lambda b, p, i: (b, 2 * groups + p)),
                  pl.BlockSpec((2 * tk, 2 * tk), lambda b, p, i: (0, 0))],
        out_specs=pl.BlockSpec((tq, width), lambda b, p, i: (b * nq + i, p)),
        out_shape=jax.ShapeDtypeStruct((t, w3 // 3), F32),
        compiler_params=_cp("parallel", "parallel", "arbitrary"),
        name="sb_attention",
    )(qkv, qkv, qkv, u)


def _softmax_attn_kernel(qn_ref, qr_ref, kn_ref, kr_ref, v_ref, o_ref, *bufs, tq, tk, npair, scale):
    qi = pl.program_id(2)
    g = pl.program_id(1)
    nh = 2 * npair
    lane = _iota((tq, LANE), 1)
    qr = qr_ref[...] * (scale * math.log2(math.e))
    slots = LANE // MLA_ROPE
    qs = []
    for h in range(nh):
        cols = slice(h // 2 * LANE, (h // 2 + 1) * LANE)
        qn = qn_ref[:, cols] * (scale * math.log2(math.e))
        nope = jnp.where(lane // MLA_NOPE == h % 2, qn, 0.0)
        rope_ = jnp.where(lane // MLA_ROPE == (g * nh + h) % slots, qr, 0.0)
        qs.append(jnp.concatenate([nope, rope_], axis=1).astype(BF16))

    def score_head(h, j, dst):
        k0 = pl.multiple_of(j * tk, tk)
        cols = slice(h // 2 * LANE, (h // 2 + 1) * LANE)
        kb = jnp.concatenate([kn_ref[pl.ds(k0, tk), cols], kr_ref[pl.ds(k0, tk), :]], axis=1)
        s = _dot_nt(qs[h], kb)
        dst[...] = s
        return jnp.max(s, axis=1, keepdims=True)

    def consume_head(h, src, j, carry, mx):
        cols = slice(h // 2 * LANE, (h // 2 + 1) * LANE)
        vb = v_ref[pl.ds(pl.multiple_of(j * tk, tk), tk), cols]
        mine = (_iota(vb.shape, 1) // MLA_V) == h % 2
        vbh = jnp.where(mine, vb, jnp.ones_like(vb))
        m, acc = carry
        s = src[...]
        if mx is None:
            s = jnp.where(_iota((tq, tk), 1) <= _iota((tq, tk), 0), s, -1e30)
            m_new = jnp.maximum(m, jnp.max(s, axis=1, keepdims=True))
        else:
            m_new = jnp.maximum(m, mx)
        p = jnp.exp2(s - m_new)
        return m_new, jnp.exp2(m - m_new) * acc + _dot(p.astype(BF16), vbh)

    def scores(j, dsts):
        return tuple(score_head(h, j, dsts[h]) for h in range(nh))

    def consume(srcs, j, carry, mx):
        return tuple(consume_head(h, srcs[h], j, carry[h], None if mx is None else mx[h])
                     for h in range(nh))

    def advance(srcs, dsts, j, carry, mx):
        out, nmx = [], []
        for h in range(nh):
            nmx.append(score_head(h, j + 1, dsts[h]))
            out.append(consume_head(h, srcs[h], j, carry[h], mx[h]))
        return tuple(out), tuple(nmx)

    sa, sb = bufs[:nh], bufs[nh:]

    def pair(t, carry):
        state, mxa = carry
        state, mxb = advance(sa, sb, 2 * t, state, mxa)
        return advance(sb, sa, 2 * t + 1, state, mxb)

    def tail_odd(carry):
        state, mxa = carry
        scores(qi, sb)
        state = consume(sa, qi - 1, state, mxa)
        return consume(sb, qi, state, None)

    def tail_even(carry):
        return consume(sa, qi, carry[0], None)

    init = ((jnp.full((tq, 1), -1e30, F32), jnp.zeros((tq, LANE), F32)),) * nh
    carry = lax.fori_loop(0, qi // 2, pair, (init, scores(0, sa)))
    state = lax.cond(qi % 2 == 1, tail_odd, tail_even, carry)
    for pr in range(npair):
        acc0, acc1 = state[2 * pr][1], state[2 * pr + 1][1]
        o_ref[:, pr * LANE:(pr + 1) * LANE] = jnp.where(
            lane < MLA_V, acc0 / pltpu.roll(acc0, MLA_V, axis=1), acc1 / pltpu.roll(acc1, MLA_V, axis=1))


def softmax_attention(q, qrot, kv, krot, scale, *, batch, heads, tq=512, npair=2):
    tk = tq
    t = q.shape[0]
    s = t // batch
    nq = s // tq
    width = npair * LANE
    groups = heads * MLA_NOPE // width
    rope_per_group = LANE // (2 * npair * MLA_ROPE)
    buf = pltpu.VMEM((tq, tk), F32)
    return pl.pallas_call(
        functools.partial(_softmax_attn_kernel, tq=tq, tk=tk, npair=npair, scale=scale),
        grid=(batch, groups, nq),
        in_specs=[pl.BlockSpec((tq, width), lambda b, p, i: (b * nq + i, p)),
                  pl.BlockSpec((tq, LANE), lambda b, p, i: (b * nq + i, p // rope_per_group)),
                  pl.BlockSpec((s, width), lambda b, p, i: (b, p)),
                  pl.BlockSpec((s, LANE), lambda b, p, i: (b, 0)),
                  pl.BlockSpec((s, width), lambda b, p, i: (b, groups + p))],
        out_specs=pl.BlockSpec((tq, width), lambda b, p, i: (b * nq + i, p)),
        out_shape=jax.ShapeDtypeStruct((t, heads * MLA_V), F32),
        scratch_shapes=[buf] * (4 * npair),
        compiler_params=_cp("parallel", "parallel", "arbitrary"),
        name="softmax_attention",
    )(q, qrot, kv, krot, kv)


def _rope_kernel(q_ref, k_ref, pos_ref, f_ref, oq_ref, ok_ref):
    half = MLA_ROPE // 2
    ang = pos_ref[...].astype(F32) * f_ref[...]
    cos, sin = jnp.cos(ang), jnp.sin(ang)
    low = _iota(ang.shape, 1) % MLA_ROPE < half

    def rot(x):
        partner = jnp.where(low, -pltpu.roll(x, LANE - half, axis=1), pltpu.roll(x, half, axis=1))
        return x * cos + partner * sin

    for c in range(q_ref.shape[1] // LANE):
        cols = slice(c * LANE, (c + 1) * LANE)
        oq_ref[:, cols] = rot(q_ref[:, cols])
    ok_ref[...] = rot(k_ref[...]).astype(ok_ref.dtype)


def rope(q, c, positions, *, q_col, qw, k_col, tm=512):
    t = q.shape[0]
    half = MLA_ROPE // 2
    inv_freq = ROPE_THETA ** (-jnp.arange(half, dtype=F32) / half)
    freq = jnp.tile(inv_freq, LANE // half).reshape(1, LANE)
    return pl.pallas_call(
        _rope_kernel,
        grid=(t // tm,),
        in_specs=[pl.BlockSpec((tm, qw), lambda i: (i, q_col)), pl.BlockSpec((tm, LANE), lambda i: (i, k_col)),
                  pl.BlockSpec((tm, 1), lambda i: (i, 0)), pl.BlockSpec((1, LANE), lambda i: (0, 0))],
        out_specs=[pl.BlockSpec((tm, qw), lambda i: (i, 0)), pl.BlockSpec((tm, LANE), lambda i: (i, 0))],
        out_shape=[jax.ShapeDtypeStruct((t, qw), F32), jax.ShapeDtypeStruct((t, LANE), BF16)],
        compiler_params=_cp("parallel"),
        name="rope",
    )(q, c, positions.reshape(t, 1), freq)


def _rwkv_prep_kernel(x_ref, xp_ref, mu_ref, w0_ref, a0_ref, kk_ref, ka_ref, rk_ref,
                      w2_ref, a2_ref, g2_ref, tri_ref, hs_ref,
                      at_ref, bt_ref, kt_ref, rt_ref, v_ref, eg_ref, g_ref, bv_ref, *, tm, seq, width):
    i = pl.program_id(0)
    x = x_ref[...]
    at_start = (i * tm) % seq == 0
    prev_row = jnp.where(at_start, 0.0, xp_ref[7:8, :])
    rolled = pltpu.roll(x, 1, axis=0)
    top = _iota((8, x.shape[1]), 0)
    prev = jnp.concatenate([jnp.where(top == 0, prev_row, rolled[:8]), rolled[8:]], axis=0)
    xm = x + (prev - x) * mu_ref[...]
    r = xm[:, :width]
    k = xm[:, width:2 * width]
    v = xm[:, 2 * width:3 * width]
    l1 = xm[:, 3 * width:3 * width + LANE]
    l2 = xm[:, 3 * width + LANE:]
    wpre = w0_ref[...] + _dot(_Split(jnp.tanh(l1)).left(1), w2_ref[...])
    a = _sigmoid(a0_ref[...] + _dot(_Split(l1).left(1), a2_ref[...]))
    g = _dot(_Split(_sigmoid(l2)).left(1), g2_ref[...])

    def hilo(t):
        ts = _Split(t)
        return jnp.concatenate([ts.hi, ts.lo], axis=1)

    lw = -jnp.exp(-_softplus(-wpre) - 0.5)
    hs = hs_ref[...]
    kk = k * kk_ref[...]
    kk = kk * lax.rsqrt(jnp.maximum(_dot(hilo(kk * kk), hs), 1e-24))
    km = k * (1.0 + (a - 1.0) * ka_ref[...])
    bonus = _dot(hilo(r * km * rk_ref[...]), hs)
    lws = _Split(lw)
    cum = _dot(tri_ref[...], jnp.concatenate([lws.hi, lws.lo], axis=0))
    eg = jnp.exp(cum)
    ieg = jnp.exp(-cum)
    at_ref[...] = -kk * jnp.exp(cum - lw)
    bt_ref[...] = kk * a * ieg
    kt_ref[...] = km * ieg
    rt_ref[...] = r * eg
    v_ref[...] = v
    eg_ref[...] = eg
    g_ref[...] = g
    bv_ref[...] = bonus * v


def rwkv_prep(x, mu, w0, a0, k_k, k_a, r_k, w2p, a2p, g2p, *, seq, width, tm=256):
    t, wx = x.shape
    ch = RW_CHUNK
    ti = jnp.arange(tm)
    tri = ((ti[:, None] // ch == ti[None, :] // ch) & (ti[None, :] <= ti[:, None])).astype(BF16)
    tri = jnp.concatenate([tri, tri], axis=1)
    ci = jnp.arange(width) // HEAD_DIM
    hs = (ci[:, None] == ci[None, :]).astype(BF16)
    hs = jnp.concatenate([hs, hs], axis=0)
    w2p, a2p, g2p = (_Split(w).right(0) for w in (w2p, a2p, g2p))
    hb = tm // 8
    row = lambda i: (i, 0)
    fix = lambda i: (0, 0)
    vec = pl.BlockSpec((1, width), fix)
    out = jax.ShapeDtypeStruct((t, width), F32)
    return pl.pallas_call(
        functools.partial(_rwkv_prep_kernel, tm=tm, seq=seq, width=width),
        grid=(t // tm,),
        in_specs=[pl.BlockSpec((tm, wx), row),
                  pl.BlockSpec((8, wx), lambda i: (jnp.maximum(i * hb - 1, 0), 0)),
                  pl.BlockSpec((1, wx), fix), vec, vec, vec, vec, vec,
                  pl.BlockSpec(w2p.shape, fix), pl.BlockSpec(a2p.shape, fix),
                  pl.BlockSpec(g2p.shape, fix), pl.BlockSpec(tri.shape, fix),
                  pl.BlockSpec(hs.shape, fix)],
        out_specs=[pl.BlockSpec((tm, width), row)] * 8,
        out_shape=[out] * 8,
        compiler_params=_cp("parallel"),
        name="rwkv_prep",
    )(x, x, mu, w0, a0, k_k, k_a, r_k, w2p, a2p, g2p, tri, hs)


def _rwkv_chunk_kernel(at_ref, bt_ref, kt_ref, rt_ref, v_ref, eg_ref,
                       m_ref, n_ref, rq_ref, yl_ref, *, nchunk, unroll):
    ch, n = RW_CHUNK, HEAD_DIM
    ti = _iota((ch, 2 * ch), 0)
    si = _iota((ch, 2 * ch), 1) % ch
    strict = si < ti
    incl = si <= ti
    same = (ti // RW_SUB) == (si // RW_SUB)
    eye = _iota((n, n), 0) == _iota((n, n), 1)
    each = lambda f, *ls: [f(*t) for t in zip(*ls)]

    def body(it, _):
        cs = [it * unroll + u for u in range(unroll)]
        sls = [pl.ds(pl.multiple_of(c * ch, ch), ch) for c in cs]
        ids = [(h, u) for h in range(2) for u in range(unroll)]
        ld = lambda ref: [ref[sls[u], h * n:(h + 1) * n] for h, u in ids]
        sp = lambda xs: [_Split(t) for t in xs]
        dup = lambda xs: [_SplitDup(t) for t in xs]
        twice = lambda t: jnp.concatenate([t, t], axis=0)
        a_, b_, k_, r_, v_ = ld(at_ref), ld(bt_ref), ld(kt_ref), ld(rt_ref), ld(v_ref)
        g_end = [eg_ref[pl.ds(cs[u] * ch + ch - 1, 1), h * n:(h + 1) * n] for h, u in ids]
        ar = sp(each(lambda a, r: jnp.concatenate([a, r], axis=0), a_, r_))
        pbk = each(lambda x, y, z: _dot_nt(x.left(1), jnp.concatenate(
            [twice(y.right(1)), twice(z.right(1))], axis=0)), ar, sp(b_), sp(k_))
        pb = [t[:, :2 * ch] for t in pbk]
        pk = [t[:, 2 * ch:] for t in pbk]
        a_ab = [jnp.where(strict, t[:ch], 0.0) for t in pb]
        a_ak = dup([jnp.where(strict, t[:ch], 0.0) for t in pk])
        a_rb = dup([jnp.where(incl, t[ch:], 0.0) for t in pb])
        a_rk = dup([jnp.where(incl, t[ch:], 0.0) for t in pk])
        d1 = [jnp.where(same, t, 0.0) for t in a_ab]
        lb = each(lambda t, d: t - d, a_ab, d1)
        d1 = dup(d1)
        d2 = dup([_dot(d.left, d.right) for d in d1])
        akv = each(_mm3_dup, a_ak, v_)
        x = each(lambda l, a, t: jnp.concatenate([l, a, t], axis=1), lb, a_, akv)
        d4 = dup([_dot(d.left, d.right) for d in d2])
        x = each(lambda d, t: t + _mm3_dup(d, t), d1, x)
        d8 = dup([_dot(d.left, d.right) for d in d4])
        x = each(lambda d, t: t + _mm3_dup(d, t), d2, x)
        x = each(lambda d, t: t + _mm3_dup(d, t), d4, x)
        x = each(lambda d, t: t + _mm3_dup(d, t), d8, x)
        e1 = dup([t[:, :2 * ch] for t in x])
        wu = [t[:, 2 * ch:] for t in x]
        e2 = dup([_dot(e.left, e.right) for e in e1])
        wu = each(lambda e, t: t + _mm3_dup(e, t), e1, wu)
        wu = each(lambda e, t: t + _mm3_dup(e, t), e2, wu)
        ry = each(_mm3_dup, a_rb, wu)
        rkv = each(_mm3_dup, a_rk, v_)
        wus, vs = sp(wu), sp(v_)
        mn = each(lambda b, g, t: _mm3_tn(_Split(b * g), t), b_, g_end, wus)
        kv = each(lambda k, g, t: _mm3_tn(_Split(k * g), t), k_, g_end, vs)
        rq = each(lambda r, t: r + t[:, :n], r_, ry)
        yl = each(lambda t, w: t[:, n:] + w, ry, rkv)
        for u in range(unroll):
            rq_ref[sls[u], :] = jnp.concatenate([rq[u], rq[unroll + u]], axis=1)
            yl_ref[sls[u], :] = jnp.concatenate([yl[u], yl[unroll + u]], axis=1)
        for i, (h, u) in enumerate(ids):
            m_ref[h, cs[u]] = mn[i][:, :n] + jnp.where(eye, g_end[i], 0.0)
            n_ref[h, cs[u]] = mn[i][:, n:] + kv[i]
        return 0

    lax.fori_loop(0, nchunk // unroll, body, 0)


def rwkv_chunk(at, bt, kt, rt, v, eg, *, batch, nchunk=8, unroll=8):
    t, w = at.shape
    s = t // batch
    ch, n = RW_CHUNK, HEAD_DIM
    pairs = w // LANE
    rows = nchunk * ch
    nr = s // rows
    spec = pl.BlockSpec((rows, LANE), lambda b, p, i: (b * nr + i, p))
    mspec = pl.BlockSpec((2, nchunk, n, n), lambda b, p, i: (b * pairs + p, i, 0, 0))
    return pl.pallas_call(
        functools.partial(_rwkv_chunk_kernel, nchunk=nchunk, unroll=unroll),
        grid=(batch, pairs, nr),
        in_specs=[spec] * 6,
        out_specs=[mspec, mspec, spec, spec],
        out_shape=[jax.ShapeDtypeStruct((batch * w // n, s // ch, n, n), F32)] * 2
        + [jax.ShapeDtypeStruct((t, w), F32)] * 2,
        compiler_params=_cp("parallel", "parallel", "parallel"),
        name="rwkv_chunk",
    )(at, bt, kt, rt, v, eg)


def _rwkv_scan_kernel(m_ref, n_ref, s_ref, st_ref, *, heads, nchunk):
    @pl.when(pl.program_id(1) == 0)
    def _():
        st_ref[...] = jnp.zeros_like(st_ref)

    def body(c, _):
        sts = [st_ref[hh] for hh in range(heads)]
        for hh in range(heads):
            s_ref[hh, c] = sts[hh]
        ms = [_Split(m_ref[hh, c]) for hh in range(heads)]
        ss = [_Split(st) for st in sts]
        new = [_mm3(a, b) for a, b in zip(ms, ss)]
        for hh in range(heads):
            st_ref[hh] = new[hh] + n_ref[hh, c]
        return 0

    lax.fori_loop(0, nchunk, body, 0)


def rwkv_scan(m, nn, *, heads=16, nchunk=8):
    bh, nc, n, _ = m.shape
    heads = min(heads, bh)
    nchunk = min(nchunk, nc)
    spec = pl.BlockSpec((heads, nchunk, n, n), lambda b, i: (b, i, 0, 0))
    return pl.pallas_call(
        functools.partial(_rwkv_scan_kernel, heads=heads, nchunk=nchunk),
        grid=(bh // heads, nc // nchunk),
        in_specs=[spec, spec],
        out_specs=spec,
        out_shape=jax.ShapeDtypeStruct((bh, nc, n, n), F32),
        scratch_shapes=[pltpu.VMEM((heads, n, n), F32)],
        compiler_params=_cp("parallel", "arbitrary"),
        name="rwkv_scan",
    )(m, nn)


def _rwkv_out_kernel(rq_ref, yl_ref, s_ref, lw_ref, lb_ref, avg_ref, o_ref, *, nchunk):
    ch, n = RW_CHUNK, HEAD_DIM
    ys = []
    for h in range(2):
        cols = slice(h * n, (h + 1) * n)
        ys.append(jnp.concatenate(
            [_mm3(_Split(rq_ref[c * ch:(c + 1) * ch, cols]), _Split(s_ref[h, c])) for c in range(nchunk)],
            axis=0))
    y = jnp.concatenate(ys, axis=1) + yl_ref[...]
    avg2 = jnp.concatenate([avg_ref[...]] * 2, axis=0)

    def mean(t):
        ts = _Split(t)
        return _dot(jnp.concatenate([ts.hi, ts.lo], axis=1), avg2)

    d = y - mean(y)
    var = mean(d * d)
    o_ref[...] = d * lax.rsqrt(var + GN_EPS) * lw_ref[...] + lb_ref[...]


def rwkv_out(rq, yl, s0, ln_w, ln_b, *, batch, nchunk=8):
    t, w = rq.shape
    s = t // batch
    n = HEAD_DIM
    pairs = w // LANE
    rows = nchunk * RW_CHUNK
    nr = s // rows
    li = jnp.arange(LANE) // n
    avg = ((li[:, None] == li[None, :]).astype(F32) / n).astype(BF16)
    spec = pl.BlockSpec((rows, LANE), lambda b, p, i: (b * nr + i, p))
    hspec = pl.BlockSpec((1, LANE), lambda b, p, i: (0, p))
    return pl.pallas_call(
        functools.partial(_rwkv_out_kernel, nchunk=nchunk),
        grid=(batch, pairs, nr),
        in_specs=[spec, spec, pl.BlockSpec((2, nchunk, n, n), lambda b, p, i: (b * pairs + p, i, 0, 0)),
                  hspec, hspec, pl.BlockSpec((LANE, LANE), lambda b, p, i: (0, 0))],
        out_specs=spec,
        out_shape=jax.ShapeDtypeStruct((t, w), F32),
        compiler_params=_cp("parallel", "parallel", "parallel"),
        name="rwkv_out",
    )(rq, yl, s0, ln_w.reshape(1, w), ln_b.reshape(1, w), avg)


def _hyb_out_kernel(oa_ref, y_ref, bv_ref, g_ref, wa_ref, wb_ref, r_ref, o_ref):
    ob = (y_ref[...] + bv_ref[...]) * g_ref[...]
    o_ref[...] = (r_ref[...] + _dot(oa_ref[...].astype(BF16), wa_ref[...])
                  + _dot(ob.astype(BF16), wb_ref[...]))


def hyb_out(res, oa, y, bv, g, wa, wb, *, tm=1024):
    t, d = res.shape
    row = lambda i: (i, 0)
    fix = lambda i: (0, 0)
    act = lambda a: pl.BlockSpec((tm, a.shape[1]), row)
    return pl.pallas_call(
        _hyb_out_kernel,
        grid=(t // tm,),
        in_specs=[act(oa), act(y), act(bv), act(g), pl.BlockSpec(wa.shape, fix),
                  pl.BlockSpec(wb.shape, fix), pl.BlockSpec((tm, d), row)],
        out_specs=pl.BlockSpec((tm, d), row),
        out_shape=jax.ShapeDtypeStruct((t, d), F32),
        compiler_params=_cp("parallel"),
        name="hyb_out",
    )(oa, y, bv, g, wa, wb, res)


def _pad_cols(w, n):
    return jnp.pad(w, ((0, 0), (0, n - w.shape[1])))


def sb_rwkv_layer(h, b, s, attn_norm, w_in, w_out, mu, w0, w2, a0, a2, g2, k_k, k_a, r_k, ln_w, ln_b):
    sbw = SB_HEADS * HEAD_DIM
    rww = RW_HEADS * HEAD_DIM
    rw_in = w_in.shape[1] - 3 * sbw
    rw_pad = -(-rw_in // LANE) * LANE
    dl, al = w2.shape[0], a2.shape[0]
    gl = g2.shape[0]
    assert dl + al == LANE and 3 * rww % LANE == 0

    qkv = rms_matmul(h, attn_norm, w_in[:, :3 * sbw].astype(BF16), out_dtype=BF16)
    rw = rms_matmul(h, attn_norm, _pad_cols(w_in[:, 3 * sbw:], rw_pad).astype(BF16))
    oa = sb_attention(qkv, batch=b, heads=SB_HEADS)

    w2p = jnp.zeros((LANE, rww), F32).at[:dl].set(w2)
    a2p = jnp.zeros((LANE, rww), F32).at[dl:].set(a2)
    g2p = jnp.zeros((rw_pad - 3 * rww - LANE, rww), F32).at[:gl].set(g2)
    vec = lambda t: t.reshape(1, rww).astype(F32)
    mu_p = _pad_cols(mu.reshape(1, rw_in), rw_pad)
    at, bt, kt, rt, vv, eg, g, bv = rwkv_prep(rw, mu_p, vec(w0), vec(a0), vec(k_k), vec(k_a), vec(r_k),
                                              w2p, a2p, g2p, seq=s, width=rww)
    m, nn, rq, yl = rwkv_chunk(at, bt, kt, rt, vv, eg, batch=b)
    s0 = rwkv_scan(m, nn)
    y = rwkv_out(rq, yl, s0, ln_w, ln_b, batch=b)
    return hyb_out(h, oa, y, bv, g, w_out[:sbw].astype(BF16), w_out[sbw:].astype(BF16))


def mla_layer(h, b, s, positions, attn_norm, w_down, q_norm, kv_norm, w_uq, w_ukv, w_o):
    nh, dn, dr, dv = MLA_HEADS, MLA_NOPE, MLA_ROPE, MLA_V
    qr, kvr = MLA_Q_RANK, MLA_KV_RANK
    assert dn == dv and qr % LANE == 0 and (qr + LANE) % kvr == 0
    wd = jnp.concatenate([w_down[:, :qr], jnp.tile(w_down[:, qr + kvr:], (1, LANE // dr)),
                          w_down[:, qr:qr + kvr]], axis=1)
    c = rms_matmul(h, attn_norm, wd.astype(BF16))
    wq = w_uq.reshape(qr, nh, dn + dr)
    wq = jnp.concatenate([wq[:, :, :dn].reshape(qr, nh * dn), wq[:, :, dn:].reshape(qr, nh * dr)], axis=1)
    q = rms_matmul(c, q_norm, wq.astype(BF16), x_col=0)
    wkv = w_ukv.reshape(kvr, nh, dn + dv)
    wkv = jnp.concatenate([wkv[:, :, :dn].reshape(kvr, nh * dn), wkv[:, :, dn:].reshape(kvr, nh * dv)], axis=1)
    kv = rms_matmul(c, kv_norm, wkv.astype(BF16), x_col=(qr + LANE) // kvr, out_dtype=BF16)
    qrot, krot = rope(q, c, positions, q_col=dn // dr, qw=nh * dr, k_col=qr // LANE)
    o = softmax_attention(q, qrot, kv, krot, 1.0 / math.sqrt(dn + dr), batch=b, heads=nh)
    return matmul_residual(h, o, w_o.astype(BF16))


def kernel(x, p, positions, attn_norm, ffn_norm, ffn_w_in, ffn_conv_w, ffn_conv_b, ffn_w_out, ple_w_proj, ple_norm, ple_gate_norm, ple_w_gate, hyb_w_in, hyb_w_out, rw_mu, rw_w0, rw_w2, rw_a0, rw_a2, rw_g2, rw_k_k, rw_k_a, rw_r_k, rw_ln_w, rw_ln_b, mla_w_down, mla_q_norm, mla_kv_norm, mla_w_uq, mla_w_ukv, mla_w_o, final_norm):
    b, s, d = x.shape
    depth = p.shape[0]
    h = x.reshape(b * s, d)
    pos = positions.reshape(b * s)
    for i in range(depth):
        j = i // 2
        if i % 2 == 0:
            h = sb_rwkv_layer(h, b, s, attn_norm[i], hyb_w_in[j], hyb_w_out[j], rw_mu[j], rw_w0[j],
                              rw_w2[j], rw_a0[j], rw_a2[j], rw_g2[j], rw_k_k[j], rw_k_a[j], rw_r_k[j],
                              rw_ln_w[j], rw_ln_b[j])
        else:
            h = mla_layer(h, b, s, pos, attn_norm[i], mla_w_down[j], mla_q_norm[j], mla_kv_norm[j],
                          mla_w_uq[j], mla_w_ukv[j], mla_w_o[j])
        ug = rms_matmul(h, ffn_norm[i], ffn_w_in[i].astype(BF16), out_dtype=BF16)
        h = convglu_ple(h, ug, ffn_conv_w[i], ffn_conv_b[i], ffn_w_out[i], p[i].reshape(b * s, -1),
                        ple_w_proj[i], ple_norm[i], ple_gate_norm[i], ple_w_gate[i], final_norm,
                        seq=s, final=(i == depth - 1))
    return h.reshape(b, s, d)
```

```python
import functools
import math

import jax
import jax.numpy as jnp
from jax import lax
from jax.experimental import pallas as pl
from jax.experimental.pallas import tpu as pltpu

F32 = jnp.float32
BF16 = jnp.bfloat16
HI = lax.Precision.HIGHEST

NORM_EPS = 1e-6
GN_EPS = 64e-5
ROPE_THETA = 10000.0
LANE = 128
VMEM_LIMIT = 48 * 1024 * 1024
HALO = 16
MXU_WIDTH = 256
TILE_VMEM_BUDGET = 36 * 1024 * 1024
WEIGHT_TILE_BYTES = 6 * 1024 * 1024

SB_HEADS = 8
RW_HEADS = 8
HEAD_DIM = 64
MLA_HEADS = 16
MLA_NOPE = 64
MLA_ROPE = 32
MLA_V = 64
MLA_Q_RANK = 384
MLA_KV_RANK = 256
SB_CUT = -151.0
RW_CHUNK = 64
RW_SUB = 16


def _cp(*sem):
    return pltpu.CompilerParams(dimension_semantics=sem, vmem_limit_bytes=VMEM_LIMIT)


def _iota(shape, dim):
    return lax.broadcasted_iota(jnp.int32, shape, dim)


def _softplus(z):
    return jnp.maximum(z, 0.0) + jnp.log1p(jnp.exp(-jnp.abs(z)))


def _sigmoid(z):
    return 1.0 / (1.0 + jnp.exp(-z))


def _dot(a, b, precision=None):
    return jnp.dot(a, b, preferred_element_type=F32, precision=precision)


def _dot_nt(a, b, precision=None):
    return lax.dot_general(a, b, (((1,), (1,)), ((), ())), preferred_element_type=F32,
                           precision=precision)


def _dot_tn(a, b, precision=None):
    return lax.dot_general(a, b, (((0,), (0,)), ((), ())), preferred_element_type=F32,
                           precision=precision)


class _Split:
    def __init__(self, x):
        self.x = x
        self.hi = x.astype(BF16)
        self.lo = (x - self.hi.astype(F32)).astype(BF16)
        self._packed = {}

    def left(self, axis):
        key = ("l", axis)
        if key not in self._packed:
            self._packed[key] = jnp.concatenate([self.hi, self.lo, self.hi], axis=axis)
        return self._packed[key]

    def right(self, axis):
        key = ("r", axis)
        if key not in self._packed:
            self._packed[key] = jnp.concatenate([self.hi, self.hi, self.lo], axis=axis)
        return self._packed[key]


def _mm3(a, b):
    return _dot(a.left(1), b.right(0))


def _mm3_nt(a, b):
    return _dot_nt(a.left(1), b.right(1))


def _mm3_tn(a, b):
    return _dot_tn(a.left(0), b.right(0))


class _SplitDup:
    def __init__(self, xd):
        hi = xd.astype(BF16)
        lo = (xd - hi.astype(F32)).astype(BF16)
        self.left = jnp.concatenate([jnp.where(_iota(xd.shape, 1) < HEAD_DIM, hi, lo), hi], axis=1)
        self.right = _rows_packed(hi, lo)


def _rows_packed(hi, lo):
    return jnp.concatenate([hi, hi, lo, jnp.zeros_like(hi)], axis=0)


def _mm3_dup(a, b):
    bs = _Split(b)
    return _dot(a.left, _rows_packed(bs.hi, bs.lo))


def _rms(x, g, eps=NORM_EPS):
    return x * lax.rsqrt(jnp.mean(x * x, axis=-1, keepdims=True) + eps) * g


def _rms_matmul_kernel(x_ref, g_ref, w_ref, o_ref, xn_ref):
    @pl.when(pl.program_id(1) == 0)
    def _():
        xn_ref[...] = _rms(x_ref[...].astype(F32), g_ref[...]).astype(BF16)

    o_ref[...] = _dot(xn_ref[...], w_ref[...]).astype(o_ref.dtype)


def _pick_tiles(t, k, n, tile_bytes):
    tn = max(c for c in range(LANE, n + 1, LANE)
             if n % c == 0 and (c == n or c % MXU_WIDTH == 0) and k * c * 2 <= WEIGHT_TILE_BYTES)
    for tm in (2048, 1024, 512, 256):
        if t % tm == 0 and tile_bytes(tm, tn) + 2 * k * tn * 2 <= TILE_VMEM_BUDGET:
            return tm, tn
    raise ValueError("no row tile fits")


def rms_matmul(x, g, w, *, x_col=0, out_dtype=F32):
    t = x.shape[0]
    k, n = w.shape
    ob = jnp.dtype(out_dtype).itemsize
    tm, tn = _pick_tiles(t, k, n, lambda tm, tn: 2 * tm * k * x.dtype.itemsize + tm * k * 2 + 2 * tm * tn * ob)
    return pl.pallas_call(
        _rms_matmul_kernel,
        grid=(t // tm, n // tn),
        in_specs=[pl.BlockSpec((tm, k), lambda i, j: (i, x_col)),
                  pl.BlockSpec((1, k), lambda i, j: (0, 0)),
                  pl.BlockSpec((k, tn), lambda i, j: (0, j))],
        out_specs=pl.BlockSpec((tm, tn), lambda i, j: (i, j)),
        out_shape=jax.ShapeDtypeStruct((t, n), out_dtype),
        scratch_shapes=[pltpu.VMEM((tm, k), BF16)],
        compiler_params=_cp("parallel", "arbitrary"),
        name="rms_matmul",
    )(x, g.reshape(1, k).astype(F32), w)


def _matmul_res_kernel(a_ref, w_ref, r_ref, o_ref):
    o_ref[...] = r_ref[...] + _dot(a_ref[...].astype(BF16), w_ref[...])


def matmul_residual(res, a, w):
    t, k = a.shape
    n = w.shape[1]
    tm, tn = _pick_tiles(t, k, n, lambda tm, tn: 2 * tm * k * a.dtype.itemsize + 4 * tm * tn * 4)
    return pl.pallas_call(
        _matmul_res_kernel,
        grid=(t // tm, n // tn),
        in_specs=[pl.BlockSpec((tm, k), lambda i, j: (i, 0)),
                  pl.BlockSpec((k, tn), lambda i, j: (0, j)),
                  pl.BlockSpec((tm, tn), lambda i, j: (i, j))],
        out_specs=pl.BlockSpec((tm, tn), lambda i, j: (i, j)),
        out_shape=jax.ShapeDtypeStruct((t, n), F32),
        compiler_params=_cp("parallel", "arbitrary"),
        name="matmul_residual",
    )(a, w, res)


def _convglu_ple_kernel(u_ref, g_ref, gp_ref, cw_ref, cb_ref, w_ref, r_ref,
                        p_ref, wp_ref, pn_ref, gn_ref, wg_ref, fn_ref, o_ref, *, tm, seq, final):
    i = pl.program_id(0)
    g = g_ref[...].astype(F32)
    at_start = (i * tm) % seq == 0
    halo = jnp.where(at_start, 0.0, gp_ref[...].astype(F32))
    h1, h2 = halo[HALO - 1:HALO, :], halo[HALO - 2:HALO - 1, :]
    r1, r2 = pltpu.roll(g, 1, axis=0), pltpu.roll(g, 2, axis=0)
    top = _iota((8, g.shape[1]), 0)
    p1 = jnp.concatenate([jnp.where(top == 0, h1, r1[:8]), r1[8:]], axis=0)
    p2 = jnp.concatenate([jnp.where(top == 0, h2, jnp.where(top == 1, h1, r2[:8])), r2[8:]], axis=0)
    c = cw_ref[0:1, :] * p2 + cw_ref[1:2, :] * p1 + cw_ref[2:3, :] * g + cb_ref[...]
    act = c * (1.0 + lax.erf(c)) * u_ref[...].astype(F32)
    h = r_ref[...] + _dot(act.astype(BF16), w_ref[...])
    e = _rms(_dot(p_ref[...].astype(BF16), wp_ref[...]), pn_ref[...])
    gate = _sigmoid(_dot(_rms(h, gn_ref[...]).astype(BF16), wg_ref[...]))
    out = h + gate * e
    if final:
        out = _rms(out, fn_ref[...])
    o_ref[...] = out


def convglu_ple(res, ug, conv_w, conv_b, w_out, p, w_proj, p_norm, g_norm, w_gate, f_norm,
                *, seq, final, tm=256):
    t, d = res.shape
    f = w_out.shape[0]
    pd = p.shape[1]
    hb = tm // HALO
    rs2 = 1.0 / math.sqrt(2.0)
    row = lambda i: (i, 0)
    fix = lambda i: (0, 0)
    vec = pl.BlockSpec((1, d), fix)
    return pl.pallas_call(
        functools.partial(_convglu_ple_kernel, tm=tm, seq=seq, final=final),
        grid=(t // tm,),
        in_specs=[pl.BlockSpec((tm, f), row),
                  pl.BlockSpec((tm, f), lambda i: (i, 1)),
                  pl.BlockSpec((HALO, f), lambda i: (jnp.maximum(i * hb - 1, 0), 1)),
                  pl.BlockSpec((3, f), fix), pl.BlockSpec((1, f), fix), pl.BlockSpec((f, d), fix),
                  pl.BlockSpec((tm, d), row), pl.BlockSpec((tm, pd), row), pl.BlockSpec((pd, d), fix),
                  vec, vec, pl.BlockSpec((d, d), fix), vec],
        out_specs=pl.BlockSpec((tm, d), row),
        out_shape=jax.ShapeDtypeStruct((t, d), F32),
        compiler_params=_cp("parallel"),
        name="convglu_ple",
    )(ug, ug, ug, conv_w * rs2, (conv_b * rs2).reshape(1, f), (w_out * rs2).astype(BF16), res,
      p, w_proj.astype(BF16), p_norm.reshape(1, d), g_norm.reshape(1, d), w_gate.astype(BF16),
      f_norm.reshape(1, d))


def _sb_attn_kernel(q_ref, k_ref, v_ref, u_ref, o_ref, *, tq, tk, nh, scale):
    qi = pl.program_id(2)
    nblk = tq // tk
    width = nh * HEAD_DIM
    qf = q_ref[...] * scale
    qlane = _iota((tq, width), 1) // HEAD_DIM
    qs = jnp.concatenate([jnp.where(qlane == h, qf, jnp.zeros_like(qf)) for h in range(nh)], axis=0)
    vlane = _iota((tk, width), 1) // HEAD_DIM
    rows = lambda x, h: x[h * tq:(h + 1) * tq]

    def sweep(k_lo, run, acc, masked):
        parts = []
        for d in reversed(range(nblk)):
            k0 = k_lo + d * tk
            t = _dot_nt(qs, k_ref[pl.ds(k0, tk), :]) * math.log2(math.e)
            sp = jnp.maximum(t, 0.0) + jnp.log2(1.0 + jnp.exp2(-jnp.abs(t)))
            lom = -sp
            valid = None
            if masked:
                valid = _iota((nh * tq, tk), 1) + d * tk < _iota((nh * tq, tk), 0) % tq
                lom = jnp.where(valid, lom, 0.0)
            hi = lom.astype(BF16)
            lo = (lom - hi.astype(F32)).astype(BF16)
            sums = _dot(jnp.concatenate([hi, lo], axis=1), u_ref[...])
            parts.append((k0, t - sp, sums, valid))
        for k0, logsig, sums, valid in parts:
            w = jnp.exp2(logsig + run + sums[:, :tk])
            if masked:
                w = jnp.where(valid, w, 0.0)
            w = w.astype(BF16)
            vb = v_ref[pl.ds(k0, tk), :]
            wcat = jnp.concatenate([rows(w, h) for h in range(nh)], axis=1)
            vcat = jnp.concatenate([jnp.where(vlane == h, vb, jnp.zeros_like(vb)) for h in range(nh)], axis=0)
            acc = acc + _dot(wcat, vcat)
            run = run + sums[:, tk:]
        return run, acc

    run = jnp.zeros((nh * tq, tk), F32)
    acc = jnp.zeros((tq, width), F32)
    run, acc = sweep(pl.multiple_of(qi * tq, tq), run, acc, True)

    def live(run):
        return jnp.max(run) > SB_CUT

    def cond(c):
        return c[0]

    def body(c):
        _, j, run, acc = c
        run, acc = sweep(pl.multiple_of(j * tq, tq), run, acc, False)
        return (j > 0) & live(run), j - 1, run, acc

    _, _, _, acc = lax.while_loop(cond, body, ((qi > 0) & live(run), qi - 1, run, acc))
    o_ref[...] = acc


def sb_attention(qkv, *, batch, heads, tq=256, nh=4):
    t, w3 = qkv.shape
    s = t // batch
    nq = s // tq
    width = nh * HEAD_DIM
    groups = heads // nh
    tk = LANE
    jj = jnp.arange(2 * tk)[:, None] % tk
    ss = jnp.arange(2 * tk)[None, :]
    u = jnp.where(ss < tk, jj > ss, True).astype(BF16)
    return pl.pallas_call(
        functools.partial(_sb_attn_kernel, tq=tq, tk=tk, nh=nh, scale=1.0 / math.sqrt(HEAD_DIM)),
        grid=(batch, groups, nq),
        in_specs=[pl.BlockSpec((tq, width), lambda b, p, i: (b * nq + i, p)),
                  pl.BlockSpec((s, width), lambda b, p, i: (b, groups + p)),
                  pl.BlockSpec((s, width), lambda b, p, i: (b, 2 * groups + p)),
                  pl.BlockSpec((2 * tk, 2 * tk), lambda b, p, i: (0, 0))],
        out_specs=pl.BlockSpec((tq, width), lambda b, p, i: (b * nq + i, p)),
        out_shape=jax.ShapeDtypeStruct((t, w3 // 3), F32),
        compiler_params=_cp("parallel", "parallel", "arbitrary"),
        name="sb_attention",
    )(qkv, qkv, qkv, u)


def _softmax_attn_kernel(qn_ref, qr_ref, kn_ref, kr_ref, v_ref, o_ref, *bufs, tq, tk, npair, scale):
    qi = pl.program_id(2)
    g = pl.program_id(1)
    nh = 2 * npair
    lane = _iota((tq, LANE), 1)
    qr = qr_ref[...] * (scale * math.log2(math.e))
    slots = LANE // MLA_ROPE
    qs = []
    for h in range(nh):
        cols = slice(h // 2 * LANE, (h // 2 + 1) * LANE)
        qn = qn_ref[:, cols] * (scale * math.log2(math.e))
        nope = jnp.where(lane // MLA_NOPE == h % 2, qn, 0.0)
        rope_ = jnp.where(lane // MLA_ROPE == (g * nh + h) % slots, qr, 0.0)
        qs.append(jnp.concatenate([nope, rope_], axis=1).astype(BF16))

    def scores(j, dsts):
        k0 = pl.multiple_of(j * tk, tk)
        mx = []
        for h in range(nh):
            cols = slice(h // 2 * LANE, (h // 2 + 1) * LANE)
            kb = jnp.concatenate([kn_ref[pl.ds(k0, tk), cols], kr_ref[pl.ds(k0, tk), :]], axis=1)
            s = _dot_nt(qs[h], kb)
            dsts[h][...] = s
            mx.append(jnp.max(s, axis=1, keepdims=True))
        return tuple(mx)

    def consume(srcs, j, carry, mx):
        out = []
        for h in range(nh):
            cols = slice(h // 2 * LANE, (h // 2 + 1) * LANE)
            vb = v_ref[pl.ds(pl.multiple_of(j * tk, tk), tk), cols]
            mine = (_iota(vb.shape, 1) // MLA_V) == h % 2
            vbh = jnp.where(mine, vb, jnp.ones_like(vb))
            m, acc = carry[h]
            s = srcs[h][...]
            if mx is None:
                s = jnp.where(_iota((tq, tk), 1) <= _iota((tq, tk), 0), s, -1e30)
                m_new = jnp.maximum(m, jnp.max(s, axis=1, keepdims=True))
            else:
                m_new = jnp.maximum(m, mx[h])
            p = jnp.exp2(s - m_new)
            acc = jnp.exp2(m - m_new) * acc + _dot(p.astype(BF16), vbh)
            out.append((m_new, acc))
        return tuple(out)

    sa, sb = bufs[:nh], bufs[nh:]

    def pair(t, carry):
        state, mxb = carry
        mxa = scores(2 * t + 1, sa)
        state = consume(sb, 2 * t, state, mxb)
        mxb = scores(2 * t + 2, sb)
        return consume(sa, 2 * t + 1, state, mxa), mxb

    def last_one(carry):
        state, mxb = carry
        return consume(sb, qi - 1, state, mxb)

    def last_two(carry):
        state, mxb = carry
        mxa = scores(qi - 1, sa)
        state = consume(sb, qi - 2, state, mxb)
        return consume(sa, qi - 1, state, mxa)

    init = ((jnp.full((tq, 1), -1e30, F32), jnp.zeros((tq, LANE), F32)),) * nh
    scores(qi, sa)

    def with_earlier_blocks():
        mxb = scores(0, sb)
        state = consume(sa, qi, init, None)
        carry = lax.fori_loop(0, (qi - 1) // 2, pair, (state, mxb))
        return lax.cond(qi % 2 == 1, last_one, last_two, carry)

    state = lax.cond(qi == 0, lambda: consume(sa, qi, init, None), with_earlier_blocks)
    for pr in range(npair):
        acc0, acc1 = state[2 * pr][1], state[2 * pr + 1][1]
        o_ref[:, pr * LANE:(pr + 1) * LANE] = jnp.where(
            lane < MLA_V, acc0 / pltpu.roll(acc0, MLA_V, axis=1), acc1 / pltpu.roll(acc1, MLA_V, axis=1))


def softmax_attention(q, qrot, kv, krot, scale, *, batch, heads, tq=512, npair=2):
    tk = tq
    t = q.shape[0]
    s = t // batch
    nq = s // tq
    width = npair * LANE
    groups = heads * MLA_NOPE // width
    rope_per_group = LANE // (2 * npair * MLA_ROPE)
    buf = pltpu.VMEM((tq, tk), F32)
    return pl.pallas_call(
        functools.partial(_softmax_attn_kernel, tq=tq, tk=tk, npair=npair, scale=scale),
        grid=(batch, groups, nq),
        in_specs=[pl.BlockSpec((tq, width), lambda b, p, i: (b * nq + i, p)),
                  pl.BlockSpec((tq, LANE), lambda b, p, i: (b * nq + i, p // rope_per_group)),
                  pl.BlockSpec((s, width), lambda b, p, i: (b, p)),
                  pl.BlockSpec((s, LANE), lambda b, p, i: (b, 0)),
                  pl.BlockSpec((s, width), lambda b, p, i: (b, groups + p))],
        out_specs=pl.BlockSpec((tq, width), lambda b, p, i: (b * nq + i, p)),
        out_shape=jax.ShapeDtypeStruct((t, heads * MLA_V), F32),
        scratch_shapes=[buf] * (4 * npair),
        compiler_params=_cp("parallel", "parallel", "arbitrary"),
        name="softmax_attention",
    )(q, qrot, kv, krot, kv)


def _rope_kernel(q_ref, k_ref, pos_ref, f_ref, oq_ref, ok_ref):
    half = MLA_ROPE // 2
    ang = pos_ref[...].astype(F32) * f_ref[...]
    cos, sin = jnp.cos(ang), jnp.sin(ang)
    low = _iota(ang.shape, 1) % MLA_ROPE < half

    def rot(x):
        partner = jnp.where(low, -pltpu.roll(x, LANE - half, axis=1), pltpu.roll(x, half, axis=1))
        return x * cos + partner * sin

    for c in range(q_ref.shape[1] // LANE):
        cols = slice(c * LANE, (c + 1) * LANE)
        oq_ref[:, cols] = rot(q_ref[:, cols])
    ok_ref[...] = rot(k_ref[...]).astype(ok_ref.dtype)


def rope(q, c, positions, *, q_col, qw, k_col, tm=512):
    t = q.shape[0]
    half = MLA_ROPE // 2
    inv_freq = ROPE_THETA ** (-jnp.arange(half, dtype=F32) / half)
    freq = jnp.tile(inv_freq, LANE // half).reshape(1, LANE)
    return pl.pallas_call(
        _rope_kernel,
        grid=(t // tm,),
        in_specs=[pl.BlockSpec((tm, qw), lambda i: (i, q_col)), pl.BlockSpec((tm, LANE), lambda i: (i, k_col)),
                  pl.BlockSpec((tm, 1), lambda i: (i, 0)), pl.BlockSpec((1, LANE), lambda i: (0, 0))],
        out_specs=[pl.BlockSpec((tm, qw), lambda i: (i, 0)), pl.BlockSpec((tm, LANE), lambda i: (i, 0))],
        out_shape=[jax.ShapeDtypeStruct((t, qw), F32), jax.ShapeDtypeStruct((t, LANE), BF16)],
        compiler_params=_cp("parallel"),
        name="rope",
    )(q, c, positions.reshape(t, 1), freq)


def _rwkv_prep_kernel(x_ref, xp_ref, mu_ref, w0_ref, a0_ref, kk_ref, ka_ref, rk_ref,
                      w2_ref, a2_ref, g2_ref, tri_ref, hs_ref,
                      at_ref, bt_ref, kt_ref, rt_ref, v_ref, eg_ref, g_ref, bv_ref, *, tm, seq, width):
    i = pl.program_id(0)
    x = x_ref[...]
    at_start = (i * tm) % seq == 0
    prev_row = jnp.where(at_start, 0.0, xp_ref[7:8, :])
    rolled = pltpu.roll(x, 1, axis=0)
    top = _iota((8, x.shape[1]), 0)
    prev = jnp.concatenate([jnp.where(top == 0, prev_row, rolled[:8]), rolled[8:]], axis=0)
    xm = x + (prev - x) * mu_ref[...]
    r = xm[:, :width]
    k = xm[:, width:2 * width]
    v = xm[:, 2 * width:3 * width]
    l1 = xm[:, 3 * width:3 * width + LANE]
    l2 = xm[:, 3 * width + LANE:]
    wpre = w0_ref[...] + _dot(_Split(jnp.tanh(l1)).left(1), w2_ref[...])
    a = _sigmoid(a0_ref[...] + _dot(_Split(l1).left(1), a2_ref[...]))
    g = _dot(_Split(_sigmoid(l2)).left(1), g2_ref[...])

    def hilo(t):
        ts = _Split(t)
        return jnp.concatenate([ts.hi, ts.lo], axis=1)

    lw = -jnp.exp(-_softplus(-wpre) - 0.5)
    hs = hs_ref[...]
    kk = k * kk_ref[...]
    kk = kk * lax.rsqrt(jnp.maximum(_dot(hilo(kk * kk), hs), 1e-24))
    km = k * (1.0 + (a - 1.0) * ka_ref[...])
    bonus = _dot(hilo(r * km * rk_ref[...]), hs)
    lws = _Split(lw)
    cum = _dot(tri_ref[...], jnp.concatenate([lws.hi, lws.lo], axis=0))
    eg = jnp.exp(cum)
    ieg = jnp.exp(-cum)
    at_ref[...] = -kk * jnp.exp(cum - lw)
    bt_ref[...] = kk * a * ieg
    kt_ref[...] = km * ieg
    rt_ref[...] = r * eg
    v_ref[...] = v
    eg_ref[...] = eg
    g_ref[...] = g
    bv_ref[...] = bonus * v


def rwkv_prep(x, mu, w0, a0, k_k, k_a, r_k, w2p, a2p, g2p, *, seq, width, tm=256):
    t, wx = x.shape
    ch = RW_CHUNK
    ti = jnp.arange(tm)
    tri = ((ti[:, None] // ch == ti[None, :] // ch) & (ti[None, :] <= ti[:, None])).astype(BF16)
    tri = jnp.concatenate([tri, tri], axis=1)
    ci = jnp.arange(width) // HEAD_DIM
    hs = (ci[:, None] == ci[None, :]).astype(BF16)
    hs = jnp.concatenate([hs, hs], axis=0)
    w2p, a2p, g2p = (_Split(w).right(0) for w in (w2p, a2p, g2p))
    hb = tm // 8
    row = lambda i: (i, 0)
    fix = lambda i: (0, 0)
    vec = pl.BlockSpec((1, width), fix)
    out = jax.ShapeDtypeStruct((t, width), F32)
    return pl.pallas_call(
        functools.partial(_rwkv_prep_kernel, tm=tm, seq=seq, width=width),
        grid=(t // tm,),
        in_specs=[pl.BlockSpec((tm, wx), row),
                  pl.BlockSpec((8, wx), lambda i: (jnp.maximum(i * hb - 1, 0), 0)),
                  pl.BlockSpec((1, wx), fix), vec, vec, vec, vec, vec,
                  pl.BlockSpec(w2p.shape, fix), pl.BlockSpec(a2p.shape, fix),
                  pl.BlockSpec(g2p.shape, fix), pl.BlockSpec(tri.shape, fix),
                  pl.BlockSpec(hs.shape, fix)],
        out_specs=[pl.BlockSpec((tm, width), row)] * 8,
        out_shape=[out] * 8,
        compiler_params=_cp("parallel"),
        name="rwkv_prep",
    )(x, x, mu, w0, a0, k_k, k_a, r_k, w2p, a2p, g2p, tri, hs)


def _rwkv_chunk_kernel(at_ref, bt_ref, kt_ref, rt_ref, v_ref, eg_ref,
                       m_ref, n_ref, rq_ref, yl_ref, *, nchunk, unroll):
    ch, n = RW_CHUNK, HEAD_DIM
    ti = _iota((ch, 2 * ch), 0)
    si = _iota((ch, 2 * ch), 1) % ch
    strict = si < ti
    incl = si <= ti
    same = (ti // RW_SUB) == (si // RW_SUB)
    eye = _iota((n, n), 0) == _iota((n, n), 1)
    each = lambda f, *ls: [f(*t) for t in zip(*ls)]

    def body(it, _):
        cs = [it * unroll + u for u in range(unroll)]
        sls = [pl.ds(pl.multiple_of(c * ch, ch), ch) for c in cs]
        ids = [(h, u) for h in range(2) for u in range(unroll)]
        ld = lambda ref: [ref[sls[u], h * n:(h + 1) * n] for h, u in ids]
        sp = lambda xs: [_Split(t) for t in xs]
        dup = lambda xs: [_SplitDup(t) for t in xs]
        twice = lambda t: jnp.concatenate([t, t], axis=0)
        a_, b_, k_, r_, v_ = ld(at_ref), ld(bt_ref), ld(kt_ref), ld(rt_ref), ld(v_ref)
        g_end = [eg_ref[pl.ds(cs[u] * ch + ch - 1, 1), h * n:(h + 1) * n] for h, u in ids]
        ar = sp(each(lambda a, r: jnp.concatenate([a, r], axis=0), a_, r_))
        pb = each(lambda x, y: _dot_nt(x.left(1), twice(y.right(1))), ar, sp(b_))
        pk = each(lambda x, y: _dot_nt(x.left(1), twice(y.right(1))), ar, sp(k_))
        a_ab = [jnp.where(strict, t[:ch], 0.0) for t in pb]
        a_ak = dup([jnp.where(strict, t[:ch], 0.0) for t in pk])
        a_rb = dup([jnp.where(incl, t[ch:], 0.0) for t in pb])
        a_rk = dup([jnp.where(incl, t[ch:], 0.0) for t in pk])
        d1 = [jnp.where(same, t, 0.0) for t in a_ab]
        lb = each(lambda t, d: t - d, a_ab, d1)
        d1 = dup(d1)
        d2 = dup([_dot(d.left, d.right) for d in d1])
        akv = each(_mm3_dup, a_ak, v_)
        x = each(lambda l, a, t: jnp.concatenate([l, a, t], axis=1), lb, a_, akv)
        d4 = dup([_dot(d.left, d.right) for d in d2])
        x = each(lambda d, t: t + _mm3_dup(d, t), d1, x)
        d8 = dup([_dot(d.left, d.right) for d in d4])
        x = each(lambda d, t: t + _mm3_dup(d, t), d2, x)
        x = each(lambda d, t: t + _mm3_dup(d, t), d4, x)
        x = each(lambda d, t: t + _mm3_dup(d, t), d8, x)
        e1 = dup([t[:, :2 * ch] for t in x])
        wu = [t[:, 2 * ch:] for t in x]
        e2 = dup([_dot(e.left, e.right) for e in e1])
        wu = each(lambda e, t: t + _mm3_dup(e, t), e1, wu)
        wu = each(lambda e, t: t + _mm3_dup(e, t), e2, wu)
        ry = each(_mm3_dup, a_rb, wu)
        rkv = each(_mm3_dup, a_rk, v_)
        wus, vs = sp(wu), sp(v_)
        mn = each(lambda b, g, t: _mm3_tn(_Split(b * g), t), b_, g_end, wus)
        kv = each(lambda k, g, t: _mm3_tn(_Split(k * g), t), k_, g_end, vs)
        rq = each(lambda r, t: r + t[:, :n], r_, ry)
        yl = each(lambda t, w: t[:, n:] + w, ry, rkv)
        for u in range(unroll):
            rq_ref[sls[u], :] = jnp.concatenate([rq[u], rq[unroll + u]], axis=1)
            yl_ref[sls[u], :] = jnp.concatenate([yl[u], yl[unroll + u]], axis=1)
        for i, (h, u) in enumerate(ids):
            m_ref[h, cs[u]] = mn[i][:, :n] + jnp.where(eye, g_end[i], 0.0)
            n_ref[h, cs[u]] = mn[i][:, n:] + kv[i]
        return 0

    lax.fori_loop(0, nchunk // unroll, body, 0)


def rwkv_chunk(at, bt, kt, rt, v, eg, *, batch, nchunk=8, unroll=8):
    t, w = at.shape
    s = t // batch
    ch, n = RW_CHUNK, HEAD_DIM
    pairs = w // LANE
    rows = nchunk * ch
    nr = s // rows
    spec = pl.BlockSpec((rows, LANE), lambda b, p, i: (b * nr + i, p))
    mspec = pl.BlockSpec((2, nchunk, n, n), lambda b, p, i: (b * pairs + p, i, 0, 0))
    return pl.pallas_call(
        functools.partial(_rwkv_chunk_kernel, nchunk=nchunk, unroll=unroll),
        grid=(batch, pairs, nr),
        in_specs=[spec] * 6,
        out_specs=[mspec, mspec, spec, spec],
        out_shape=[jax.ShapeDtypeStruct((batch * w // n, s // ch, n, n), F32)] * 2
        + [jax.ShapeDtypeStruct((t, w), F32)] * 2,
        compiler_params=_cp("parallel", "parallel", "parallel"),
        name="rwkv_chunk",
    )(at, bt, kt, rt, v, eg)


def _rwkv_scan_kernel(m_ref, n_ref, s_ref, st_ref, *, heads, nchunk):
    @pl.when(pl.program_id(1) == 0)
    def _():
        st_ref[...] = jnp.zeros_like(st_ref)

    def body(c, _):
        sts = [st_ref[hh] for hh in range(heads)]
        for hh in range(heads):
            s_ref[hh, c] = sts[hh]
        ms = [_Split(m_ref[hh, c]) for hh in range(heads)]
        ss = [_Split(st) for st in sts]
        new = [_mm3(a, b) for a, b in zip(ms, ss)]
        for hh in range(heads):
            st_ref[hh] = new[hh] + n_ref[hh, c]
        return 0

    lax.fori_loop(0, nchunk, body, 0)


def rwkv_scan(m, nn, *, heads=16, nchunk=8):
    bh, nc, n, _ = m.shape
    heads = min(heads, bh)
    nchunk = min(nchunk, nc)
    spec = pl.BlockSpec((heads, nchunk, n, n), lambda b, i: (b, i, 0, 0))
    return pl.pallas_call(
        functools.partial(_rwkv_scan_kernel, heads=heads, nchunk=nchunk),
        grid=(bh // heads, nc // nchunk),
        in_specs=[spec, spec],
        out_specs=spec,
        out_shape=jax.ShapeDtypeStruct((bh, nc, n, n), F32),
        scratch_shapes=[pltpu.VMEM((heads, n, n), F32)],
        compiler_params=_cp("parallel", "arbitrary"),
        name="rwkv_scan",
    )(m, nn)


def _rwkv_out_kernel(rq_ref, yl_ref, s_ref, lw_ref, lb_ref, avg_ref, o_ref, *, nchunk):
    ch, n = RW_CHUNK, HEAD_DIM
    ys = []
    for h in range(2):
        cols = slice(h * n, (h + 1) * n)
        ys.append(jnp.concatenate(
            [_mm3(_Split(rq_ref[c * ch:(c + 1) * ch, cols]), _Split(s_ref[h, c])) for c in range(nchunk)],
            axis=0))
    y = jnp.concatenate(ys, axis=1) + yl_ref[...]
    avg2 = jnp.concatenate([avg_ref[...]] * 2, axis=0)

    def mean(t):
        ts = _Split(t)
        return _dot(jnp.concatenate([ts.hi, ts.lo], axis=1), avg2)

    d = y - mean(y)
    var = mean(d * d)
    o_ref[...] = d * lax.rsqrt(var + GN_EPS) * lw_ref[...] + lb_ref[...]


def rwkv_out(rq, yl, s0, ln_w, ln_b, *, batch, nchunk=8):
    t, w = rq.shape
    s = t // batch
    n = HEAD_DIM
    pairs = w // LANE
    rows = nchunk * RW_CHUNK
    nr = s // rows
    li = jnp.arange(LANE) // n
    avg = ((li[:, None] == li[None, :]).astype(F32) / n).astype(BF16)
    spec = pl.BlockSpec((rows, LANE), lambda b, p, i: (b * nr + i, p))
    hspec = pl.BlockSpec((1, LANE), lambda b, p, i: (0, p))
    return pl.pallas_call(
        functools.partial(_rwkv_out_kernel, nchunk=nchunk),
        grid=(batch, pairs, nr),
        in_specs=[spec, spec, pl.BlockSpec((2, nchunk, n, n), lambda b, p, i: (b * pairs + p, i, 0, 0)),
                  hspec, hspec, pl.BlockSpec((LANE, LANE), lambda b, p, i: (0, 0))],
        out_specs=spec,
        out_shape=jax.ShapeDtypeStruct((t, w), F32),
        compiler_params=_cp("parallel", "parallel", "parallel"),
        name="rwkv_out",
    )(rq, yl, s0, ln_w.reshape(1, w), ln_b.reshape(1, w), avg)


def _hyb_out_kernel(oa_ref, y_ref, bv_ref, g_ref, wa_ref, wb_ref, r_ref, o_ref):
    ob = (y_ref[...] + bv_ref[...]) * g_ref[...]
    o_ref[...] = (r_ref[...] + _dot(oa_ref[...].astype(BF16), wa_ref[...])
                  + _dot(ob.astype(BF16), wb_ref[...]))


def hyb_out(res, oa, y, bv, g, wa, wb, *, tm=1024):
    t, d = res.shape
    row = lambda i: (i, 0)
    fix = lambda i: (0, 0)
    act = lambda a: pl.BlockSpec((tm, a.shape[1]), row)
    return pl.pallas_call(
        _hyb_out_kernel,
        grid=(t // tm,),
        in_specs=[act(oa), act(y), act(bv), act(g), pl.BlockSpec(wa.shape, fix),
                  pl.BlockSpec(wb.shape, fix), pl.BlockSpec((tm, d), row)],
        out_specs=pl.BlockSpec((tm, d), row),
        out_shape=jax.ShapeDtypeStruct((t, d), F32),
        compiler_params=_cp("parallel"),
        name="hyb_out",
    )(oa, y, bv, g, wa, wb, res)


def _pad_cols(w, n):
    return jnp.pad(w, ((0, 0), (0, n - w.shape[1])))


def sb_rwkv_layer(h, b, s, attn_norm, w_in, w_out, mu, w0, w2, a0, a2, g2, k_k, k_a, r_k, ln_w, ln_b):
    sbw = SB_HEADS * HEAD_DIM
    rww = RW_HEADS * HEAD_DIM
    rw_in = w_in.shape[1] - 3 * sbw
    rw_pad = -(-rw_in // LANE) * LANE
    dl, al = w2.shape[0], a2.shape[0]
    gl = g2.shape[0]
    assert dl + al == LANE and 3 * rww % LANE == 0

    qkv = rms_matmul(h, attn_norm, w_in[:, :3 * sbw].astype(BF16), out_dtype=BF16)
    rw = rms_matmul(h, attn_norm, _pad_cols(w_in[:, 3 * sbw:], rw_pad).astype(BF16))
    oa = sb_attention(qkv, batch=b, heads=SB_HEADS)

    w2p = jnp.zeros((LANE, rww), F32).at[:dl].set(w2)
    a2p = jnp.zeros((LANE, rww), F32).at[dl:].set(a2)
    g2p = jnp.zeros((rw_pad - 3 * rww - LANE, rww), F32).at[:gl].set(g2)
    vec = lambda t: t.reshape(1, rww).astype(F32)
    mu_p = _pad_cols(mu.reshape(1, rw_in), rw_pad)
    at, bt, kt, rt, vv, eg, g, bv = rwkv_prep(rw, mu_p, vec(w0), vec(a0), vec(k_k), vec(k_a), vec(r_k),
                                              w2p, a2p, g2p, seq=s, width=rww)
    m, nn, rq, yl = rwkv_chunk(at, bt, kt, rt, vv, eg, batch=b)
    s0 = rwkv_scan(m, nn)
    y = rwkv_out(rq, yl, s0, ln_w, ln_b, batch=b)
    return hyb_out(h, oa, y, bv, g, w_out[:sbw].astype(BF16), w_out[sbw:].astype(BF16))


def mla_layer(h, b, s, positions, attn_norm, w_down, q_norm, kv_norm, w_uq, w_ukv, w_o):
    nh, dn, dr, dv = MLA_HEADS, MLA_NOPE, MLA_ROPE, MLA_V
    qr, kvr = MLA_Q_RANK, MLA_KV_RANK
    assert dn == dv and qr % LANE == 0 and (qr + LANE) % kvr == 0
    wd = jnp.concatenate([w_down[:, :qr], jnp.tile(w_down[:, qr + kvr:], (1, LANE // dr)),
                          w_down[:, qr:qr + kvr]], axis=1)
    c = rms_matmul(h, attn_norm, wd.astype(BF16))
    wq = w_uq.reshape(qr, nh, dn + dr)
    wq = jnp.concatenate([wq[:, :, :dn].reshape(qr, nh * dn), wq[:, :, dn:].reshape(qr, nh * dr)], axis=1)
    q = rms_matmul(c, q_norm, wq.astype(BF16), x_col=0)
    wkv = w_ukv.reshape(kvr, nh, dn + dv)
    wkv = jnp.concatenate([wkv[:, :, :dn].reshape(kvr, nh * dn), wkv[:, :, dn:].reshape(kvr, nh * dv)], axis=1)
    kv = rms_matmul(c, kv_norm, wkv.astype(BF16), x_col=(qr + LANE) // kvr, out_dtype=BF16)
    qrot, krot = rope(q, c, positions, q_col=dn // dr, qw=nh * dr, k_col=qr // LANE)
    o = softmax_attention(q, qrot, kv, krot, 1.0 / math.sqrt(dn + dr), batch=b, heads=nh)
    return matmul_residual(h, o, w_o.astype(BF16))


def kernel(x, p, positions, attn_norm, ffn_norm, ffn_w_in, ffn_conv_w, ffn_conv_b, ffn_w_out, ple_w_proj, ple_norm, ple_gate_norm, ple_w_gate, hyb_w_in, hyb_w_out, rw_mu, rw_w0, rw_w2, rw_a0, rw_a2, rw_g2, rw_k_k, rw_k_a, rw_r_k, rw_ln_w, rw_ln_b, mla_w_down, mla_q_norm, mla_kv_norm, mla_w_uq, mla_w_ukv, mla_w_o, final_norm):
    b, s, d = x.shape
    depth = p.shape[0]
    h = x.reshape(b * s, d)
    pos = positions.reshape(b * s)
    for i in range(depth):
        j = i // 2
        if i % 2 == 0:
            h = sb_rwkv_layer(h, b, s, attn_norm[i], hyb_w_in[j], hyb_w_out[j], rw_mu[j], rw_w0[j],
                              rw_w2[j], rw_a0[j], rw_a2[j], rw_g2[j], rw_k_k[j], rw_k_a[j], rw_r_k[j],
                              rw_ln_w[j], rw_ln_b[j])
        else:
            h = mla_layer(h, b, s, pos, attn_norm[i], mla_w_down[j], mla_q_norm[j], mla_kv_norm[j],
                          mla_w_uq[j], mla_w_ukv[j], mla_w_o[j])
        ug = rms_matmul(h, ffn_norm[i], ffn_w_in[i].astype(BF16), out_dtype=BF16)
        h = convglu_ple(h, ug, ffn_conv_w[i], ffn_conv_b[i], ffn_w_out[i], p[i].reshape(b * s, -1),
                        ple_w_proj[i], ple_norm[i], ple_gate_norm[i], ple_w_gate[i], final_norm,
                        seq=s, final=(i == depth - 1))
    return h.reshape(b, s, d)
```

```python
import functools
import math

import jax
import jax.numpy as jnp
from jax import lax
from jax.experimental import pallas as pl
from jax.experimental.pallas import tpu as pltpu

F32 = jnp.float32
BF16 = jnp.bfloat16

NORM_EPS = 1e-6
GN_EPS = 64e-5
ROPE_THETA = 10000.0
LANE = 128
VMEM_LIMIT = 48 * 1024 * 1024
HALO = 16
MXU_WIDTH = 256
TILE_VMEM_BUDGET = 36 * 1024 * 1024
WEIGHT_TILE_BYTES = 6 * 1024 * 1024

SB_HEADS = 8
RW_HEADS = 8
HEAD_DIM = 64
MLA_HEADS = 16
MLA_NOPE = 64
MLA_ROPE = 32
MLA_V = 64
MLA_Q_RANK = 384
MLA_KV_RANK = 256
SB_CUT = -151.0
RW_CHUNK = 64
RW_SUB = 16


def _cp(*sem):
    return pltpu.CompilerParams(dimension_semantics=sem, vmem_limit_bytes=VMEM_LIMIT)


def _iota(shape, dim):
    return lax.broadcasted_iota(jnp.int32, shape, dim)


def _softplus(z):
    return jnp.maximum(z, 0.0) + jnp.log1p(jnp.exp(-jnp.abs(z)))


def _sigmoid(z):
    return 1.0 / (1.0 + jnp.exp(-z))


def _dot(a, b):
    return jnp.dot(a, b, preferred_element_type=F32)


def _dot_nt(a, b):
    return lax.dot_general(a, b, (((1,), (1,)), ((), ())), preferred_element_type=F32)


def _dot_tn(a, b):
    return lax.dot_general(a, b, (((0,), (0,)), ((), ())), preferred_element_type=F32)


class _Split:
    def __init__(self, x):
        self.x = x
        self.hi = x.astype(BF16)
        self.lo = (x - self.hi.astype(F32)).astype(BF16)
        self._packed = {}

    def left(self, axis):
        key = ("l", axis)
        if key not in self._packed:
            self._packed[key] = jnp.concatenate([self.hi, self.lo, self.hi], axis=axis)
        return self._packed[key]

    def right(self, axis):
        key = ("r", axis)
        if key not in self._packed:
            self._packed[key] = jnp.concatenate([self.hi, self.hi, self.lo], axis=axis)
        return self._packed[key]


def _mm3(a, b):
    return _dot(a.left(1), b.right(0))


def _mm3_tn(a, b):
    return _dot_tn(a.left(0), b.right(0))


class _SplitDup:
    def __init__(self, xd):
        hi = xd.astype(BF16)
        lo = (xd - hi.astype(F32)).astype(BF16)
        self.left = jnp.concatenate([jnp.where(_iota(xd.shape, 1) < HEAD_DIM, hi, lo), hi], axis=1)
        self.right = _rows_packed(hi, lo)


def _rows_packed(hi, lo):
    return jnp.concatenate([hi, hi, lo, jnp.zeros_like(hi)], axis=0)


def _mm3_dup(a, b):
    bs = _Split(b)
    return _dot(a.left, _rows_packed(bs.hi, bs.lo))


def _rms(x, g, eps=NORM_EPS):
    return x * lax.rsqrt(jnp.mean(x * x, axis=-1, keepdims=True) + eps) * g


def _rms_matmul_kernel(x_ref, g_ref, w_ref, o_ref, xn_ref):
    @pl.when(pl.program_id(1) == 0)
    def _():
        xn_ref[...] = _rms(x_ref[...].astype(F32), g_ref[...]).astype(BF16)

    o_ref[...] = _dot(xn_ref[...], w_ref[...]).astype(o_ref.dtype)


def _pick_tiles(t, k, n, tile_bytes):
    tn = max(c for c in range(LANE, n + 1, LANE)
             if n % c == 0 and (c == n or c % MXU_WIDTH == 0) and k * c * 2 <= WEIGHT_TILE_BYTES)
    for tm in (2048, 1024, 512, 256):
        if t % tm == 0 and tile_bytes(tm, tn) + 2 * k * tn * 2 <= TILE_VMEM_BUDGET:
            return tm, tn
    raise ValueError("no row tile fits")


def rms_matmul(x, g, w, *, x_col=0, out_dtype=F32):
    t = x.shape[0]
    k, n = w.shape
    ob = jnp.dtype(out_dtype).itemsize
    tm, tn = _pick_tiles(t, k, n, lambda tm, tn: 2 * tm * k * x.dtype.itemsize + tm * k * 2 + 2 * tm * tn * ob)
    return pl.pallas_call(
        _rms_matmul_kernel,
        grid=(t // tm, n // tn),
        in_specs=[pl.BlockSpec((tm, k), lambda i, j: (i, x_col)),
                  pl.BlockSpec((1, k), lambda i, j: (0, 0)),
                  pl.BlockSpec((k, tn), lambda i, j: (0, j))],
        out_specs=pl.BlockSpec((tm, tn), lambda i, j: (i, j)),
        out_shape=jax.ShapeDtypeStruct((t, n), out_dtype),
        scratch_shapes=[pltpu.VMEM((tm, k), BF16)],
        compiler_params=_cp("parallel", "arbitrary"),
        name="rms_matmul",
    )(x, g.reshape(1, k).astype(F32), w)


def _matmul_res_kernel(a_ref, w_ref, r_ref, o_ref):
    o_ref[...] = r_ref[...] + _dot(a_ref[...].astype(BF16), w_ref[...])


def matmul_residual(res, a, w):
    t, k = a.shape
    n = w.shape[1]
    tm, tn = _pick_tiles(t, k, n, lambda tm, tn: 2 * tm * k * a.dtype.itemsize + 4 * tm * tn * 4)
    return pl.pallas_call(
        _matmul_res_kernel,
        grid=(t // tm, n // tn),
        in_specs=[pl.BlockSpec((tm, k), lambda i, j: (i, 0)),
                  pl.BlockSpec((k, tn), lambda i, j: (0, j)),
                  pl.BlockSpec((tm, tn), lambda i, j: (i, j))],
        out_specs=pl.BlockSpec((tm, tn), lambda i, j: (i, j)),
        out_shape=jax.ShapeDtypeStruct((t, n), F32),
        compiler_params=_cp("parallel", "arbitrary"),
        name="matmul_residual",
    )(a, w, res)


def _convglu_ple_kernel(u_ref, g_ref, gp_ref, cw_ref, cb_ref, w_ref, r_ref,
                        p_ref, wp_ref, pn_ref, gn_ref, wg_ref, fn_ref, o_ref, *, tm, seq, final):
    i = pl.program_id(0)
    g = g_ref[...].astype(F32)
    at_start = (i * tm) % seq == 0
    halo = jnp.where(at_start, 0.0, gp_ref[...].astype(F32))
    h1, h2 = halo[HALO - 1:HALO, :], halo[HALO - 2:HALO - 1, :]
    r1, r2 = pltpu.roll(g, 1, axis=0), pltpu.roll(g, 2, axis=0)
    top = _iota((8, g.shape[1]), 0)
    p1 = jnp.concatenate([jnp.where(top == 0, h1, r1[:8]), r1[8:]], axis=0)
    p2 = jnp.concatenate([jnp.where(top == 0, h2, jnp.where(top == 1, h1, r2[:8])), r2[8:]], axis=0)
    c = cw_ref[0:1, :] * p2 + cw_ref[1:2, :] * p1 + cw_ref[2:3, :] * g + cb_ref[...]
    act = c * (1.0 + lax.erf(c)) * u_ref[...].astype(F32)
    h = r_ref[...] + _dot(act.astype(BF16), w_ref[...])
    e = _rms(_dot(p_ref[...].astype(BF16), wp_ref[...]), pn_ref[...])
    gate = _sigmoid(_dot(_rms(h, gn_ref[...]).astype(BF16), wg_ref[...]))
    out = h + gate * e
    if final:
        out = _rms(out, fn_ref[...])
    o_ref[...] = out


def convglu_ple(res, ug, conv_w, conv_b, w_out, p, w_proj, p_norm, g_norm, w_gate, f_norm,
                *, seq, final, tm=256):
    t, d = res.shape
    f = w_out.shape[0]
    pd = p.shape[1]
    hb = tm // HALO
    rs2 = 1.0 / math.sqrt(2.0)
    row = lambda i: (i, 0)
    fix = lambda i: (0, 0)
    vec = pl.BlockSpec((1, d), fix)
    return pl.pallas_call(
        functools.partial(_convglu_ple_kernel, tm=tm, seq=seq, final=final),
        grid=(t // tm,),
        in_specs=[pl.BlockSpec((tm, f), row),
                  pl.BlockSpec((tm, f), lambda i: (i, 1)),
                  pl.BlockSpec((HALO, f), lambda i: (jnp.maximum(i * hb - 1, 0), 1)),
                  pl.BlockSpec((3, f), fix), pl.BlockSpec((1, f), fix), pl.BlockSpec((f, d), fix),
                  pl.BlockSpec((tm, d), row), pl.BlockSpec((tm, pd), row), pl.BlockSpec((pd, d), fix),
                  vec, vec, pl.BlockSpec((d, d), fix), vec],
        out_specs=pl.BlockSpec((tm, d), row),
        out_shape=jax.ShapeDtypeStruct((t, d), F32),
        compiler_params=_cp("parallel"),
        name="convglu_ple",
    )(ug, ug, ug, conv_w * rs2, (conv_b * rs2).reshape(1, f), (w_out * rs2).astype(BF16), res,
      p, w_proj.astype(BF16), p_norm.reshape(1, d), g_norm.reshape(1, d), w_gate.astype(BF16),
      f_norm.reshape(1, d))


def _sb_attn_kernel(q_ref, k_ref, v_ref, u_ref, o_ref, *, tq, tk, nh, scale):
    qi = pl.program_id(2)
    nblk = tq // tk
    width = nh * HEAD_DIM
    qf = q_ref[...] * scale
    qlane = _iota((tq, width), 1) // HEAD_DIM
    qs = jnp.concatenate([jnp.where(qlane == h, qf, jnp.zeros_like(qf)) for h in range(nh)], axis=0)
    vlane = _iota((tk, width), 1) // HEAD_DIM
    rows = lambda x, h: x[h * tq:(h + 1) * tq]

    def sweep(k_lo, run, acc, masked):
        parts = []
        for d in reversed(range(nblk)):
            k0 = k_lo + d * tk
            t = _dot_nt(qs, k_ref[pl.ds(k0, tk), :]) * math.log2(math.e)
            sp = jnp.maximum(t, 0.0) + jnp.log2(1.0 + jnp.exp2(-jnp.abs(t)))
            lom = -sp
            valid = None
            if masked:
                valid = _iota((nh * tq, tk), 1) + d * tk < _iota((nh * tq, tk), 0) % tq
                lom = jnp.where(valid, lom, 0.0)
            hi = lom.astype(BF16)
            lo = (lom - hi.astype(F32)).astype(BF16)
            sums = _dot(jnp.concatenate([hi, lo], axis=1), u_ref[...])
            parts.append((k0, t - sp, sums, valid))
        for k0, logsig, sums, valid in parts:
            w = jnp.exp2(logsig + run + sums[:, :tk])
            if masked:
                w = jnp.where(valid, w, 0.0)
            w = w.astype(BF16)
            vb = v_ref[pl.ds(k0, tk), :]
            wcat = jnp.concatenate([rows(w, h) for h in range(nh)], axis=1)
            vcat = jnp.concatenate([jnp.where(vlane == h, vb, jnp.zeros_like(vb)) for h in range(nh)], axis=0)
            acc = acc + _dot(wcat, vcat)
            run = run + sums[:, tk:]
        return run, acc

    run = jnp.zeros((nh * tq, tk), F32)
    acc = jnp.zeros((tq, width), F32)
    run, acc = sweep(pl.multiple_of(qi * tq, tq), run, acc, True)

    def live(run):
        return jnp.max(run) > SB_CUT

    def cond(c):
        return c[0]

    def body(c):
        _, j, run, acc = c
        run, acc = sweep(pl.multiple_of(j * tq, tq), run, acc, False)
        return (j > 0) & live(run), j - 1, run, acc

    _, _, _, acc = lax.while_loop(cond, body, ((qi > 0) & live(run), qi - 1, run, acc))
    o_ref[...] = acc


def sb_attention(qkv, *, batch, heads, tq=256, nh=4):
    t, w3 = qkv.shape
    s = t // batch
    nq = s // tq
    width = nh * HEAD_DIM
    groups = heads // nh
    tk = LANE
    jj = jnp.arange(2 * tk)[:, None] % tk
    ss = jnp.arange(2 * tk)[None, :]
    u = jnp.where(ss < tk, jj > ss, True).astype(BF16)
    return pl.pallas_call(
        functools.partial(_sb_attn_kernel, tq=tq, tk=tk, nh=nh, scale=1.0 / math.sqrt(HEAD_DIM)),
        grid=(batch, groups, nq),
        in_specs=[pl.BlockSpec((tq, width), lambda b, p, i: (b * nq + i, p)),
                  pl.BlockSpec((s, width), lambda b, p, i: (b, groups + p)),
                  pl.BlockSpec((s, width), lambda b, p, i: (b, 2 * groups + p)),
                  pl.BlockSpec((2 * tk, 2 * tk), lambda b, p, i: (0, 0))],
        out_specs=pl.BlockSpec((tq, width), lambda b, p, i: (b * nq + i, p)),
        out_shape=jax.ShapeDtypeStruct((t, w3 // 3), F32),
        compiler_params=_cp("parallel", "parallel", "arbitrary"),
        name="sb_attention",
    )(qkv, qkv, qkv, u)


def _softmax_attn_kernel(qn_ref, qr_ref, kn_ref, kr_ref, v_ref, o_ref, *bufs, tq, tk, npair, scale):
    qi = pl.program_id(2)
    g = pl.program_id(1)
    nh = 2 * npair
    lane = _iota((tq, LANE), 1)
    qr = qr_ref[...] * (scale * math.log2(math.e))
    slots = LANE // MLA_ROPE
    qs = []
    for h in range(nh):
        cols = slice(h // 2 * LANE, (h // 2 + 1) * LANE)
        qn = qn_ref[:, cols] * (scale * math.log2(math.e))
        nope = jnp.where(lane // MLA_NOPE == h % 2, qn, 0.0)
        rope_ = jnp.where(lane // MLA_ROPE == (g * nh + h) % slots, qr, 0.0)
        qs.append(jnp.concatenate([nope, rope_], axis=1).astype(BF16))

    def scores(j, dsts):
        k0 = pl.multiple_of(j * tk, tk)
        mx = []
        for h in range(nh):
            cols = slice(h // 2 * LANE, (h // 2 + 1) * LANE)
            kb = jnp.concatenate([kn_ref[pl.ds(k0, tk), cols], kr_ref[pl.ds(k0, tk), :]], axis=1)
            s = _dot_nt(qs[h], kb)
            dsts[h][...] = s
            mx.append(jnp.max(s, axis=1, keepdims=True))
        return tuple(mx)

    def consume(srcs, j, carry, mx):
        out = []
        for h in range(nh):
            cols = slice(h // 2 * LANE, (h // 2 + 1) * LANE)
            vb = v_ref[pl.ds(pl.multiple_of(j * tk, tk), tk), cols]
            mine = (_iota(vb.shape, 1) // MLA_V) == h % 2
            vbh = jnp.where(mine, vb, jnp.ones_like(vb))
            m, acc = carry[h]
            s = srcs[h][...]
            if mx is None:
                s = jnp.where(_iota((tq, tk), 1) <= _iota((tq, tk), 0), s, -1e30)
                m_new = jnp.maximum(m, jnp.max(s, axis=1, keepdims=True))
            else:
                m_new = jnp.maximum(m, mx[h])
            p = jnp.exp2(s - m_new)
            acc = jnp.exp2(m - m_new) * acc + _dot(p.astype(BF16), vbh)
            out.append((m_new, acc))
        return tuple(out)

    sa, sb = bufs[:nh], bufs[nh:]

    def pair(t, carry):
        state, mxa = carry
        mxb = scores(2 * t + 1, sb)
        state = consume(sa, 2 * t, state, mxa)
        mxa = scores(2 * t + 2, sa)
        return consume(sb, 2 * t + 1, state, mxb), mxa

    def tail_odd(carry):
        state, mxa = carry
        scores(qi, sb)
        state = consume(sa, qi - 1, state, mxa)
        return consume(sb, qi, state, None)

    def tail_even(carry):
        return consume(sa, qi, carry[0], None)

    init = ((jnp.full((tq, 1), -1e30, F32), jnp.zeros((tq, LANE), F32)),) * nh
    carry = lax.fori_loop(0, qi // 2, pair, (init, scores(0, sa)))
    state = lax.cond(qi % 2 == 1, tail_odd, tail_even, carry)
    for pr in range(npair):
        acc0, acc1 = state[2 * pr][1], state[2 * pr + 1][1]
        o_ref[:, pr * LANE:(pr + 1) * LANE] = jnp.where(
            lane < MLA_V, acc0 / pltpu.roll(acc0, MLA_V, axis=1), acc1 / pltpu.roll(acc1, MLA_V, axis=1))


def softmax_attention(q, qrot, kv, krot, scale, *, batch, heads, tq=512, npair=2):
    tk = tq
    t = q.shape[0]
    s = t // batch
    nq = s // tq
    width = npair * LANE
    groups = heads * MLA_NOPE // width
    rope_per_group = LANE // (2 * npair * MLA_ROPE)
    buf = pltpu.VMEM((tq, tk), F32)
    return pl.pallas_call(
        functools.partial(_softmax_attn_kernel, tq=tq, tk=tk, npair=npair, scale=scale),
        grid=(batch, groups, nq),
        in_specs=[pl.BlockSpec((tq, width), lambda b, p, i: (b * nq + i, p)),
                  pl.BlockSpec((tq, LANE), lambda b, p, i: (b * nq + i, p // rope_per_group)),
                  pl.BlockSpec((s, width), lambda b, p, i: (b, p)),
                  pl.BlockSpec((s, LANE), lambda b, p, i: (b, 0)),
                  pl.BlockSpec((s, width), lambda b, p, i: (b, groups + p))],
        out_specs=pl.BlockSpec((tq, width), lambda b, p, i: (b * nq + i, p)),
        out_shape=jax.ShapeDtypeStruct((t, heads * MLA_V), F32),
        scratch_shapes=[buf] * (4 * npair),
        compiler_params=_cp("parallel", "parallel", "arbitrary"),
        name="softmax_attention",
    )(q, qrot, kv, krot, kv)


def _rope_kernel(q_ref, k_ref, pos_ref, f_ref, oq_ref, ok_ref):
    half = MLA_ROPE // 2
    ang = pos_ref[...].astype(F32) * f_ref[...]
    cos, sin = jnp.cos(ang), jnp.sin(ang)
    low = _iota(ang.shape, 1) % MLA_ROPE < half

    def rot(x):
        partner = jnp.where(low, -pltpu.roll(x, LANE - half, axis=1), pltpu.roll(x, half, axis=1))
        return x * cos + partner * sin

    for c in range(q_ref.shape[1] // LANE):
        cols = slice(c * LANE, (c + 1) * LANE)
        oq_ref[:, cols] = rot(q_ref[:, cols])
    ok_ref[...] = rot(k_ref[...]).astype(ok_ref.dtype)


def rope(q, c, positions, *, q_col, qw, k_col, tm=512):
    t = q.shape[0]
    half = MLA_ROPE // 2
    inv_freq = ROPE_THETA ** (-jnp.arange(half, dtype=F32) / half)
    freq = jnp.tile(inv_freq, LANE // half).reshape(1, LANE)
    return pl.pallas_call(
        _rope_kernel,
        grid=(t // tm,),
        in_specs=[pl.BlockSpec((tm, qw), lambda i: (i, q_col)), pl.BlockSpec((tm, LANE), lambda i: (i, k_col)),
                  pl.BlockSpec((tm, 1), lambda i: (i, 0)), pl.BlockSpec((1, LANE), lambda i: (0, 0))],
        out_specs=[pl.BlockSpec((tm, qw), lambda i: (i, 0)), pl.BlockSpec((tm, LANE), lambda i: (i, 0))],
        out_shape=[jax.ShapeDtypeStruct((t, qw), F32), jax.ShapeDtypeStruct((t, LANE), BF16)],
        compiler_params=_cp("parallel"),
        name="rope",
    )(q, c, positions.reshape(t, 1), freq)


def _rwkv_prep_kernel(x_ref, xp_ref, mu_ref, w0_ref, a0_ref, kk_ref, ka_ref, rk_ref,
                      w2_ref, a2_ref, g2_ref, tri_ref, hs_ref,
                      at_ref, bt_ref, kt_ref, rt_ref, v_ref, eg_ref, g_ref, bv_ref, *, tm, seq, width):
    i = pl.program_id(0)
    x = x_ref[...]
    at_start = (i * tm) % seq == 0
    prev_row = jnp.where(at_start, 0.0, xp_ref[7:8, :])
    rolled = pltpu.roll(x, 1, axis=0)
    top = _iota((8, x.shape[1]), 0)
    prev = jnp.concatenate([jnp.where(top == 0, prev_row, rolled[:8]), rolled[8:]], axis=0)
    xm = x + (prev - x) * mu_ref[...]
    r = xm[:, :width]
    k = xm[:, width:2 * width]
    v = xm[:, 2 * width:3 * width]
    l1 = xm[:, 3 * width:3 * width + LANE]
    l2 = xm[:, 3 * width + LANE:]
    wpre = w0_ref[...] + _dot(_Split(jnp.tanh(l1)).left(1), w2_ref[...])
    a = _sigmoid(a0_ref[...] + _dot(_Split(l1).left(1), a2_ref[...]))
    g = _dot(_Split(_sigmoid(l2)).left(1), g2_ref[...])

    def hilo(t):
        ts = _Split(t)
        return jnp.concatenate([ts.hi, ts.lo], axis=1)

    lw = -jnp.exp(-_softplus(-wpre) - 0.5)
    hs = hs_ref[...]
    kk = k * kk_ref[...]
    kk = kk * lax.rsqrt(jnp.maximum(_dot(hilo(kk * kk), hs), 1e-24))
    km = k * (1.0 + (a - 1.0) * ka_ref[...])
    bonus = _dot(hilo(r * km * rk_ref[...]), hs)
    lws = _Split(lw)
    cum = _dot(tri_ref[...], jnp.concatenate([lws.hi, lws.lo], axis=0))
    eg = jnp.exp(cum)
    ieg = jnp.exp(-cum)
    at_ref[...] = -kk * jnp.exp(cum - lw)
    bt_ref[...] = kk * a * ieg
    kt_ref[...] = km * ieg
    rt_ref[...] = r * eg
    v_ref[...] = v
    eg_ref[...] = eg
    g_ref[...] = g
    bv_ref[...] = bonus * v


def rwkv_prep(x, mu, w0, a0, k_k, k_a, r_k, w2p, a2p, g2p, *, seq, width, tm=256):
    t, wx = x.shape
    ch = RW_CHUNK
    ti = jnp.arange(tm)
    tri = ((ti[:, None] // ch == ti[None, :] // ch) & (ti[None, :] <= ti[:, None])).astype(BF16)
    tri = jnp.concatenate([tri, tri], axis=1)
    ci = jnp.arange(width) // HEAD_DIM
    hs = (ci[:, None] == ci[None, :]).astype(BF16)
    hs = jnp.concatenate([hs, hs], axis=0)
    w2p, a2p, g2p = (_Split(w).right(0) for w in (w2p, a2p, g2p))
    hb = tm // 8
    row = lambda i: (i, 0)
    fix = lambda i: (0, 0)
    vec = pl.BlockSpec((1, width), fix)
    out = jax.ShapeDtypeStruct((t, width), F32)
    return pl.pallas_call(
        functools.partial(_rwkv_prep_kernel, tm=tm, seq=seq, width=width),
        grid=(t // tm,),
        in_specs=[pl.BlockSpec((tm, wx), row),
                  pl.BlockSpec((8, wx), lambda i: (jnp.maximum(i * hb - 1, 0), 0)),
                  pl.BlockSpec((1, wx), fix), vec, vec, vec, vec, vec,
                  pl.BlockSpec(w2p.shape, fix), pl.BlockSpec(a2p.shape, fix),
                  pl.BlockSpec(g2p.shape, fix), pl.BlockSpec(tri.shape, fix),
                  pl.BlockSpec(hs.shape, fix)],
        out_specs=[pl.BlockSpec((tm, width), row)] * 8,
        out_shape=[out] * 8,
        compiler_params=_cp("parallel"),
        name="rwkv_prep",
    )(x, x, mu, w0, a0, k_k, k_a, r_k, w2p, a2p, g2p, tri, hs)


def _rwkv_chunk_kernel(at_ref, bt_ref, kt_ref, rt_ref, v_ref, eg_ref,
                       m_ref, n_ref, rq_ref, yl_ref, *, nchunk, unroll):
    ch, n = RW_CHUNK, HEAD_DIM
    ti = _iota((ch, 2 * ch), 0)
    si = _iota((ch, 2 * ch), 1) % ch
    strict = si < ti
    incl = si <= ti
    same = (ti // RW_SUB) == (si // RW_SUB)
    eye = _iota((n, n), 0) == _iota((n, n), 1)
    each = lambda f, *ls: [f(*t) for t in zip(*ls)]

    def body(it, _):
        cs = [it * unroll + u for u in range(unroll)]
        sls = [pl.ds(pl.multiple_of(c * ch, ch), ch) for c in cs]
        ids = [(h, u) for h in range(2) for u in range(unroll)]
        ld = lambda ref: [ref[sls[u], h * n:(h + 1) * n] for h, u in ids]
        sp = lambda xs: [_Split(t) for t in xs]
        dup = lambda xs: [_SplitDup(t) for t in xs]
        twice = lambda t: jnp.concatenate([t, t], axis=0)
        a_, b_, k_, r_, v_ = ld(at_ref), ld(bt_ref), ld(kt_ref), ld(rt_ref), ld(v_ref)
        g_end = [eg_ref[pl.ds(cs[u] * ch + ch - 1, 1), h * n:(h + 1) * n] for h, u in ids]
        ar = sp(each(lambda a, r: jnp.concatenate([a, r], axis=0), a_, r_))
        pb = each(lambda x, y: _dot_nt(x.left(1), twice(y.right(1))), ar, sp(b_))
        pk = each(lambda x, y: _dot_nt(x.left(1), twice(y.right(1))), ar, sp(k_))
        a_ab = [jnp.where(strict, t[:ch], 0.0) for t in pb]
        a_ak = dup([jnp.where(strict, t[:ch], 0.0) for t in pk])
        a_rb = dup([jnp.where(incl, t[ch:], 0.0) for t in pb])
        a_rk = dup([jnp.where(incl, t[ch:], 0.0) for t in pk])
        d1 = [jnp.where(same, t, 0.0) for t in a_ab]
        lb = each(lambda t, d: t - d, a_ab, d1)
        d1 = dup(d1)
        d2 = dup([_dot(d.left, d.right) for d in d1])
        akv = each(_mm3_dup, a_ak, v_)
        x = each(lambda l, a, t: jnp.concatenate([l, a, t], axis=1), lb, a_, akv)
        d4 = dup([_dot(d.left, d.right) for d in d2])
        x = each(lambda d, t: t + _mm3_dup(d, t), d1, x)
        d8 = dup([_dot(d.left, d.right) for d in d4])
        x = each(lambda d, t: t + _mm3_dup(d, t), d2, x)
        x = each(lambda d, t: t + _mm3_dup(d, t), d4, x)
        x = each(lambda d, t: t + _mm3_dup(d, t), d8, x)
        e1 = dup([t[:, :2 * ch] for t in x])
        wu = [t[:, 2 * ch:] for t in x]
        e2 = dup([_dot(e.left, e.right) for e in e1])
        wu = each(lambda e, t: t + _mm3_dup(e, t), e1, wu)
        wu = each(lambda e, t: t + _mm3_dup(e, t), e2, wu)
        ry = each(_mm3_dup, a_rb, wu)
        rkv = each(_mm3_dup, a_rk, v_)
        wus, vs = sp(wu), sp(v_)
        mn = each(lambda b, g, t: _mm3_tn(_Split(b * g), t), b_, g_end, wus)
        kv = each(lambda k, g, t: _mm3_tn(_Split(k * g), t), k_, g_end, vs)
        rq = each(lambda r, t: r + t[:, :n], r_, ry)
        yl = each(lambda t, w: t[:, n:] + w, ry, rkv)
        for u in range(unroll):
            rq_ref[sls[u], :] = jnp.concatenate([rq[u], rq[unroll + u]], axis=1)
            yl_ref[sls[u], :] = jnp.concatenate([yl[u], yl[unroll + u]], axis=1)
        for i, (h, u) in enumerate(ids):
            m_ref[h, cs[u]] = mn[i][:, :n] + jnp.where(eye, g_end[i], 0.0)
            n_ref[h, cs[u]] = mn[i][:, n:] + kv[i]
        return 0

    lax.fori_loop(0, nchunk // unroll, body, 0)


def rwkv_chunk(at, bt, kt, rt, v, eg, *, batch, nchunk=8, unroll=8):
    t, w = at.shape
    s = t // batch
    ch, n = RW_CHUNK, HEAD_DIM
    pairs = w // LANE
    rows = nchunk * ch
    nr = s // rows
    spec = pl.BlockSpec((rows, LANE), lambda b, p, i: (b * nr + i, p))
    mspec = pl.BlockSpec((2, nchunk, n, n), lambda b, p, i: (b * pairs + p, i, 0, 0))
    return pl.pallas_call(
        functools.partial(_rwkv_chunk_kernel, nchunk=nchunk, unroll=unroll),
        grid=(batch, pairs, nr),
        in_specs=[spec] * 6,
        out_specs=[mspec, mspec, spec, spec],
        out_shape=[jax.ShapeDtypeStruct((batch * w // n, s // ch, n, n), F32)] * 2
        + [jax.ShapeDtypeStruct((t, w), F32)] * 2,
        compiler_params=_cp("parallel", "parallel", "parallel"),
        name="rwkv_chunk",
    )(at, bt, kt, rt, v, eg)


def _rwkv_scan_kernel(m_ref, n_ref, s_ref, st_ref, *, heads, nchunk):
    @pl.when(pl.program_id(1) == 0)
    def _():
        st_ref[...] = jnp.zeros_like(st_ref)

    def body(c, _):
        sts = [st_ref[hh] for hh in range(heads)]
        for hh in range(heads):
            s_ref[hh, c] = sts[hh]
        ms = [_Split(m_ref[hh, c]) for hh in range(heads)]
        ss = [_Split(st) for st in sts]
        new = [_mm3(a, b) for a, b in zip(ms, ss)]
        for hh in range(heads):
            st_ref[hh] = new[hh] + n_ref[hh, c]
        return 0

    lax.fori_loop(0, nchunk, body, 0)


def rwkv_scan(m, nn, *, heads=16, nchunk=8):
    bh, nc, n, _ = m.shape
    heads = min(heads, bh)
    nchunk = min(nchunk, nc)
    spec = pl.BlockSpec((heads, nchunk, n, n), lambda b, i: (b, i, 0, 0))
    return pl.pallas_call(
        functools.partial(_rwkv_scan_kernel, heads=heads, nchunk=nchunk),
        grid=(bh // heads, nc // nchunk),
        in_specs=[spec, spec],
        out_specs=spec,
        out_shape=jax.ShapeDtypeStruct((bh, nc, n, n), F32),
        scratch_shapes=[pltpu.VMEM((heads, n, n), F32)],
        compiler_params=_cp("parallel", "arbitrary"),
        name="rwkv_scan",
    )(m, nn)


def _rwkv_out_kernel(rq_ref, yl_ref, s_ref, lw_ref, lb_ref, avg_ref, o_ref, *, nchunk):
    ch, n = RW_CHUNK, HEAD_DIM
    ys = []
    for h in range(2):
        cols = slice(h * n, (h + 1) * n)
        ys.append(jnp.concatenate(
            [_mm3(_Split(rq_ref[c * ch:(c + 1) * ch, cols]), _Split(s_ref[h, c])) for c in range(nchunk)],
            axis=0))
    y = jnp.concatenate(ys, axis=1) + yl_ref[...]
    avg2 = jnp.concatenate([avg_ref[...]] * 2, axis=0)

    def mean(t):
        ts = _Split(t)
        return _dot(jnp.concatenate([ts.hi, ts.lo], axis=1), avg2)

    d = y - mean(y)
    var = mean(d * d)
    o_ref[...] = d * lax.rsqrt(var + GN_EPS) * lw_ref[...] + lb_ref[...]


def rwkv_out(rq, yl, s0, ln_w, ln_b, *, batch, nchunk=16):
    t, w = rq.shape
    s = t // batch
    n = HEAD_DIM
    pairs = w // LANE
    rows = nchunk * RW_CHUNK
    nr = s // rows
    li = jnp.arange(LANE) // n
    avg = ((li[:, None] == li[None, :]).astype(F32) / n).astype(BF16)
    spec = pl.BlockSpec((rows, LANE), lambda b, p, i: (b * nr + i, p))
    hspec = pl.BlockSpec((1, LANE), lambda b, p, i: (0, p))
    return pl.pallas_call(
        functools.partial(_rwkv_out_kernel, nchunk=nchunk),
        grid=(batch, pairs, nr),
        in_specs=[spec, spec, pl.BlockSpec((2, nchunk, n, n), lambda b, p, i: (b * pairs + p, i, 0, 0)),
                  hspec, hspec, pl.BlockSpec((LANE, LANE), lambda b, p, i: (0, 0))],
        out_specs=spec,
        out_shape=jax.ShapeDtypeStruct((t, w), F32),
        compiler_params=_cp("parallel", "parallel", "parallel"),
        name="rwkv_out",
    )(rq, yl, s0, ln_w.reshape(1, w), ln_b.reshape(1, w), avg)


def _hyb_out_kernel(oa_ref, y_ref, bv_ref, g_ref, wa_ref, wb_ref, r_ref, o_ref):
    ob = (y_ref[...] + bv_ref[...]) * g_ref[...]
    o_ref[...] = (r_ref[...] + _dot(oa_ref[...].astype(BF16), wa_ref[...])
                  + _dot(ob.astype(BF16), wb_ref[...]))


def hyb_out(res, oa, y, bv, g, wa, wb, *, tm=1024):
    t, d = res.shape
    row = lambda i: (i, 0)
    fix = lambda i: (0, 0)
    act = lambda a: pl.BlockSpec((tm, a.shape[1]), row)
    return pl.pallas_call(
        _hyb_out_kernel,
        grid=(t // tm,),
        in_specs=[act(oa), act(y), act(bv), act(g), pl.BlockSpec(wa.shape, fix),
                  pl.BlockSpec(wb.shape, fix), pl.BlockSpec((tm, d), row)],
        out_specs=pl.BlockSpec((tm, d), row),
        out_shape=jax.ShapeDtypeStruct((t, d), F32),
        compiler_params=_cp("parallel"),
        name="hyb_out",
    )(oa, y, bv, g, wa, wb, res)


def _pad_cols(w, n):
    return jnp.pad(w, ((0, 0), (0, n - w.shape[1])))


def sb_rwkv_layer(h, b, s, attn_norm, w_in, w_out, mu, w0, w2, a0, a2, g2, k_k, k_a, r_k, ln_w, ln_b):
    sbw = SB_HEADS * HEAD_DIM
    rww = RW_HEADS * HEAD_DIM
    rw_in = w_in.shape[1] - 3 * sbw
    rw_pad = -(-rw_in // LANE) * LANE
    dl, al = w2.shape[0], a2.shape[0]
    gl = g2.shape[0]
    assert dl + al == LANE and 3 * rww % LANE == 0

    qkv = rms_matmul(h, attn_norm, w_in[:, :3 * sbw].astype(BF16), out_dtype=BF16)
    rw = rms_matmul(h, attn_norm, _pad_cols(w_in[:, 3 * sbw:], rw_pad).astype(BF16))
    oa = sb_attention(qkv, batch=b, heads=SB_HEADS)

    w2p = jnp.zeros((LANE, rww), F32).at[:dl].set(w2)
    a2p = jnp.zeros((LANE, rww), F32).at[dl:].set(a2)
    g2p = jnp.zeros((rw_pad - 3 * rww - LANE, rww), F32).at[:gl].set(g2)
    vec = lambda t: t.reshape(1, rww).astype(F32)
    mu_p = _pad_cols(mu.reshape(1, rw_in), rw_pad)
    at, bt, kt, rt, vv, eg, g, bv = rwkv_prep(rw, mu_p, vec(w0), vec(a0), vec(k_k), vec(k_a), vec(r_k),
                                              w2p, a2p, g2p, seq=s, width=rww)
    m, nn, rq, yl = rwkv_chunk(at, bt, kt, rt, vv, eg, batch=b)
    s0 = rwkv_scan(m, nn)
    y = rwkv_out(rq, yl, s0, ln_w, ln_b, batch=b)
    return hyb_out(h, oa, y, bv, g, w_out[:sbw].astype(BF16), w_out[sbw:].astype(BF16))


def mla_layer(h, b, s, positions, attn_norm, w_down, q_norm, kv_norm, w_uq, w_ukv, w_o):
    nh, dn, dr, dv = MLA_HEADS, MLA_NOPE, MLA_ROPE, MLA_V
    qr, kvr = MLA_Q_RANK, MLA_KV_RANK
    assert dn == dv and qr % LANE == 0 and (qr + LANE) % kvr == 0
    wd = jnp.concatenate([w_down[:, :qr], jnp.tile(w_down[:, qr + kvr:], (1, LANE // dr)),
                          w_down[:, qr:qr + kvr]], axis=1)
    c = rms_matmul(h, attn_norm, wd.astype(BF16))
    wq = w_uq.reshape(qr, nh, dn + dr)
    wq = jnp.concatenate([wq[:, :, :dn].reshape(qr, nh * dn), wq[:, :, dn:].reshape(qr, nh * dr)], axis=1)
    q = rms_matmul(c, q_norm, wq.astype(BF16), x_col=0)
    wkv = w_ukv.reshape(kvr, nh, dn + dv)
    wkv = jnp.concatenate([wkv[:, :, :dn].reshape(kvr, nh * dn), wkv[:, :, dn:].reshape(kvr, nh * dv)], axis=1)
    kv = rms_matmul(c, kv_norm, wkv.astype(BF16), x_col=(qr + LANE) // kvr, out_dtype=BF16)
    qrot, krot = rope(q, c, positions, q_col=dn // dr, qw=nh * dr, k_col=qr // LANE)
    o = softmax_attention(q, qrot, kv, krot, 1.0 / math.sqrt(dn + dr), batch=b, heads=nh)
    return matmul_residual(h, o, w_o.astype(BF16))


def kernel(x, p, positions, attn_norm, ffn_norm, ffn_w_in, ffn_conv_w, ffn_conv_b, ffn_w_out, ple_w_proj, ple_norm, ple_gate_norm, ple_w_gate, hyb_w_in, hyb_w_out, rw_mu, rw_w0, rw_w2, rw_a0, rw_a2, rw_g2, rw_k_k, rw_k_a, rw_r_k, rw_ln_w, rw_ln_b, mla_w_down, mla_q_norm, mla_kv_norm, mla_w_uq, mla_w_ukv, mla_w_o, final_norm):
    b, s, d = x.shape
    depth = p.shape[0]
    h = x.reshape(b * s, d)
    pos = positions.reshape(b * s)
    for i in range(depth):
        j = i // 2
        if i % 2 == 0:
            h = sb_rwkv_layer(h, b, s, attn_norm[i], hyb_w_in[j], hyb_w_out[j], rw_mu[j], rw_w0[j],
                              rw_w2[j], rw_a0[j], rw_a2[j], rw_g2[j], rw_k_k[j], rw_k_a[j], rw_r_k[j],
                              rw_ln_w[j], rw_ln_b[j])
        else:
            h = mla_layer(h, b, s, pos, attn_norm[i], mla_w_down[j], mla_q_norm[j], mla_kv_norm[j],
                          mla_w_uq[j], mla_w_ukv[j], mla_w_o[j])
        ug = rms_matmul(h, ffn_norm[i], ffn_w_in[i].astype(BF16), out_dtype=BF16)
        h = convglu_ple(h, ug, ffn_conv_w[i], ffn_conv_b[i], ffn_w_out[i], p[i].reshape(b * s, -1),
                        ple_w_proj[i], ple_norm[i], ple_gate_norm[i], ple_w_gate[i], final_norm,
                        seq=s, final=(i == depth - 1))
    return h.reshape(b, s, d)
```

```python
import functools
import math

import jax
import jax.numpy as jnp
from jax import lax
from jax.experimental import pallas as pl
from jax.experimental.pallas import tpu as pltpu

F32 = jnp.float32
BF16 = jnp.bfloat16

NORM_EPS = 1e-6
GN_EPS = 64e-5
ROPE_THETA = 10000.0
LANE = 128
VMEM_LIMIT = 48 * 1024 * 1024
HALO = 16
MXU_WIDTH = 256
TILE_VMEM_BUDGET = 36 * 1024 * 1024
WEIGHT_TILE_BYTES = 6 * 1024 * 1024

SB_HEADS = 8
RW_HEADS = 8
HEAD_DIM = 64
MLA_HEADS = 16
MLA_NOPE = 64
MLA_ROPE = 32
MLA_V = 64
MLA_Q_RANK = 384
MLA_KV_RANK = 256
SB_CUT = -151.0
RW_CHUNK = 64
RW_SUB = 16


def _cp(*sem):
    return pltpu.CompilerParams(dimension_semantics=sem, vmem_limit_bytes=VMEM_LIMIT)


def _iota(shape, dim):
    return lax.broadcasted_iota(jnp.int32, shape, dim)


def _softplus(z):
    return jnp.maximum(z, 0.0) + jnp.log1p(jnp.exp(-jnp.abs(z)))


def _sigmoid(z):
    return 1.0 / (1.0 + jnp.exp(-z))


def _dot(a, b):
    return jnp.dot(a, b, preferred_element_type=F32)


def _dot_nt(a, b):
    return lax.dot_general(a, b, (((1,), (1,)), ((), ())), preferred_element_type=F32)


def _dot_tn(a, b):
    return lax.dot_general(a, b, (((0,), (0,)), ((), ())), preferred_element_type=F32)


class _Split:
    def __init__(self, x):
        self.x = x
        self.hi = x.astype(BF16)
        self.lo = (x - self.hi.astype(F32)).astype(BF16)
        self._packed = {}

    def left(self, axis):
        key = ("l", axis)
        if key not in self._packed:
            self._packed[key] = jnp.concatenate([self.hi, self.lo, self.hi], axis=axis)
        return self._packed[key]

    def right(self, axis):
        key = ("r", axis)
        if key not in self._packed:
            self._packed[key] = jnp.concatenate([self.hi, self.hi, self.lo], axis=axis)
        return self._packed[key]


def _mm3(a, b):
    return _dot(a.left(1), b.right(0))


def _mm3_tn(a, b):
    return _dot_tn(a.left(0), b.right(0))


class _SplitDup:
    def __init__(self, xd):
        hi = xd.astype(BF16)
        lo = (xd - hi.astype(F32)).astype(BF16)
        self.left = jnp.concatenate([jnp.where(_iota(xd.shape, 1) < HEAD_DIM, hi, lo), hi], axis=1)
        self.right = _rows_packed(hi, lo)


def _rows_packed(hi, lo):
    return jnp.concatenate([hi, hi, lo, jnp.zeros_like(hi)], axis=0)


def _mm3_dup(a, b):
    bs = _Split(b)
    return _dot(a.left, _rows_packed(bs.hi, bs.lo))


def _rms(x, g, eps=NORM_EPS):
    return x * lax.rsqrt(jnp.mean(x * x, axis=-1, keepdims=True) + eps) * g


def _rms_matmul_kernel(x_ref, g_ref, w_ref, o_ref, xn_ref):
    @pl.when(pl.program_id(1) == 0)
    def _():
        xn_ref[...] = _rms(x_ref[...].astype(F32), g_ref[...]).astype(BF16)

    o_ref[...] = _dot(xn_ref[...], w_ref[...]).astype(o_ref.dtype)


def _pick_tiles(t, k, n, tile_bytes):
    tn = max(c for c in range(LANE, n + 1, LANE)
             if n % c == 0 and (c == n or c % MXU_WIDTH == 0) and k * c * 2 <= WEIGHT_TILE_BYTES)
    for tm in (2048, 1024, 512, 256):
        if t % tm == 0 and tile_bytes(tm, tn) + 2 * k * tn * 2 <= TILE_VMEM_BUDGET:
            return tm, tn
    raise ValueError("no row tile fits")


def rms_matmul(x, g, w, *, x_col=0, out_dtype=F32):
    t = x.shape[0]
    k, n = w.shape
    ob = jnp.dtype(out_dtype).itemsize
    tm, tn = _pick_tiles(t, k, n, lambda tm, tn: 2 * tm * k * x.dtype.itemsize + tm * k * 2 + 2 * tm * tn * ob)
    return pl.pallas_call(
        _rms_matmul_kernel,
        grid=(t // tm, n // tn),
        in_specs=[pl.BlockSpec((tm, k), lambda i, j: (i, x_col)),
                  pl.BlockSpec((1, k), lambda i, j: (0, 0)),
                  pl.BlockSpec((k, tn), lambda i, j: (0, j))],
        out_specs=pl.BlockSpec((tm, tn), lambda i, j: (i, j)),
        out_shape=jax.ShapeDtypeStruct((t, n), out_dtype),
        scratch_shapes=[pltpu.VMEM((tm, k), BF16)],
        compiler_params=_cp("parallel", "arbitrary"),
        name="rms_matmul",
    )(x, g.reshape(1, k).astype(F32), w)


def _matmul_res_kernel(a_ref, w_ref, r_ref, o_ref):
    o_ref[...] = r_ref[...] + _dot(a_ref[...].astype(BF16), w_ref[...])


def matmul_residual(res, a, w):
    t, k = a.shape
    n = w.shape[1]
    tm, tn = _pick_tiles(t, k, n, lambda tm, tn: 2 * tm * k * a.dtype.itemsize + 4 * tm * tn * 4)
    return pl.pallas_call(
        _matmul_res_kernel,
        grid=(t // tm, n // tn),
        in_specs=[pl.BlockSpec((tm, k), lambda i, j: (i, 0)),
                  pl.BlockSpec((k, tn), lambda i, j: (0, j)),
                  pl.BlockSpec((tm, tn), lambda i, j: (i, j))],
        out_specs=pl.BlockSpec((tm, tn), lambda i, j: (i, j)),
        out_shape=jax.ShapeDtypeStruct((t, n), F32),
        compiler_params=_cp("parallel", "arbitrary"),
        name="matmul_residual",
    )(a, w, res)


def _convglu_ple_kernel(u_ref, g_ref, gp_ref, cw_ref, cb_ref, w_ref, r_ref,
                        p_ref, wp_ref, pn_ref, gn_ref, wg_ref, fn_ref, o_ref, *, tm, seq, final):
    i = pl.program_id(0)
    g = g_ref[...].astype(F32)
    at_start = (i * tm) % seq == 0
    halo = jnp.where(at_start, 0.0, gp_ref[...].astype(F32))
    h1, h2 = halo[HALO - 1:HALO, :], halo[HALO - 2:HALO - 1, :]
    r1, r2 = pltpu.roll(g, 1, axis=0), pltpu.roll(g, 2, axis=0)
    top = _iota((8, g.shape[1]), 0)
    p1 = jnp.concatenate([jnp.where(top == 0, h1, r1[:8]), r1[8:]], axis=0)
    p2 = jnp.concatenate([jnp.where(top == 0, h2, jnp.where(top == 1, h1, r2[:8])), r2[8:]], axis=0)
    c = cw_ref[0:1, :] * p2 + cw_ref[1:2, :] * p1 + cw_ref[2:3, :] * g + cb_ref[...]
    act = c * (1.0 + lax.erf(c)) * u_ref[...].astype(F32)
    h = r_ref[...] + _dot(act.astype(BF16), w_ref[...])
    e = _rms(_dot(p_ref[...].astype(BF16), wp_ref[...]), pn_ref[...])
    gate = _sigmoid(_dot(_rms(h, gn_ref[...]).astype(BF16), wg_ref[...]))
    out = h + gate * e
    if final:
        out = _rms(out, fn_ref[...])
    o_ref[...] = out


def convglu_ple(res, ug, conv_w, conv_b, w_out, p, w_proj, p_norm, g_norm, w_gate, f_norm,
                *, seq, final, tm=256):
    t, d = res.shape
    f = w_out.shape[0]
    pd = p.shape[1]
    hb = tm // HALO
    rs2 = 1.0 / math.sqrt(2.0)
    row = lambda i: (i, 0)
    fix = lambda i: (0, 0)
    vec = pl.BlockSpec((1, d), fix)
    return pl.pallas_call(
        functools.partial(_convglu_ple_kernel, tm=tm, seq=seq, final=final),
        grid=(t // tm,),
        in_specs=[pl.BlockSpec((tm, f), row),
                  pl.BlockSpec((tm, f), lambda i: (i, 1)),
                  pl.BlockSpec((HALO, f), lambda i: (jnp.maximum(i * hb - 1, 0), 1)),
                  pl.BlockSpec((3, f), fix), pl.BlockSpec((1, f), fix), pl.BlockSpec((f, d), fix),
                  pl.BlockSpec((tm, d), row), pl.BlockSpec((tm, pd), row), pl.BlockSpec((pd, d), fix),
                  vec, vec, pl.BlockSpec((d, d), fix), vec],
        out_specs=pl.BlockSpec((tm, d), row),
        out_shape=jax.ShapeDtypeStruct((t, d), F32),
        compiler_params=_cp("parallel"),
        name="convglu_ple",
    )(ug, ug, ug, conv_w * rs2, (conv_b * rs2).reshape(1, f), (w_out * rs2).astype(BF16), res,
      p, w_proj.astype(BF16), p_norm.reshape(1, d), g_norm.reshape(1, d), w_gate.astype(BF16),
      f_norm.reshape(1, d))


def _sb_attn_kernel(q_ref, k_ref, v_ref, u_ref, o_ref, *, tq, tk, nh, scale):
    qi = pl.program_id(2)
    nblk = tq // tk
    width = nh * HEAD_DIM
    qf = q_ref[...] * scale
    qlane = _iota((tq, width), 1) // HEAD_DIM
    qs = jnp.concatenate([jnp.where(qlane == h, qf, jnp.zeros_like(qf)) for h in range(nh)], axis=0)
    vlane = _iota((tk, width), 1) // HEAD_DIM
    rows = lambda x, h: x[h * tq:(h + 1) * tq]

    def sweep(k_lo, run, acc, masked):
        parts = []
        for d in reversed(range(nblk)):
            k0 = k_lo + d * tk
            t = _dot_nt(qs, k_ref[pl.ds(k0, tk), :]) * math.log2(math.e)
            sp = jnp.maximum(t, 0.0) + jnp.log2(1.0 + jnp.exp2(-jnp.abs(t)))
            lom = -sp
            valid = None
            if masked:
                valid = _iota((nh * tq, tk), 1) + d * tk < _iota((nh * tq, tk), 0) % tq
                lom = jnp.where(valid, lom, 0.0)
            hi = lom.astype(BF16)
            lo = (lom - hi.astype(F32)).astype(BF16)
            sums = _dot(jnp.concatenate([hi, lo], axis=1), u_ref[...])
            parts.append((k0, t - sp, sums, valid))
        for k0, logsig, sums, valid in parts:
            w = jnp.exp2(logsig + run + sums[:, :tk])
            if masked:
                w = jnp.where(valid, w, 0.0)
            w = w.astype(BF16)
            vb = v_ref[pl.ds(k0, tk), :]
            wcat = jnp.concatenate([rows(w, h) for h in range(nh)], axis=1)
            vcat = jnp.concatenate([jnp.where(vlane == h, vb, jnp.zeros_like(vb)) for h in range(nh)], axis=0)
            acc = acc + _dot(wcat, vcat)
            run = run + sums[:, tk:]
        return run, acc

    run = jnp.zeros((nh * tq, tk), F32)
    acc = jnp.zeros((tq, width), F32)
    run, acc = sweep(pl.multiple_of(qi * tq, tq), run, acc, True)

    def live(run):
        return jnp.max(run) > SB_CUT

    def cond(c):
        return c[0]

    def body(c):
        _, j, run, acc = c
        run, acc = sweep(pl.multiple_of(j * tq, tq), run, acc, False)
        return (j > 0) & live(run), j - 1, run, acc

    _, _, _, acc = lax.while_loop(cond, body, ((qi > 0) & live(run), qi - 1, run, acc))
    o_ref[...] = acc


def sb_attention(qkv, *, batch, heads, tq=256, nh=4):
    t, w3 = qkv.shape
    s = t // batch
    nq = s // tq
    width = nh * HEAD_DIM
    groups = heads // nh
    tk = LANE
    jj = jnp.arange(2 * tk)[:, None] % tk
    ss = jnp.arange(2 * tk)[None, :]
    u = jnp.where(ss < tk, jj > ss, True).astype(BF16)
    return pl.pallas_call(
        functools.partial(_sb_attn_kernel, tq=tq, tk=tk, nh=nh, scale=1.0 / math.sqrt(HEAD_DIM)),
        grid=(batch, groups, nq),
        in_specs=[pl.BlockSpec((tq, width), lambda b, p, i: (b * nq + i, p)),
                  pl.BlockSpec((s, width), lambda b, p, i: (b, groups + p)),
                  pl.BlockSpec((s, width), lambda b, p, i: (b, 2 * groups + p)),
                  pl.BlockSpec((2 * tk, 2 * tk), lambda b, p, i: (0, 0))],
        out_specs=pl.BlockSpec((tq, width), lambda b, p, i: (b * nq + i, p)),
        out_shape=jax.ShapeDtypeStruct((t, w3 // 3), F32),
        compiler_params=_cp("parallel", "parallel", "arbitrary"),
        name="sb_attention",
    )(qkv, qkv, qkv, u)


def _softmax_attn_kernel(qn_ref, qr_ref, kn_ref, kr_ref, v_ref, o_ref, *bufs, tq, tk, npair, scale):
    qi = pl.program_id(2)
    g = pl.program_id(1)
    nh = 2 * npair
    lane = _iota((tq, LANE), 1)
    qr = qr_ref[...] * (scale * math.log2(math.e))
    slots = LANE // MLA_ROPE
    qs = []
    for h in range(nh):
        cols = slice(h // 2 * LANE, (h // 2 + 1) * LANE)
        qn = qn_ref[:, cols] * (scale * math.log2(math.e))
        nope = jnp.where(lane // MLA_NOPE == h % 2, qn, 0.0)
        rope_ = jnp.where(lane // MLA_ROPE == (g * nh + h) % slots, qr, 0.0)
        qs.append(jnp.concatenate([nope, rope_], axis=1).astype(BF16))

    def scores(j, dsts):
        k0 = pl.multiple_of(j * tk, tk)
        mx = []
        for h in range(nh):
            cols = slice(h // 2 * LANE, (h // 2 + 1) * LANE)
            kb = jnp.concatenate([kn_ref[pl.ds(k0, tk), cols], kr_ref[pl.ds(k0, tk), :]], axis=1)
            s = _dot_nt(qs[h], kb)
            dsts[h][...] = s
            mx.append(jnp.max(s, axis=1, keepdims=True))
        return tuple(mx)

    def consume(srcs, j, carry, mx):
        out = []
        for h in range(nh):
            cols = slice(h // 2 * LANE, (h // 2 + 1) * LANE)
            vb = v_ref[pl.ds(pl.multiple_of(j * tk, tk), tk), cols]
            mine = (_iota(vb.shape, 1) // MLA_V) == h % 2
            vbh = jnp.where(mine, vb, jnp.ones_like(vb))
            m, acc = carry[h]
            s = srcs[h][...]
            if mx is None:
                s = jnp.where(_iota((tq, tk), 1) <= _iota((tq, tk), 0), s, -1e30)
                m_new = jnp.maximum(m, jnp.max(s, axis=1, keepdims=True))
            else:
                m_new = jnp.maximum(m, mx[h])
            p = jnp.exp2(s - m_new)
            acc = jnp.exp2(m - m_new) * acc + _dot(p.astype(BF16), vbh)
            out.append((m_new, acc))
        return tuple(out)

    sa, sb = bufs[:nh], bufs[nh:]

    def pair(t, carry):
        state, mxa = carry
        mxb = scores(2 * t + 1, sb)
        state = consume(sa, 2 * t, state, mxa)
        mxa = scores(2 * t + 2, sa)
        return consume(sb, 2 * t + 1, state, mxb), mxa

    def tail_odd(carry):
        state, mxa = carry
        scores(qi, sb)
        state = consume(sa, qi - 1, state, mxa)
        return consume(sb, qi, state, None)

    def tail_even(carry):
        return consume(sa, qi, carry[0], None)

    init = ((jnp.full((tq, 1), -1e30, F32), jnp.zeros((tq, LANE), F32)),) * nh
    carry = lax.fori_loop(0, qi // 2, pair, (init, scores(0, sa)))
    state = lax.cond(qi % 2 == 1, tail_odd, tail_even, carry)
    for pr in range(npair):
        acc0, acc1 = state[2 * pr][1], state[2 * pr + 1][1]
        o_ref[:, pr * LANE:(pr + 1) * LANE] = jnp.where(
            lane < MLA_V, acc0 / pltpu.roll(acc0, MLA_V, axis=1), acc1 / pltpu.roll(acc1, MLA_V, axis=1))


def softmax_attention(q, qrot, kv, krot, scale, *, batch, heads, tq=512, npair=2):
    tk = tq
    t = q.shape[0]
    s = t // batch
    nq = s // tq
    width = npair * LANE
    groups = heads * MLA_NOPE // width
    rope_per_group = LANE // (2 * npair * MLA_ROPE)
    buf = pltpu.VMEM((tq, tk), F32)
    return pl.pallas_call(
        functools.partial(_softmax_attn_kernel, tq=tq, tk=tk, npair=npair, scale=scale),
        grid=(batch, groups, nq),
        in_specs=[pl.BlockSpec((tq, width), lambda b, p, i: (b * nq + i, p)),
                  pl.BlockSpec((tq, LANE), lambda b, p, i: (b * nq + i, p // rope_per_group)),
                  pl.BlockSpec((s, width), lambda b, p, i: (b, p)),
                  pl.BlockSpec((s, LANE), lambda b, p, i: (b, 0)),
                  pl.BlockSpec((s, width), lambda b, p, i: (b, groups + p))],
        out_specs=pl.BlockSpec((tq, width), lambda b, p, i: (b * nq + i, p)),
        out_shape=jax.ShapeDtypeStruct((t, heads * MLA_V), F32),
        scratch_shapes=[buf] * (4 * npair),
        compiler_params=_cp("parallel", "parallel", "arbitrary"),
        name="softmax_attention",
    )(q, qrot, kv, krot, kv)


def _rope_kernel(q_ref, k_ref, pos_ref, f_ref, oq_ref, ok_ref):
    half = MLA_ROPE // 2
    ang = pos_ref[...].astype(F32) * f_ref[...]
    cos, sin = jnp.cos(ang), jnp.sin(ang)
    low = _iota(ang.shape, 1) % MLA_ROPE < half

    def rot(x):
        partner = jnp.where(low, -pltpu.roll(x, LANE - half, axis=1), pltpu.roll(x, half, axis=1))
        return x * cos + partner * sin

    for c in range(q_ref.shape[1] // LANE):
        cols = slice(c * LANE, (c + 1) * LANE)
        oq_ref[:, cols] = rot(q_ref[:, cols])
    ok_ref[...] = rot(k_ref[...]).astype(ok_ref.dtype)


def rope(q, c, positions, *, q_col, qw, k_col, tm=512):
    t = q.shape[0]
    half = MLA_ROPE // 2
    inv_freq = ROPE_THETA ** (-jnp.arange(half, dtype=F32) / half)
    freq = jnp.tile(inv_freq, LANE // half).reshape(1, LANE)
    return pl.pallas_call(
        _rope_kernel,
        grid=(t // tm,),
        in_specs=[pl.BlockSpec((tm, qw), lambda i: (i, q_col)), pl.BlockSpec((tm, LANE), lambda i: (i, k_col)),
                  pl.BlockSpec((tm, 1), lambda i: (i, 0)), pl.BlockSpec((1, LANE), lambda i: (0, 0))],
        out_specs=[pl.BlockSpec((tm, qw), lambda i: (i, 0)), pl.BlockSpec((tm, LANE), lambda i: (i, 0))],
        out_shape=[jax.ShapeDtypeStruct((t, qw), F32), jax.ShapeDtypeStruct((t, LANE), BF16)],
        compiler_params=_cp("parallel"),
        name="rope",
    )(q, c, positions.reshape(t, 1), freq)


def _rwkv_prep_kernel(x_ref, xp_ref, mu_ref, w0_ref, a0_ref, kk_ref, ka_ref, rk_ref,
                      w2_ref, a2_ref, g2_ref, tri_ref, hs_ref,
                      at_ref, bt_ref, kt_ref, rt_ref, v_ref, eg_ref, g_ref, bv_ref, *, tm, seq, width):
    i = pl.program_id(0)
    x = x_ref[...]
    at_start = (i * tm) % seq == 0
    prev_row = jnp.where(at_start, 0.0, xp_ref[7:8, :])
    rolled = pltpu.roll(x, 1, axis=0)
    top = _iota((8, x.shape[1]), 0)
    prev = jnp.concatenate([jnp.where(top == 0, prev_row, rolled[:8]), rolled[8:]], axis=0)
    xm = x + (prev - x) * mu_ref[...]
    r = xm[:, :width]
    k = xm[:, width:2 * width]
    v = xm[:, 2 * width:3 * width]
    l1 = xm[:, 3 * width:3 * width + LANE]
    l2 = xm[:, 3 * width + LANE:]
    wpre = w0_ref[...] + _dot(_Split(jnp.tanh(l1)).left(1), w2_ref[...])
    a = _sigmoid(a0_ref[...] + _dot(_Split(l1).left(1), a2_ref[...]))
    g = _dot(_Split(_sigmoid(l2)).left(1), g2_ref[...])

    def hilo(t):
        ts = _Split(t)
        return jnp.concatenate([ts.hi, ts.lo], axis=1)

    lw = -jnp.exp(-_softplus(-wpre) - 0.5)
    hs = hs_ref[...]
    kk = k * kk_ref[...]
    kk = kk * lax.rsqrt(jnp.maximum(_dot(hilo(kk * kk), hs), 1e-24))
    km = k * (1.0 + (a - 1.0) * ka_ref[...])
    bonus = _dot(hilo(r * km * rk_ref[...]), hs)
    lws = _Split(lw)
    cum = _dot(tri_ref[...], jnp.concatenate([lws.hi, lws.lo], axis=0))
    eg = jnp.exp(cum)
    ieg = jnp.exp(-cum)
    at_ref[...] = -kk * jnp.exp(cum - lw)
    bt_ref[...] = kk * a * ieg
    kt_ref[...] = km * ieg
    rt_ref[...] = r * eg
    v_ref[...] = v
    eg_ref[...] = eg
    g_ref[...] = g
    bv_ref[...] = bonus * v


def rwkv_prep(x, mu, w0, a0, k_k, k_a, r_k, w2p, a2p, g2p, *, seq, width, tm=256):
    t, wx = x.shape
    ch = RW_CHUNK
    ti = jnp.arange(tm)
    tri = ((ti[:, None] // ch == ti[None, :] // ch) & (ti[None, :] <= ti[:, None])).astype(BF16)
    tri = jnp.concatenate([tri, tri], axis=1)
    ci = jnp.arange(width) // HEAD_DIM
    hs = (ci[:, None] == ci[None, :]).astype(BF16)
    hs = jnp.concatenate([hs, hs], axis=0)
    w2p, a2p, g2p = (_Split(w).right(0) for w in (w2p, a2p, g2p))
    hb = tm // 8
    row = lambda i: (i, 0)
    fix = lambda i: (0, 0)
    vec = pl.BlockSpec((1, width), fix)
    out = jax.ShapeDtypeStruct((t, width), F32)
    return pl.pallas_call(
        functools.partial(_rwkv_prep_kernel, tm=tm, seq=seq, width=width),
        grid=(t // tm,),
        in_specs=[pl.BlockSpec((tm, wx), row),
                  pl.BlockSpec((8, wx), lambda i: (jnp.maximum(i * hb - 1, 0), 0)),
                  pl.BlockSpec((1, wx), fix), vec, vec, vec, vec, vec,
                  pl.BlockSpec(w2p.shape, fix), pl.BlockSpec(a2p.shape, fix),
                  pl.BlockSpec(g2p.shape, fix), pl.BlockSpec(tri.shape, fix),
                  pl.BlockSpec(hs.shape, fix)],
        out_specs=[pl.BlockSpec((tm, width), row)] * 8,
        out_shape=[out] * 8,
        compiler_params=_cp("parallel"),
        name="rwkv_prep",
    )(x, x, mu, w0, a0, k_k, k_a, r_k, w2p, a2p, g2p, tri, hs)


def _rwkv_chunk_kernel(at_ref, bt_ref, kt_ref, rt_ref, v_ref, eg_ref,
                       m_ref, n_ref, rq_ref, yl_ref, *, nchunk, unroll):
    ch, n = RW_CHUNK, HEAD_DIM
    ti = _iota((ch, 2 * ch), 0)
    si = _iota((ch, 2 * ch), 1) % ch
    strict = si < ti
    incl = si <= ti
    same = (ti // RW_SUB) == (si // RW_SUB)
    eye = _iota((n, n), 0) == _iota((n, n), 1)
    each = lambda f, *ls: [f(*t) for t in zip(*ls)]

    def body(it, _):
        cs = [it * unroll + u for u in range(unroll)]
        sls = [pl.ds(pl.multiple_of(c * ch, ch), ch) for c in cs]
        ids = [(h, u) for h in range(2) for u in range(unroll)]
        ld = lambda ref: [ref[sls[u], h * n:(h + 1) * n] for h, u in ids]
        sp = lambda xs: [_Split(t) for t in xs]
        dup = lambda xs: [_SplitDup(t) for t in xs]
        twice = lambda t: jnp.concatenate([t, t], axis=0)
        a_, b_, k_, r_, v_ = ld(at_ref), ld(bt_ref), ld(kt_ref), ld(rt_ref), ld(v_ref)
        g_end = [eg_ref[pl.ds(cs[u] * ch + ch - 1, 1), h * n:(h + 1) * n] for h, u in ids]
        ar = sp(each(lambda a, r: jnp.concatenate([a, r], axis=0), a_, r_))
        pb = each(lambda x, y: _dot_nt(x.left(1), twice(y.right(1))), ar, sp(b_))
        pk = each(lambda x, y: _dot_nt(x.left(1), twice(y.right(1))), ar, sp(k_))
        a_ab = [jnp.where(strict, t[:ch], 0.0) for t in pb]
        a_ak = dup([jnp.where(strict, t[:ch], 0.0) for t in pk])
        a_rb = dup([jnp.where(incl, t[ch:], 0.0) for t in pb])
        a_rk = dup([jnp.where(incl, t[ch:], 0.0) for t in pk])
        d1 = [jnp.where(same, t, 0.0) for t in a_ab]
        lb = each(lambda t, d: t - d, a_ab, d1)
        d1 = dup(d1)
        d2 = dup([_dot(d.left, d.right) for d in d1])
        akv = each(_mm3_dup, a_ak, v_)
        x = each(lambda l, a, t: jnp.concatenate([l, a, t], axis=1), lb, a_, akv)
        d4 = dup([_dot(d.left, d.right) for d in d2])
        x = each(lambda d, t: t + _mm3_dup(d, t), d1, x)
        d8 = dup([_dot(d.left, d.right) for d in d4])
        x = each(lambda d, t: t + _mm3_dup(d, t), d2, x)
        x = each(lambda d, t: t + _mm3_dup(d, t), d4, x)
        x = each(lambda d, t: t + _mm3_dup(d, t), d8, x)
        e1 = dup([t[:, :2 * ch] for t in x])
        wu = [t[:, 2 * ch:] for t in x]
        e2 = dup([_dot(e.left, e.right) for e in e1])
        wu = each(lambda e, t: t + _mm3_dup(e, t), e1, wu)
        wu = each(lambda e, t: t + _mm3_dup(e, t), e2, wu)
        ry = each(_mm3_dup, a_rb, wu)
        rkv = each(_mm3_dup, a_rk, v_)
        wus, vs = sp(wu), sp(v_)
        mn = each(lambda b, g, t: _mm3_tn(_Split(b * g), t), b_, g_end, wus)
        kv = each(lambda k, g, t: _mm3_tn(_Split(k * g), t), k_, g_end, vs)
        rq = each(lambda r, t: r + t[:, :n], r_, ry)
        yl = each(lambda t, w: t[:, n:] + w, ry, rkv)
        for u in range(unroll):
            rq_ref[sls[u], :] = jnp.concatenate([rq[u], rq[unroll + u]], axis=1)
            yl_ref[sls[u], :] = jnp.concatenate([yl[u], yl[unroll + u]], axis=1)
        for i, (h, u) in enumerate(ids):
            m_ref[h, cs[u]] = mn[i][:, :n] + jnp.where(eye, g_end[i], 0.0)
            n_ref[h, cs[u]] = mn[i][:, n:] + kv[i]
        return 0

    lax.fori_loop(0, nchunk // unroll, body, 0)


def rwkv_chunk(at, bt, kt, rt, v, eg, *, batch, nchunk=16, unroll=8):
    t, w = at.shape
    s = t // batch
    ch, n = RW_CHUNK, HEAD_DIM
    pairs = w // LANE
    rows = nchunk * ch
    nr = s // rows
    spec = pl.BlockSpec((rows, LANE), lambda b, p, i: (b * nr + i, p))
    mspec = pl.BlockSpec((2, nchunk, n, n), lambda b, p, i: (b * pairs + p, i, 0, 0))
    return pl.pallas_call(
        functools.partial(_rwkv_chunk_kernel, nchunk=nchunk, unroll=unroll),
        grid=(batch, pairs, nr),
        in_specs=[spec] * 6,
        out_specs=[mspec, mspec, spec, spec],
        out_shape=[jax.ShapeDtypeStruct((batch * w // n, s // ch, n, n), F32)] * 2
        + [jax.ShapeDtypeStruct((t, w), F32)] * 2,
        compiler_params=_cp("parallel", "parallel", "parallel"),
        name="rwkv_chunk",
    )(at, bt, kt, rt, v, eg)


def _rwkv_scan_kernel(m_ref, n_ref, s_ref, st_ref, *, heads, nchunk):
    @pl.when(pl.program_id(1) == 0)
    def _():
        st_ref[...] = jnp.zeros_like(st_ref)

    def body(c, _):
        sts = [st_ref[hh] for hh in range(heads)]
        for hh in range(heads):
            s_ref[hh, c] = sts[hh]
        ms = [_Split(m_ref[hh, c]) for hh in range(heads)]
        ss = [_Split(st) for st in sts]
        new = [_mm3(a, b) for a, b in zip(ms, ss)]
        for hh in range(heads):
            st_ref[hh] = new[hh] + n_ref[hh, c]
        return 0

    lax.fori_loop(0, nchunk, body, 0)


def rwkv_scan(m, nn, *, heads=16, nchunk=8):
    bh, nc, n, _ = m.shape
    heads = min(heads, bh)
    nchunk = min(nchunk, nc)
    spec = pl.BlockSpec((heads, nchunk, n, n), lambda b, i: (b, i, 0, 0))
    return pl.pallas_call(
        functools.partial(_rwkv_scan_kernel, heads=heads, nchunk=nchunk),
        grid=(bh // heads, nc // nchunk),
        in_specs=[spec, spec],
        out_specs=spec,
        out_shape=jax.ShapeDtypeStruct((bh, nc, n, n), F32),
        scratch_shapes=[pltpu.VMEM((heads, n, n), F32)],
        compiler_params=_cp("parallel", "arbitrary"),
        name="rwkv_scan",
    )(m, nn)


def _rwkv_out_kernel(rq_ref, yl_ref, s_ref, lw_ref, lb_ref, avg_ref, o_ref, *, nchunk):
    ch, n = RW_CHUNK, HEAD_DIM
    ys = []
    for h in range(2):
        cols = slice(h * n, (h + 1) * n)
        ys.append(jnp.concatenate(
            [_mm3(_Split(rq_ref[c * ch:(c + 1) * ch, cols]), _Split(s_ref[h, c])) for c in range(nchunk)],
            axis=0))
    y = jnp.concatenate(ys, axis=1) + yl_ref[...]
    avg2 = jnp.concatenate([avg_ref[...]] * 2, axis=0)

    def mean(t):
        ts = _Split(t)
        return _dot(jnp.concatenate([ts.hi, ts.lo], axis=1), avg2)

    d = y - mean(y)
    var = mean(d * d)
    o_ref[...] = d * lax.rsqrt(var + GN_EPS) * lw_ref[...] + lb_ref[...]


def rwkv_out(rq, yl, s0, ln_w, ln_b, *, batch, nchunk=16):
    t, w = rq.shape
    s = t // batch
    n = HEAD_DIM
    pairs = w // LANE
    rows = nchunk * RW_CHUNK
    nr = s // rows
    li = jnp.arange(LANE) // n
    avg = ((li[:, None] == li[None, :]).astype(F32) / n).astype(BF16)
    spec = pl.BlockSpec((rows, LANE), lambda b, p, i: (b * nr + i, p))
    hspec = pl.BlockSpec((1, LANE), lambda b, p, i: (0, p))
    return pl.pallas_call(
        functools.partial(_rwkv_out_kernel, nchunk=nchunk),
        grid=(batch, pairs, nr),
        in_specs=[spec, spec, pl.BlockSpec((2, nchunk, n, n), lambda b, p, i: (b * pairs + p, i, 0, 0)),
                  hspec, hspec, pl.BlockSpec((LANE, LANE), lambda b, p, i: (0, 0))],
        out_specs=spec,
        out_shape=jax.ShapeDtypeStruct((t, w), F32),
        compiler_params=_cp("parallel", "parallel", "parallel"),
        name="rwkv_out",
    )(rq, yl, s0, ln_w.reshape(1, w), ln_b.reshape(1, w), avg)


def _hyb_out_kernel(oa_ref, y_ref, bv_ref, g_ref, wa_ref, wb_ref, r_ref, o_ref):
    ob = (y_ref[...] + bv_ref[...]) * g_ref[...]
    o_ref[...] = (r_ref[...] + _dot(oa_ref[...].astype(BF16), wa_ref[...])
                  + _dot(ob.astype(BF16), wb_ref[...]))


def hyb_out(res, oa, y, bv, g, wa, wb, *, tm=1024):
    t, d = res.shape
    row = lambda i: (i, 0)
    fix = lambda i: (0, 0)
    act = lambda a: pl.BlockSpec((tm, a.shape[1]), row)
    return pl.pallas_call(
        _hyb_out_kernel,
        grid=(t // tm,),
        in_specs=[act(oa), act(y), act(bv), act(g), pl.BlockSpec(wa.shape, fix),
                  pl.BlockSpec(wb.shape, fix), pl.BlockSpec((tm, d), row)],
        out_specs=pl.BlockSpec((tm, d), row),
        out_shape=jax.ShapeDtypeStruct((t, d), F32),
        compiler_params=_cp("parallel"),
        name="hyb_out",
    )(oa, y, bv, g, wa, wb, res)


def _pad_cols(w, n):
    return jnp.pad(w, ((0, 0), (0, n - w.shape[1])))


def sb_rwkv_layer(h, b, s, attn_norm, w_in, w_out, mu, w0, w2, a0, a2, g2, k_k, k_a, r_k, ln_w, ln_b):
    sbw = SB_HEADS * HEAD_DIM
    rww = RW_HEADS * HEAD_DIM
    rw_in = w_in.shape[1] - 3 * sbw
    rw_pad = -(-rw_in // LANE) * LANE
    dl, al = w2.shape[0], a2.shape[0]
    gl = g2.shape[0]
    assert dl + al == LANE and 3 * rww % LANE == 0

    qkv = rms_matmul(h, attn_norm, w_in[:, :3 * sbw].astype(BF16), out_dtype=BF16)
    rw = rms_matmul(h, attn_norm, _pad_cols(w_in[:, 3 * sbw:], rw_pad).astype(BF16))
    oa = sb_attention(qkv, batch=b, heads=SB_HEADS)

    w2p = jnp.zeros((LANE, rww), F32).at[:dl].set(w2)
    a2p = jnp.zeros((LANE, rww), F32).at[dl:].set(a2)
    g2p = jnp.zeros((rw_pad - 3 * rww - LANE, rww), F32).at[:gl].set(g2)
    vec = lambda t: t.reshape(1, rww).astype(F32)
    mu_p = _pad_cols(mu.reshape(1, rw_in), rw_pad)
    at, bt, kt, rt, vv, eg, g, bv = rwkv_prep(rw, mu_p, vec(w0), vec(a0), vec(k_k), vec(k_a), vec(r_k),
                                              w2p, a2p, g2p, seq=s, width=rww)
    m, nn, rq, yl = rwkv_chunk(at, bt, kt, rt, vv, eg, batch=b)
    s0 = rwkv_scan(m, nn)
    y = rwkv_out(rq, yl, s0, ln_w, ln_b, batch=b)
    return hyb_out(h, oa, y, bv, g, w_out[:sbw].astype(BF16), w_out[sbw:].astype(BF16))


def mla_layer(h, b, s, positions, attn_norm, w_down, q_norm, kv_norm, w_uq, w_ukv, w_o):
    nh, dn, dr, dv = MLA_HEADS, MLA_NOPE, MLA_ROPE, MLA_V
    qr, kvr = MLA_Q_RANK, MLA_KV_RANK
    assert dn == dv and qr % LANE == 0 and (qr + LANE) % kvr == 0
    wd = jnp.concatenate([w_down[:, :qr], jnp.tile(w_down[:, qr + kvr:], (1, LANE // dr)),
                          w_down[:, qr:qr + kvr]], axis=1)
    c = rms_matmul(h, attn_norm, wd.astype(BF16))
    wq = w_uq.reshape(qr, nh, dn + dr)
    wq = jnp.concatenate([wq[:, :, :dn].reshape(qr, nh * dn), wq[:, :, dn:].reshape(qr, nh * dr)], axis=1)
    q = rms_matmul(c, q_norm, wq.astype(BF16), x_col=0)
    wkv = w_ukv.reshape(kvr, nh, dn + dv)
    wkv = jnp.concatenate([wkv[:, :, :dn].reshape(kvr, nh * dn), wkv[:, :, dn:].reshape(kvr, nh * dv)], axis=1)
    kv = rms_matmul(c, kv_norm, wkv.astype(BF16), x_col=(qr + LANE) // kvr, out_dtype=BF16)
    qrot, krot = rope(q, c, positions, q_col=dn // dr, qw=nh * dr, k_col=qr // LANE)
    o = softmax_attention(q, qrot, kv, krot, 1.0 / math.sqrt(dn + dr), batch=b, heads=nh)
    return matmul_residual(h, o, w_o.astype(BF16))


def kernel(x, p, positions, attn_norm, ffn_norm, ffn_w_in, ffn_conv_w, ffn_conv_b, ffn_w_out, ple_w_proj, ple_norm, ple_gate_norm, ple_w_gate, hyb_w_in, hyb_w_out, rw_mu, rw_w0, rw_w2, rw_a0, rw_a2, rw_g2, rw_k_k, rw_k_a, rw_r_k, rw_ln_w, rw_ln_b, mla_w_down, mla_q_norm, mla_kv_norm, mla_w_uq, mla_w_ukv, mla_w_o, final_norm):
    b, s, d = x.shape
    depth = p.shape[0]
    h = x.reshape(b * s, d)
    pos = positions.reshape(b * s)
    for i in range(depth):
        j = i // 2
        if i % 2 == 0:
            h = sb_rwkv_layer(h, b, s, attn_norm[i], hyb_w_in[j], hyb_w_out[j], rw_mu[j], rw_w0[j],
                              rw_w2[j], rw_a0[j], rw_a2[j], rw_g2[j], rw_k_k[j], rw_k_a[j], rw_r_k[j],
                              rw_ln_w[j], rw_ln_b[j])
        else:
            h = mla_layer(h, b, s, pos, attn_norm[i], mla_w_down[j], mla_q_norm[j], mla_kv_norm[j],
                          mla_w_uq[j], mla_w_ukv[j], mla_w_o[j])
        ug = rms_matmul(h, ffn_norm[i], ffn_w_in[i].astype(BF16), out_dtype=BF16)
        h = convglu_ple(h, ug, ffn_conv_w[i], ffn_conv_b[i], ffn_w_out[i], p[i].reshape(b * s, -1),
                        ple_w_proj[i], ple_norm[i], ple_gate_norm[i], ple_w_gate[i], final_norm,
                        seq=s, final=(i == depth - 1))
    return h.reshape(b, s, d)
```

```python
import functools
import math

import jax
import jax.numpy as jnp
from jax import lax
from jax.experimental import pallas as pl
from jax.experimental.pallas import tpu as pltpu

F32 = jnp.float32
BF16 = jnp.bfloat16

NORM_EPS = 1e-6
GN_EPS = 64e-5
ROPE_THETA = 10000.0
LANE = 128
VMEM_LIMIT = 48 * 1024 * 1024
HALO = 16
MXU_WIDTH = 256
TILE_VMEM_BUDGET = 36 * 1024 * 1024
WEIGHT_TILE_BYTES = 6 * 1024 * 1024

SB_HEADS = 8
RW_HEADS = 8
HEAD_DIM = 64
MLA_HEADS = 16
MLA_NOPE = 64
MLA_ROPE = 32
MLA_V = 64
MLA_Q_RANK = 384
MLA_KV_RANK = 256
SB_CUT = -151.0
RW_CHUNK = 64
RW_SUB = 16


def _cp(*sem):
    return pltpu.CompilerParams(dimension_semantics=sem, vmem_limit_bytes=VMEM_LIMIT)


def _iota(shape, dim):
    return lax.broadcasted_iota(jnp.int32, shape, dim)


def _softplus(z):
    return jnp.maximum(z, 0.0) + jnp.log1p(jnp.exp(-jnp.abs(z)))


def _sigmoid(z):
    return 1.0 / (1.0 + jnp.exp(-z))


def _dot(a, b):
    return jnp.dot(a, b, preferred_element_type=F32)


def _dot_nt(a, b):
    return lax.dot_general(a, b, (((1,), (1,)), ((), ())), preferred_element_type=F32)


def _dot_tn(a, b):
    return lax.dot_general(a, b, (((0,), (0,)), ((), ())), preferred_element_type=F32)


class _Split:
    def __init__(self, x):
        self.x = x
        self.hi = x.astype(BF16)
        self.lo = (x - self.hi.astype(F32)).astype(BF16)
        self._packed = {}

    def left(self, axis):
        key = ("l", axis)
        if key not in self._packed:
            self._packed[key] = jnp.concatenate([self.hi, self.lo, self.hi], axis=axis)
        return self._packed[key]

    def right(self, axis):
        key = ("r", axis)
        if key not in self._packed:
            self._packed[key] = jnp.concatenate([self.hi, self.hi, self.lo], axis=axis)
        return self._packed[key]


def _mm3(a, b):
    return _dot(a.left(1), b.right(0))


def _mm3_tn(a, b):
    return _dot_tn(a.left(0), b.right(0))


class _SplitDup:
    def __init__(self, xd):
        hi = xd.astype(BF16)
        lo = (xd - hi.astype(F32)).astype(BF16)
        self.left = jnp.concatenate([jnp.where(_iota(xd.shape, 1) < HEAD_DIM, hi, lo), hi], axis=1)
        self.right = _rows_packed(hi, lo)


def _rows_packed(hi, lo):
    return jnp.concatenate([hi, hi, lo, jnp.zeros_like(hi)], axis=0)


def _mm3_dup(a, b):
    bs = _Split(b)
    return _dot(a.left, _rows_packed(bs.hi, bs.lo))


def _rms(x, g, eps=NORM_EPS):
    return x * lax.rsqrt(jnp.mean(x * x, axis=-1, keepdims=True) + eps) * g


def _rms_matmul_kernel(x_ref, g_ref, w_ref, o_ref, xn_ref):
    @pl.when(pl.program_id(1) == 0)
    def _():
        xn_ref[...] = _rms(x_ref[...].astype(F32), g_ref[...]).astype(BF16)

    o_ref[...] = _dot(xn_ref[...], w_ref[...]).astype(o_ref.dtype)


def _pick_tiles(t, k, n, tile_bytes):
    tn = max(c for c in range(LANE, n + 1, LANE)
             if n % c == 0 and (c == n or c % MXU_WIDTH == 0) and k * c * 2 <= WEIGHT_TILE_BYTES)
    for tm in (2048, 1024, 512, 256):
        if t % tm == 0 and tile_bytes(tm, tn) + 2 * k * tn * 2 <= TILE_VMEM_BUDGET:
            return tm, tn
    raise ValueError("no row tile fits")


def rms_matmul(x, g, w, *, x_col=0, out_dtype=F32):
    t = x.shape[0]
    k, n = w.shape
    ob = jnp.dtype(out_dtype).itemsize
    tm, tn = _pick_tiles(t, k, n, lambda tm, tn: 2 * tm * k * x.dtype.itemsize + tm * k * 2 + 2 * tm * tn * ob)
    return pl.pallas_call(
        _rms_matmul_kernel,
        grid=(t // tm, n // tn),
        in_specs=[pl.BlockSpec((tm, k), lambda i, j: (i, x_col)),
                  pl.BlockSpec((1, k), lambda i, j: (0, 0)),
                  pl.BlockSpec((k, tn), lambda i, j: (0, j))],
        out_specs=pl.BlockSpec((tm, tn), lambda i, j: (i, j)),
        out_shape=jax.ShapeDtypeStruct((t, n), out_dtype),
        scratch_shapes=[pltpu.VMEM((tm, k), BF16)],
        compiler_params=_cp("parallel", "arbitrary"),
        name="rms_matmul",
    )(x, g.reshape(1, k).astype(F32), w)


def _matmul_res_kernel(a_ref, w_ref, r_ref, o_ref):
    o_ref[...] = r_ref[...] + _dot(a_ref[...].astype(BF16), w_ref[...])


def matmul_residual(res, a, w):
    t, k = a.shape
    n = w.shape[1]
    tm, tn = _pick_tiles(t, k, n, lambda tm, tn: 2 * tm * k * a.dtype.itemsize + 4 * tm * tn * 4)
    return pl.pallas_call(
        _matmul_res_kernel,
        grid=(t // tm, n // tn),
        in_specs=[pl.BlockSpec((tm, k), lambda i, j: (i, 0)),
                  pl.BlockSpec((k, tn), lambda i, j: (0, j)),
                  pl.BlockSpec((tm, tn), lambda i, j: (i, j))],
        out_specs=pl.BlockSpec((tm, tn), lambda i, j: (i, j)),
        out_shape=jax.ShapeDtypeStruct((t, n), F32),
        compiler_params=_cp("parallel", "arbitrary"),
        name="matmul_residual",
    )(a, w, res)


def _convglu_ple_kernel(u_ref, g_ref, gp_ref, cw_ref, cb_ref, w_ref, r_ref,
                        p_ref, wp_ref, pn_ref, gn_ref, wg_ref, fn_ref, o_ref, *, tm, seq, final):
    i = pl.program_id(0)
    g = g_ref[...].astype(F32)
    at_start = (i * tm) % seq == 0
    halo = jnp.where(at_start, 0.0, gp_ref[...].astype(F32))
    h1, h2 = halo[HALO - 1:HALO, :], halo[HALO - 2:HALO - 1, :]
    r1, r2 = pltpu.roll(g, 1, axis=0), pltpu.roll(g, 2, axis=0)
    top = _iota((8, g.shape[1]), 0)
    p1 = jnp.concatenate([jnp.where(top == 0, h1, r1[:8]), r1[8:]], axis=0)
    p2 = jnp.concatenate([jnp.where(top == 0, h2, jnp.where(top == 1, h1, r2[:8])), r2[8:]], axis=0)
    c = cw_ref[0:1, :] * p2 + cw_ref[1:2, :] * p1 + cw_ref[2:3, :] * g + cb_ref[...]
    act = c * (1.0 + lax.erf(c)) * u_ref[...].astype(F32)
    h = r_ref[...] + _dot(act.astype(BF16), w_ref[...])
    e = _rms(_dot(p_ref[...].astype(BF16), wp_ref[...]), pn_ref[...])
    gate = _sigmoid(_dot(_rms(h, gn_ref[...]).astype(BF16), wg_ref[...]))
    out = h + gate * e
    if final:
        out = _rms(out, fn_ref[...])
    o_ref[...] = out


def convglu_ple(res, ug, conv_w, conv_b, w_out, p, w_proj, p_norm, g_norm, w_gate, f_norm,
                *, seq, final, tm=256):
    t, d = res.shape
    f = w_out.shape[0]
    pd = p.shape[1]
    hb = tm // HALO
    rs2 = 1.0 / math.sqrt(2.0)
    row = lambda i: (i, 0)
    fix = lambda i: (0, 0)
    vec = pl.BlockSpec((1, d), fix)
    return pl.pallas_call(
        functools.partial(_convglu_ple_kernel, tm=tm, seq=seq, final=final),
        grid=(t // tm,),
        in_specs=[pl.BlockSpec((tm, f), row),
                  pl.BlockSpec((tm, f), lambda i: (i, 1)),
                  pl.BlockSpec((HALO, f), lambda i: (jnp.maximum(i * hb - 1, 0), 1)),
                  pl.BlockSpec((3, f), fix), pl.BlockSpec((1, f), fix), pl.BlockSpec((f, d), fix),
                  pl.BlockSpec((tm, d), row), pl.BlockSpec((tm, pd), row), pl.BlockSpec((pd, d), fix),
                  vec, vec, pl.BlockSpec((d, d), fix), vec],
        out_specs=pl.BlockSpec((tm, d), row),
        out_shape=jax.ShapeDtypeStruct((t, d), F32),
        compiler_params=_cp("parallel"),
        name="convglu_ple",
    )(ug, ug, ug, conv_w * rs2, (conv_b * rs2).reshape(1, f), (w_out * rs2).astype(BF16), res,
      p, w_proj.astype(BF16), p_norm.reshape(1, d), g_norm.reshape(1, d), w_gate.astype(BF16),
      f_norm.reshape(1, d))


def _sb_attn_kernel(q_ref, k_ref, v_ref, u_ref, o_ref, *, tq, tk, nh, scale):
    qi = pl.program_id(2)
    nblk = tq // tk
    width = nh * HEAD_DIM
    qf = q_ref[...] * scale
    qlane = _iota((tq, width), 1) // HEAD_DIM
    qs = jnp.concatenate([jnp.where(qlane == h, qf, jnp.zeros_like(qf)) for h in range(nh)], axis=0)
    vlane = _iota((tk, width), 1) // HEAD_DIM
    rows = lambda x, h: x[h * tq:(h + 1) * tq]

    def sweep(k_lo, run, acc, masked):
        parts = []
        for d in reversed(range(nblk)):
            k0 = k_lo + d * tk
            t = _dot_nt(qs, k_ref[pl.ds(k0, tk), :]) * math.log2(math.e)
            sp = jnp.maximum(t, 0.0) + jnp.log2(1.0 + jnp.exp2(-jnp.abs(t)))
            lom = -sp
            valid = None
            if masked:
                valid = _iota((nh * tq, tk), 1) + d * tk < _iota((nh * tq, tk), 0) % tq
                lom = jnp.where(valid, lom, 0.0)
            hi = lom.astype(BF16)
            lo = (lom - hi.astype(F32)).astype(BF16)
            sums = _dot(jnp.concatenate([hi, lo], axis=1), u_ref[...])
            parts.append((k0, t - sp, sums, valid))
        for k0, logsig, sums, valid in parts:
            w = jnp.exp2(logsig + run + sums[:, :tk])
            if masked:
                w = jnp.where(valid, w, 0.0)
            w = w.astype(BF16)
            vb = v_ref[pl.ds(k0, tk), :]
            wcat = jnp.concatenate([rows(w, h) for h in range(nh)], axis=1)
            vcat = jnp.concatenate([jnp.where(vlane == h, vb, jnp.zeros_like(vb)) for h in range(nh)], axis=0)
            acc = acc + _dot(wcat, vcat)
            run = run + sums[:, tk:]
        return run, acc

    run = jnp.zeros((nh * tq, tk), F32)
    acc = jnp.zeros((tq, width), F32)
    run, acc = sweep(pl.multiple_of(qi * tq, tq), run, acc, True)

    def live(run):
        return jnp.max(run) > SB_CUT

    def cond(c):
        return c[0]

    def body(c):
        _, j, run, acc = c
        run, acc = sweep(pl.multiple_of(j * tq, tq), run, acc, False)
        return (j > 0) & live(run), j - 1, run, acc

    _, _, _, acc = lax.while_loop(cond, body, ((qi > 0) & live(run), qi - 1, run, acc))
    o_ref[...] = acc.astype(o_ref.dtype)


def sb_attention(qkv, *, batch, heads, tq=256, nh=4):
    t, w3 = qkv.shape
    s = t // batch
    nq = s // tq
    width = nh * HEAD_DIM
    groups = heads // nh
    tk = LANE
    jj = jnp.arange(2 * tk)[:, None] % tk
    ss = jnp.arange(2 * tk)[None, :]
    u = jnp.where(ss < tk, jj > ss, True).astype(BF16)
    return pl.pallas_call(
        functools.partial(_sb_attn_kernel, tq=tq, tk=tk, nh=nh, scale=1.0 / math.sqrt(HEAD_DIM)),
        grid=(batch, groups, nq),
        in_specs=[pl.BlockSpec((tq, width), lambda b, p, i: (b * nq + i, p)),
                  pl.BlockSpec((s, width), lambda b, p, i: (b, groups + p)),
                  pl.BlockSpec((s, width), lambda b, p, i: (b, 2 * groups + p)),
                  pl.BlockSpec((2 * tk, 2 * tk), lambda b, p, i: (0, 0))],
        out_specs=pl.BlockSpec((tq, width), lambda b, p, i: (b * nq + i, p)),
        out_shape=jax.ShapeDtypeStruct((t, w3 // 3), BF16),
        compiler_params=_cp("parallel", "parallel", "arbitrary"),
        name="sb_attention",
    )(qkv, qkv, qkv, u)


def _softmax_attn_kernel(qn_ref, qr_ref, kn_ref, kr_ref, v_ref, o_ref, *bufs, tq, tk, npair, scale):
    qi = pl.program_id(2)
    g = pl.program_id(1)
    nh = 2 * npair
    lane = _iota((tq, LANE), 1)
    qr = qr_ref[...] * (scale * math.log2(math.e))
    slots = LANE // MLA_ROPE
    qs = []
    for h in range(nh):
        cols = slice(h // 2 * LANE, (h // 2 + 1) * LANE)
        qn = qn_ref[:, cols] * (scale * math.log2(math.e))
        nope = jnp.where(lane // MLA_NOPE == h % 2, qn, 0.0)
        rope_ = jnp.where(lane // MLA_ROPE == (g * nh + h) % slots, qr, 0.0)
        qs.append(jnp.concatenate([nope, rope_], axis=1).astype(BF16))

    def scores(j, dsts):
        k0 = pl.multiple_of(j * tk, tk)
        mx = []
        for h in range(nh):
            cols = slice(h // 2 * LANE, (h // 2 + 1) * LANE)
            kb = jnp.concatenate([kn_ref[pl.ds(k0, tk), cols], kr_ref[pl.ds(k0, tk), :]], axis=1)
            s = _dot_nt(qs[h], kb)
            dsts[h][...] = s
            mx.append(jnp.max(s, axis=1, keepdims=True))
        return tuple(mx)

    def consume(srcs, j, carry, mx):
        out = []
        for h in range(nh):
            cols = slice(h // 2 * LANE, (h // 2 + 1) * LANE)
            vb = v_ref[pl.ds(pl.multiple_of(j * tk, tk), tk), cols]
            mine = (_iota(vb.shape, 1) // MLA_V) == h % 2
            vbh = jnp.where(mine, vb, jnp.ones_like(vb))
            m, acc = carry[h]
            s = srcs[h][...]
            if mx is None:
                s = jnp.where(_iota((tq, tk), 1) <= _iota((tq, tk), 0), s, -1e30)
                m_new = jnp.maximum(m, jnp.max(s, axis=1, keepdims=True))
            else:
                m_new = jnp.maximum(m, mx[h])
            p = jnp.exp2(s - m_new)
            acc = jnp.exp2(m - m_new) * acc + _dot(p.astype(BF16), vbh)
            out.append((m_new, acc))
        return tuple(out)

    sa, sb = bufs[:nh], bufs[nh:]

    def pair(t, carry):
        state, mxa = carry
        mxb = scores(2 * t + 1, sb)
        state = consume(sa, 2 * t, state, mxa)
        mxa = scores(2 * t + 2, sa)
        return consume(sb, 2 * t + 1, state, mxb), mxa

    def tail_odd(carry):
        state, mxa = carry
        scores(qi, sb)
        state = consume(sa, qi - 1, state, mxa)
        return consume(sb, qi, state, None)

    def tail_even(carry):
        return consume(sa, qi, carry[0], None)

    init = ((jnp.full((tq, 1), -1e30, F32), jnp.zeros((tq, LANE), F32)),) * nh
    carry = lax.fori_loop(0, qi // 2, pair, (init, scores(0, sa)))
    state = lax.cond(qi % 2 == 1, tail_odd, tail_even, carry)
    for pr in range(npair):
        acc0, acc1 = state[2 * pr][1], state[2 * pr + 1][1]
        o_ref[:, pr * LANE:(pr + 1) * LANE] = jnp.where(
            lane < MLA_V, acc0 / pltpu.roll(acc0, MLA_V, axis=1), acc1 / pltpu.roll(acc1, MLA_V, axis=1)
        ).astype(o_ref.dtype)


def softmax_attention(q, qrot, kv, krot, scale, *, batch, heads, tq=512, npair=2):
    tk = tq
    t = q.shape[0]
    s = t // batch
    nq = s // tq
    width = npair * LANE
    groups = heads * MLA_NOPE // width
    rope_per_group = LANE // (2 * npair * MLA_ROPE)
    buf = pltpu.VMEM((tq, tk), F32)
    return pl.pallas_call(
        functools.partial(_softmax_attn_kernel, tq=tq, tk=tk, npair=npair, scale=scale),
        grid=(batch, groups, nq),
        in_specs=[pl.BlockSpec((tq, width), lambda b, p, i: (b * nq + i, p)),
                  pl.BlockSpec((tq, LANE), lambda b, p, i: (b * nq + i, p // rope_per_group)),
                  pl.BlockSpec((s, width), lambda b, p, i: (b, p)),
                  pl.BlockSpec((s, LANE), lambda b, p, i: (b, 0)),
                  pl.BlockSpec((s, width), lambda b, p, i: (b, groups + p))],
        out_specs=pl.BlockSpec((tq, width), lambda b, p, i: (b * nq + i, p)),
        out_shape=jax.ShapeDtypeStruct((t, heads * MLA_V), BF16),
        scratch_shapes=[buf] * (4 * npair),
        compiler_params=_cp("parallel", "parallel", "arbitrary"),
        name="softmax_attention",
    )(q, qrot, kv, krot, kv)


def _rope_kernel(q_ref, k_ref, pos_ref, f_ref, oq_ref, ok_ref):
    half = MLA_ROPE // 2
    ang = pos_ref[...].astype(F32) * f_ref[...]
    cos, sin = jnp.cos(ang), jnp.sin(ang)
    low = _iota(ang.shape, 1) % MLA_ROPE < half

    def rot(x):
        partner = jnp.where(low, -pltpu.roll(x, LANE - half, axis=1), pltpu.roll(x, half, axis=1))
        return x * cos + partner * sin

    for c in range(q_ref.shape[1] // LANE):
        cols = slice(c * LANE, (c + 1) * LANE)
        oq_ref[:, cols] = rot(q_ref[:, cols])
    ok_ref[...] = rot(k_ref[...]).astype(ok_ref.dtype)


def rope(q, c, positions, *, q_col, qw, k_col, tm=512):
    t = q.shape[0]
    half = MLA_ROPE // 2
    inv_freq = ROPE_THETA ** (-jnp.arange(half, dtype=F32) / half)
    freq = jnp.tile(inv_freq, LANE // half).reshape(1, LANE)
    return pl.pallas_call(
        _rope_kernel,
        grid=(t // tm,),
        in_specs=[pl.BlockSpec((tm, qw), lambda i: (i, q_col)), pl.BlockSpec((tm, LANE), lambda i: (i, k_col)),
                  pl.BlockSpec((tm, 1), lambda i: (i, 0)), pl.BlockSpec((1, LANE), lambda i: (0, 0))],
        out_specs=[pl.BlockSpec((tm, qw), lambda i: (i, 0)), pl.BlockSpec((tm, LANE), lambda i: (i, 0))],
        out_shape=[jax.ShapeDtypeStruct((t, qw), F32), jax.ShapeDtypeStruct((t, LANE), BF16)],
        compiler_params=_cp("parallel"),
        name="rope",
    )(q, c, positions.reshape(t, 1), freq)


def _rwkv_prep_kernel(x_ref, xp_ref, mu_ref, w0_ref, a0_ref, kk_ref, ka_ref, rk_ref,
                      w2_ref, a2_ref, g2_ref, tri_ref, hs_ref,
                      at_ref, bt_ref, kt_ref, rt_ref, v_ref, eg_ref, g_ref, bv_ref, *, tm, seq, width):
    i = pl.program_id(0)
    x = x_ref[...]
    at_start = (i * tm) % seq == 0
    prev_row = jnp.where(at_start, 0.0, xp_ref[7:8, :])
    rolled = pltpu.roll(x, 1, axis=0)
    top = _iota((8, x.shape[1]), 0)
    prev = jnp.concatenate([jnp.where(top == 0, prev_row, rolled[:8]), rolled[8:]], axis=0)
    xm = x + (prev - x) * mu_ref[...]
    r = xm[:, :width]
    k = xm[:, width:2 * width]
    v = xm[:, 2 * width:3 * width]
    l1 = xm[:, 3 * width:3 * width + LANE]
    l2 = xm[:, 3 * width + LANE:]
    wpre = w0_ref[...] + _dot(_Split(jnp.tanh(l1)).left(1), w2_ref[...])
    a = _sigmoid(a0_ref[...] + _dot(_Split(l1).left(1), a2_ref[...]))
    g = _dot(_Split(_sigmoid(l2)).left(1), g2_ref[...])

    def hilo(t):
        ts = _Split(t)
        return jnp.concatenate([ts.hi, ts.lo], axis=1)

    lw = -jnp.exp(-_softplus(-wpre) - 0.5)
    hs = hs_ref[...]
    kk = k * kk_ref[...]
    kk = kk * lax.rsqrt(jnp.maximum(_dot(hilo(kk * kk), hs), 1e-24))
    km = k * (1.0 + (a - 1.0) * ka_ref[...])
    bonus = _dot(hilo(r * km * rk_ref[...]), hs)
    lws = _Split(lw)
    cum = _dot(tri_ref[...], jnp.concatenate([lws.hi, lws.lo], axis=0))
    eg = jnp.exp(cum)
    ieg = jnp.exp(-cum)
    at_ref[...] = -kk * jnp.exp(cum - lw)
    bt_ref[...] = kk * a * ieg
    kt_ref[...] = km * ieg
    rt_ref[...] = r * eg
    v_ref[...] = v
    eg_ref[...] = eg
    g_ref[...] = g
    bv_ref[...] = bonus * v


def rwkv_prep(x, mu, w0, a0, k_k, k_a, r_k, w2p, a2p, g2p, *, seq, width, tm=256):
    t, wx = x.shape
    ch = RW_CHUNK
    ti = jnp.arange(tm)
    tri = ((ti[:, None] // ch == ti[None, :] // ch) & (ti[None, :] <= ti[:, None])).astype(BF16)
    tri = jnp.concatenate([tri, tri], axis=1)
    ci = jnp.arange(width) // HEAD_DIM
    hs = (ci[:, None] == ci[None, :]).astype(BF16)
    hs = jnp.concatenate([hs, hs], axis=0)
    w2p, a2p, g2p = (_Split(w).right(0) for w in (w2p, a2p, g2p))
    hb = tm // 8
    row = lambda i: (i, 0)
    fix = lambda i: (0, 0)
    vec = pl.BlockSpec((1, width), fix)
    out = jax.ShapeDtypeStruct((t, width), F32)
    return pl.pallas_call(
        functools.partial(_rwkv_prep_kernel, tm=tm, seq=seq, width=width),
        grid=(t // tm,),
        in_specs=[pl.BlockSpec((tm, wx), row),
                  pl.BlockSpec((8, wx), lambda i: (jnp.maximum(i * hb - 1, 0), 0)),
                  pl.BlockSpec((1, wx), fix), vec, vec, vec, vec, vec,
                  pl.BlockSpec(w2p.shape, fix), pl.BlockSpec(a2p.shape, fix),
                  pl.BlockSpec(g2p.shape, fix), pl.BlockSpec(tri.shape, fix),
                  pl.BlockSpec(hs.shape, fix)],
        out_specs=[pl.BlockSpec((tm, width), row)] * 8,
        out_shape=[out] * 8,
        compiler_params=_cp("parallel"),
        name="rwkv_prep",
    )(x, x, mu, w0, a0, k_k, k_a, r_k, w2p, a2p, g2p, tri, hs)


def _rwkv_chunk_kernel(at_ref, bt_ref, kt_ref, rt_ref, v_ref, eg_ref,
                       m_ref, n_ref, rq_ref, yl_ref, *, nchunk, unroll):
    ch, n = RW_CHUNK, HEAD_DIM
    ti = _iota((ch, 2 * ch), 0)
    si = _iota((ch, 2 * ch), 1) % ch
    strict = si < ti
    incl = si <= ti
    same = (ti // RW_SUB) == (si // RW_SUB)
    eye = _iota((n, n), 0) == _iota((n, n), 1)
    each = lambda f, *ls: [f(*t) for t in zip(*ls)]

    def body(it, _):
        cs = [it * unroll + u for u in range(unroll)]
        sls = [pl.ds(pl.multiple_of(c * ch, ch), ch) for c in cs]
        ids = [(h, u) for h in range(2) for u in range(unroll)]
        ld = lambda ref: [ref[sls[u], h * n:(h + 1) * n] for h, u in ids]
        sp = lambda xs: [_Split(t) for t in xs]
        dup = lambda xs: [_SplitDup(t) for t in xs]
        twice = lambda t: jnp.concatenate([t, t], axis=0)
        a_, b_, k_, r_, v_ = ld(at_ref), ld(bt_ref), ld(kt_ref), ld(rt_ref), ld(v_ref)
        g_end = [eg_ref[pl.ds(cs[u] * ch + ch - 1, 1), h * n:(h + 1) * n] for h, u in ids]
        ar = sp(each(lambda a, r: jnp.concatenate([a, r], axis=0), a_, r_))
        pb = each(lambda x, y: _dot_nt(x.left(1), twice(y.right(1))), ar, sp(b_))
        pk = each(lambda x, y: _dot_nt(x.left(1), twice(y.right(1))), ar, sp(k_))
        a_ab = [jnp.where(strict, t[:ch], 0.0) for t in pb]
        a_ak = dup([jnp.where(strict, t[:ch], 0.0) for t in pk])
        a_rb = dup([jnp.where(incl, t[ch:], 0.0) for t in pb])
        a_rk = dup([jnp.where(incl, t[ch:], 0.0) for t in pk])
        d1 = [jnp.where(same, t, 0.0) for t in a_ab]
        lb = each(lambda t, d: t - d, a_ab, d1)
        d1 = dup(d1)
        d2 = dup([_dot(d.left, d.right) for d in d1])
        akv = each(_mm3_dup, a_ak, v_)
        x = each(lambda l, a, t: jnp.concatenate([l, a, t], axis=1), lb, a_, akv)
        d4 = dup([_dot(d.left, d.right) for d in d2])
        x = each(lambda d, t: t + _mm3_dup(d, t), d1, x)
        d8 = dup([_dot(d.left, d.right) for d in d4])
        x = each(lambda d, t: t + _mm3_dup(d, t), d2, x)
        x = each(lambda d, t: t + _mm3_dup(d, t), d4, x)
        x = each(lambda d, t: t + _mm3_dup(d, t), d8, x)
        e1 = dup([t[:, :2 * ch] for t in x])
        wu = [t[:, 2 * ch:] for t in x]
        e2 = dup([_dot(e.left, e.right) for e in e1])
        wu = each(lambda e, t: t + _mm3_dup(e, t), e1, wu)
        wu = each(lambda e, t: t + _mm3_dup(e, t), e2, wu)
        ry = each(_mm3_dup, a_rb, wu)
        rkv = each(_mm3_dup, a_rk, v_)
        wus, vs = sp(wu), sp(v_)
        mn = each(lambda b, g, t: _mm3_tn(_Split(b * g), t), b_, g_end, wus)
        kv = each(lambda k, g, t: _mm3_tn(_Split(k * g), t), k_, g_end, vs)
        rq = each(lambda r, t: r + t[:, :n], r_, ry)
        yl = each(lambda t, w: t[:, n:] + w, ry, rkv)
        for u in range(unroll):
            rq_ref[sls[u], :] = jnp.concatenate([rq[u], rq[unroll + u]], axis=1)
            yl_ref[sls[u], :] = jnp.concatenate([yl[u], yl[unroll + u]], axis=1)
        for i, (h, u) in enumerate(ids):
            m_ref[h, cs[u]] = mn[i][:, :n] + jnp.where(eye, g_end[i], 0.0)
            n_ref[h, cs[u]] = mn[i][:, n:] + kv[i]
        return 0

    lax.fori_loop(0, nchunk // unroll, body, 0)


def rwkv_chunk(at, bt, kt, rt, v, eg, *, batch, nchunk=8, unroll=8):
    t, w = at.shape
    s = t // batch
    ch, n = RW_CHUNK, HEAD_DIM
    pairs = w // LANE
    rows = nchunk * ch
    nr = s // rows
    spec = pl.BlockSpec((rows, LANE), lambda b, p, i: (b * nr + i, p))
    mspec = pl.BlockSpec((2, nchunk, n, n), lambda b, p, i: (b * pairs + p, i, 0, 0))
    return pl.pallas_call(
        functools.partial(_rwkv_chunk_kernel, nchunk=nchunk, unroll=unroll),
        grid=(batch, pairs, nr),
        in_specs=[spec] * 6,
        out_specs=[mspec, mspec, spec, spec],
        out_shape=[jax.ShapeDtypeStruct((batch * w // n, s // ch, n, n), F32)] * 2
        + [jax.ShapeDtypeStruct((t, w), F32)] * 2,
        compiler_params=_cp("parallel", "parallel", "parallel"),
        name="rwkv_chunk",
    )(at, bt, kt, rt, v, eg)


def _rwkv_scan_kernel(m_ref, n_ref, s_ref, st_ref, *, heads, nchunk):
    @pl.when(pl.program_id(1) == 0)
    def _():
        st_ref[...] = jnp.zeros_like(st_ref)

    def body(c, _):
        sts = [st_ref[hh] for hh in range(heads)]
        for hh in range(heads):
            s_ref[hh, c] = sts[hh]
        ms = [_Split(m_ref[hh, c]) for hh in range(heads)]
        ss = [_Split(st) for st in sts]
        new = [_mm3(a, b) for a, b in zip(ms, ss)]
        for hh in range(heads):
            st_ref[hh] = new[hh] + n_ref[hh, c]
        return 0

    lax.fori_loop(0, nchunk, body, 0)


def rwkv_scan(m, nn, *, heads=16, nchunk=8):
    bh, nc, n, _ = m.shape
    heads = min(heads, bh)
    nchunk = min(nchunk, nc)
    spec = pl.BlockSpec((heads, nchunk, n, n), lambda b, i: (b, i, 0, 0))
    return pl.pallas_call(
        functools.partial(_rwkv_scan_kernel, heads=heads, nchunk=nchunk),
        grid=(bh // heads, nc // nchunk),
        in_specs=[spec, spec],
        out_specs=spec,
        out_shape=jax.ShapeDtypeStruct((bh, nc, n, n), F32),
        scratch_shapes=[pltpu.VMEM((heads, n, n), F32)],
        compiler_params=_cp("parallel", "arbitrary"),
        name="rwkv_scan",
    )(m, nn)


def _rwkv_out_kernel(rq_ref, yl_ref, s_ref, lw_ref, lb_ref, avg_ref, o_ref, *, nchunk):
    ch, n = RW_CHUNK, HEAD_DIM
    ys = []
    for h in range(2):
        cols = slice(h * n, (h + 1) * n)
        ys.append(jnp.concatenate(
            [_mm3(_Split(rq_ref[c * ch:(c + 1) * ch, cols]), _Split(s_ref[h, c])) for c in range(nchunk)],
            axis=0))
    y = jnp.concatenate(ys, axis=1) + yl_ref[...]
    avg2 = jnp.concatenate([avg_ref[...]] * 2, axis=0)

    def mean(t):
        ts = _Split(t)
        return _dot(jnp.concatenate([ts.hi, ts.lo], axis=1), avg2)

    d = y - mean(y)
    var = mean(d * d)
    o_ref[...] = d * lax.rsqrt(var + GN_EPS) * lw_ref[...] + lb_ref[...]


def rwkv_out(rq, yl, s0, ln_w, ln_b, *, batch, nchunk=16):
    t, w = rq.shape
    s = t // batch
    n = HEAD_DIM
    pairs = w // LANE
    rows = nchunk * RW_CHUNK
    nr = s // rows
    li = jnp.arange(LANE) // n
    avg = ((li[:, None] == li[None, :]).astype(F32) / n).astype(BF16)
    spec = pl.BlockSpec((rows, LANE), lambda b, p, i: (b * nr + i, p))
    hspec = pl.BlockSpec((1, LANE), lambda b, p, i: (0, p))
    return pl.pallas_call(
        functools.partial(_rwkv_out_kernel, nchunk=nchunk),
        grid=(batch, pairs, nr),
        in_specs=[spec, spec, pl.BlockSpec((2, nchunk, n, n), lambda b, p, i: (b * pairs + p, i, 0, 0)),
                  hspec, hspec, pl.BlockSpec((LANE, LANE), lambda b, p, i: (0, 0))],
        out_specs=spec,
        out_shape=jax.ShapeDtypeStruct((t, w), F32),
        compiler_params=_cp("parallel", "parallel", "parallel"),
        name="rwkv_out",
    )(rq, yl, s0, ln_w.reshape(1, w), ln_b.reshape(1, w), avg)


def _hyb_out_kernel(oa_ref, y_ref, bv_ref, g_ref, wa_ref, wb_ref, r_ref, o_ref):
    ob = (y_ref[...] + bv_ref[...]) * g_ref[...]
    o_ref[...] = (r_ref[...] + _dot(oa_ref[...].astype(BF16), wa_ref[...])
                  + _dot(ob.astype(BF16), wb_ref[...]))


def hyb_out(res, oa, y, bv, g, wa, wb, *, tm=1024):
    t, d = res.shape
    row = lambda i: (i, 0)
    fix = lambda i: (0, 0)
    act = lambda a: pl.BlockSpec((tm, a.shape[1]), row)
    return pl.pallas_call(
        _hyb_out_kernel,
        grid=(t // tm,),
        in_specs=[act(oa), act(y), act(bv), act(g), pl.BlockSpec(wa.shape, fix),
                  pl.BlockSpec(wb.shape, fix), pl.BlockSpec((tm, d), row)],
        out_specs=pl.BlockSpec((tm, d), row),
        out_shape=jax.ShapeDtypeStruct((t, d), F32),
        compiler_params=_cp("parallel"),
        name="hyb_out",
    )(oa, y, bv, g, wa, wb, res)


def _pad_cols(w, n):
    return jnp.pad(w, ((0, 0), (0, n - w.shape[1])))


def sb_rwkv_layer(h, b, s, attn_norm, w_in, w_out, mu, w0, w2, a0, a2, g2, k_k, k_a, r_k, ln_w, ln_b):
    sbw = SB_HEADS * HEAD_DIM
    rww = RW_HEADS * HEAD_DIM
    rw_in = w_in.shape[1] - 3 * sbw
    rw_pad = -(-rw_in // LANE) * LANE
    dl, al = w2.shape[0], a2.shape[0]
    gl = g2.shape[0]
    assert dl + al == LANE and 3 * rww % LANE == 0

    qkv = rms_matmul(h, attn_norm, w_in[:, :3 * sbw].astype(BF16), out_dtype=BF16)
    rw = rms_matmul(h, attn_norm, _pad_cols(w_in[:, 3 * sbw:], rw_pad).astype(BF16))
    oa = sb_attention(qkv, batch=b, heads=SB_HEADS)

    w2p = jnp.zeros((LANE, rww), F32).at[:dl].set(w2)
    a2p = jnp.zeros((LANE, rww), F32).at[dl:].set(a2)
    g2p = jnp.zeros((rw_pad - 3 * rww - LANE, rww), F32).at[:gl].set(g2)
    vec = lambda t: t.reshape(1, rww).astype(F32)
    mu_p = _pad_cols(mu.reshape(1, rw_in), rw_pad)
    at, bt, kt, rt, vv, eg, g, bv = rwkv_prep(rw, mu_p, vec(w0), vec(a0), vec(k_k), vec(k_a), vec(r_k),
                                              w2p, a2p, g2p, seq=s, width=rww)
    m, nn, rq, yl = rwkv_chunk(at, bt, kt, rt, vv, eg, batch=b)
    s0 = rwkv_scan(m, nn)
    y = rwkv_out(rq, yl, s0, ln_w, ln_b, batch=b)
    return hyb_out(h, oa, y, bv, g, w_out[:sbw].astype(BF16), w_out[sbw:].astype(BF16))


def mla_layer(h, b, s, positions, attn_norm, w_down, q_norm, kv_norm, w_uq, w_ukv, w_o):
    nh, dn, dr, dv = MLA_HEADS, MLA_NOPE, MLA_ROPE, MLA_V
    qr, kvr = MLA_Q_RANK, MLA_KV_RANK
    assert dn == dv and qr % LANE == 0 and (qr + LANE) % kvr == 0
    wd = jnp.concatenate([w_down[:, :qr], jnp.tile(w_down[:, qr + kvr:], (1, LANE // dr)),
                          w_down[:, qr:qr + kvr]], axis=1)
    c = rms_matmul(h, attn_norm, wd.astype(BF16))
    wq = w_uq.reshape(qr, nh, dn + dr)
    wq = jnp.concatenate([wq[:, :, :dn].reshape(qr, nh * dn), wq[:, :, dn:].reshape(qr, nh * dr)], axis=1)
    q = rms_matmul(c, q_norm, wq.astype(BF16), x_col=0)
    wkv = w_ukv.reshape(kvr, nh, dn + dv)
    wkv = jnp.concatenate([wkv[:, :, :dn].reshape(kvr, nh * dn), wkv[:, :, dn:].reshape(kvr, nh * dv)], axis=1)
    kv = rms_matmul(c, kv_norm, wkv.astype(BF16), x_col=(qr + LANE) // kvr, out_dtype=BF16)
    qrot, krot = rope(q, c, positions, q_col=dn // dr, qw=nh * dr, k_col=qr // LANE)
    o = softmax_attention(q, qrot, kv, krot, 1.0 / math.sqrt(dn + dr), batch=b, heads=nh)
    return matmul_residual(h, o, w_o.astype(BF16))


def kernel(x, p, positions, attn_norm, ffn_norm, ffn_w_in, ffn_conv_w, ffn_conv_b, ffn_w_out, ple_w_proj, ple_norm, ple_gate_norm, ple_w_gate, hyb_w_in, hyb_w_out, rw_mu, rw_w0, rw_w2, rw_a0, rw_a2, rw_g2, rw_k_k, rw_k_a, rw_r_k, rw_ln_w, rw_ln_b, mla_w_down, mla_q_norm, mla_kv_norm, mla_w_uq, mla_w_ukv, mla_w_o, final_norm):
    b, s, d = x.shape
    depth = p.shape[0]
    h = x.reshape(b * s, d)
    pos = positions.reshape(b * s)
    for i in range(depth):
        j = i // 2
        if i % 2 == 0:
            h = sb_rwkv_layer(h, b, s, attn_norm[i], hyb_w_in[j], hyb_w_out[j], rw_mu[j], rw_w0[j],
                              rw_w2[j], rw_a0[j], rw_a2[j], rw_g2[j], rw_k_k[j], rw_k_a[j], rw_r_k[j],
                              rw_ln_w[j], rw_ln_b[j])
        else:
            h = mla_layer(h, b, s, pos, attn_norm[i], mla_w_down[j], mla_q_norm[j], mla_kv_norm[j],
                          mla_w_uq[j], mla_w_ukv[j], mla_w_o[j])
        ug = rms_matmul(h, ffn_norm[i], ffn_w_in[i].astype(BF16), out_dtype=BF16)
        h = convglu_ple(h, ug, ffn_conv_w[i], ffn_conv_b[i], ffn_w_out[i], p[i].reshape(b * s, -1),
                        ple_w_proj[i], ple_norm[i], ple_gate_norm[i], ple_w_gate[i], final_norm,
                        seq=s, final=(i == depth - 1))
    return h.reshape(b, s, d)
```
